```python
import math
import jax, jax.numpy as jnp
from jax import lax
import numpy as np

D_MODEL = 2048
BATCH = 4
SEQ = 2048
DEPTH = 2

N_EVEN = (DEPTH + 1) // 2
N_ODD = DEPTH // 2

D_SSD = D_MODEL
SSD_HEAD_DIM = 64
SSD_HEADS = D_SSD // SSD_HEAD_DIM
SSD_GROUPS = 4
SSD_STATE = 128
SSD_CONV = 4
SSD_CHUNK = 128
SSD_CONV_DIM = D_SSD + 2 * SSD_GROUPS * SSD_STATE
D_CONF = D_MODEL
CONF_KERNEL = 31
IN_EVEN = D_SSD + SSD_CONV_DIM + SSD_HEADS + 2 * D_CONF

ATTN_HEADS = 16
ATTN_HEAD_DIM = D_MODEL // ATTN_HEADS
DILATED_PATTERNS = ((128, 1), (512, 4), (2048, 16))
ATTN_BLOCK = 128
REL_BUCKETS = 32
REL_MAX_DIST = 2048

D_FF = ((8 * D_MODEL // 3 + 255) // 256) * 256
N_EXPERTS = 8
TOP_K = 2
EPS = 1e-6

kernel_name = "hybrid_ssd_conformer_dilated_attn_moe"


def rms_norm(x, g):
    xf = x.astype(jnp.float32)
    y = xf * lax.rsqrt(jnp.mean(xf * xf, axis=-1, keepdims=True) + EPS)
    return y.astype(x.dtype) * g


def layer_norm(x, g, b):
    xf = x.astype(jnp.float32)
    mu = jnp.mean(xf, axis=-1, keepdims=True)
    var = jnp.mean(jnp.square(xf - mu), axis=-1, keepdims=True)
    return ((xf - mu) * lax.rsqrt(var + EPS)).astype(x.dtype) * g + b


def causal_dwconv(u, w, b):
    k = w.shape[0]
    out = lax.conv_general_dilated(
        u, w[:, None, :].astype(u.dtype), window_strides=(1,), padding=[(k - 1, 0)],
        dimension_numbers=("NWC", "WIO", "NWC"), feature_group_count=u.shape[-1])
    return out + b


def swiglu(t, w1, w3, w2):
    return (jax.nn.silu(t @ w1) * (t @ w3)) @ w2


def ssd_chunked(x, dt, a, bm, cm):
    bsz, s, h, p = x.shape
    nc = s // SSD_CHUNK
    r = h // SSD_GROUPS
    xd = (x * dt[..., None]).reshape(bsz, nc, SSD_CHUNK, SSD_GROUPS, r, p)
    adt = (dt * a).astype(jnp.float32).reshape(bsz, nc, SSD_CHUNK, SSD_GROUPS, r)
    bc = bm.reshape(bsz, nc, SSD_CHUNK, SSD_GROUPS, SSD_STATE)
    cc = cm.reshape(bsz, nc, SSD_CHUNK, SSD_GROUPS, SSD_STATE)
    a_cs = jnp.cumsum(adt, axis=2)
    causal = jnp.tril(jnp.ones((SSD_CHUNK, SSD_CHUNK), dtype=bool))
    seg = a_cs[:, :, :, None] - a_cs[:, :, None, :]
    decay = jnp.exp(jnp.where(causal[:, :, None, None], seg, -jnp.inf))
    cb = jnp.einsum("bclgn,bcsgn->bclsg", cc, bc)
    y_diag = jnp.einsum("bclsgr,bcsgrp->bclgrp", cb[..., None] * decay, xd)
    decay_to_end = jnp.exp(a_cs[:, :, -1:] - a_cs)
    states = jnp.einsum("bclgn,bclgrp->bcgrpn", bc, xd * decay_to_end[..., None])
    chunk_decay = jnp.exp(a_cs[:, :, -1])

    def step(carry, inp):
        st, dec = inp
        return carry * dec[..., None, None] + st, carry

    init = jnp.zeros((bsz, SSD_GROUPS, r, p, SSD_STATE), dtype=states.dtype)
    _, prev = lax.scan(step, init, (jnp.moveaxis(states, 1, 0), jnp.moveaxis(chunk_decay, 1, 0)))
    prev = jnp.moveaxis(prev, 0, 1)
    y_off = jnp.einsum("bclgn,bcgrpn->bclgrp", cc, prev) * jnp.exp(a_cs)[..., None]
    return (y_diag + y_off).reshape(bsz, s, h, p).astype(x.dtype)


def even_mixer(h, w_in, conv_w, conv_b, dt_bias, a_log, d_skip, ssd_norm_w,
               conf_dw_w, conf_dw_b, conf_ln_w, conf_ln_b, w_out):
    bsz, s, _ = h.shape
    proj = h @ w_in
    i0 = D_SSD
    i1 = i0 + SSD_CONV_DIM
    i2 = i1 + SSD_HEADS
    i3 = i2 + D_CONF
    z, xbc, dt, conf_a, conf_g = jnp.split(proj, [i0, i1, i2, i3], axis=-1)
    xbc = jax.nn.silu(causal_dwconv(xbc, conv_w, conv_b))
    xs, bm, cm = jnp.split(xbc, [D_SSD, D_SSD + SSD_GROUPS * SSD_STATE], axis=-1)
    dt = jax.nn.softplus(dt.astype(jnp.float32) + dt_bias)
    a = -jnp.exp(a_log.astype(jnp.float32))
    xh = xs.reshape(bsz, s, SSD_HEADS, SSD_HEAD_DIM)
    y = ssd_chunked(xh, dt, a,
                    bm.reshape(bsz, s, SSD_GROUPS, SSD_STATE),
                    cm.reshape(bsz, s, SSD_GROUPS, SSD_STATE))
    y = y + xh * d_skip[:, None]
    yg = (y.reshape(bsz, s, D_SSD) * jax.nn.silu(z)).reshape(bsz, s, SSD_GROUPS, D_SSD // SSD_GROUPS)
    y_ssd = rms_norm(yg, 1.0).reshape(bsz, s, D_SSD) * ssd_norm_w
    u = conf_a * jax.nn.sigmoid(conf_g)
    u = causal_dwconv(u, conf_dw_w, conf_dw_b)
    u = jax.nn.silu(layer_norm(u, conf_ln_w, conf_ln_b))
    return jnp.concatenate([y_ssd, u], axis=-1) @ w_out


def t5_bucket(dist):
    max_exact = REL_BUCKETS // 2
    d_f = jnp.maximum(dist, 1).astype(jnp.float32)
    large = max_exact + (jnp.log(d_f / max_exact) / math.log(REL_MAX_DIST / max_exact)
                         * (REL_BUCKETS - max_exact)).astype(jnp.int32)
    large = jnp.minimum(large, REL_BUCKETS - 1)
    return jnp.where(dist < max_exact, dist, large)


def dilated_branch(q, k, v, rel_bias, window, dilation):
    bsz, s, h, hd = q.shape
    length = s // dilation
    n_back = window // dilation
    qb = math.gcd(length, ATTN_BLOCK)
    kb = qb + n_back
    nblk = length // qb

    def to_sub(t):
        return t.reshape(bsz, length, dilation, h, hd).transpose(0, 2, 3, 1, 4)

    qs = to_sub(q)
    pad = [(0, 0), (0, 0), (0, 0), (n_back, 0), (0, 0)]
    ks = jnp.pad(to_sub(k), pad)
    vs = jnp.pad(to_sub(v), pad)
    i = jnp.arange(qb)[:, None]
    j = jnp.arange(kb)[None, :]
    steps = i + n_back - j
    band = (steps >= 0) & (steps <= n_back)
    bias = rel_bias[t5_bucket(jnp.clip(steps, 0, n_back) * dilation)]
    bias = jnp.transpose(bias, (2, 0, 1)).astype(jnp.float32)

    def block(blk):
        m0 = blk * qb
        qblk = lax.dynamic_slice_in_dim(qs, m0, qb, axis=3)
        kblk = lax.dynamic_slice_in_dim(ks, m0, kb, axis=3)
        vblk = lax.dynamic_slice_in_dim(vs, m0, kb, axis=3)
        logits = jnp.einsum("brhqd,brhkd->brhqk", qblk, kblk).astype(jnp.float32) + bias
        valid = band & (m0 - n_back + j >= 0)
        logits = jnp.where(valid, logits, -jnp.inf)
        mx = jnp.max(logits, axis=-1, keepdims=True)
        p = jnp.exp(logits - mx)
        den = jnp.sum(p, axis=-1, keepdims=True)
        o = jnp.einsum("brhqk,brhkd->brhqd", (p / den).astype(v.dtype), vblk)
        return o, (mx + jnp.log(den))[..., 0]

    o, lse = lax.map(block, jnp.arange(nblk))
    o = o.transpose(1, 2, 3, 0, 4, 5).reshape(bsz, dilation, h, length, hd)
    o = o.transpose(0, 3, 1, 2, 4).reshape(bsz, s, h, hd)
    lse = lse.transpose(1, 2, 3, 0, 4).reshape(bsz, dilation, h, length)
    lse = lse.transpose(0, 3, 1, 2).reshape(bsz, s, h)
    return o, lse


def odd_mixer(h, w_qkv, w_out, rel_bias):
    bsz, s, _ = h.shape
    qkv = (h @ w_qkv).reshape(bsz, s, 3, ATTN_HEADS, ATTN_HEAD_DIM)
    q = qkv[:, :, 0] * (ATTN_HEAD_DIM ** -0.5)
    k = qkv[:, :, 1]
    v = qkv[:, :, 2]
    outs, lses = [], []
    for window, dilation in DILATED_PATTERNS:
        o, lse = dilated_branch(q, k, v, rel_bias, window, dilation)
        outs.append(o)
        lses.append(lse)
    wts = jax.nn.softmax(jnp.stack(lses, axis=0), axis=0)
    o = outs[0] * wts[0][..., None].astype(q.dtype)
    for g in range(1, len(outs)):
        o = o + outs[g] * wts[g][..., None].astype(q.dtype)
    return o.reshape(bsz, s, ATTN_HEADS * ATTN_HEAD_DIM) @ w_out


def moe_swiglu(h, w_router, w1, w3, w2):
    bsz, s, d = h.shape
    t = h.reshape(bsz * s, d)
    logits = (t @ w_router).astype(jnp.float32)
    top_v, top_i = lax.top_k(logits, TOP_K)
    gk = jax.nn.softmax(top_v, axis=-1)
    gates = jnp.sum(jax.nn.one_hot(top_i, N_EXPERTS, dtype=jnp.float32) * gk[..., None], axis=1)
    out = jnp.zeros_like(t)
    for e in range(N_EXPERTS):
        out = out + gates[:, e:e + 1].astype(t.dtype) * swiglu(t, w1[e], w3[e], w2[e])
    return out.reshape(bsz, s, d)


def setup_inputs(seed: int = 0) -> dict:
    key = jax.random.key(seed)
    ks = iter(jax.random.split(key, 40))
    f32 = jnp.float32

    def nrm(shape, fan_in):
        return jax.random.normal(next(ks), shape, f32) * (fan_in ** -0.5)

    def gain(shape):
        return 1.0 + 0.02 * jax.random.normal(next(ks), shape, f32)

    def small(shape, scale=0.02):
        return scale * jax.random.normal(next(ks), shape, f32)

    dt0 = jnp.exp(jax.random.uniform(next(ks), (N_EVEN, SSD_HEADS), f32,
                                     math.log(1e-3), math.log(1e-1)))
    dt_bias = dt0 + jnp.log(-jnp.expm1(-dt0))
    a_log = jnp.log(jax.random.uniform(next(ks), (N_EVEN, SSD_HEADS), f32, 1.0, 16.0))
    return {
        "x": jax.random.normal(next(ks), (BATCH, SEQ, D_MODEL), f32),
        "norm_mix": gain((DEPTH, D_MODEL)),
        "norm_ffn": gain((DEPTH, D_MODEL)),
        "norm_final": gain((D_MODEL,)),
        "even_w_in": nrm((N_EVEN, D_MODEL, IN_EVEN), D_MODEL),
        "ssd_conv_w": nrm((N_EVEN, SSD_CONV, SSD_CONV_DIM), SSD_CONV),
        "ssd_conv_b": small((N_EVEN, SSD_CONV_DIM)),
        "ssd_dt_bias": dt_bias,
        "ssd_a_log": a_log,
        "ssd_d": gain((N_EVEN, SSD_HEADS)),
        "ssd_norm_w": gain((N_EVEN, D_SSD)),
        "conf_dw_w": nrm((N_EVEN, CONF_KERNEL, D_CONF), CONF_KERNEL),
        "conf_dw_b": small((N_EVEN, D_CONF)),
        "conf_ln_w": gain((N_EVEN, D_CONF)),
        "conf_ln_b": small((N_EVEN, D_CONF)),
        "even_w_out": nrm((N_EVEN, D_SSD + D_CONF, D_MODEL), D_SSD + D_CONF),
        "ffn_w1": nrm((N_EVEN, D_MODEL, D_FF), D_MODEL),
        "ffn_w3": nrm((N_EVEN, D_MODEL, D_FF), D_MODEL),
        "ffn_w2": nrm((N_EVEN, D_FF, D_MODEL), D_FF),
        "attn_w_qkv": nrm((N_ODD, D_MODEL, 3 * ATTN_HEADS * ATTN_HEAD_DIM), D_MODEL),
        "attn_w_out": nrm((N_ODD, ATTN_HEADS * ATTN_HEAD_DIM, D_MODEL), ATTN_HEADS * ATTN_HEAD_DIM),
        "rel_bias": small((REL_BUCKETS, ATTN_HEADS), 0.2),
        "moe_router": nrm((N_ODD, D_MODEL, N_EXPERTS), D_MODEL),
        "moe_w1": nrm((N_ODD, N_EXPERTS, D_MODEL, D_FF), D_MODEL),
        "moe_w3": nrm((N_ODD, N_EXPERTS, D_MODEL, D_FF), D_MODEL),
        "moe_w2": nrm((N_ODD, N_EXPERTS, D_FF, D_MODEL), D_FF),
    }


def reference(x, norm_mix, norm_ffn, norm_final, even_w_in, ssd_conv_w, ssd_conv_b,
              ssd_dt_bias, ssd_a_log, ssd_d, ssd_norm_w, conf_dw_w, conf_dw_b,
              conf_ln_w, conf_ln_b, even_w_out, ffn_w1, ffn_w3, ffn_w2,
              attn_w_qkv, attn_w_out, rel_bias, moe_router, moe_w1, moe_w3, moe_w2):
    for layer in range(DEPTH):
        h = rms_norm(x, norm_mix[layer])
        if layer % 2 == 0:
            e = layer // 2
            x = x + even_mixer(h, even_w_in[e], ssd_conv_w[e], ssd_conv_b[e], ssd_dt_bias[e],
                               ssd_a_log[e], ssd_d[e], ssd_norm_w[e], conf_dw_w[e], conf_dw_b[e],
                               conf_ln_w[e], conf_ln_b[e], even_w_out[e])
            x = x + swiglu(rms_norm(x, norm_ffn[layer]), ffn_w1[e], ffn_w3[e], ffn_w2[e])
        else:
            o = layer // 2
            x = x + odd_mixer(h, attn_w_qkv[o], attn_w_out[o], rel_bias)
            x = x + moe_swiglu(rms_norm(x, norm_ffn[layer]), moe_router[o], moe_w1[o],
                               moe_w3[o], moe_w2[o])
    return rms_norm(x, norm_final)
```

```python
import functools
import math

import numpy as np
import jax
import jax.numpy as jnp
from jax import lax
from jax.experimental import pallas as pl
from jax.experimental.pallas import tpu as pltpu

SSD_HEAD_DIM = 64
SSD_GROUPS = 4
SSD_STATE = 128
SSD_CONV = 4
SSD_CHUNK = 128
CONF_KERNEL = 31
ATTN_HEADS = 16
DILATED_PATTERNS = ((128, 1), (512, 4), (2048, 16))
ATTN_BLOCK = 128
REL_BUCKETS = 32
REL_MAX_DIST = 2048
N_EXPERTS = 8
TOP_K = 2
EPS = 1e-6

V7X_LANES = 128
V7X_VMEM_BYTES = 64 * 1024 * 1024
VMEM_LIMIT = 56 * 1024 * 1024

F32 = jnp.float32
BF16 = jnp.bfloat16
NEG_INF = float("-inf")


def _cparams(*sem):
    return pltpu.CompilerParams(dimension_semantics=tuple(sem), vmem_limit_bytes=VMEM_LIMIT)


def _rms(x, g):
    ms = jnp.mean(x * x, axis=-1, keepdims=True)
    return x * lax.rsqrt(ms + EPS) * g


def _rmsnorm_kernel(x_ref, g_ref, h_ref):
    h_ref[...] = _rms(x_ref[...], g_ref[...]).astype(h_ref.dtype)


def rmsnorm(x, g, out_dtype=BF16, tm=512):
    t, d = x.shape
    return pl.pallas_call(
        _rmsnorm_kernel,
        out_shape=jax.ShapeDtypeStruct((t, d), out_dtype),
        grid=(t // tm,),
        in_specs=[pl.BlockSpec((tm, d), lambda i: (i, 0)), pl.BlockSpec((1, d), lambda i: (0, 0))],
        out_specs=pl.BlockSpec((tm, d), lambda i: (i, 0)),
        compiler_params=_cparams("parallel"),
        name="rmsnorm",
    )(x, g.reshape(1, d))


def _add_norm_kernel(x_ref, y_ref, g_ref, xo_ref, h_ref):
    xn = x_ref[...] + y_ref[...].astype(F32)
    xo_ref[...] = xn
    h_ref[...] = _rms(xn, g_ref[...]).astype(h_ref.dtype)


def add_norm(x, y, g, tm=512):
    t, d = x.shape
    return pl.pallas_call(
        _add_norm_kernel,
        out_shape=(jax.ShapeDtypeStruct((t, d), F32), jax.ShapeDtypeStruct((t, d), BF16)),
        grid=(t // tm,),
        in_specs=[pl.BlockSpec((tm, d), lambda i: (i, 0)), pl.BlockSpec((tm, d), lambda i: (i, 0)),
                  pl.BlockSpec((1, d), lambda i: (0, 0))],
        out_specs=(pl.BlockSpec((tm, d), lambda i: (i, 0)), pl.BlockSpec((tm, d), lambda i: (i, 0))),
        compiler_params=_cparams("parallel"),
        name="add_norm",
    )(x, y, g.reshape(1, d))


def _mm_kernel(*refs, n_lhs, ks, has_res):
    xs = refs[:n_lhs]
    w_ref = refs[n_lhs]
    r_ref = refs[n_lhs + 1] if has_res else None
    o_ref = refs[-1]
    acc = None
    k0 = 0
    for x_ref, k in zip(xs, ks):
        part = jnp.dot(x_ref[...], w_ref[k0:k0 + k, :].astype(BF16), preferred_element_type=F32)
        acc = part if acc is None else acc + part
        k0 += k
    if has_res:
        acc = acc + r_ref[...]
    o_ref[...] = acc.astype(o_ref.dtype)


def matmul(xs, w, *, n_cols, col_block_off=0, tn, tm=1024, res=None, out_dtype=BF16, name="matmul"):
    t = xs[0].shape[0]
    ks = tuple(x.shape[1] for x in xs)
    ktot = sum(ks)
    assert w.shape[0] == ktot and n_cols % tn == 0 and t % tm == 0
    in_specs = [pl.BlockSpec((tm, k), lambda i, j: (i, 0)) for k in ks]
    in_specs.append(pl.BlockSpec((ktot, tn), lambda i, j: (0, j + col_block_off)))
    args = list(xs) + [w]
    if res is not None:
        in_specs.append(pl.BlockSpec((tm, tn), lambda i, j: (i, j)))
        args.append(res)
    return pl.pallas_call(
        functools.partial(_mm_kernel, n_lhs=len(xs), ks=ks, has_res=res is not None),
        out_shape=jax.ShapeDtypeStruct((t, n_cols), out_dtype),
        grid=(t // tm, n_cols // tn),
        in_specs=in_specs,
        out_specs=pl.BlockSpec((tm, tn), lambda i, j: (i, j)),
        compiler_params=_cparams("parallel", "arbitrary"),
        name=name,
    )(*args)


HALO_BF16 = 16


def _ssd_conv_kernel(cur_ref, halo_ref, w_ref, b_ref, o_ref, buf_ref, *, ts):
    i = pl.program_id(1)
    halo = halo_ref[...].astype(F32)
    buf_ref[0:HALO_BF16, :] = jnp.where(i > 0, halo, 0.0)
    buf_ref[HALO_BF16:HALO_BF16 + ts, :] = cur_ref[...].astype(F32)
    acc = b_ref[...]
    for k in range(SSD_CONV):
        off = HALO_BF16 - (SSD_CONV - 1) + k
        acc = acc + w_ref[k:k + 1, :] * buf_ref[off:off + ts, :]
    o_ref[...] = (acc * jax.nn.sigmoid(acc)).astype(o_ref.dtype)


def ssd_conv(zx, conv_w, conv_b, *, bsz, seq, col_off, ts=512, tc=512):
    c = conv_w.shape[1]
    t = zx.shape[0]
    nsb = seq // ts
    cb0 = col_off // tc
    hb = ts // HALO_BF16
    return pl.pallas_call(
        functools.partial(_ssd_conv_kernel, ts=ts),
        out_shape=jax.ShapeDtypeStruct((t, c), BF16),
        grid=(bsz, nsb, c // tc),
        in_specs=[
            pl.BlockSpec((ts, tc), lambda b, i, j: (b * nsb + i, cb0 + j)),
            pl.BlockSpec((HALO_BF16, tc), lambda b, i, j: (jnp.maximum((b * nsb + i) * hb - 1, 0), cb0 + j)),
            pl.BlockSpec((SSD_CONV, tc), lambda b, i, j: (0, j)),
            pl.BlockSpec((1, tc), lambda b, i, j: (0, j)),
        ],
        out_specs=pl.BlockSpec((ts, tc), lambda b, i, j: (b * nsb + i, j)),
        scratch_shapes=[pltpu.VMEM((ts + HALO_BF16, tc), F32)],
        compiler_params=_cparams("parallel", "parallel", "parallel"),
        name="ssd_conv",
    )(zx, zx, conv_w, conv_b.reshape(1, c))


def _ssd_prep_kernel(raw_ref, bias_ref, alog_ref, dt_ref, acs_ref, dtt_ref, acst_ref, *, seq):
    raw = raw_ref[...] + bias_ref[...]
    dt = jnp.maximum(raw, 0.0) + jnp.log1p(jnp.exp(-jnp.abs(raw)))
    a = -jnp.exp(alog_ref[...])
    x = dt * a
    row = lax.broadcasted_iota(jnp.int32, x.shape, 0) % SSD_CHUNK
    sh = 1
    while sh < SSD_CHUNK:
        x = x + jnp.where(row >= sh, pltpu.roll(x, sh, 0), 0.0)
        sh *= 2
    dt_ref[...] = dt
    acs_ref[...] = x
    for c in range(seq // SSD_CHUNK):
        sl = slice(c * SSD_CHUNK, (c + 1) * SSD_CHUNK)
        dtt_ref[sl, :] = dt[sl, :].T
        acst_ref[sl, :] = x[sl, :].T


def ssd_prep(dt_raw, dt_bias_p, a_log_p, *, bsz, seq):
    t = dt_raw.shape[0]
    shp = jax.ShapeDtypeStruct((t, V7X_LANES), F32)
    blk = pl.BlockSpec((seq, V7X_LANES), lambda b: (b, 0))
    vec = pl.BlockSpec((1, V7X_LANES), lambda b: (0, 0))
    return pl.pallas_call(
        functools.partial(_ssd_prep_kernel, seq=seq),
        out_shape=(shp, shp, shp, shp),
        grid=(bsz,),
        in_specs=[blk, vec, vec],
        out_specs=(blk, blk, blk, blk),
        compiler_params=_cparams("parallel"),
        name="ssd_prep",
    )(dt_raw, dt_bias_p, a_log_p)


def _ssd_main_kernel(x_ref, b_ref, c_ref, z_ref, dt_ref, acs_ref, dtt_ref, acst_ref, dsk_ref, nw_ref,
                     y_ref, st_ref, *, heads_per_group):
    ci = pl.program_id(1)
    L = SSD_CHUNK
    gw = heads_per_group * SSD_HEAD_DIM
    npair = heads_per_group // 2

    @pl.when(ci == 0)
    def _():
        st_ref[...] = jnp.zeros_like(st_ref)

    dt = dt_ref[...]
    acs = acs_ref[...]
    dtt = dtt_ref[...]
    acst = acst_ref[...]
    row = lax.broadcasted_iota(jnp.int32, (L, L), 0)
    col = lax.broadcasted_iota(jnp.int32, (L, L), 1)
    causal = col <= row
    lane_lo = lax.broadcasted_iota(jnp.int32, (L, 2 * SSD_HEAD_DIM), 1) < SSD_HEAD_DIM
    lane_lo1 = lax.broadcasted_iota(jnp.int32, (1, 2 * SSD_HEAD_DIM), 1) < SSD_HEAD_DIM
    zero_b = jnp.zeros((L, 2 * SSD_HEAD_DIM), BF16)

    for g in range(SSD_GROUPS):
        bm = b_ref[:, g * SSD_STATE:(g + 1) * SSD_STATE]
        cm = c_ref[:, g * SSD_STATE:(g + 1) * SSD_STATE]
        bmt = bm.astype(F32).T.astype(BF16)
        cb = lax.dot_general(cm, bm, (((1,), (1,)), ((), ())), preferred_element_type=F32)
        ys = []
        for q in range(npair):
            c0 = g * gw + q * 2 * SSD_HEAD_DIM
            xp = x_ref[:, c0:c0 + 2 * SSD_HEAD_DIM]
            xpf = xp.astype(F32)
            ms = []
            for hh in range(2):
                h = g * heads_per_group + 2 * q + hh
                seg = acs[:, h:h + 1] - acst[h:h + 1, :]
                dec = jnp.exp(jnp.where(causal, seg, NEG_INF))
                ms.append((cb * dec * dtt[h:h + 1, :]).astype(BF16))
            h0 = g * heads_per_group + 2 * q
            h1 = h0 + 1
            lhs = jnp.concatenate(ms, axis=1)
            rhs = jnp.concatenate([jnp.where(lane_lo, xp, zero_b), jnp.where(lane_lo, zero_b, xp)], axis=0)
            y_diag = jnp.dot(lhs, rhs, preferred_element_type=F32)
            st = st_ref[g, :, q * 2 * SSD_HEAD_DIM:(q + 1) * 2 * SSD_HEAD_DIM]
            e_l = jnp.where(lane_lo, jnp.exp(acs[:, h0:h0 + 1]), jnp.exp(acs[:, h1:h1 + 1]))
            y_off = jnp.dot(cm, st.astype(BF16), preferred_element_type=F32) * e_l
            dsk = dsk_ref[:, c0:c0 + 2 * SSD_HEAD_DIM]
            ys.append(y_diag + y_off + xpf * dsk)
            last0 = acs[L - 1:L, h0:h0 + 1]
            last1 = acs[L - 1:L, h1:h1 + 1]
            w_l = jnp.where(lane_lo, dt[:, h0:h0 + 1] * jnp.exp(last0 - acs[:, h0:h0 + 1]),
                            dt[:, h1:h1 + 1] * jnp.exp(last1 - acs[:, h1:h1 + 1]))
            xw = (xpf * w_l).astype(BF16)
            st_new = jnp.dot(bmt, xw, preferred_element_type=F32)
            cd = jnp.where(lane_lo1, jnp.exp(last0), jnp.exp(last1))
            st_ref[g, :, q * 2 * SSD_HEAD_DIM:(q + 1) * 2 * SSD_HEAD_DIM] = st * cd + st_new
        y = jnp.concatenate(ys, axis=1)
        zg = z_ref[:, g * gw:(g + 1) * gw].astype(F32)
        yg = y * (zg * jax.nn.sigmoid(zg))
        ms_ = jnp.mean(yg * yg, axis=-1, keepdims=True)
        y_ref[:, g * gw:(g + 1) * gw] = (yg * lax.rsqrt(ms_ + EPS) * nw_ref[:, g * gw:(g + 1) * gw]).astype(y_ref.dtype)


def ssd_main(xc, zx, dt, acs, dtt, acst, dskip_e, norm_w, *, bsz, seq, d_ssd):
    t = xc.shape[0]
    nc = seq // SSD_CHUNK
    heads = d_ssd // SSD_HEAD_DIM
    hpg = heads // SSD_GROUPS
    bc_w = SSD_GROUPS * SSD_STATE
    assert d_ssd % bc_w == 0 and hpg % 2 == 0
    rowmap = lambda b, c: (b * nc + c, 0)
    return pl.pallas_call(
        functools.partial(_ssd_main_kernel, heads_per_group=hpg),
        out_shape=jax.ShapeDtypeStruct((t, d_ssd), BF16),
        grid=(bsz, nc),
        in_specs=[
            pl.BlockSpec((SSD_CHUNK, d_ssd), rowmap),
            pl.BlockSpec((SSD_CHUNK, bc_w), lambda b, c: (b * nc + c, d_ssd // bc_w)),
            pl.BlockSpec((SSD_CHUNK, bc_w), lambda b, c: (b * nc + c, d_ssd // bc_w + 1)),
            pl.BlockSpec((SSD_CHUNK, d_ssd), rowmap),
            pl.BlockSpec((SSD_CHUNK, V7X_LANES), rowmap),
            pl.BlockSpec((SSD_CHUNK, V7X_LANES), rowmap),
            pl.BlockSpec((SSD_CHUNK, V7X_LANES), rowmap),
            pl.BlockSpec((SSD_CHUNK, V7X_LANES), rowmap),
            pl.BlockSpec((1, d_ssd), lambda b, c: (0, 0)),
            pl.BlockSpec((1, d_ssd), lambda b, c: (0, 0)),
        ],
        out_specs=pl.BlockSpec((SSD_CHUNK, d_ssd), rowmap),
        scratch_shapes=[pltpu.VMEM((SSD_GROUPS, SSD_STATE, hpg * SSD_HEAD_DIM), F32)],
        compiler_params=_cparams("parallel", "arbitrary"),
        name="ssd_main",
    )(xc, xc, xc, zx, dt, acs, dtt, acst, dskip_e, norm_w.reshape(1, d_ssd))


CONF_HALO = 32


def _conf_kernel(a_ref, g_ref, ah_ref, gh_ref, w_ref, b_ref, lw_ref, lb_ref, o_ref, buf_ref, *, ts, rc, lt):
    i = pl.program_id(1)
    d = a_ref.shape[1]
    ah = ah_ref[...].astype(F32)
    gh = gh_ref[...].astype(F32)
    buf_ref[0:CONF_HALO, :] = jnp.where(i > 0, ah * jax.nn.sigmoid(gh), 0.0)
    a = a_ref[...].astype(F32)
    g = g_ref[...].astype(F32)
    buf_ref[CONF_HALO:CONF_HALO + ts, :] = a * jax.nn.sigmoid(g)
    base = CONF_HALO - (CONF_KERNEL - 1)
    outs = []
    for r in range(ts // rc):
        cols = []
        for c in range(d // lt):
            acc = jnp.broadcast_to(b_ref[:, c * lt:(c + 1) * lt], (rc, lt))
            for k in range(CONF_KERNEL):
                off = base + r * rc + k
                acc = acc + w_ref[k:k + 1, c * lt:(c + 1) * lt] * buf_ref[off:off + rc, c * lt:(c + 1) * lt]
            cols.append(acc)
        outs.append(jnp.concatenate(cols, axis=1))
    u = jnp.concatenate(outs, axis=0)
    mu = jnp.mean(u, axis=-1, keepdims=True)
    uc = u - mu
    var = jnp.mean(uc * uc, axis=-1, keepdims=True)
    y = uc * lax.rsqrt(var + EPS) * lw_ref[...] + lb_ref[...]
    o_ref[...] = (y * jax.nn.sigmoid(y)).astype(o_ref.dtype)


def conf_module(conf, dw_w, dw_b, ln_w, ln_b, *, bsz, seq, ts=128, rc=32, lt=512):
    t = conf.shape[0]
    d = dw_w.shape[1]
    nsb = seq // ts
    hb = ts // CONF_HALO
    cur = lambda col: pl.BlockSpec((ts, d), lambda b, i: (b * nsb + i, col))
    halo = lambda col: pl.BlockSpec((CONF_HALO, d), lambda b, i: (jnp.maximum((b * nsb + i) * hb - 1, 0), col))
    vec = pl.BlockSpec((1, d), lambda b, i: (0, 0))
    return pl.pallas_call(
        functools.partial(_conf_kernel, ts=ts, rc=rc, lt=lt),
        out_shape=jax.ShapeDtypeStruct((t, d), BF16),
        grid=(bsz, nsb),
        in_specs=[cur(0), cur(1), halo(0), halo(1), pl.BlockSpec((CONF_KERNEL, d), lambda b, i: (0, 0)),
                  vec, vec, vec],
        out_specs=pl.BlockSpec((ts, d), lambda b, i: (b * nsb + i, 0)),
        scratch_shapes=[pltpu.VMEM((ts + CONF_HALO, d), F32)],
        compiler_params=_cparams("parallel", "parallel"),
        name="conf_module",
    )(conf, conf, conf, conf, dw_w, dw_b.reshape(1, d), ln_w.reshape(1, d), ln_b.reshape(1, d))


FFN_SUB = 256


def _ffn_kernel(ge_ref, gb_ref, gr_ref, x_ref, w1_ref, w3_ref, w2_ref, o_ref, acc_ref, *, nf):
    g = pl.program_id(0)
    f = pl.program_id(1)
    rows = gr_ref[g]
    valid = rows > 0

    @pl.when(valid & (f == 0))
    def _():
        acc_ref[...] = jnp.zeros_like(acc_ref)

    @pl.when(valid)
    def _():
        w1 = w1_ref[...].astype(BF16)
        w3 = w3_ref[...].astype(BF16)
        w2 = w2_ref[...].astype(BF16)
        nsub = (rows + FFN_SUB - 1) // FFN_SUB

        def body(s, carry):
            r0 = pl.multiple_of(s * FFN_SUB, FFN_SUB)
            xs = x_ref[pl.ds(r0, FFN_SUB), :]
            h1 = jnp.dot(xs, w1, preferred_element_type=F32)
            h3 = jnp.dot(xs, w3, preferred_element_type=F32)
            hh = (h1 * jax.nn.sigmoid(h1) * h3).astype(BF16)
            acc_ref[pl.ds(r0, FFN_SUB), :] += jnp.dot(hh, w2, preferred_element_type=F32)
            return carry

        lax.fori_loop(0, nsub, body, 0)

    @pl.when(valid & (f == nf - 1))
    def _():
        o_ref[...] = acc_ref[...].astype(o_ref.dtype)


def ffn_groups(xs, w1, w3, w2, grp_expert, grp_block, grp_rows, *, gsz=1024, tf=256):
    r, d = xs.shape
    e, _, ff = w1.shape
    nf = ff // tf
    ng = grp_expert.shape[0]

    def fmap(f, g, gr):
        return jnp.where(gr[g] > 0, f, nf - 1)

    grid_spec = pltpu.PrefetchScalarGridSpec(
        num_scalar_prefetch=3,
        grid=(ng, nf),
        in_specs=[
            pl.BlockSpec((gsz, d), lambda g, f, ge, gb, gr: (gb[g], 0)),
            pl.BlockSpec((None, d, tf), lambda g, f, ge, gb, gr: (ge[g], 0, fmap(f, g, gr))),
            pl.BlockSpec((None, d, tf), lambda g, f, ge, gb, gr: (ge[g], 0, fmap(f, g, gr))),
            pl.BlockSpec((None, tf, d), lambda g, f, ge, gb, gr: (ge[g], fmap(f, g, gr), 0)),
        ],
        out_specs=pl.BlockSpec((gsz, d), lambda g, f, ge, gb, gr: (gb[g], 0)),
        scratch_shapes=[pltpu.VMEM((gsz, d), F32)],
    )
    return pl.pallas_call(
        functools.partial(_ffn_kernel, nf=nf),
        out_shape=jax.ShapeDtypeStruct((r, d), BF16),
        grid_spec=grid_spec,
        compiler_params=_cparams("arbitrary", "arbitrary"),
        name="ffn_groups",
    )(grp_expert, grp_block, grp_rows, xs, w1, w3, w2)


def _t5_bucket_np(dist):
    max_exact = REL_BUCKETS // 2
    d_f = np.maximum(dist, 1).astype(np.float32)
    large = max_exact + (np.log(d_f / np.float32(max_exact)) / np.float32(math.log(REL_MAX_DIST / max_exact))
                         * np.float32(REL_BUCKETS - max_exact)).astype(np.int32)
    large = np.minimum(large, REL_BUCKETS - 1)
    return np.where(dist < max_exact, dist, large)


def _attn_branch_kernel(q_ref, kc_ref, kp_ref, vc_ref, vp_ref, bias_ref, o_ref, lse_ref, *, hd, use_prev, scale):
    i = pl.program_id(2)
    tq = q_ref.shape[0]
    lane = lax.broadcasted_iota(jnp.int32, (tq, V7X_LANES), 1)
    lse_tile = jnp.zeros((tq, V7X_LANES), F32)
    nt = (((1,), (1,)), ((), ()))
    for h in range(ATTN_HEADS):
        sl = slice(h * hd, (h + 1) * hd)
        q = q_ref[:, sl]
        s_c = lax.dot_general(q, kc_ref[:, sl], nt, preferred_element_type=F32) * scale + bias_ref[h, :, tq:2 * tq]
        m = jnp.max(s_c, axis=-1, keepdims=True)
        if use_prev:
            s_p = lax.dot_general(q, kp_ref[:, sl], nt, preferred_element_type=F32) * scale + bias_ref[h, :, 0:tq]
            s_p = jnp.where(i > 0, s_p, NEG_INF)
            m = jnp.maximum(m, jnp.max(s_p, axis=-1, keepdims=True))
        p_c = jnp.exp(s_c - m)
        den = jnp.sum(p_c, axis=-1, keepdims=True)
        acc = jnp.dot(p_c.astype(BF16), vc_ref[:, sl], preferred_element_type=F32)
        if use_prev:
            p_p = jnp.exp(s_p - m)
            den = den + jnp.sum(p_p, axis=-1, keepdims=True)
            acc = acc + jnp.dot(p_p.astype(BF16), vp_ref[:, sl], preferred_element_type=F32)
        o_ref[:, sl] = (acc / den).astype(o_ref.dtype)
        lse_tile = jnp.where(lane == h, m + jnp.log(den), lse_tile)
    lse_ref[...] = lse_tile


def attn_branch(qkv, bias, *, bsz, seq, dil, hd):
    dm = ATTN_HEADS * hd
    length = seq // dil
    tq = math.gcd(length, ATTN_BLOCK)
    nblk = length // tq
    use_prev = nblk > 1
    view = qkv.reshape(bsz * length, dil * 3 * dm)
    cur = lambda which: pl.BlockSpec((tq, dm), lambda b, r, i: (b * nblk + i, r * 3 + which))
    prev = lambda which: pl.BlockSpec((tq, dm), lambda b, r, i: (b * nblk + jnp.maximum(i - 1, 0), r * 3 + which))
    o, lse = pl.pallas_call(
        functools.partial(_attn_branch_kernel, hd=hd, use_prev=use_prev, scale=hd ** -0.5),
        out_shape=(jax.ShapeDtypeStruct((bsz * length, dil * dm), BF16),
                   jax.ShapeDtypeStruct((bsz * length, dil * V7X_LANES), F32)),
        grid=(bsz, dil, nblk),
        in_specs=[cur(0), cur(1), prev(1), cur(2), prev(2),
                  pl.BlockSpec((ATTN_HEADS, tq, 2 * tq), lambda b, r, i: (0, 0, 0))],
        out_specs=(pl.BlockSpec((tq, dm), lambda b, r, i: (b * nblk + i, r)),
                   pl.BlockSpec((tq, V7X_LANES), lambda b, r, i: (b * nblk + i, r))),
        compiler_params=_cparams("parallel", "parallel", "arbitrary"),
        name=f"attn_branch_d{dil}",
    )(view, view, view, view, view, bias)
    return o.reshape(bsz * seq, dm), lse.reshape(bsz * seq, V7X_LANES)


def _attn_combine_kernel(*refs, nb, hd):
    o_refs = refs[:nb]
    l_refs = refs[nb:2 * nb]
    out_ref = refs[2 * nb]
    ls = [r[...] for r in l_refs]
    m = ls[0]
    for l in ls[1:]:
        m = jnp.maximum(m, l)
    es = [jnp.exp(l - m) for l in ls]
    tot = es[0]
    for e in es[1:]:
        tot = tot + e
    ws = [e / tot for e in es]
    for h in range(ATTN_HEADS):
        sl = slice(h * hd, (h + 1) * hd)
        acc = None
        for b in range(nb):
            term = o_refs[b][:, sl].astype(F32) * ws[b][:, h:h + 1]
            acc = term if acc is None else acc + term
        out_ref[:, sl] = acc.astype(out_ref.dtype)


def attn_combine(os_, ls_, *, hd, tm=512):
    t, dm = os_[0].shape
    nb = len(os_)
    ob = pl.BlockSpec((tm, dm), lambda i: (i, 0))
    lb = pl.BlockSpec((tm, V7X_LANES), lambda i: (i, 0))
    return pl.pallas_call(
        functools.partial(_attn_combine_kernel, nb=nb, hd=hd),
        out_shape=jax.ShapeDtypeStruct((t, dm), BF16),
        grid=(t // tm,),
        in_specs=[ob] * nb + [lb] * nb,
        out_specs=ob,
        compiler_params=_cparams("parallel"),
        name="attn_combine",
    )(*os_, *ls_)


def attn_bias_tables(rel_bias, seq):
    tabs = []
    for window, dil in DILATED_PATTERNS:
        length = seq // dil
        n_back = window // dil
        tq = math.gcd(length, ATTN_BLOCK)
        assert n_back == tq, "kernel assumes the look-back equals one query block"
        i = np.arange(tq)[:, None]
        j = np.arange(2 * tq)[None, :]
        steps = i + n_back - j
        band = (steps >= 0) & (steps <= n_back)
        idx = _t5_bucket_np(np.clip(steps, 0, n_back) * dil)
        tab = jnp.transpose(rel_bias[idx], (2, 0, 1)).astype(F32)
        tabs.append(jnp.where(band[None], tab, NEG_INF))
    return tabs


def _split_bf16(x):
    hi = x.astype(BF16)
    lo = (x - hi.astype(F32)).astype(BF16)
    return hi, lo


def _moe_route_kernel(x_ref, g_ref, wr_ref, h_ref, meta_ref, cnt_ref, carry_ref):
    step = pl.program_id(0)
    tm = x_ref.shape[0]

    @pl.when(step == 0)
    def _():
        carry_ref[...] = jnp.zeros_like(carry_ref)

    h = _rms(x_ref[...], g_ref[...])
    h_hi, h_lo = _split_bf16(h)
    h_ref[...] = h_hi
    w_hi, w_lo = _split_bf16(wr_ref[...])
    logits = (jnp.dot(h_hi, w_hi, preferred_element_type=F32) + jnp.dot(h_lo, w_hi, preferred_element_type=F32)
              + jnp.dot(h_hi, w_lo, preferred_element_type=F32))
    lane = lax.broadcasted_iota(jnp.int32, (tm, V7X_LANES), 1).astype(F32)
    lg = jnp.where(lane < N_EXPERTS, logits, NEG_INF)
    m1 = jnp.max(lg, axis=-1, keepdims=True)
    i1 = jnp.min(jnp.where(lg == m1, lane, float(V7X_LANES)), axis=-1, keepdims=True)
    lg2 = jnp.where(lane == i1, NEG_INF, lg)
    m2 = jnp.max(lg2, axis=-1, keepdims=True)
    i2 = jnp.min(jnp.where(lg2 == m2, lane, float(V7X_LANES)), axis=-1, keepdims=True)
    e2 = jnp.exp(m2 - m1)
    g1 = 1.0 / (1.0 + e2)
    g2 = e2 / (1.0 + e2)
    oh = ((lane == i1) | (lane == i2)).astype(BF16)
    r_i = lax.broadcasted_iota(jnp.int32, (tm, tm), 0)
    c_i = lax.broadcasted_iota(jnp.int32, (tm, tm), 1)
    tri = (c_i < r_i).astype(BF16)
    rank = jnp.dot(tri, oh, preferred_element_type=F32) + carry_ref[0:1, :]
    r1 = jnp.sum(jnp.where(lane == i1, rank, 0.0), axis=-1, keepdims=True)
    r2 = jnp.sum(jnp.where(lane == i2, rank, 0.0), axis=-1, keepdims=True)
    carry_ref[0:1, :] = carry_ref[0:1, :] + jnp.sum(oh.astype(F32), axis=0, keepdims=True)
    meta = jnp.where(lane == 0, i1.astype(F32), 0.0)
    meta = jnp.where(lane == 1, i2.astype(F32), meta)
    meta = jnp.where(lane == 2, g1, meta)
    meta = jnp.where(lane == 3, g2, meta)
    meta = jnp.where(lane == 4, r1, meta)
    meta = jnp.where(lane == 5, r2, meta)
    meta_ref[...] = meta
    cnt_ref[...] = jnp.broadcast_to(carry_ref[0:1, :], cnt_ref.shape)


def moe_route(x, g, w_router_p, tm=256):
    t, d = x.shape
    return pl.pallas_call(
        _moe_route_kernel,
        out_shape=(jax.ShapeDtypeStruct((t, d), BF16), jax.ShapeDtypeStruct((t, V7X_LANES), F32),
                   jax.ShapeDtypeStruct((8, V7X_LANES), F32)),
        grid=(t // tm,),
        in_specs=[pl.BlockSpec((tm, d), lambda i: (i, 0)), pl.BlockSpec((1, d), lambda i: (0, 0)),
                  pl.BlockSpec((d, V7X_LANES), lambda i: (0, 0))],
        out_specs=(pl.BlockSpec((tm, d), lambda i: (i, 0)), pl.BlockSpec((tm, V7X_LANES), lambda i: (i, 0)),
                   pl.BlockSpec((8, V7X_LANES), lambda i: (0, 0))),
        scratch_shapes=[pltpu.VMEM((8, V7X_LANES), F32)],
        compiler_params=_cparams("arbitrary"),
        name="moe_route",
    )(x, g.reshape(1, d), w_router_p)


def _moe_final_kernel(x_ref, ya_ref, yb_ref, meta_ref, g_ref, o_ref):
    meta = meta_ref[...]
    xn = x_ref[...] + ya_ref[...].astype(F32) * meta[:, 2:3] + yb_ref[...].astype(F32) * meta[:, 3:4]
    o_ref[...] = _rms(xn, g_ref[...])


def moe_final(x, ya, yb, meta, g, tm=512):
    t, d = x.shape
    blk = pl.BlockSpec((tm, d), lambda i: (i, 0))
    return pl.pallas_call(
        _moe_final_kernel,
        out_shape=jax.ShapeDtypeStruct((t, d), F32),
        grid=(t // tm,),
        in_specs=[blk, blk, blk, pl.BlockSpec((tm, V7X_LANES), lambda i: (i, 0)), pl.BlockSpec((1, d), lambda i: (0, 0))],
        out_specs=blk,
        compiler_params=_cparams("parallel"),
        name="moe_final",
    )(x, ya, yb, meta, g.reshape(1, d))


def _pad_lanes(v, fill=0.0):
    v = v.reshape(1, -1).astype(F32)
    return jnp.pad(v, ((0, 0), (0, V7X_LANES - v.shape[1])), constant_values=fill)


def even_layer(x, h, p, *, bsz, seq):
    t, d = x.shape
    d_ssd = d
    d_conf = d
    heads = d_ssd // SSD_HEAD_DIM
    conv_dim = d_ssd + 2 * SSD_GROUPS * SSD_STATE
    i1 = d_ssd + conv_dim
    i2 = i1 + heads
    w_in = p["w_in"]
    zx = matmul([h], w_in, n_cols=i1, tn=1024, name="in_proj_zx")
    w_dt = jnp.pad(w_in[:, i1:i2], ((0, 0), (0, V7X_LANES - heads)))
    dt_raw = matmul([h], w_dt, n_cols=V7X_LANES, tn=V7X_LANES, out_dtype=F32, name="in_proj_dt")
    conf = matmul([h], w_in[:, i2:], n_cols=2 * d_conf, tn=1024, name="in_proj_conf")
    xc = ssd_conv(zx, p["conv_w"], p["conv_b"], bsz=bsz, seq=seq, col_off=d_ssd)
    dt, acs, dtt, acst = ssd_prep(dt_raw, _pad_lanes(p["dt_bias"]), _pad_lanes(p["a_log"]), bsz=bsz, seq=seq)
    dskip_e = jnp.repeat(p["d_skip"].astype(F32), SSD_HEAD_DIM).reshape(1, d_ssd)
    y_ssd = ssd_main(xc, zx, dt, acs, dtt, acst, dskip_e, p["ssd_norm_w"], bsz=bsz, seq=seq, d_ssd=d_ssd)
    u = conf_module(conf, p["conf_dw_w"], p["conf_dw_b"], p["conf_ln_w"], p["conf_ln_b"], bsz=bsz, seq=seq)
    return matmul([y_ssd, u], p["w_out"], n_cols=d, tn=512, res=x, out_dtype=F32, name="even_out_proj")


def dense_ffn(h, w1, w3, w2, gsz=1024):
    t = h.shape[0]
    ng = t // gsz
    ge = jnp.zeros((ng,), jnp.int32)
    gb = jnp.arange(ng, dtype=jnp.int32)
    gr = jnp.full((ng,), gsz, jnp.int32)
    return ffn_groups(h, w1[None], w3[None], w2[None], ge, gb, gr, gsz=gsz)


def odd_layer_attn(x, h, p, *, bsz, seq):
    t, d = x.shape
    hd = d // ATTN_HEADS
    qkv = matmul([h], p["w_qkv"], n_cols=3 * d, tn=1024, name="qkv_proj")
    tabs = attn_bias_tables(p["rel_bias"], seq)
    os_, ls_ = [], []
    for (window, dil), tab in zip(DILATED_PATTERNS, tabs):
        o, l = attn_branch(qkv, tab, bsz=bsz, seq=seq, dil=dil, hd=hd)
        os_.append(o)
        ls_.append(l)
    o = attn_combine(os_, ls_, hd=hd)
    return matmul([o], p["w_attn_out"], n_cols=d, tn=1024, res=x, out_dtype=F32, name="attn_out_proj")


def moe_layer(x, norm_g, w_router, w1, w3, w2, final_g, gsz=1024):
    t, d = x.shape
    ne = w1.shape[0]
    w_router_p = jnp.pad(w_router.astype(F32), ((0, 0), (0, V7X_LANES - ne)))
    h, meta, cnt = moe_route(x, norm_g, w_router_p)
    counts = cnt[0, :ne].astype(jnp.int32)
    ngrp_e = (counts + gsz - 1) // gsz
    grp_end = jnp.cumsum(ngrp_e)
    grp_start = grp_end - ngrp_e
    row_off = grp_start * gsz
    ng = (t * TOP_K) // gsz + ne
    j = jnp.arange(ng, dtype=jnp.int32)
    total = grp_end[-1]
    last = jnp.maximum(total - 1, 0)
    jj = jnp.minimum(j, last)
    ge = jnp.searchsorted(grp_end, jj, side="right").astype(jnp.int32)
    ge = jnp.minimum(ge, ne - 1)
    rows = jnp.clip(counts[ge] - (jj - grp_start[ge]) * gsz, 0, gsz)
    gr = jnp.where(j < total, rows, 0).astype(jnp.int32)
    gb = jj.astype(jnp.int32)
    e_idx = meta[:, 0:2].astype(jnp.int32)
    pos = row_off[e_idx] + meta[:, 4:6].astype(jnp.int32)
    tok = jnp.broadcast_to(jnp.arange(t, dtype=jnp.int32)[:, None], (t, TOP_K))
    tok_of_row = jnp.zeros((ng * gsz,), jnp.int32).at[pos.reshape(-1)].set(tok.reshape(-1))
    xs = jnp.take(h, tok_of_row, axis=0)
    ys = ffn_groups(xs, w1, w3, w2, ge, gb, gr, gsz=gsz)
    ya = jnp.take(ys, pos[:, 0], axis=0)
    yb = jnp.take(ys, pos[:, 1], axis=0)
    return moe_final(x, ya, yb, meta, final_g)


def kernel(x, norm_mix, norm_ffn, norm_final, even_w_in, ssd_conv_w, ssd_conv_b, ssd_dt_bias, ssd_a_log, ssd_d, ssd_norm_w, conf_dw_w, conf_dw_b, conf_ln_w, conf_ln_b, even_w_out, ffn_w1, ffn_w3, ffn_w2, attn_w_qkv, attn_w_out, rel_bias, moe_router, moe_w1, moe_w3, moe_w2):
    bsz, seq, d = x.shape
    assert norm_mix.shape[0] == 2, "two-layer trunk: one even (SSD+Conformer/FFN) and one odd (attention/MoE) layer"
    xf = x.reshape(bsz * seq, d)
    h = rmsnorm(xf, norm_mix[0])
    p_even = dict(w_in=even_w_in[0], conv_w=ssd_conv_w[0], conv_b=ssd_conv_b[0], dt_bias=ssd_dt_bias[0],
                  a_log=ssd_a_log[0], d_skip=ssd_d[0], ssd_norm_w=ssd_norm_w[0], conf_dw_w=conf_dw_w[0],
                  conf_dw_b=conf_dw_b[0], conf_ln_w=conf_ln_w[0], conf_ln_b=conf_ln_b[0], w_out=even_w_out[0])
    x1 = even_layer(xf, h, p_even, bsz=bsz, seq=seq)
    h1 = rmsnorm(x1, norm_ffn[0])
    y_ffn = dense_ffn(h1, ffn_w1[0], ffn_w3[0], ffn_w2[0])
    x2, h2 = add_norm(x1, y_ffn, norm_mix[1])
    p_odd = dict(w_qkv=attn_w_qkv[0], w_attn_out=attn_w_out[0], rel_bias=rel_bias)
    x3 = odd_layer_attn(x2, h2, p_odd, bsz=bsz, seq=seq)
    out = moe_layer(x3, norm_ffn[1], moe_router[0], moe_w1[0], moe_w3[0], moe_w2[0], norm_final)
    return out.reshape(bsz, seq, d)
```

```python
import functools
import math

import numpy as np
import jax
import jax.numpy as jnp
from jax import lax
from jax.experimental import pallas as pl
from jax.experimental.pallas import tpu as pltpu

SSD_HEAD_DIM = 64
SSD_GROUPS = 4
SSD_STATE = 128
SSD_CONV = 4
SSD_CHUNK = 128
CONF_KERNEL = 31
ATTN_HEADS = 16
DILATED_PATTERNS = ((128, 1), (512, 4), (2048, 16))
ATTN_BLOCK = 128
REL_BUCKETS = 32
REL_MAX_DIST = 2048
N_EXPERTS = 8
TOP_K = 2
EPS = 1e-6

V7X_LANES = 128
V7X_VMEM_BYTES = 64 * 1024 * 1024
VMEM_LIMIT = 56 * 1024 * 1024

F32 = jnp.float32
BF16 = jnp.bfloat16
NEG_INF = float("-inf")


def _cparams(*sem):
    return pltpu.CompilerParams(dimension_semantics=tuple(sem), vmem_limit_bytes=VMEM_LIMIT)


def _rms(x, g):
    ms = jnp.mean(x * x, axis=-1, keepdims=True)
    return x * lax.rsqrt(ms + EPS) * g


def _rmsnorm_kernel(x_ref, g_ref, h_ref):
    h_ref[...] = _rms(x_ref[...], g_ref[...]).astype(h_ref.dtype)


def rmsnorm(x, g, out_dtype=BF16, tm=512):
    t, d = x.shape
    return pl.pallas_call(
        _rmsnorm_kernel,
        out_shape=jax.ShapeDtypeStruct((t, d), out_dtype),
        grid=(t // tm,),
        in_specs=[pl.BlockSpec((tm, d), lambda i: (i, 0)), pl.BlockSpec((1, d), lambda i: (0, 0))],
        out_specs=pl.BlockSpec((tm, d), lambda i: (i, 0)),
        compiler_params=_cparams("parallel"),
        name="rmsnorm",
    )(x, g.reshape(1, d))


def _add_norm_kernel(x_ref, y_ref, g_ref, xo_ref, h_ref):
    xn = x_ref[...] + y_ref[...].astype(F32)
    xo_ref[...] = xn
    h_ref[...] = _rms(xn, g_ref[...]).astype(h_ref.dtype)


def add_norm(x, y, g, tm=512):
    t, d = x.shape
    return pl.pallas_call(
        _add_norm_kernel,
        out_shape=(jax.ShapeDtypeStruct((t, d), F32), jax.ShapeDtypeStruct((t, d), BF16)),
        grid=(t // tm,),
        in_specs=[pl.BlockSpec((tm, d), lambda i: (i, 0)), pl.BlockSpec((tm, d), lambda i: (i, 0)),
                  pl.BlockSpec((1, d), lambda i: (0, 0))],
        out_specs=(pl.BlockSpec((tm, d), lambda i: (i, 0)), pl.BlockSpec((tm, d), lambda i: (i, 0))),
        compiler_params=_cparams("parallel"),
        name="add_norm",
    )(x, y, g.reshape(1, d))


def _mm_kernel(*refs, n_lhs, ks, has_res):
    xs = refs[:n_lhs]
    w_ref = refs[n_lhs]
    r_ref = refs[n_lhs + 1] if has_res else None
    o_ref = refs[-1]
    acc = None
    k0 = 0
    for x_ref, k in zip(xs, ks):
        part = jnp.dot(x_ref[...], w_ref[k0:k0 + k, :].astype(BF16), preferred_element_type=F32)
        acc = part if acc is None else acc + part
        k0 += k
    if has_res:
        acc = acc + r_ref[...]
    o_ref[...] = acc.astype(o_ref.dtype)


def matmul(xs, w, *, n_cols, col_block_off=0, tn, tm=1024, res=None, out_dtype=BF16, name="matmul"):
    t = xs[0].shape[0]
    ks = tuple(x.shape[1] for x in xs)
    ktot = sum(ks)
    assert w.shape[0] == ktot and n_cols % tn == 0 and t % tm == 0
    in_specs = [pl.BlockSpec((tm, k), lambda i, j: (i, 0)) for k in ks]
    in_specs.append(pl.BlockSpec((ktot, tn), lambda i, j: (0, j + col_block_off)))
    args = list(xs) + [w]
    if res is not None:
        in_specs.append(pl.BlockSpec((tm, tn), lambda i, j: (i, j)))
        args.append(res)
    return pl.pallas_call(
        functools.partial(_mm_kernel, n_lhs=len(xs), ks=ks, has_res=res is not None),
        out_shape=jax.ShapeDtypeStruct((t, n_cols), out_dtype),
        grid=(t // tm, n_cols // tn),
        in_specs=in_specs,
        out_specs=pl.BlockSpec((tm, tn), lambda i, j: (i, j)),
        compiler_params=_cparams("parallel", "arbitrary"),
        name=name,
    )(*args)


HALO_BF16 = 16


def _ssd_conv_kernel(cur_ref, halo_ref, w_ref, b_ref, o_ref, buf_ref, *, ts):
    i = pl.program_id(1)
    halo = halo_ref[...].astype(F32)
    buf_ref[0:HALO_BF16, :] = jnp.where(i > 0, halo, 0.0)
    buf_ref[HALO_BF16:HALO_BF16 + ts, :] = cur_ref[...].astype(F32)
    acc = b_ref[...]
    for k in range(SSD_CONV):
        off = HALO_BF16 - (SSD_CONV - 1) + k
        acc = acc + w_ref[k:k + 1, :] * buf_ref[off:off + ts, :]
    o_ref[...] = (acc * jax.nn.sigmoid(acc)).astype(o_ref.dtype)


def ssd_conv(zx, conv_w, conv_b, *, bsz, seq, col_off, ts=512, tc=512):
    c = conv_w.shape[1]
    t = zx.shape[0]
    nsb = seq // ts
    cb0 = col_off // tc
    hb = ts // HALO_BF16
    return pl.pallas_call(
        functools.partial(_ssd_conv_kernel, ts=ts),
        out_shape=jax.ShapeDtypeStruct((t, c), BF16),
        grid=(bsz, nsb, c // tc),
        in_specs=[
            pl.BlockSpec((ts, tc), lambda b, i, j: (b * nsb + i, cb0 + j)),
            pl.BlockSpec((HALO_BF16, tc), lambda b, i, j: (jnp.maximum((b * nsb + i) * hb - 1, 0), cb0 + j)),
            pl.BlockSpec((SSD_CONV, tc), lambda b, i, j: (0, j)),
            pl.BlockSpec((1, tc), lambda b, i, j: (0, j)),
        ],
        out_specs=pl.BlockSpec((ts, tc), lambda b, i, j: (b * nsb + i, j)),
        scratch_shapes=[pltpu.VMEM((ts + HALO_BF16, tc), F32)],
        compiler_params=_cparams("parallel", "parallel", "parallel"),
        name="ssd_conv",
    )(zx, zx, conv_w, conv_b.reshape(1, c))


def _ssd_prep_kernel(raw_ref, bias_ref, alog_ref, dt_ref, acs_ref, dtt_ref, acst_ref, *, seq):
    raw = raw_ref[...] + bias_ref[...]
    dt = jnp.maximum(raw, 0.0) + jnp.log1p(jnp.exp(-jnp.abs(raw)))
    a = -jnp.exp(alog_ref[...])
    x = dt * a
    row = lax.broadcasted_iota(jnp.int32, x.shape, 0) % SSD_CHUNK
    sh = 1
    while sh < SSD_CHUNK:
        x = x + jnp.where(row >= sh, pltpu.roll(x, sh, 0), 0.0)
        sh *= 2
    dt_ref[...] = dt
    acs_ref[...] = x
    for c in range(seq // SSD_CHUNK):
        sl = slice(c * SSD_CHUNK, (c + 1) * SSD_CHUNK)
        dtt_ref[sl, :] = dt[sl, :].T
        acst_ref[sl, :] = x[sl, :].T


def ssd_prep(dt_raw, dt_bias_p, a_log_p, *, bsz, seq):
    t = dt_raw.shape[0]
    shp = jax.ShapeDtypeStruct((t, V7X_LANES), F32)
    blk = pl.BlockSpec((seq, V7X_LANES), lambda b: (b, 0))
    vec = pl.BlockSpec((1, V7X_LANES), lambda b: (0, 0))
    return pl.pallas_call(
        functools.partial(_ssd_prep_kernel, seq=seq),
        out_shape=(shp, shp, shp, shp),
        grid=(bsz,),
        in_specs=[blk, vec, vec],
        out_specs=(blk, blk, blk, blk),
        compiler_params=_cparams("parallel"),
        name="ssd_prep",
    )(dt_raw, dt_bias_p, a_log_p)


def _ssd_main_kernel(x_ref, b_ref, c_ref, z_ref, dt_ref, acs_ref, dtt_ref, acst_ref, dsk_ref, nw_ref,
                     y_ref, st_ref, *, heads_per_group):
    ci = pl.program_id(1)
    L = SSD_CHUNK
    gw = heads_per_group * SSD_HEAD_DIM
    npair = heads_per_group // 2

    @pl.when(ci == 0)
    def _():
        st_ref[...] = jnp.zeros_like(st_ref)

    dt = dt_ref[...]
    acs = acs_ref[...]
    dtt = dtt_ref[...]
    acst = acst_ref[...]
    row = lax.broadcasted_iota(jnp.int32, (L, L), 0)
    col = lax.broadcasted_iota(jnp.int32, (L, L), 1)
    causal = col <= row
    lane_lo = lax.broadcasted_iota(jnp.int32, (L, 2 * SSD_HEAD_DIM), 1) < SSD_HEAD_DIM
    lane_lo1 = lax.broadcasted_iota(jnp.int32, (1, 2 * SSD_HEAD_DIM), 1) < SSD_HEAD_DIM
    zero_b = jnp.zeros((L, 2 * SSD_HEAD_DIM), BF16)

    for g in range(SSD_GROUPS):
        bm = b_ref[:, g * SSD_STATE:(g + 1) * SSD_STATE]
        cm = c_ref[:, g * SSD_STATE:(g + 1) * SSD_STATE]
        bmt = bm.astype(F32).T.astype(BF16)
        cb = lax.dot_general(cm, bm, (((1,), (1,)), ((), ())), preferred_element_type=F32)
        ys = []
        for q in range(npair):
            c0 = g * gw + q * 2 * SSD_HEAD_DIM
            xp = x_ref[:, c0:c0 + 2 * SSD_HEAD_DIM]
            xpf = xp.astype(F32)
            ms = []
            for hh in range(2):
                h = g * heads_per_group + 2 * q + hh
                seg = acs[:, h:h + 1] - acst[h:h + 1, :]
                dec = jnp.exp(jnp.where(causal, seg, NEG_INF))
                ms.append((cb * dec * dtt[h:h + 1, :]).astype(BF16))
            h0 = g * heads_per_group + 2 * q
            h1 = h0 + 1
            lhs = jnp.concatenate(ms, axis=1)
            rhs = jnp.concatenate([jnp.where(lane_lo, xp, zero_b), jnp.where(lane_lo, zero_b, xp)], axis=0)
            y_diag = jnp.dot(lhs, rhs, preferred_element_type=F32)
            st = st_ref[g, :, q * 2 * SSD_HEAD_DIM:(q + 1) * 2 * SSD_HEAD_DIM]
            e_l = jnp.where(lane_lo, jnp.exp(acs[:, h0:h0 + 1]), jnp.exp(acs[:, h1:h1 + 1]))
            y_off = jnp.dot(cm, st.astype(BF16), preferred_element_type=F32) * e_l
            dsk = dsk_ref[:, c0:c0 + 2 * SSD_HEAD_DIM]
            ys.append(y_diag + y_off + xpf * dsk)
            last0 = acs[L - 1:L, h0:h0 + 1]
            last1 = acs[L - 1:L, h1:h1 + 1]
            w_l = jnp.where(lane_lo, dt[:, h0:h0 + 1] * jnp.exp(last0 - acs[:, h0:h0 + 1]),
                            dt[:, h1:h1 + 1] * jnp.exp(last1 - acs[:, h1:h1 + 1]))
            xw = (xpf * w_l).astype(BF16)
            st_new = jnp.dot(bmt, xw, preferred_element_type=F32)
            cd = jnp.where(lane_lo1, jnp.exp(last0), jnp.exp(last1))
            st_ref[g, :, q * 2 * SSD_HEAD_DIM:(q + 1) * 2 * SSD_HEAD_DIM] = st * cd + st_new
        y = jnp.concatenate(ys, axis=1)
        zg = z_ref[:, g * gw:(g + 1) * gw].astype(F32)
        yg = y * (zg * jax.nn.sigmoid(zg))
        ms_ = jnp.mean(yg * yg, axis=-1, keepdims=True)
        y_ref[:, g * gw:(g + 1) * gw] = (yg * lax.rsqrt(ms_ + EPS) * nw_ref[:, g * gw:(g + 1) * gw]).astype(y_ref.dtype)


def ssd_main(xc, zx, dt, acs, dtt, acst, dskip_e, norm_w, *, bsz, seq, d_ssd):
    t = xc.shape[0]
    nc = seq // SSD_CHUNK
    heads = d_ssd // SSD_HEAD_DIM
    hpg = heads // SSD_GROUPS
    bc_w = SSD_GROUPS * SSD_STATE
    assert d_ssd % bc_w == 0 and hpg % 2 == 0
    rowmap = lambda b, c: (b * nc + c, 0)
    return pl.pallas_call(
        functools.partial(_ssd_main_kernel, heads_per_group=hpg),
        out_shape=jax.ShapeDtypeStruct((t, d_ssd), BF16),
        grid=(bsz, nc),
        in_specs=[
            pl.BlockSpec((SSD_CHUNK, d_ssd), rowmap),
            pl.BlockSpec((SSD_CHUNK, bc_w), lambda b, c: (b * nc + c, d_ssd // bc_w)),
            pl.BlockSpec((SSD_CHUNK, bc_w), lambda b, c: (b * nc + c, d_ssd // bc_w + 1)),
            pl.BlockSpec((SSD_CHUNK, d_ssd), rowmap),
            pl.BlockSpec((SSD_CHUNK, V7X_LANES), rowmap),
            pl.BlockSpec((SSD_CHUNK, V7X_LANES), rowmap),
            pl.BlockSpec((SSD_CHUNK, V7X_LANES), rowmap),
            pl.BlockSpec((SSD_CHUNK, V7X_LANES), rowmap),
            pl.BlockSpec((1, d_ssd), lambda b, c: (0, 0)),
            pl.BlockSpec((1, d_ssd), lambda b, c: (0, 0)),
        ],
        out_specs=pl.BlockSpec((SSD_CHUNK, d_ssd), rowmap),
        scratch_shapes=[pltpu.VMEM((SSD_GROUPS, SSD_STATE, hpg * SSD_HEAD_DIM), F32)],
        compiler_params=_cparams("parallel", "arbitrary"),
        name="ssd_main",
    )(xc, xc, xc, zx, dt, acs, dtt, acst, dskip_e, norm_w.reshape(1, d_ssd))


CONF_HALO = 32


def _conf_kernel(a_ref, g_ref, ah_ref, gh_ref, w_ref, b_ref, lw_ref, lb_ref, o_ref, buf_ref, *, ts, rc, lt):
    i = pl.program_id(1)
    d = a_ref.shape[1]
    ah = ah_ref[...].astype(F32)
    gh = gh_ref[...].astype(F32)
    buf_ref[0:CONF_HALO, :] = jnp.where(i > 0, ah * jax.nn.sigmoid(gh), 0.0)
    a = a_ref[...].astype(F32)
    g = g_ref[...].astype(F32)
    buf_ref[CONF_HALO:CONF_HALO + ts, :] = a * jax.nn.sigmoid(g)
    base = CONF_HALO - (CONF_KERNEL - 1)
    outs = []
    for r in range(ts // rc):
        cols = []
        for c in range(d // lt):
            acc = jnp.broadcast_to(b_ref[:, c * lt:(c + 1) * lt], (rc, lt))
            for k in range(CONF_KERNEL):
                off = base + r * rc + k
                acc = acc + w_ref[k:k + 1, c * lt:(c + 1) * lt] * buf_ref[off:off + rc, c * lt:(c + 1) * lt]
            cols.append(acc)
        outs.append(jnp.concatenate(cols, axis=1))
    u = jnp.concatenate(outs, axis=0)
    mu = jnp.mean(u, axis=-1, keepdims=True)
    uc = u - mu
    var = jnp.mean(uc * uc, axis=-1, keepdims=True)
    y = uc * lax.rsqrt(var + EPS) * lw_ref[...] + lb_ref[...]
    o_ref[...] = (y * jax.nn.sigmoid(y)).astype(o_ref.dtype)


def conf_module(conf, dw_w, dw_b, ln_w, ln_b, *, bsz, seq, ts=128, rc=32, lt=512):
    t = conf.shape[0]
    d = dw_w.shape[1]
    nsb = seq // ts
    hb = ts // CONF_HALO
    cur = lambda col: pl.BlockSpec((ts, d), lambda b, i: (b * nsb + i, col))
    halo = lambda col: pl.BlockSpec((CONF_HALO, d), lambda b, i: (jnp.maximum((b * nsb + i) * hb - 1, 0), col))
    vec = pl.BlockSpec((1, d), lambda b, i: (0, 0))
    return pl.pallas_call(
        functools.partial(_conf_kernel, ts=ts, rc=rc, lt=lt),
        out_shape=jax.ShapeDtypeStruct((t, d), BF16),
        grid=(bsz, nsb),
        in_specs=[cur(0), cur(1), halo(0), halo(1), pl.BlockSpec((CONF_KERNEL, d), lambda b, i: (0, 0)),
                  vec, vec, vec],
        out_specs=pl.BlockSpec((ts, d), lambda b, i: (b * nsb + i, 0)),
        scratch_shapes=[pltpu.VMEM((ts + CONF_HALO, d), F32)],
        compiler_params=_cparams("parallel", "parallel"),
        name="conf_module",
    )(conf, conf, conf, conf, dw_w, dw_b.reshape(1, d), ln_w.reshape(1, d), ln_b.reshape(1, d))


FFN_SUB = 256


FFN_NCHUNK = 512


def pack_bf16_pair(a, b):
    ua = lax.bitcast_convert_type(a.astype(BF16).astype(F32), jnp.uint32)
    ub = lax.bitcast_convert_type(b.astype(BF16).astype(F32), jnp.uint32)
    return (ua >> 16) | (ub & jnp.uint32(0xFFFF0000))


def unpack_bf16_pair(u):
    a = lax.bitcast_convert_type(u << 16, F32)
    b = lax.bitcast_convert_type(u & jnp.uint32(0xFFFF0000), F32)
    return a, b


def _ffn_kernel(ge_ref, gb_ref, gr_ref, x_ref, w1_ref, w3_ref, w2_ref, o_ref, acc_ref, *xb_scr, nf, gsz, packed):
    g = pl.program_id(0)
    f = pl.program_id(1)
    rows = gr_ref[g]
    valid = rows > 0
    d = acc_ref.shape[1]

    @pl.when(valid & (f == 0))
    def _():
        acc_ref[...] = jnp.zeros_like(acc_ref)
        if packed:
            a, b = unpack_bf16_pair(x_ref[...])
            xb_scr[0][:, 0:d // 2] = a.astype(BF16)
            xb_scr[0][:, d // 2:d] = b.astype(BF16)

    xsrc = xb_scr[0] if packed else x_ref

    def slab(r0, nrows, w1, w3, w2):
        xs = xsrc[pl.ds(r0, nrows), :]
        h1 = jnp.dot(xs, w1, preferred_element_type=F32)
        h3 = jnp.dot(xs, w3, preferred_element_type=F32)
        hh = (h1 * jax.nn.sigmoid(h1) * h3).astype(BF16)
        for c in range(d // FFN_NCHUNK):
            cs = slice(c * FFN_NCHUNK, (c + 1) * FFN_NCHUNK)
            acc_ref[pl.ds(r0, nrows), cs] += jnp.dot(hh, w2[:, cs], preferred_element_type=F32)

    full = rows > gsz - FFN_SUB

    @pl.when(full)
    def _():
        slab(0, gsz, w1_ref[...].astype(BF16), w3_ref[...].astype(BF16), w2_ref[...].astype(BF16))

    @pl.when(valid & jnp.logical_not(full))
    def _():
        nsub = (rows + FFN_SUB - 1) // FFN_SUB
        w1 = w1_ref[...].astype(BF16)
        w3 = w3_ref[...].astype(BF16)
        w2 = w2_ref[...].astype(BF16)

        def body(s, carry):
            slab(pl.multiple_of(s * FFN_SUB, FFN_SUB), FFN_SUB, w1, w3, w2)
            return carry

        lax.fori_loop(0, nsub, body, 0)

    @pl.when(jnp.logical_not(valid) & (f == 0))
    def _():
        o_ref[...] = jnp.zeros_like(o_ref)

    @pl.when(valid & (f == nf - 1))
    def _():
        if packed:
            o_ref[...] = pack_bf16_pair(acc_ref[:, 0:d // 2], acc_ref[:, d // 2:d])
        else:
            o_ref[...] = acc_ref[...].astype(o_ref.dtype)


def ffn_groups(xs, w1, w3, w2, grp_expert, grp_block, grp_rows, *, gsz=1024, tf=256, packed=False):
    e, d, ff = w1.shape
    r, dx = xs.shape
    nf = ff // tf
    ng = grp_expert.shape[0]

    def fmap(f, g, gr):
        return jnp.where(gr[g] > 0, f, nf - 1)

    scratch = [pltpu.VMEM((gsz, d), F32)]
    if packed:
        scratch.append(pltpu.VMEM((gsz, d), BF16))
    grid_spec = pltpu.PrefetchScalarGridSpec(
        num_scalar_prefetch=3,
        grid=(ng, nf),
        in_specs=[
            pl.BlockSpec((gsz, dx), lambda g, f, ge, gb, gr: (gb[g], 0)),
            pl.BlockSpec((None, d, tf), lambda g, f, ge, gb, gr: (ge[g], 0, fmap(f, g, gr))),
            pl.BlockSpec((None, d, tf), lambda g, f, ge, gb, gr: (ge[g], 0, fmap(f, g, gr))),
            pl.BlockSpec((None, tf, d), lambda g, f, ge, gb, gr: (ge[g], fmap(f, g, gr), 0)),
        ],
        out_specs=pl.BlockSpec((gsz, dx), lambda g, f, ge, gb, gr: (gb[g], 0)),
        scratch_shapes=scratch,
    )
    return pl.pallas_call(
        functools.partial(_ffn_kernel, nf=nf, gsz=gsz, packed=packed),
        out_shape=jax.ShapeDtypeStruct((r, dx), xs.dtype),
        grid_spec=grid_spec,
        compiler_params=_cparams("arbitrary", "arbitrary"),
        name="ffn_groups",
    )(grp_expert, grp_block, grp_rows, xs, w1, w3, w2)


def _row_copy(src_ref, src_row, dst_ref, dst_row, sem):
    return pltpu.make_async_copy(src_ref.at[pl.ds(src_row, 1), :], dst_ref.at[pl.ds(dst_row, 1), :], sem)


def _dispatch_kernel(pos_ref, h_ref, xs_in_ref, xs_ref, sem):
    del xs_in_ref
    tm = h_ref.shape[0]
    base = pl.program_id(0) * tm

    def issue(i, c):
        for k in range(TOP_K):
            _row_copy(h_ref, i, xs_ref, pos_ref[(base + i) * TOP_K + k], sem).start()
        return c

    lax.fori_loop(0, tm, issue, 0)

    def drain(i, c):
        for k in range(TOP_K):
            _row_copy(h_ref, i, xs_ref, pos_ref[(base + i) * TOP_K + k], sem).wait()
        return c

    lax.fori_loop(0, tm, drain, 0)


def moe_dispatch(hp, pos_flat, n_rows, tm=512):
    t, w = hp.shape
    xs0 = jnp.zeros((n_rows, w), hp.dtype)
    grid_spec = pltpu.PrefetchScalarGridSpec(
        num_scalar_prefetch=1,
        grid=(t // tm,),
        in_specs=[pl.BlockSpec((tm, w), lambda i, pos: (i, 0)), pl.BlockSpec(memory_space=pl.ANY)],
        out_specs=pl.BlockSpec(memory_space=pl.ANY),
        scratch_shapes=[pltpu.SemaphoreType.DMA],
    )
    return pl.pallas_call(
        _dispatch_kernel,
        out_shape=jax.ShapeDtypeStruct((n_rows, w), hp.dtype),
        grid_spec=grid_spec,
        input_output_aliases={2: 0},
        compiler_params=_cparams("arbitrary"),
        name="moe_dispatch",
    )(pos_flat, hp, xs0)


def _t5_bucket_np(dist):
    max_exact = REL_BUCKETS // 2
    d_f = np.maximum(dist, 1).astype(np.float32)
    large = max_exact + (np.log(d_f / np.float32(max_exact)) / np.float32(math.log(REL_MAX_DIST / max_exact))
                         * np.float32(REL_BUCKETS - max_exact)).astype(np.int32)
    large = np.minimum(large, REL_BUCKETS - 1)
    return np.where(dist < max_exact, dist, large)


def _attn_branch_kernel(q_ref, kc_ref, kp_ref, vc_ref, vp_ref, bias_ref, o_ref, lse_ref, *, hd, use_prev, scale):
    i = pl.program_id(2)
    tq = q_ref.shape[0]
    lane = lax.broadcasted_iota(jnp.int32, (tq, V7X_LANES), 1)
    lse_tile = jnp.zeros((tq, V7X_LANES), F32)
    nt = (((1,), (1,)), ((), ()))
    for h in range(ATTN_HEADS):
        sl = slice(h * hd, (h + 1) * hd)
        q = q_ref[:, sl]
        s_c = lax.dot_general(q, kc_ref[:, sl], nt, preferred_element_type=F32) * scale + bias_ref[h, :, tq:2 * tq]
        m = jnp.max(s_c, axis=-1, keepdims=True)
        if use_prev:
            s_p = lax.dot_general(q, kp_ref[:, sl], nt, preferred_element_type=F32) * scale + bias_ref[h, :, 0:tq]
            s_p = jnp.where(i > 0, s_p, NEG_INF)
            m = jnp.maximum(m, jnp.max(s_p, axis=-1, keepdims=True))
        p_c = jnp.exp(s_c - m)
        den = jnp.sum(p_c, axis=-1, keepdims=True)
        acc = jnp.dot(p_c.astype(BF16), vc_ref[:, sl], preferred_element_type=F32)
        if use_prev:
            p_p = jnp.exp(s_p - m)
            den = den + jnp.sum(p_p, axis=-1, keepdims=True)
            acc = acc + jnp.dot(p_p.astype(BF16), vp_ref[:, sl], preferred_element_type=F32)
        o_ref[:, sl] = (acc / den).astype(o_ref.dtype)
        lse_tile = jnp.where(lane == h, m + jnp.log(den), lse_tile)
    lse_ref[...] = lse_tile


def attn_branch(qkv, bias, *, bsz, seq, dil, hd):
    dm = ATTN_HEADS * hd
    length = seq // dil
    tq = math.gcd(length, ATTN_BLOCK)
    nblk = length // tq
    use_prev = nblk > 1
    view = qkv.reshape(bsz * length, dil * 3 * dm)
    cur = lambda which: pl.BlockSpec((tq, dm), lambda b, r, i: (b * nblk + i, r * 3 + which))
    prev = lambda which: pl.BlockSpec((tq, dm), lambda b, r, i: (b * nblk + jnp.maximum(i - 1, 0), r * 3 + which))
    o, lse = pl.pallas_call(
        functools.partial(_attn_branch_kernel, hd=hd, use_prev=use_prev, scale=hd ** -0.5),
        out_shape=(jax.ShapeDtypeStruct((bsz * length, dil * dm), BF16),
                   jax.ShapeDtypeStruct((bsz * length, dil * V7X_LANES), F32)),
        grid=(bsz, dil, nblk),
        in_specs=[cur(0), cur(1), prev(1), cur(2), prev(2),
                  pl.BlockSpec((ATTN_HEADS, tq, 2 * tq), lambda b, r, i: (0, 0, 0))],
        out_specs=(pl.BlockSpec((tq, dm), lambda b, r, i: (b * nblk + i, r)),
                   pl.BlockSpec((tq, V7X_LANES), lambda b, r, i: (b * nblk + i, r))),
        compiler_params=_cparams("parallel", "parallel", "arbitrary"),
        name=f"attn_branch_d{dil}",
    )(view, view, view, view, view, bias)
    return o.reshape(bsz * seq, dm), lse.reshape(bsz * seq, V7X_LANES)


def _attn_combine_kernel(*refs, nb, hd):
    o_refs = refs[:nb]
    l_refs = refs[nb:2 * nb]
    out_ref = refs[2 * nb]
    ls = [r[...] for r in l_refs]
    m = ls[0]
    for l in ls[1:]:
        m = jnp.maximum(m, l)
    es = [jnp.exp(l - m) for l in ls]
    tot = es[0]
    for e in es[1:]:
        tot = tot + e
    ws = [e / tot for e in es]
    for h in range(ATTN_HEADS):
        sl = slice(h * hd, (h + 1) * hd)
        acc = None
        for b in range(nb):
            term = o_refs[b][:, sl].astype(F32) * ws[b][:, h:h + 1]
            acc = term if acc is None else acc + term
        out_ref[:, sl] = acc.astype(out_ref.dtype)


def attn_combine(os_, ls_, *, hd, tm=512):
    t, dm = os_[0].shape
    nb = len(os_)
    ob = pl.BlockSpec((tm, dm), lambda i: (i, 0))
    lb = pl.BlockSpec((tm, V7X_LANES), lambda i: (i, 0))
    return pl.pallas_call(
        functools.partial(_attn_combine_kernel, nb=nb, hd=hd),
        out_shape=jax.ShapeDtypeStruct((t, dm), BF16),
        grid=(t // tm,),
        in_specs=[ob] * nb + [lb] * nb,
        out_specs=ob,
        compiler_params=_cparams("parallel"),
        name="attn_combine",
    )(*os_, *ls_)


def attn_bias_tables(rel_bias, seq):
    tabs = []
    for window, dil in DILATED_PATTERNS:
        length = seq // dil
        n_back = window // dil
        tq = math.gcd(length, ATTN_BLOCK)
        assert n_back == tq, "kernel assumes the look-back equals one query block"
        i = np.arange(tq)[:, None]
        j = np.arange(2 * tq)[None, :]
        steps = i + n_back - j
        band = (steps >= 0) & (steps <= n_back)
        idx = _t5_bucket_np(np.clip(steps, 0, n_back) * dil)
        tab = jnp.transpose(rel_bias[idx], (2, 0, 1)).astype(F32)
        tabs.append(jnp.where(band[None], tab, NEG_INF))
    return tabs


def _split_bf16(x):
    hi = x.astype(BF16)
    lo = (x - hi.astype(F32)).astype(BF16)
    return hi, lo


def _moe_route_kernel(x_ref, g_ref, wr_ref, h_ref, meta_ref, cnt_ref, carry_ref):
    step = pl.program_id(0)
    tm = x_ref.shape[0]

    @pl.when(step == 0)
    def _():
        carry_ref[...] = jnp.zeros_like(carry_ref)

    h = _rms(x_ref[...], g_ref[...])
    h_hi, h_lo = _split_bf16(h)
    dh = h.shape[1] // 2
    h_ref[...] = pack_bf16_pair(h[:, 0:dh], h[:, dh:2 * dh])
    w_hi, w_lo = _split_bf16(wr_ref[...])
    logits = (jnp.dot(h_hi, w_hi, preferred_element_type=F32) + jnp.dot(h_lo, w_hi, preferred_element_type=F32)
              + jnp.dot(h_hi, w_lo, preferred_element_type=F32))
    lane = lax.broadcasted_iota(jnp.int32, (tm, V7X_LANES), 1).astype(F32)
    lg = jnp.where(lane < N_EXPERTS, logits, NEG_INF)
    m1 = jnp.max(lg, axis=-1, keepdims=True)
    i1 = jnp.min(jnp.where(lg == m1, lane, float(V7X_LANES)), axis=-1, keepdims=True)
    lg2 = jnp.where(lane == i1, NEG_INF, lg)
    m2 = jnp.max(lg2, axis=-1, keepdims=True)
    i2 = jnp.min(jnp.where(lg2 == m2, lane, float(V7X_LANES)), axis=-1, keepdims=True)
    e2 = jnp.exp(m2 - m1)
    g1 = 1.0 / (1.0 + e2)
    g2 = e2 / (1.0 + e2)
    oh = ((lane == i1) | (lane == i2)).astype(BF16)
    r_i = lax.broadcasted_iota(jnp.int32, (tm, tm), 0)
    c_i = lax.broadcasted_iota(jnp.int32, (tm, tm), 1)
    tri = (c_i < r_i).astype(BF16)
    rank = jnp.dot(tri, oh, preferred_element_type=F32) + carry_ref[0:1, :]
    r1 = jnp.sum(jnp.where(lane == i1, rank, 0.0), axis=-1, keepdims=True)
    r2 = jnp.sum(jnp.where(lane == i2, rank, 0.0), axis=-1, keepdims=True)
    carry_ref[0:1, :] = carry_ref[0:1, :] + jnp.sum(oh.astype(F32), axis=0, keepdims=True)
    meta = jnp.where(lane == 0, i1.astype(F32), 0.0)
    meta = jnp.where(lane == 1, i2.astype(F32), meta)
    meta = jnp.where(lane == 2, g1, meta)
    meta = jnp.where(lane == 3, g2, meta)
    meta = jnp.where(lane == 4, r1, meta)
    meta = jnp.where(lane == 5, r2, meta)
    meta_ref[...] = meta
    cnt_ref[...] = jnp.broadcast_to(carry_ref[0:1, :], cnt_ref.shape)


def moe_route(x, g, w_router_p, tm=256):
    t, d = x.shape
    return pl.pallas_call(
        _moe_route_kernel,
        out_shape=(jax.ShapeDtypeStruct((t, d // 2), jnp.uint32), jax.ShapeDtypeStruct((t, V7X_LANES), F32),
                   jax.ShapeDtypeStruct((8, V7X_LANES), F32)),
        grid=(t // tm,),
        in_specs=[pl.BlockSpec((tm, d), lambda i: (i, 0)), pl.BlockSpec((1, d), lambda i: (0, 0)),
                  pl.BlockSpec((d, V7X_LANES), lambda i: (0, 0))],
        out_specs=(pl.BlockSpec((tm, d // 2), lambda i: (i, 0)), pl.BlockSpec((tm, V7X_LANES), lambda i: (i, 0)),
                   pl.BlockSpec((8, V7X_LANES), lambda i: (0, 0))),
        scratch_shapes=[pltpu.VMEM((8, V7X_LANES), F32)],
        compiler_params=_cparams("arbitrary"),
        name="moe_route",
    )(x, g.reshape(1, d), w_router_p)


def _moe_final_kernel(pos_ref, x_ref, meta_ref, g_ref, ys_ref, o_ref, ybuf, sem):
    tm = x_ref.shape[0]
    dh = ybuf.shape[2]
    base = pl.program_id(0) * tm

    def issue(i, c):
        for k in range(TOP_K):
            _row_copy(ys_ref, pos_ref[(base + i) * TOP_K + k], ybuf.at[k], i, sem).start()
        return c

    lax.fori_loop(0, tm, issue, 0)

    def drain(i, c):
        for k in range(TOP_K):
            _row_copy(ys_ref, pos_ref[(base + i) * TOP_K + k], ybuf.at[k], i, sem).wait()
        return c

    lax.fori_loop(0, tm, drain, 0)
    meta = meta_ref[...]
    g1 = meta[:, 2:3]
    g2 = meta[:, 3:4]
    a0, b0 = unpack_bf16_pair(ybuf[0])
    a1, b1 = unpack_bf16_pair(ybuf[1])
    x = x_ref[...]
    xa = x[:, 0:dh] + a0 * g1 + a1 * g2
    xb = x[:, dh:2 * dh] + b0 * g1 + b1 * g2
    ms = (jnp.sum(xa * xa, axis=-1, keepdims=True) + jnp.sum(xb * xb, axis=-1, keepdims=True)) * (1.0 / (2 * dh))
    inv = lax.rsqrt(ms + EPS)
    o_ref[:, 0:dh] = xa * inv * g_ref[:, 0:dh]
    o_ref[:, dh:2 * dh] = xb * inv * g_ref[:, dh:2 * dh]


def moe_final(x, ys, pos_flat, meta, g, tm=256):
    t, d = x.shape
    grid_spec = pltpu.PrefetchScalarGridSpec(
        num_scalar_prefetch=1,
        grid=(t // tm,),
        in_specs=[pl.BlockSpec((tm, d), lambda i, pos: (i, 0)), pl.BlockSpec((tm, V7X_LANES), lambda i, pos: (i, 0)),
                  pl.BlockSpec((1, d), lambda i, pos: (0, 0)), pl.BlockSpec(memory_space=pl.ANY)],
        out_specs=pl.BlockSpec((tm, d), lambda i, pos: (i, 0)),
        scratch_shapes=[pltpu.VMEM((TOP_K, tm, d // 2), jnp.uint32), pltpu.SemaphoreType.DMA],
    )
    return pl.pallas_call(
        _moe_final_kernel,
        out_shape=jax.ShapeDtypeStruct((t, d), F32),
        grid_spec=grid_spec,
        compiler_params=_cparams("arbitrary"),
        name="moe_final",
    )(pos_flat, x, meta, g.reshape(1, d), ys)


def _pad_lanes(v, fill=0.0):
    v = v.reshape(1, -1).astype(F32)
    return jnp.pad(v, ((0, 0), (0, V7X_LANES - v.shape[1])), constant_values=fill)


def even_layer(x, h, p, *, bsz, seq):
    t, d = x.shape
    d_ssd = d
    d_conf = d
    heads = d_ssd // SSD_HEAD_DIM
    conv_dim = d_ssd + 2 * SSD_GROUPS * SSD_STATE
    i1 = d_ssd + conv_dim
    i2 = i1 + heads
    w_in = p["w_in"]
    zx = matmul([h], w_in, n_cols=i1, tn=1024, name="in_proj_zx")
    w_dt = jnp.pad(w_in[:, i1:i2], ((0, 0), (0, V7X_LANES - heads)))
    dt_raw = matmul([h], w_dt, n_cols=V7X_LANES, tn=V7X_LANES, out_dtype=F32, name="in_proj_dt")
    conf = matmul([h], w_in[:, i2:], n_cols=2 * d_conf, tn=1024, name="in_proj_conf")
    xc = ssd_conv(zx, p["conv_w"], p["conv_b"], bsz=bsz, seq=seq, col_off=d_ssd)
    dt, acs, dtt, acst = ssd_prep(dt_raw, _pad_lanes(p["dt_bias"]), _pad_lanes(p["a_log"]), bsz=bsz, seq=seq)
    dskip_e = jnp.repeat(p["d_skip"].astype(F32), SSD_HEAD_DIM).reshape(1, d_ssd)
    y_ssd = ssd_main(xc, zx, dt, acs, dtt, acst, dskip_e, p["ssd_norm_w"], bsz=bsz, seq=seq, d_ssd=d_ssd)
    u = conf_module(conf, p["conf_dw_w"], p["conf_dw_b"], p["conf_ln_w"], p["conf_ln_b"], bsz=bsz, seq=seq)
    return matmul([y_ssd, u], p["w_out"], n_cols=d, tn=512, res=x, out_dtype=F32, name="even_out_proj")


def dense_ffn(h, w1, w3, w2, gsz=1024):
    t = h.shape[0]
    ng = t // gsz
    ge = jnp.zeros((ng,), jnp.int32)
    gb = jnp.arange(ng, dtype=jnp.int32)
    gr = jnp.full((ng,), gsz, jnp.int32)
    return ffn_groups(h, w1[None], w3[None], w2[None], ge, gb, gr, gsz=gsz)


def odd_layer_attn(x, h, p, *, bsz, seq):
    t, d = x.shape
    hd = d // ATTN_HEADS
    qkv = matmul([h], p["w_qkv"], n_cols=3 * d, tn=1024, name="qkv_proj")
    tabs = attn_bias_tables(p["rel_bias"], seq)
    os_, ls_ = [], []
    for (window, dil), tab in zip(DILATED_PATTERNS, tabs):
        o, l = attn_branch(qkv, tab, bsz=bsz, seq=seq, dil=dil, hd=hd)
        os_.append(o)
        ls_.append(l)
    o = attn_combine(os_, ls_, hd=hd)
    return matmul([o], p["w_attn_out"], n_cols=d, tn=1024, res=x, out_dtype=F32, name="attn_out_proj")


def moe_layer(x, norm_g, w_router, w1, w3, w2, final_g, gsz=1024):
    t, d = x.shape
    ne = w1.shape[0]
    w_router_p = jnp.pad(w_router.astype(F32), ((0, 0), (0, V7X_LANES - ne)))
    h, meta, cnt = moe_route(x, norm_g, w_router_p)
    counts = cnt[0, :ne].astype(jnp.int32)
    ngrp_e = (counts + gsz - 1) // gsz
    grp_end = jnp.cumsum(ngrp_e)
    grp_start = grp_end - ngrp_e
    row_off = grp_start * gsz
    ng = (t * TOP_K) // gsz + ne
    j = jnp.arange(ng, dtype=jnp.int32)
    total = grp_end[-1]
    last = jnp.maximum(total - 1, 0)
    jj = jnp.minimum(j, last)
    ge = jnp.searchsorted(grp_end, jj, side="right").astype(jnp.int32)
    ge = jnp.minimum(ge, ne - 1)
    rows = jnp.clip(counts[ge] - (jj - grp_start[ge]) * gsz, 0, gsz)
    gr = jnp.where(j < total, rows, 0).astype(jnp.int32)
    gb = j
    e_idx = meta[:, 0:2].astype(jnp.int32)
    pos = (row_off[e_idx] + meta[:, 4:6].astype(jnp.int32)).reshape(-1)
    xs = moe_dispatch(h, pos, ng * gsz)
    ys = ffn_groups(xs, w1, w3, w2, ge, gb, gr, gsz=gsz, packed=True)
    return moe_final(x, ys, pos, meta, final_g)


def kernel(x, norm_mix, norm_ffn, norm_final, even_w_in, ssd_conv_w, ssd_conv_b, ssd_dt_bias, ssd_a_log, ssd_d, ssd_norm_w, conf_dw_w, conf_dw_b, conf_ln_w, conf_ln_b, even_w_out, ffn_w1, ffn_w3, ffn_w2, attn_w_qkv, attn_w_out, rel_bias, moe_router, moe_w1, moe_w3, moe_w2):
    bsz, seq, d = x.shape
    assert norm_mix.shape[0] == 2, "two-layer trunk: one even (SSD+Conformer/FFN) and one odd (attention/MoE) layer"
    xf = x.reshape(bsz * seq, d)
    h = rmsnorm(xf, norm_mix[0])
    p_even = dict(w_in=even_w_in[0], conv_w=ssd_conv_w[0], conv_b=ssd_conv_b[0], dt_bias=ssd_dt_bias[0],
                  a_log=ssd_a_log[0], d_skip=ssd_d[0], ssd_norm_w=ssd_norm_w[0], conf_dw_w=conf_dw_w[0],
                  conf_dw_b=conf_dw_b[0], conf_ln_w=conf_ln_w[0], conf_ln_b=conf_ln_b[0], w_out=even_w_out[0])
    x1 = even_layer(xf, h, p_even, bsz=bsz, seq=seq)
    h1 = rmsnorm(x1, norm_ffn[0])
    y_ffn = dense_ffn(h1, ffn_w1[0], ffn_w3[0], ffn_w2[0])
    x2, h2 = add_norm(x1, y_ffn, norm_mix[1])
    p_odd = dict(w_qkv=attn_w_qkv[0], w_attn_out=attn_w_out[0], rel_bias=rel_bias)
    x3 = odd_layer_attn(x2, h2, p_odd, bsz=bsz, seq=seq)
    out = moe_layer(x3, norm_ffn[1], moe_router[0], moe_w1[0], moe_w3[0], moe_w2[0], norm_final)
    return out.reshape(bsz, seq, d)
```

```python
import functools
import math

import numpy as np
import jax
import jax.numpy as jnp
from jax import lax
from jax.experimental import pallas as pl
from jax.experimental.pallas import tpu as pltpu

SSD_HEAD_DIM = 64
SSD_GROUPS = 4
SSD_STATE = 128
SSD_CONV = 4
SSD_CHUNK = 128
CONF_KERNEL = 31
ATTN_HEADS = 16
DILATED_PATTERNS = ((128, 1), (512, 4), (2048, 16))
ATTN_BLOCK = 128
REL_BUCKETS = 32
REL_MAX_DIST = 2048
N_EXPERTS = 8
TOP_K = 2
EPS = 1e-6

V7X_LANES = 128
V7X_VMEM_BYTES = 64 * 1024 * 1024
VMEM_LIMIT = 56 * 1024 * 1024

F32 = jnp.float32
BF16 = jnp.bfloat16
NEG_INF = float("-inf")


def _cparams(*sem):
    return pltpu.CompilerParams(dimension_semantics=tuple(sem), vmem_limit_bytes=VMEM_LIMIT)


def _rms(x, g):
    ms = jnp.mean(x * x, axis=-1, keepdims=True)
    return x * lax.rsqrt(ms + EPS) * g


def _rmsnorm_kernel(x_ref, g_ref, h_ref):
    h_ref[...] = _rms(x_ref[...], g_ref[...]).astype(h_ref.dtype)


def rmsnorm(x, g, out_dtype=BF16, tm=512):
    t, d = x.shape
    return pl.pallas_call(
        _rmsnorm_kernel,
        out_shape=jax.ShapeDtypeStruct((t, d), out_dtype),
        grid=(t // tm,),
        in_specs=[pl.BlockSpec((tm, d), lambda i: (i, 0)), pl.BlockSpec((1, d), lambda i: (0, 0))],
        out_specs=pl.BlockSpec((tm, d), lambda i: (i, 0)),
        compiler_params=_cparams("parallel"),
        name="rmsnorm",
    )(x, g.reshape(1, d))


def _add_norm_res_kernel(x_ref, y_ref, g_ref, xo_ref, ho_ref, xs_scr, hs_scr):
    nres, tp, d = xo_ref.shape
    xn = x_ref[...] + y_ref[...].astype(F32)
    h = _rms(xn, g_ref[...])
    for s in range(d // V7X_LANES):
        ls = slice(s * V7X_LANES, (s + 1) * V7X_LANES)
        xs_scr[s] = xn[:, ls]
        hs_scr[s] = h[:, ls]
    for r in range(nres):
        rows = pl.ds(r, tp, stride=nres)
        for s in range(d // V7X_LANES):
            ls = slice(s * V7X_LANES, (s + 1) * V7X_LANES)
            xo_ref[r, :, ls] = xs_scr[s, rows, :]
            ho_ref[r, :, ls] = hs_scr[s, rows, :].astype(ho_ref.dtype)


def add_norm_to_residue_major(x, y, g, *, bsz, seq, nres, tp=128):
    t, d = x.shape
    lsub = seq // nres
    tm = nres * tp
    nj = lsub // tp
    slab = pltpu.VMEM((d // V7X_LANES, tm, V7X_LANES), F32)
    xo, ho = pl.pallas_call(
        _add_norm_res_kernel,
        out_shape=(jax.ShapeDtypeStruct((bsz * nres, lsub, d), F32), jax.ShapeDtypeStruct((bsz * nres, lsub, d), BF16)),
        grid=(bsz, nj),
        in_specs=[pl.BlockSpec((tm, d), lambda b, j: (b * nj + j, 0)), pl.BlockSpec((tm, d), lambda b, j: (b * nj + j, 0)),
                  pl.BlockSpec((1, d), lambda b, j: (0, 0))],
        out_specs=(pl.BlockSpec((nres, tp, d), lambda b, j: (b, j, 0)), pl.BlockSpec((nres, tp, d), lambda b, j: (b, j, 0))),
        scratch_shapes=[slab, slab],
        compiler_params=_cparams("parallel", "parallel"),
        name="add_norm_reorder",
    )(x, y, g.reshape(1, d))
    return xo.reshape(t, d), ho.reshape(t, d)


def _mm_kernel(*refs, n_lhs, ks, has_res):
    xs = refs[:n_lhs]
    w_ref = refs[n_lhs]
    r_ref = refs[n_lhs + 1] if has_res else None
    o_ref = refs[-1]
    acc = None
    k0 = 0
    for x_ref, k in zip(xs, ks):
        part = jnp.dot(x_ref[...], w_ref[k0:k0 + k, :].astype(BF16), preferred_element_type=F32)
        acc = part if acc is None else acc + part
        k0 += k
    if has_res:
        acc = acc + r_ref[...]
    o_ref[...] = acc.astype(o_ref.dtype)


def matmul(xs, w, *, n_cols, col_block_off=0, tn, tm=1024, res=None, out_dtype=BF16, name="matmul"):
    t = xs[0].shape[0]
    ks = tuple(x.shape[1] for x in xs)
    ktot = sum(ks)
    assert w.shape[0] == ktot and n_cols % tn == 0 and t % tm == 0
    in_specs = [pl.BlockSpec((tm, k), lambda i, j: (i, 0)) for k in ks]
    in_specs.append(pl.BlockSpec((ktot, tn), lambda i, j: (0, j + col_block_off)))
    args = list(xs) + [w]
    if res is not None:
        in_specs.append(pl.BlockSpec((tm, tn), lambda i, j: (i, j)))
        args.append(res)
    return pl.pallas_call(
        functools.partial(_mm_kernel, n_lhs=len(xs), ks=ks, has_res=res is not None),
        out_shape=jax.ShapeDtypeStruct((t, n_cols), out_dtype),
        grid=(t // tm, n_cols // tn),
        in_specs=in_specs,
        out_specs=pl.BlockSpec((tm, tn), lambda i, j: (i, j)),
        compiler_params=_cparams("parallel", "arbitrary"),
        name=name,
    )(*args)


HALO_BF16 = 16


def _ssd_conv_kernel(cur_ref, halo_ref, w_ref, b_ref, o_ref, buf_ref, *, ts):
    i = pl.program_id(1)
    halo = halo_ref[...].astype(F32)
    buf_ref[0:HALO_BF16, :] = jnp.where(i > 0, halo, 0.0)
    buf_ref[HALO_BF16:HALO_BF16 + ts, :] = cur_ref[...].astype(F32)
    acc = b_ref[...]
    for k in range(SSD_CONV):
        off = HALO_BF16 - (SSD_CONV - 1) + k
        acc = acc + w_ref[k:k + 1, :] * buf_ref[off:off + ts, :]
    o_ref[...] = (acc * jax.nn.sigmoid(acc)).astype(o_ref.dtype)


def ssd_conv(zx, conv_w, conv_b, *, bsz, seq, col_off, ts=512, tc=512):
    c = conv_w.shape[1]
    t = zx.shape[0]
    nsb = seq // ts
    cb0 = col_off // tc
    hb = ts // HALO_BF16
    return pl.pallas_call(
        functools.partial(_ssd_conv_kernel, ts=ts),
        out_shape=jax.ShapeDtypeStruct((t, c), BF16),
        grid=(bsz, nsb, c // tc),
        in_specs=[
            pl.BlockSpec((ts, tc), lambda b, i, j: (b * nsb + i, cb0 + j)),
            pl.BlockSpec((HALO_BF16, tc), lambda b, i, j: (jnp.maximum((b * nsb + i) * hb - 1, 0), cb0 + j)),
            pl.BlockSpec((SSD_CONV, tc), lambda b, i, j: (0, j)),
            pl.BlockSpec((1, tc), lambda b, i, j: (0, j)),
        ],
        out_specs=pl.BlockSpec((ts, tc), lambda b, i, j: (b * nsb + i, j)),
        scratch_shapes=[pltpu.VMEM((ts + HALO_BF16, tc), F32)],
        compiler_params=_cparams("parallel", "parallel", "parallel"),
        name="ssd_conv",
    )(zx, zx, conv_w, conv_b.reshape(1, c))


def _ssd_prep_kernel(raw_ref, bias_ref, alog_ref, dt_ref, acs_ref, dtt_ref, acst_ref, *, seq):
    raw = raw_ref[...] + bias_ref[...]
    dt = jnp.maximum(raw, 0.0) + jnp.log1p(jnp.exp(-jnp.abs(raw)))
    a = -jnp.exp(alog_ref[...])
    x = dt * a
    row = lax.broadcasted_iota(jnp.int32, x.shape, 0) % SSD_CHUNK
    sh = 1
    while sh < SSD_CHUNK:
        x = x + jnp.where(row >= sh, pltpu.roll(x, sh, 0), 0.0)
        sh *= 2
    dt_ref[...] = dt
    acs_ref[...] = x
    for c in range(seq // SSD_CHUNK):
        sl = slice(c * SSD_CHUNK, (c + 1) * SSD_CHUNK)
        dtt_ref[sl, :] = dt[sl, :].T
        acst_ref[sl, :] = x[sl, :].T


def ssd_prep(dt_raw, dt_bias_p, a_log_p, *, bsz, seq):
    t = dt_raw.shape[0]
    shp = jax.ShapeDtypeStruct((t, V7X_LANES), F32)
    blk = pl.BlockSpec((seq, V7X_LANES), lambda b: (b, 0))
    vec = pl.BlockSpec((1, V7X_LANES), lambda b: (0, 0))
    return pl.pallas_call(
        functools.partial(_ssd_prep_kernel, seq=seq),
        out_shape=(shp, shp, shp, shp),
        grid=(bsz,),
        in_specs=[blk, vec, vec],
        out_specs=(blk, blk, blk, blk),
        compiler_params=_cparams("parallel"),
        name="ssd_prep",
    )(dt_raw, dt_bias_p, a_log_p)


def _ssd_main_kernel(x_ref, b_ref, c_ref, z_ref, dt_ref, acs_ref, dtt_ref, acst_ref, dsk_ref, nw_ref,
                     y_ref, st_ref, *, heads_per_group):
    ci = pl.program_id(1)
    L = SSD_CHUNK
    gw = heads_per_group * SSD_HEAD_DIM
    npair = heads_per_group // 2

    @pl.when(ci == 0)
    def _():
        st_ref[...] = jnp.zeros_like(st_ref)

    dt = dt_ref[...]
    acs = acs_ref[...]
    dtt = dtt_ref[...]
    acst = acst_ref[...]
    row = lax.broadcasted_iota(jnp.int32, (L, L), 0)
    col = lax.broadcasted_iota(jnp.int32, (L, L), 1)
    causal = col <= row
    lane_lo = lax.broadcasted_iota(jnp.int32, (L, 2 * SSD_HEAD_DIM), 1) < SSD_HEAD_DIM
    lane_lo1 = lax.broadcasted_iota(jnp.int32, (1, 2 * SSD_HEAD_DIM), 1) < SSD_HEAD_DIM
    zero_b = jnp.zeros((L, 2 * SSD_HEAD_DIM), BF16)

    for g in range(SSD_GROUPS):
        bm = b_ref[:, g * SSD_STATE:(g + 1) * SSD_STATE]
        cm = c_ref[:, g * SSD_STATE:(g + 1) * SSD_STATE]
        bmt = bm.astype(F32).T.astype(BF16)
        cb = lax.dot_general(cm, bm, (((1,), (1,)), ((), ())), preferred_element_type=F32)
        ys = []
        for q in range(npair):
            c0 = g * gw + q * 2 * SSD_HEAD_DIM
            xp = x_ref[:, c0:c0 + 2 * SSD_HEAD_DIM]
            xpf = xp.astype(F32)
            ms = []
            for hh in range(2):
                h = g * heads_per_group + 2 * q + hh
                seg = acs[:, h:h + 1] - acst[h:h + 1, :]
                dec = jnp.exp(jnp.where(causal, seg, NEG_INF))
                ms.append((cb * dec * dtt[h:h + 1, :]).astype(BF16))
            h0 = g * heads_per_group + 2 * q
            h1 = h0 + 1
            lhs = jnp.concatenate(ms, axis=1)
            rhs = jnp.concatenate([jnp.where(lane_lo, xp, zero_b), jnp.where(lane_lo, zero_b, xp)], axis=0)
            y_diag = jnp.dot(lhs, rhs, preferred_element_type=F32)
            st = st_ref[g, :, q * 2 * SSD_HEAD_DIM:(q + 1) * 2 * SSD_HEAD_DIM]
            e_l = jnp.where(lane_lo, jnp.exp(acs[:, h0:h0 + 1]), jnp.exp(acs[:, h1:h1 + 1]))
            y_off = jnp.dot(cm, st.astype(BF16), preferred_element_type=F32) * e_l
            dsk = dsk_ref[:, c0:c0 + 2 * SSD_HEAD_DIM]
            ys.append(y_diag + y_off + xpf * dsk)
            last0 = acs[L - 1:L, h0:h0 + 1]
            last1 = acs[L - 1:L, h1:h1 + 1]
            w_l = jnp.where(lane_lo, dt[:, h0:h0 + 1] * jnp.exp(last0 - acs[:, h0:h0 + 1]),
                            dt[:, h1:h1 + 1] * jnp.exp(last1 - acs[:, h1:h1 + 1]))
            xw = (xpf * w_l).astype(BF16)
            st_new = jnp.dot(bmt, xw, preferred_element_type=F32)
            cd = jnp.where(lane_lo1, jnp.exp(last0), jnp.exp(last1))
            st_ref[g, :, q * 2 * SSD_HEAD_DIM:(q + 1) * 2 * SSD_HEAD_DIM] = st * cd + st_new
        y = jnp.concatenate(ys, axis=1)
        zg = z_ref[:, g * gw:(g + 1) * gw].astype(F32)
        yg = y * (zg * jax.nn.sigmoid(zg))
        ms_ = jnp.mean(yg * yg, axis=-1, keepdims=True)
        y_ref[:, g * gw:(g + 1) * gw] = (yg * lax.rsqrt(ms_ + EPS) * nw_ref[:, g * gw:(g + 1) * gw]).astype(y_ref.dtype)


def ssd_main(xc, zx, dt, acs, dtt, acst, dskip_e, norm_w, *, bsz, seq, d_ssd):
    t = xc.shape[0]
    nc = seq // SSD_CHUNK
    heads = d_ssd // SSD_HEAD_DIM
    hpg = heads // SSD_GROUPS
    bc_w = SSD_GROUPS * SSD_STATE
    assert d_ssd % bc_w == 0 and hpg % 2 == 0
    rowmap = lambda b, c: (b * nc + c, 0)
    return pl.pallas_call(
        functools.partial(_ssd_main_kernel, heads_per_group=hpg),
        out_shape=jax.ShapeDtypeStruct((t, d_ssd), BF16),
        grid=(bsz, nc),
        in_specs=[
            pl.BlockSpec((SSD_CHUNK, d_ssd), rowmap),
            pl.BlockSpec((SSD_CHUNK, bc_w), lambda b, c: (b * nc + c, d_ssd // bc_w)),
            pl.BlockSpec((SSD_CHUNK, bc_w), lambda b, c: (b * nc + c, d_ssd // bc_w + 1)),
            pl.BlockSpec((SSD_CHUNK, d_ssd), rowmap),
            pl.BlockSpec((SSD_CHUNK, V7X_LANES), rowmap),
            pl.BlockSpec((SSD_CHUNK, V7X_LANES), rowmap),
            pl.BlockSpec((SSD_CHUNK, V7X_LANES), rowmap),
            pl.BlockSpec((SSD_CHUNK, V7X_LANES), rowmap),
            pl.BlockSpec((1, d_ssd), lambda b, c: (0, 0)),
            pl.BlockSpec((1, d_ssd), lambda b, c: (0, 0)),
        ],
        out_specs=pl.BlockSpec((SSD_CHUNK, d_ssd), rowmap),
        scratch_shapes=[pltpu.VMEM((SSD_GROUPS, SSD_STATE, hpg * SSD_HEAD_DIM), F32)],
        compiler_params=_cparams("parallel", "arbitrary"),
        name="ssd_main",
    )(xc, xc, xc, zx, dt, acs, dtt, acst, dskip_e, norm_w.reshape(1, d_ssd))


CONF_HALO = 32


def _conf_kernel(a_ref, g_ref, ah_ref, gh_ref, w_ref, b_ref, lw_ref, lb_ref, o_ref, buf_ref, *, ts, rc, lt):
    i = pl.program_id(1)
    d = a_ref.shape[1]
    ah = ah_ref[...].astype(F32)
    gh = gh_ref[...].astype(F32)
    buf_ref[0:CONF_HALO, :] = jnp.where(i > 0, ah * jax.nn.sigmoid(gh), 0.0)
    a = a_ref[...].astype(F32)
    g = g_ref[...].astype(F32)
    buf_ref[CONF_HALO:CONF_HALO + ts, :] = a * jax.nn.sigmoid(g)
    base = CONF_HALO - (CONF_KERNEL - 1)
    outs = []
    for r in range(ts // rc):
        cols = []
        for c in range(d // lt):
            acc = jnp.broadcast_to(b_ref[:, c * lt:(c + 1) * lt], (rc, lt))
            for k in range(CONF_KERNEL):
                off = base + r * rc + k
                acc = acc + w_ref[k:k + 1, c * lt:(c + 1) * lt] * buf_ref[off:off + rc, c * lt:(c + 1) * lt]
            cols.append(acc)
        outs.append(jnp.concatenate(cols, axis=1))
    u = jnp.concatenate(outs, axis=0)
    mu = jnp.mean(u, axis=-1, keepdims=True)
    uc = u - mu
    var = jnp.mean(uc * uc, axis=-1, keepdims=True)
    y = uc * lax.rsqrt(var + EPS) * lw_ref[...] + lb_ref[...]
    o_ref[...] = (y * jax.nn.sigmoid(y)).astype(o_ref.dtype)


def conf_module(conf, dw_w, dw_b, ln_w, ln_b, *, bsz, seq, ts=128, rc=32, lt=512):
    t = conf.shape[0]
    d = dw_w.shape[1]
    nsb = seq // ts
    hb = ts // CONF_HALO
    cur = lambda col: pl.BlockSpec((ts, d), lambda b, i: (b * nsb + i, col))
    halo = lambda col: pl.BlockSpec((CONF_HALO, d), lambda b, i: (jnp.maximum((b * nsb + i) * hb - 1, 0), col))
    vec = pl.BlockSpec((1, d), lambda b, i: (0, 0))
    return pl.pallas_call(
        functools.partial(_conf_kernel, ts=ts, rc=rc, lt=lt),
        out_shape=jax.ShapeDtypeStruct((t, d), BF16),
        grid=(bsz, nsb),
        in_specs=[cur(0), cur(1), halo(0), halo(1), pl.BlockSpec((CONF_KERNEL, d), lambda b, i: (0, 0)),
                  vec, vec, vec],
        out_specs=pl.BlockSpec((ts, d), lambda b, i: (b * nsb + i, 0)),
        scratch_shapes=[pltpu.VMEM((ts + CONF_HALO, d), F32)],
        compiler_params=_cparams("parallel", "parallel"),
        name="conf_module",
    )(conf, conf, conf, conf, dw_w, dw_b.reshape(1, d), ln_w.reshape(1, d), ln_b.reshape(1, d))


FFN_SUB = 256


FFN_NCHUNK = 512


def pack_bf16_pair(a, b):
    ua = lax.bitcast_convert_type(a.astype(BF16).astype(F32), jnp.uint32)
    ub = lax.bitcast_convert_type(b.astype(BF16).astype(F32), jnp.uint32)
    return (ua >> 16) | (ub & jnp.uint32(0xFFFF0000))


def unpack_bf16_pair(u):
    a = lax.bitcast_convert_type(u << 16, F32)
    b = lax.bitcast_convert_type(u & jnp.uint32(0xFFFF0000), F32)
    return a, b


def _ffn_kernel(ge_ref, gb_ref, gr_ref, x_ref, w1_ref, w3_ref, w2_ref, o_ref, acc_ref, *xb_scr, nf, gsz, packed):
    g = pl.program_id(0)
    f = pl.program_id(1)
    rows = gr_ref[g]
    valid = rows > 0
    d = acc_ref.shape[1]

    @pl.when(valid & (f == 0))
    def _():
        acc_ref[...] = jnp.zeros_like(acc_ref)
        if packed:
            a, b = unpack_bf16_pair(x_ref[...])
            xb_scr[0][:, 0:d // 2] = a.astype(BF16)
            xb_scr[0][:, d // 2:d] = b.astype(BF16)

    xsrc = xb_scr[0] if packed else x_ref

    def slab(r0, nrows, w1, w3, w2):
        xs = xsrc[pl.ds(r0, nrows), :]
        h1 = jnp.dot(xs, w1, preferred_element_type=F32)
        h3 = jnp.dot(xs, w3, preferred_element_type=F32)
        hh = (h1 * jax.nn.sigmoid(h1) * h3).astype(BF16)
        for c in range(d // FFN_NCHUNK):
            cs = slice(c * FFN_NCHUNK, (c + 1) * FFN_NCHUNK)
            acc_ref[pl.ds(r0, nrows), cs] += jnp.dot(hh, w2[:, cs], preferred_element_type=F32)

    full = rows > gsz - FFN_SUB

    @pl.when(full)
    def _():
        slab(0, gsz, w1_ref[...].astype(BF16), w3_ref[...].astype(BF16), w2_ref[...].astype(BF16))

    @pl.when(valid & jnp.logical_not(full))
    def _():
        nsub = (rows + FFN_SUB - 1) // FFN_SUB
        w1 = w1_ref[...].astype(BF16)
        w3 = w3_ref[...].astype(BF16)
        w2 = w2_ref[...].astype(BF16)

        def body(s, carry):
            slab(pl.multiple_of(s * FFN_SUB, FFN_SUB), FFN_SUB, w1, w3, w2)
            return carry

        lax.fori_loop(0, nsub, body, 0)

    @pl.when(jnp.logical_not(valid) & (f == 0))
    def _():
        o_ref[...] = jnp.zeros_like(o_ref)

    @pl.when(valid & (f == nf - 1))
    def _():
        if packed:
            o_ref[...] = pack_bf16_pair(acc_ref[:, 0:d // 2], acc_ref[:, d // 2:d])
        else:
            o_ref[...] = acc_ref[...].astype(o_ref.dtype)


def ffn_groups(xs, w1, w3, w2, grp_expert, grp_block, grp_rows, *, gsz=1024, tf=256, packed=False):
    e, d, ff = w1.shape
    r, dx = xs.shape
    nf = ff // tf
    ng = grp_expert.shape[0]

    def fmap(f, g, gr):
        return jnp.where(gr[g] > 0, f, nf - 1)

    scratch = [pltpu.VMEM((gsz, d), F32)]
    if packed:
        scratch.append(pltpu.VMEM((gsz, d), BF16))
    grid_spec = pltpu.PrefetchScalarGridSpec(
        num_scalar_prefetch=3,
        grid=(ng, nf),
        in_specs=[
            pl.BlockSpec((gsz, dx), lambda g, f, ge, gb, gr: (gb[g], 0)),
            pl.BlockSpec((None, d, tf), lambda g, f, ge, gb, gr: (ge[g], 0, fmap(f, g, gr))),
            pl.BlockSpec((None, d, tf), lambda g, f, ge, gb, gr: (ge[g], 0, fmap(f, g, gr))),
            pl.BlockSpec((None, tf, d), lambda g, f, ge, gb, gr: (ge[g], fmap(f, g, gr), 0)),
        ],
        out_specs=pl.BlockSpec((gsz, dx), lambda g, f, ge, gb, gr: (gb[g], 0)),
        scratch_shapes=scratch,
    )
    return pl.pallas_call(
        functools.partial(_ffn_kernel, nf=nf, gsz=gsz, packed=packed),
        out_shape=jax.ShapeDtypeStruct((r, dx), xs.dtype),
        grid_spec=grid_spec,
        compiler_params=_cparams("arbitrary", "arbitrary"),
        name="ffn_groups",
    )(grp_expert, grp_block, grp_rows, xs, w1, w3, w2)


def _row_copy(src_ref, src_row, dst_ref, dst_row, sem):
    return pltpu.make_async_copy(src_ref.at[pl.ds(src_row, 1), :], dst_ref.at[pl.ds(dst_row, 1), :], sem)


def _dispatch_kernel(pos_ref, h_ref, xs_in_ref, xs_ref, sem):
    del xs_in_ref
    tm = h_ref.shape[0]
    base = pl.program_id(0) * tm

    def issue(i, c):
        for k in range(TOP_K):
            _row_copy(h_ref, i, xs_ref, pos_ref[(base + i) * TOP_K + k], sem).start()
        return c

    lax.fori_loop(0, tm, issue, 0)

    def drain(i, c):
        for k in range(TOP_K):
            _row_copy(h_ref, i, xs_ref, pos_ref[(base + i) * TOP_K + k], sem).wait()
        return c

    lax.fori_loop(0, tm, drain, 0)


def moe_dispatch(hp, pos_flat, n_rows, tm=512):
    t, w = hp.shape
    xs0 = jnp.zeros((n_rows, w), hp.dtype)
    grid_spec = pltpu.PrefetchScalarGridSpec(
        num_scalar_prefetch=1,
        grid=(t // tm,),
        in_specs=[pl.BlockSpec((tm, w), lambda i, pos: (i, 0)), pl.BlockSpec(memory_space=pl.ANY)],
        out_specs=pl.BlockSpec(memory_space=pl.ANY),
        scratch_shapes=[pltpu.SemaphoreType.DMA],
    )
    return pl.pallas_call(
        _dispatch_kernel,
        out_shape=jax.ShapeDtypeStruct((n_rows, w), hp.dtype),
        grid_spec=grid_spec,
        input_output_aliases={2: 0},
        compiler_params=_cparams("arbitrary"),
        name="moe_dispatch",
    )(pos_flat, hp, xs0)


def _t5_bucket_np(dist):
    max_exact = REL_BUCKETS // 2
    d_f = np.maximum(dist, 1).astype(np.float32)
    large = max_exact + (np.log(d_f / np.float32(max_exact)) / np.float32(math.log(REL_MAX_DIST / max_exact))
                         * np.float32(REL_BUCKETS - max_exact)).astype(np.int32)
    large = np.minimum(large, REL_BUCKETS - 1)
    return np.where(dist < max_exact, dist, large)


RES = 4
TQ = ATTN_BLOCK
PIECE = TQ // RES


def _bucket_index_tables():
    assert DILATED_PATTERNS == ((128, 1), (512, 4), (2048, 16)) and RES == 4 and TQ == 128

    def fin(steps, dil, n_back=TQ):
        ok = (steps >= 0) & (steps <= n_back)
        return np.where(ok, _t5_bucket_np(np.clip(steps, 0, n_back) * dil), -1).astype(np.int32)

    i = np.arange(TQ)[:, None]
    t4 = fin(i + TQ - np.arange(2 * TQ)[None, :], 4)
    t16 = fin(i - np.arange(TQ)[None, :], 16)
    rq, mq = i // PIECE, i % PIECE
    jk = np.arange(2 * TQ)[None, :]
    rk, mk = jk // (2 * PIECE), jk % (2 * PIECE)
    t1 = fin(RES * (PIECE + mq - mk) + (rq - rk), 1)
    jk0 = np.arange(TQ)[None, :]
    rk0, mk0 = jk0 // PIECE, jk0 % PIECE
    t1first = fin(RES * (mq - mk0) + (rq - rk0), 1)
    return t4, t1, t1first, t16


def _bias_tab_kernel(rel_ref, *refs):
    n = len(refs) // 2
    h = pl.program_id(0)
    for idx_ref, o_ref in zip(refs[:n], refs[n:]):
        idx = idx_ref[...]
        tab = jnp.full(idx.shape, NEG_INF, F32)
        for b in range(REL_BUCKETS):
            tab = jnp.where(idx == b, rel_ref[b, h], tab)
        o_ref[0] = tab


def attn_bias_tables(rel_bias):
    idx_tabs = [jnp.asarray(t) for t in _bucket_index_tables()]
    nh = rel_bias.shape[1]
    return pl.pallas_call(
        _bias_tab_kernel,
        out_shape=tuple(jax.ShapeDtypeStruct((nh,) + t.shape, F32) for t in idx_tabs),
        grid=(nh,),
        in_specs=[pl.BlockSpec(memory_space=pltpu.SMEM)] + [pl.BlockSpec(t.shape, lambda h: (0, 0)) for t in idx_tabs],
        out_specs=tuple(pl.BlockSpec((1,) + t.shape, lambda h: (h, 0, 0)) for t in idx_tabs),
        compiler_params=_cparams("parallel"),
        name="attn_bias_tables",
    )(rel_bias.astype(F32), *idx_tabs)


def _attn_kernel(q_ref, k_ref, v_ref, t4_ref, t1_ref, t1f_ref, t16_ref, o_ref, acc, m_s, l_s, qs, ks, vs, *, scale):
    nres, lsub, hd = q_ref.shape
    nt = (((1,), (1,)), ((), ()))

    def tile(q, k, v, bias):
        s = lax.dot_general(q, k, nt, preferred_element_type=F32) * scale + bias
        m = jnp.max(s, axis=-1, keepdims=True)
        p = jnp.exp(s - m)
        l = jnp.sum(p, axis=-1, keepdims=True)
        o = jnp.dot(p.astype(BF16), v, preferred_element_type=F32)
        return o, jnp.broadcast_to(m, o.shape), jnp.broadcast_to(l, o.shape)

    def merge(r, rows, o, m, l):
        m0 = m_s[r, rows, :]
        mn = jnp.maximum(m0, m)
        a = jnp.exp(m0 - mn)
        b = jnp.exp(m - mn)
        acc[r, rows, :] = acc[r, rows, :] * a + o * b
        l_s[r, rows, :] = l_s[r, rows, :] * a + l * b
        m_s[r, rows, :] = mn

    for r in range(nres):
        for i in range(lsub // TQ):
            rows = pl.ds(i * TQ, TQ)
            if i == 0:
                o, m, l = tile(q_ref[r, rows, :], k_ref[r, rows, :], v_ref[r, rows, :], t4_ref[0, :, TQ:2 * TQ])
            else:
                krows = pl.ds((i - 1) * TQ, 2 * TQ)
                o, m, l = tile(q_ref[r, rows, :], k_ref[r, krows, :], v_ref[r, krows, :], t4_ref[0])
            acc[r, rows, :] = o
            m_s[r, rows, :] = m
            l_s[r, rows, :] = l

    for j in range(lsub // PIECE):
        q = jnp.concatenate([q_ref[r, pl.ds(j * PIECE, PIECE), :] for r in range(nres)], axis=0)
        if j == 0:
            krows, bias = pl.ds(0, PIECE), t1f_ref[0]
        else:
            krows, bias = pl.ds((j - 1) * PIECE, 2 * PIECE), t1_ref[0]
        k = jnp.concatenate([k_ref[r, krows, :] for r in range(nres)], axis=0)
        v = jnp.concatenate([v_ref[r, krows, :] for r in range(nres)], axis=0)
        o, m, l = tile(q, k, v, bias)
        for r in range(nres):
            part = slice(r * PIECE, (r + 1) * PIECE)
            merge(r, pl.ds(j * PIECE, PIECE), o[part], m[part], l[part])

    for r in range(nres):
        qs[...] = q_ref[r].astype(F32)
        ks[...] = k_ref[r].astype(F32)
        vs[...] = v_ref[r].astype(F32)
        for c in range(lsub // TQ):
            rows = pl.ds(c, TQ, stride=lsub // TQ)
            o, m, l = tile(qs[rows, :].astype(BF16), ks[rows, :].astype(BF16), vs[rows, :].astype(BF16), t16_ref[0])
            merge(r, rows, o, m, l)

    for r in range(nres):
        o_ref[r] = (acc[r] / l_s[r]).astype(o_ref.dtype)


def attention(qkv, tabs, *, bsz, seq, hd):
    dm = ATTN_HEADS * hd
    lsub = seq // RES
    assert hd == V7X_LANES and lsub // TQ == 16 // RES and seq // 16 == TQ
    view = qkv.reshape(bsz * RES, lsub, 3 * dm)
    blk = lambda which: pl.BlockSpec((RES, lsub, hd), lambda b, h: (b, 0, which * ATTN_HEADS + h))
    tab = lambda t: pl.BlockSpec((1,) + t.shape[1:], lambda b, h: (h, 0, 0))
    st = pltpu.VMEM((RES, lsub, hd), F32)
    row = pltpu.VMEM((lsub, hd), F32)
    o = pl.pallas_call(
        functools.partial(_attn_kernel, scale=hd ** -0.5),
        out_shape=jax.ShapeDtypeStruct((bsz * RES, lsub, dm), BF16),
        grid=(bsz, ATTN_HEADS),
        in_specs=[blk(0), blk(1), blk(2)] + [tab(t) for t in tabs],
        out_specs=pl.BlockSpec((RES, lsub, hd), lambda b, h: (b, 0, h)),
        scratch_shapes=[st, st, st, row, row, row],
        compiler_params=_cparams("parallel", "arbitrary"),
        name="attention",
    )(view, view, view, *tabs)
    return o.reshape(bsz * seq, dm)


def _split_bf16(x):
    hi = x.astype(BF16)
    lo = (x - hi.astype(F32)).astype(BF16)
    return hi, lo


def _moe_route_kernel(x_ref, g_ref, wr_ref, h_ref, meta_ref, cnt_ref, carry_ref):
    step = pl.program_id(0)
    tm = x_ref.shape[0]

    @pl.when(step == 0)
    def _():
        carry_ref[...] = jnp.zeros_like(carry_ref)

    h = _rms(x_ref[...], g_ref[...])
    h_hi, h_lo = _split_bf16(h)
    dh = h.shape[1] // 2
    h_ref[...] = pack_bf16_pair(h[:, 0:dh], h[:, dh:2 * dh])
    w_hi, w_lo = _split_bf16(wr_ref[...])
    logits = (jnp.dot(h_hi, w_hi, preferred_element_type=F32) + jnp.dot(h_lo, w_hi, preferred_element_type=F32)
              + jnp.dot(h_hi, w_lo, preferred_element_type=F32))
    lane = lax.broadcasted_iota(jnp.int32, (tm, V7X_LANES), 1).astype(F32)
    lg = jnp.where(lane < N_EXPERTS, logits, NEG_INF)
    m1 = jnp.max(lg, axis=-1, keepdims=True)
    i1 = jnp.min(jnp.where(lg == m1, lane, float(V7X_LANES)), axis=-1, keepdims=True)
    lg2 = jnp.where(lane == i1, NEG_INF, lg)
    m2 = jnp.max(lg2, axis=-1, keepdims=True)
    i2 = jnp.min(jnp.where(lg2 == m2, lane, float(V7X_LANES)), axis=-1, keepdims=True)
    e2 = jnp.exp(m2 - m1)
    g1 = 1.0 / (1.0 + e2)
    g2 = e2 / (1.0 + e2)
    oh = ((lane == i1) | (lane == i2)).astype(BF16)
    r_i = lax.broadcasted_iota(jnp.int32, (tm, tm), 0)
    c_i = lax.broadcasted_iota(jnp.int32, (tm, tm), 1)
    tri = (c_i < r_i).astype(BF16)
    rank = jnp.dot(tri, oh, preferred_element_type=F32) + carry_ref[0:1, :]
    r1 = jnp.sum(jnp.where(lane == i1, rank, 0.0), axis=-1, keepdims=True)
    r2 = jnp.sum(jnp.where(lane == i2, rank, 0.0), axis=-1, keepdims=True)
    carry_ref[0:1, :] = carry_ref[0:1, :] + jnp.sum(oh.astype(F32), axis=0, keepdims=True)
    meta = jnp.where(lane == 0, i1.astype(F32), 0.0)
    meta = jnp.where(lane == 1, i2.astype(F32), meta)
    meta = jnp.where(lane == 2, g1, meta)
    meta = jnp.where(lane == 3, g2, meta)
    meta = jnp.where(lane == 4, r1, meta)
    meta = jnp.where(lane == 5, r2, meta)
    meta_ref[...] = meta
    cnt_ref[...] = jnp.broadcast_to(carry_ref[0:1, :], cnt_ref.shape)


def moe_route(x, g, w_router_p, tm=256):
    t, d = x.shape
    return pl.pallas_call(
        _moe_route_kernel,
        out_shape=(jax.ShapeDtypeStruct((t, d // 2), jnp.uint32), jax.ShapeDtypeStruct((t, V7X_LANES), F32),
                   jax.ShapeDtypeStruct((8, V7X_LANES), F32)),
        grid=(t // tm,),
        in_specs=[pl.BlockSpec((tm, d), lambda i: (i, 0)), pl.BlockSpec((1, d), lambda i: (0, 0)),
                  pl.BlockSpec((d, V7X_LANES), lambda i: (0, 0))],
        out_specs=(pl.BlockSpec((tm, d // 2), lambda i: (i, 0)), pl.BlockSpec((tm, V7X_LANES), lambda i: (i, 0)),
                   pl.BlockSpec((8, V7X_LANES), lambda i: (0, 0))),
        scratch_shapes=[pltpu.VMEM((8, V7X_LANES), F32)],
        compiler_params=_cparams("arbitrary"),
        name="moe_route",
    )(x, g.reshape(1, d), w_router_p)


def _moe_final_kernel(pos_ref, x_ref, meta_ref, g_ref, ys_ref, o_ref, ybuf, o_scr, sem, *, lsub):
    nres, tp, d = x_ref.shape
    dh = d // 2
    b = pl.program_id(0)
    j = pl.program_id(1)

    def token(i):
        r = i // tp
        return (b * nres + r) * lsub + j * tp + (i - r * tp)

    def issue(i, c):
        for k in range(TOP_K):
            _row_copy(ys_ref, pos_ref[token(i) * TOP_K + k], ybuf.at[k], i, sem).start()
        return c

    lax.fori_loop(0, nres * tp, issue, 0)

    def drain(i, c):
        for k in range(TOP_K):
            _row_copy(ys_ref, pos_ref[token(i) * TOP_K + k], ybuf.at[k], i, sem).wait()
        return c

    lax.fori_loop(0, nres * tp, drain, 0)
    for r in range(nres):
        rows = slice(r * tp, (r + 1) * tp)
        meta = meta_ref[r]
        g1 = meta[:, 2:3]
        g2 = meta[:, 3:4]
        a0, b0 = unpack_bf16_pair(ybuf[0, rows, :])
        a1, b1 = unpack_bf16_pair(ybuf[1, rows, :])
        x = x_ref[r]
        xa = x[:, 0:dh] + a0 * g1 + a1 * g2
        xb = x[:, dh:d] + b0 * g1 + b1 * g2
        ms = (jnp.sum(xa * xa, axis=-1, keepdims=True) + jnp.sum(xb * xb, axis=-1, keepdims=True)) * (1.0 / d)
        inv = lax.rsqrt(ms + EPS)
        ya = xa * inv * g_ref[:, 0:dh]
        yb = xb * inv * g_ref[:, dh:d]
        out_rows = pl.ds(r, tp, stride=nres)
        for s in range(dh // V7X_LANES):
            ls = slice(s * V7X_LANES, (s + 1) * V7X_LANES)
            o_scr[s, out_rows, :] = ya[:, ls]
            o_scr[dh // V7X_LANES + s, out_rows, :] = yb[:, ls]
    for s in range(d // V7X_LANES):
        o_ref[:, s * V7X_LANES:(s + 1) * V7X_LANES] = o_scr[s]


def moe_final(x, ys, pos_flat, meta, g, *, bsz, seq, tp=128):
    t, d = x.shape
    lsub = seq // RES
    tm = RES * tp
    xv = x.reshape(bsz * RES, lsub, d)
    mv = meta.reshape(bsz * RES, lsub, V7X_LANES)
    grid_spec = pltpu.PrefetchScalarGridSpec(
        num_scalar_prefetch=1,
        grid=(bsz, lsub // tp),
        in_specs=[pl.BlockSpec((RES, tp, d), lambda b, j, pos: (b, j, 0)),
                  pl.BlockSpec((RES, tp, V7X_LANES), lambda b, j, pos: (b, j, 0)),
                  pl.BlockSpec((1, d), lambda b, j, pos: (0, 0)), pl.BlockSpec(memory_space=pl.ANY)],
        out_specs=pl.BlockSpec((tm, d), lambda b, j, pos: (b * (lsub // tp) + j, 0)),
        scratch_shapes=[pltpu.VMEM((TOP_K, tm, d // 2), jnp.uint32), pltpu.VMEM((d // V7X_LANES, tm, V7X_LANES), F32),
                        pltpu.SemaphoreType.DMA],
    )
    return pl.pallas_call(
        functools.partial(_moe_final_kernel, lsub=lsub),
        out_shape=jax.ShapeDtypeStruct((t, d), F32),
        grid_spec=grid_spec,
        compiler_params=_cparams("arbitrary", "arbitrary"),
        name="moe_final",
    )(pos_flat, xv, mv, g.reshape(1, d), ys)


def _pad_lanes(v, fill=0.0):
    v = v.reshape(1, -1).astype(F32)
    return jnp.pad(v, ((0, 0), (0, V7X_LANES - v.shape[1])), constant_values=fill)


def even_layer(x, h, p, *, bsz, seq):
    t, d = x.shape
    d_ssd = d
    d_conf = d
    heads = d_ssd // SSD_HEAD_DIM
    conv_dim = d_ssd + 2 * SSD_GROUPS * SSD_STATE
    i1 = d_ssd + conv_dim
    i2 = i1 + heads
    w_in = p["w_in"]
    zx = matmul([h], w_in, n_cols=i1, tn=1024, name="in_proj_zx")
    w_dt = jnp.pad(w_in[:, i1:i2], ((0, 0), (0, V7X_LANES - heads)))
    dt_raw = matmul([h], w_dt, n_cols=V7X_LANES, tn=V7X_LANES, out_dtype=F32, name="in_proj_dt")
    conf = matmul([h], w_in[:, i2:], n_cols=2 * d_conf, tn=1024, name="in_proj_conf")
    xc = ssd_conv(zx, p["conv_w"], p["conv_b"], bsz=bsz, seq=seq, col_off=d_ssd)
    dt, acs, dtt, acst = ssd_prep(dt_raw, _pad_lanes(p["dt_bias"]), _pad_lanes(p["a_log"]), bsz=bsz, seq=seq)
    dskip_e = jnp.repeat(p["d_skip"].astype(F32), SSD_HEAD_DIM).reshape(1, d_ssd)
    y_ssd = ssd_main(xc, zx, dt, acs, dtt, acst, dskip_e, p["ssd_norm_w"], bsz=bsz, seq=seq, d_ssd=d_ssd)
    u = conf_module(conf, p["conf_dw_w"], p["conf_dw_b"], p["conf_ln_w"], p["conf_ln_b"], bsz=bsz, seq=seq)
    return matmul([y_ssd, u], p["w_out"], n_cols=d, tn=512, res=x, out_dtype=F32, name="even_out_proj")


def dense_ffn(h, w1, w3, w2, gsz=1024):
    t = h.shape[0]
    ng = t // gsz
    ge = jnp.zeros((ng,), jnp.int32)
    gb = jnp.arange(ng, dtype=jnp.int32)
    gr = jnp.full((ng,), gsz, jnp.int32)
    return ffn_groups(h, w1[None], w3[None], w2[None], ge, gb, gr, gsz=gsz)


def odd_layer_attn(x, h, p, *, bsz, seq):
    t, d = x.shape
    hd = d // ATTN_HEADS
    qkv = matmul([h], p["w_qkv"], n_cols=3 * d, tn=1024, name="qkv_proj")
    tabs = attn_bias_tables(p["rel_bias"])
    o = attention(qkv, tabs, bsz=bsz, seq=seq, hd=hd)
    return matmul([o], p["w_attn_out"], n_cols=d, tn=1024, res=x, out_dtype=F32, name="attn_out_proj")


def moe_layer(x, norm_g, w_router, w1, w3, w2, final_g, *, bsz, seq, gsz=1024):
    t, d = x.shape
    ne = w1.shape[0]
    w_router_p = jnp.pad(w_router.astype(F32), ((0, 0), (0, V7X_LANES - ne)))
    h, meta, cnt = moe_route(x, norm_g, w_router_p)
    counts = cnt[0, :ne].astype(jnp.int32)
    ngrp_e = (counts + gsz - 1) // gsz
    grp_end = jnp.cumsum(ngrp_e)
    grp_start = grp_end - ngrp_e
    row_off = grp_start * gsz
    ng = (t * TOP_K) // gsz + ne
    j = jnp.arange(ng, dtype=jnp.int32)
    total = grp_end[-1]
    last = jnp.maximum(total - 1, 0)
    jj = jnp.minimum(j, last)
    ge = jnp.searchsorted(grp_end, jj, side="right").astype(jnp.int32)
    ge = jnp.minimum(ge, ne - 1)
    rows = jnp.clip(counts[ge] - (jj - grp_start[ge]) * gsz, 0, gsz)
    gr = jnp.where(j < total, rows, 0).astype(jnp.int32)
    gb = j
    e_idx = meta[:, 0:2].astype(jnp.int32)
    pos = (row_off[e_idx] + meta[:, 4:6].astype(jnp.int32)).reshape(-1)
    xs = moe_dispatch(h, pos, ng * gsz)
    ys = ffn_groups(xs, w1, w3, w2, ge, gb, gr, gsz=gsz, packed=True)
    return moe_final(x, ys, pos, meta, final_g, bsz=bsz, seq=seq)


def kernel(x, norm_mix, norm_ffn, norm_final, even_w_in, ssd_conv_w, ssd_conv_b, ssd_dt_bias, ssd_a_log, ssd_d, ssd_norm_w, conf_dw_w, conf_dw_b, conf_ln_w, conf_ln_b, even_w_out, ffn_w1, ffn_w3, ffn_w2, attn_w_qkv, attn_w_out, rel_bias, moe_router, moe_w1, moe_w3, moe_w2):
    bsz, seq, d = x.shape
    assert norm_mix.shape[0] == 2, "two-layer trunk: one even (SSD+Conformer/FFN) and one odd (attention/MoE) layer"
    xf = x.reshape(bsz * seq, d)
    h = rmsnorm(xf, norm_mix[0])
    p_even = dict(w_in=even_w_in[0], conv_w=ssd_conv_w[0], conv_b=ssd_conv_b[0], dt_bias=ssd_dt_bias[0],
                  a_log=ssd_a_log[0], d_skip=ssd_d[0], ssd_norm_w=ssd_norm_w[0], conf_dw_w=conf_dw_w[0],
                  conf_dw_b=conf_dw_b[0], conf_ln_w=conf_ln_w[0], conf_ln_b=conf_ln_b[0], w_out=even_w_out[0])
    x1 = even_layer(xf, h, p_even, bsz=bsz, seq=seq)
    h1 = rmsnorm(x1, norm_ffn[0])
    y_ffn = dense_ffn(h1, ffn_w1[0], ffn_w3[0], ffn_w2[0])
    x2, h2 = add_norm_to_residue_major(x1, y_ffn, norm_mix[1], bsz=bsz, seq=seq, nres=RES)
    p_odd = dict(w_qkv=attn_w_qkv[0], w_attn_out=attn_w_out[0], rel_bias=rel_bias)
    x3 = odd_layer_attn(x2, h2, p_odd, bsz=bsz, seq=seq)
    out = moe_layer(x3, norm_ffn[1], moe_router[0], moe_w1[0], moe_w3[0], moe_w2[0], norm_final, bsz=bsz, seq=seq)
    return out.reshape(bsz, seq, d)
```

```python
import functools
import math

import numpy as np
import jax
import jax.numpy as jnp
from jax import lax
from jax.experimental import pallas as pl
from jax.experimental.pallas import tpu as pltpu

SSD_HEAD_DIM = 64
SSD_GROUPS = 4
SSD_STATE = 128
SSD_CONV = 4
SSD_CHUNK = 128
CONF_KERNEL = 31
ATTN_HEADS = 16
DILATED_PATTERNS = ((128, 1), (512, 4), (2048, 16))
ATTN_BLOCK = 128
REL_BUCKETS = 32
REL_MAX_DIST = 2048
N_EXPERTS = 8
TOP_K = 2
EPS = 1e-6

V7X_LANES = 128
V7X_VMEM_BYTES = 64 * 1024 * 1024
VMEM_LIMIT = 56 * 1024 * 1024

F32 = jnp.float32
BF16 = jnp.bfloat16
NEG_INF = float("-inf")


def _cparams(*sem):
    return pltpu.CompilerParams(dimension_semantics=tuple(sem), vmem_limit_bytes=VMEM_LIMIT)


def _rms(x, g):
    ms = jnp.mean(x * x, axis=-1, keepdims=True)
    return x * lax.rsqrt(ms + EPS) * g


def _rmsnorm_kernel(x_ref, g_ref, h_ref):
    h_ref[...] = _rms(x_ref[...], g_ref[...]).astype(h_ref.dtype)


def rmsnorm(x, g, out_dtype=BF16, tm=512):
    t, d = x.shape
    return pl.pallas_call(
        _rmsnorm_kernel,
        out_shape=jax.ShapeDtypeStruct((t, d), out_dtype),
        grid=(t // tm,),
        in_specs=[pl.BlockSpec((tm, d), lambda i: (i, 0)), pl.BlockSpec((1, d), lambda i: (0, 0))],
        out_specs=pl.BlockSpec((tm, d), lambda i: (i, 0)),
        compiler_params=_cparams("parallel"),
        name="rmsnorm",
    )(x, g.reshape(1, d))


def _add_norm_res_kernel(x_ref, y_ref, g_ref, xo_ref, ho_ref, xs_scr, hs_scr):
    nres, tp, d = xo_ref.shape
    xn = x_ref[...] + y_ref[...].astype(F32)
    h = _rms(xn, g_ref[...])
    for s in range(d // V7X_LANES):
        ls = slice(s * V7X_LANES, (s + 1) * V7X_LANES)
        xs_scr[s] = xn[:, ls]
        hs_scr[s] = h[:, ls]
    for r in range(nres):
        rows = pl.ds(r, tp, stride=nres)
        for s in range(d // V7X_LANES):
            ls = slice(s * V7X_LANES, (s + 1) * V7X_LANES)
            xo_ref[r, :, ls] = xs_scr[s, rows, :]
            ho_ref[r, :, ls] = hs_scr[s, rows, :].astype(ho_ref.dtype)


def add_norm_to_residue_major(x, y, g, *, bsz, seq, nres, tp=128):
    t, d = x.shape
    lsub = seq // nres
    tm = nres * tp
    nj = lsub // tp
    slab = pltpu.VMEM((d // V7X_LANES, tm, V7X_LANES), F32)
    xo, ho = pl.pallas_call(
        _add_norm_res_kernel,
        out_shape=(jax.ShapeDtypeStruct((bsz * nres, lsub, d), F32), jax.ShapeDtypeStruct((bsz * nres, lsub, d), BF16)),
        grid=(bsz, nj),
        in_specs=[pl.BlockSpec((tm, d), lambda b, j: (b * nj + j, 0)), pl.BlockSpec((tm, d), lambda b, j: (b * nj + j, 0)),
                  pl.BlockSpec((1, d), lambda b, j: (0, 0))],
        out_specs=(pl.BlockSpec((nres, tp, d), lambda b, j: (b, j, 0)), pl.BlockSpec((nres, tp, d), lambda b, j: (b, j, 0))),
        scratch_shapes=[slab, slab],
        compiler_params=_cparams("parallel", "parallel"),
        name="add_norm_reorder",
    )(x, y, g.reshape(1, d))
    return xo.reshape(t, d), ho.reshape(t, d)


def _mm_kernel(*refs, n_lhs, ks, has_res):
    xs = refs[:n_lhs]
    w_ref = refs[n_lhs]
    r_ref = refs[n_lhs + 1] if has_res else None
    o_ref = refs[-1]
    acc = None
    k0 = 0
    for x_ref, k in zip(xs, ks):
        part = jnp.dot(x_ref[...], w_ref[k0:k0 + k, :].astype(BF16), preferred_element_type=F32)
        acc = part if acc is None else acc + part
        k0 += k
    if has_res:
        acc = acc + r_ref[...]
    o_ref[...] = acc.astype(o_ref.dtype)


def matmul(xs, w, *, n_cols, col_block_off=0, tn, tm=1024, res=None, out_dtype=BF16, name="matmul"):
    t = xs[0].shape[0]
    ks = tuple(x.shape[1] for x in xs)
    ktot = sum(ks)
    assert w.shape[0] == ktot and n_cols % tn == 0 and t % tm == 0
    in_specs = [pl.BlockSpec((tm, k), lambda i, j: (i, 0)) for k in ks]
    in_specs.append(pl.BlockSpec((ktot, tn), lambda i, j: (0, j + col_block_off)))
    args = list(xs) + [w]
    if res is not None:
        in_specs.append(pl.BlockSpec((tm, tn), lambda i, j: (i, j)))
        args.append(res)
    return pl.pallas_call(
        functools.partial(_mm_kernel, n_lhs=len(xs), ks=ks, has_res=res is not None),
        out_shape=jax.ShapeDtypeStruct((t, n_cols), out_dtype),
        grid=(t // tm, n_cols // tn),
        in_specs=in_specs,
        out_specs=pl.BlockSpec((tm, tn), lambda i, j: (i, j)),
        compiler_params=_cparams("parallel", "arbitrary"),
        name=name,
    )(*args)


def _mm_nt_kernel(x_ref, wt_ref, o_ref):
    wt = wt_ref[...].astype(BF16)
    acc = lax.dot_general(x_ref[...], wt, (((1,), (1,)), ((), ())), preferred_element_type=F32)
    o_ref[...] = acc.astype(o_ref.dtype)


def matmul_nt(x, w_t, *, row_start, n_cols, tn, tm=1024, out_dtype=BF16, name="matmul_nt"):
    t, k = x.shape
    assert w_t.shape[1] == k and n_cols % tn == 0 and t % tm == 0
    if row_start % tn == 0:
        w_spec = pl.BlockSpec((tn, k), lambda i, j: (row_start // tn + j, 0))
    else:
        assert row_start % SUBLANES == 0 and tn % SUBLANES == 0
        w_spec = pl.BlockSpec((pl.Element(tn), pl.Element(k)),
                              lambda i, j: ((row_start // SUBLANES + j * (tn // SUBLANES)) * SUBLANES, 0))
    return pl.pallas_call(
        _mm_nt_kernel,
        out_shape=jax.ShapeDtypeStruct((t, n_cols), out_dtype),
        grid=(t // tm, n_cols // tn),
        in_specs=[pl.BlockSpec((tm, k), lambda i, j: (i, 0)), w_spec],
        out_specs=pl.BlockSpec((tm, tn), lambda i, j: (i, j)),
        compiler_params=_cparams("parallel", "arbitrary"),
        name=name,
    )(x, w_t)


HALO_BF16 = 16


def _ssd_conv_kernel(cur_ref, halo_ref, w_ref, b_ref, o_ref, buf_ref, *, ts):
    i = pl.program_id(1)
    halo = halo_ref[...].astype(F32)
    buf_ref[0:HALO_BF16, :] = jnp.where(i > 0, halo, 0.0)
    buf_ref[HALO_BF16:HALO_BF16 + ts, :] = cur_ref[...].astype(F32)
    acc = b_ref[...]
    for k in range(SSD_CONV):
        off = HALO_BF16 - (SSD_CONV - 1) + k
        acc = acc + w_ref[k:k + 1, :] * buf_ref[off:off + ts, :]
    o_ref[...] = (acc * jax.nn.sigmoid(acc)).astype(o_ref.dtype)


def ssd_conv(zx, conv_w, conv_b, *, bsz, seq, col_off, ts=512, tc=512):
    c = conv_w.shape[1]
    t = zx.shape[0]
    nsb = seq // ts
    cb0 = col_off // tc
    hb = ts // HALO_BF16
    return pl.pallas_call(
        functools.partial(_ssd_conv_kernel, ts=ts),
        out_shape=jax.ShapeDtypeStruct((t, c), BF16),
        grid=(bsz, nsb, c // tc),
        in_specs=[
            pl.BlockSpec((ts, tc), lambda b, i, j: (b * nsb + i, cb0 + j)),
            pl.BlockSpec((HALO_BF16, tc), lambda b, i, j: (jnp.maximum((b * nsb + i) * hb - 1, 0), cb0 + j)),
            pl.BlockSpec((SSD_CONV, tc), lambda b, i, j: (0, j)),
            pl.BlockSpec((1, tc), lambda b, i, j: (0, j)),
        ],
        out_specs=pl.BlockSpec((ts, tc), lambda b, i, j: (b * nsb + i, j)),
        scratch_shapes=[pltpu.VMEM((ts + HALO_BF16, tc), F32)],
        compiler_params=_cparams("parallel", "parallel", "parallel"),
        name="ssd_conv",
    )(zx, zx, conv_w, conv_b.reshape(1, c))


def _ssd_prep_kernel(raw_ref, bias_ref, alog_ref, dt_ref, acs_ref, dtt_ref, acst_ref, *, seq):
    raw = raw_ref[...] + bias_ref[...]
    dt = jnp.maximum(raw, 0.0) + jnp.log1p(jnp.exp(-jnp.abs(raw)))
    a = -jnp.exp(alog_ref[...])
    x = dt * a
    row = lax.broadcasted_iota(jnp.int32, x.shape, 0) % SSD_CHUNK
    sh = 1
    while sh < SSD_CHUNK:
        x = x + jnp.where(row >= sh, pltpu.roll(x, sh, 0), 0.0)
        sh *= 2
    dt_ref[...] = dt
    acs_ref[...] = x
    for c in range(seq // SSD_CHUNK):
        sl = slice(c * SSD_CHUNK, (c + 1) * SSD_CHUNK)
        dtt_ref[sl, :] = dt[sl, :].T
        acst_ref[sl, :] = x[sl, :].T


def ssd_prep(dt_raw, dt_bias_p, a_log_p, *, bsz, seq):
    t = dt_raw.shape[0]
    shp = jax.ShapeDtypeStruct((t, V7X_LANES), F32)
    blk = pl.BlockSpec((seq, V7X_LANES), lambda b: (b, 0))
    vec = pl.BlockSpec((1, V7X_LANES), lambda b: (0, 0))
    return pl.pallas_call(
        functools.partial(_ssd_prep_kernel, seq=seq),
        out_shape=(shp, shp, shp, shp),
        grid=(bsz,),
        in_specs=[blk, vec, vec],
        out_specs=(blk, blk, blk, blk),
        compiler_params=_cparams("parallel"),
        name="ssd_prep",
    )(dt_raw, dt_bias_p, a_log_p)


def _ssd_main_kernel(x_ref, b_ref, c_ref, z_ref, dt_ref, acs_ref, dtt_ref, acst_ref, dsk_ref, nw_ref,
                     y_ref, st_ref, *, heads_per_group):
    ci = pl.program_id(1)
    L = SSD_CHUNK
    gw = heads_per_group * SSD_HEAD_DIM
    npair = heads_per_group // 2

    @pl.when(ci == 0)
    def _():
        st_ref[...] = jnp.zeros_like(st_ref)

    dt = dt_ref[...]
    acs = acs_ref[...]
    dtt = dtt_ref[...]
    acst = acst_ref[...]
    row = lax.broadcasted_iota(jnp.int32, (L, L), 0)
    col = lax.broadcasted_iota(jnp.int32, (L, L), 1)
    causal = col <= row
    lane_lo = lax.broadcasted_iota(jnp.int32, (L, 2 * SSD_HEAD_DIM), 1) < SSD_HEAD_DIM
    lane_lo1 = lax.broadcasted_iota(jnp.int32, (1, 2 * SSD_HEAD_DIM), 1) < SSD_HEAD_DIM
    zero_b = jnp.zeros((L, 2 * SSD_HEAD_DIM), BF16)

    for g in range(SSD_GROUPS):
        bm = b_ref[:, g * SSD_STATE:(g + 1) * SSD_STATE]
        cm = c_ref[:, g * SSD_STATE:(g + 1) * SSD_STATE]
        bmt = bm.astype(F32).T.astype(BF16)
        cb = lax.dot_general(cm, bm, (((1,), (1,)), ((), ())), preferred_element_type=F32)
        ys = []
        for q in range(npair):
            c0 = g * gw + q * 2 * SSD_HEAD_DIM
            xp = x_ref[:, c0:c0 + 2 * SSD_HEAD_DIM]
            xpf = xp.astype(F32)
            ms = []
            for hh in range(2):
                h = g * heads_per_group + 2 * q + hh
                seg = acs[:, h:h + 1] - acst[h:h + 1, :]
                dec = jnp.exp(jnp.where(causal, seg, NEG_INF))
                ms.append((cb * dec * dtt[h:h + 1, :]).astype(BF16))
            h0 = g * heads_per_group + 2 * q
            h1 = h0 + 1
            lhs = jnp.concatenate(ms, axis=1)
            rhs = jnp.concatenate([jnp.where(lane_lo, xp, zero_b), jnp.where(lane_lo, zero_b, xp)], axis=0)
            y_diag = jnp.dot(lhs, rhs, preferred_element_type=F32)
            st = st_ref[g, :, q * 2 * SSD_HEAD_DIM:(q + 1) * 2 * SSD_HEAD_DIM]
            e_l = jnp.where(lane_lo, jnp.exp(acs[:, h0:h0 + 1]), jnp.exp(acs[:, h1:h1 + 1]))
            y_off = jnp.dot(cm, st.astype(BF16), preferred_element_type=F32) * e_l
            dsk = dsk_ref[:, c0:c0 + 2 * SSD_HEAD_DIM]
            ys.append(y_diag + y_off + xpf * dsk)
            last0 = acs[L - 1:L, h0:h0 + 1]
            last1 = acs[L - 1:L, h1:h1 + 1]
            w_l = jnp.where(lane_lo, dt[:, h0:h0 + 1] * jnp.exp(last0 - acs[:, h0:h0 + 1]),
                            dt[:, h1:h1 + 1] * jnp.exp(last1 - acs[:, h1:h1 + 1]))
            xw = (xpf * w_l).astype(BF16)
            st_new = jnp.dot(bmt, xw, preferred_element_type=F32)
            cd = jnp.where(lane_lo1, jnp.exp(last0), jnp.exp(last1))
            st_ref[g, :, q * 2 * SSD_HEAD_DIM:(q + 1) * 2 * SSD_HEAD_DIM] = st * cd + st_new
        y = jnp.concatenate(ys, axis=1)
        zg = z_ref[:, g * gw:(g + 1) * gw].astype(F32)
        yg = y * (zg * jax.nn.sigmoid(zg))
        ms_ = jnp.mean(yg * yg, axis=-1, keepdims=True)
        y_ref[:, g * gw:(g + 1) * gw] = (yg * lax.rsqrt(ms_ + EPS) * nw_ref[:, g * gw:(g + 1) * gw]).astype(y_ref.dtype)


def ssd_main(xc, zx, dt, acs, dtt, acst, dskip_e, norm_w, *, bsz, seq, d_ssd):
    t = xc.shape[0]
    nc = seq // SSD_CHUNK
    heads = d_ssd // SSD_HEAD_DIM
    hpg = heads // SSD_GROUPS
    bc_w = SSD_GROUPS * SSD_STATE
    assert d_ssd % bc_w == 0 and hpg % 2 == 0
    rowmap = lambda b, c: (b * nc + c, 0)
    return pl.pallas_call(
        functools.partial(_ssd_main_kernel, heads_per_group=hpg),
        out_shape=jax.ShapeDtypeStruct((t, d_ssd), BF16),
        grid=(bsz, nc),
        in_specs=[
            pl.BlockSpec((SSD_CHUNK, d_ssd), rowmap),
            pl.BlockSpec((SSD_CHUNK, bc_w), lambda b, c: (b * nc + c, d_ssd // bc_w)),
            pl.BlockSpec((SSD_CHUNK, bc_w), lambda b, c: (b * nc + c, d_ssd // bc_w + 1)),
            pl.BlockSpec((SSD_CHUNK, d_ssd), rowmap),
            pl.BlockSpec((SSD_CHUNK, V7X_LANES), rowmap),
            pl.BlockSpec((SSD_CHUNK, V7X_LANES), rowmap),
            pl.BlockSpec((SSD_CHUNK, V7X_LANES), rowmap),
            pl.BlockSpec((SSD_CHUNK, V7X_LANES), rowmap),
            pl.BlockSpec((1, d_ssd), lambda b, c: (0, 0)),
            pl.BlockSpec((1, d_ssd), lambda b, c: (0, 0)),
        ],
        out_specs=pl.BlockSpec((SSD_CHUNK, d_ssd), rowmap),
        scratch_shapes=[pltpu.VMEM((SSD_GROUPS, SSD_STATE, hpg * SSD_HEAD_DIM), F32)],
        compiler_params=_cparams("parallel", "arbitrary"),
        name="ssd_main",
    )(xc, xc, xc, zx, dt, acs, dtt, acst, dskip_e, norm_w.reshape(1, d_ssd))


CONF_HALO = 32


SUBLANES = 8


def _conf_kernel(a_ref, g_ref, ah_ref, gh_ref, w_ref, b_ref, lw_ref, lb_ref, o_ref, buf_ref, sh_ref, *, ts, rc, lt):
    i = pl.program_id(1)
    d = a_ref.shape[1]
    ah = ah_ref[...].astype(F32)
    gh = gh_ref[...].astype(F32)
    buf_ref[0:CONF_HALO, :] = jnp.where(i > 0, ah * jax.nn.sigmoid(gh), 0.0)
    a = a_ref[...].astype(F32)
    g = g_ref[...].astype(F32)
    buf_ref[CONF_HALO:CONF_HALO + ts, :] = a * jax.nn.sigmoid(g)
    span = ts + CONF_HALO - SUBLANES
    for j in range(1, SUBLANES):
        sh_ref[j - 1, 0:span, :] = buf_ref[j:j + span, :]
    base = CONF_HALO - (CONF_KERNEL - 1)
    outs = []
    for r in range(ts // rc):
        cols = []
        for c in range(d // lt):
            ls = slice(c * lt, (c + 1) * lt)
            acc = jnp.broadcast_to(b_ref[:, ls], (rc, lt))
            for k in range(CONF_KERNEL):
                j = (base + k) % SUBLANES
                off = r * rc + (base + k) - j
                src = buf_ref[off:off + rc, ls] if j == 0 else sh_ref[j - 1, off:off + rc, ls]
                acc = acc + w_ref[k:k + 1, ls] * src
            cols.append(acc)
        outs.append(jnp.concatenate(cols, axis=1))
    u = jnp.concatenate(outs, axis=0)
    mu = jnp.mean(u, axis=-1, keepdims=True)
    uc = u - mu
    var = jnp.mean(uc * uc, axis=-1, keepdims=True)
    y = uc * lax.rsqrt(var + EPS) * lw_ref[...] + lb_ref[...]
    o_ref[...] = (y * jax.nn.sigmoid(y)).astype(o_ref.dtype)


def conf_module(conf, dw_w, dw_b, ln_w, ln_b, *, bsz, seq, ts=128, rc=32, lt=512):
    t = conf.shape[0]
    d = dw_w.shape[1]
    nsb = seq // ts
    hb = ts // CONF_HALO
    cur = lambda col: pl.BlockSpec((ts, d), lambda b, i: (b * nsb + i, col))
    halo = lambda col: pl.BlockSpec((CONF_HALO, d), lambda b, i: (jnp.maximum((b * nsb + i) * hb - 1, 0), col))
    vec = pl.BlockSpec((1, d), lambda b, i: (0, 0))
    return pl.pallas_call(
        functools.partial(_conf_kernel, ts=ts, rc=rc, lt=lt),
        out_shape=jax.ShapeDtypeStruct((t, d), BF16),
        grid=(bsz, nsb),
        in_specs=[cur(0), cur(1), halo(0), halo(1), pl.BlockSpec((CONF_KERNEL, d), lambda b, i: (0, 0)),
                  vec, vec, vec],
        out_specs=pl.BlockSpec((ts, d), lambda b, i: (b * nsb + i, 0)),
        scratch_shapes=[pltpu.VMEM((ts + CONF_HALO, d), F32), pltpu.VMEM((SUBLANES - 1, ts + CONF_HALO, d), F32)],
        compiler_params=_cparams("parallel", "parallel"),
        name="conf_module",
    )(conf, conf, conf, conf, dw_w, dw_b.reshape(1, d), ln_w.reshape(1, d), ln_b.reshape(1, d))


FFN_SUB = 256


FFN_NCHUNK = 512


def pack_bf16_pair(a, b):
    ua = lax.bitcast_convert_type(a.astype(BF16).astype(F32), jnp.uint32)
    ub = lax.bitcast_convert_type(b.astype(BF16).astype(F32), jnp.uint32)
    return (ua >> 16) | (ub & jnp.uint32(0xFFFF0000))


def unpack_bf16_pair(u):
    a = lax.bitcast_convert_type(u << 16, F32)
    b = lax.bitcast_convert_type(u & jnp.uint32(0xFFFF0000), F32)
    return a, b


def _ffn_kernel(ge_ref, gb_ref, gr_ref, x_ref, w1_ref, w3_ref, w2_ref, o_ref, acc_ref, *xb_scr, nf, gsz, packed, ragged):
    g = pl.program_id(0)
    f = pl.program_id(1)
    rows = gr_ref[g]
    valid = rows > 0
    d = acc_ref.shape[1]

    @pl.when(valid & (f == 0))
    def _():
        acc_ref[...] = jnp.zeros_like(acc_ref)
        if packed:
            a, b = unpack_bf16_pair(x_ref[...])
            xb_scr[0][:, 0:d // 2] = a.astype(BF16)
            xb_scr[0][:, d // 2:d] = b.astype(BF16)

    xsrc = xb_scr[0] if packed else x_ref

    def slab(r0, nrows, w1, w3, w2):
        xs = xsrc[pl.ds(r0, nrows), :]
        h1 = jnp.dot(xs, w1, preferred_element_type=F32)
        h3 = jnp.dot(xs, w3, preferred_element_type=F32)
        hh = (h1 * jax.nn.sigmoid(h1) * h3).astype(BF16)
        for c in range(d // FFN_NCHUNK):
            cs = slice(c * FFN_NCHUNK, (c + 1) * FFN_NCHUNK)
            acc_ref[pl.ds(r0, nrows), cs] += jnp.dot(hh, w2[:, cs], preferred_element_type=F32)

    nslab = (rows + FFN_SUB - 1) // FFN_SUB
    for ns in (range(1, gsz // FFN_SUB + 1) if ragged else (gsz // FFN_SUB,)):
        @pl.when(nslab == ns)
        def _(ns=ns):
            slab(0, ns * FFN_SUB, w1_ref[...].astype(BF16), w3_ref[...].astype(BF16), w2_ref[...].astype(BF16))

    @pl.when(jnp.logical_not(valid) & (f == 0))
    def _():
        o_ref[...] = jnp.zeros_like(o_ref)

    @pl.when(valid & (f == nf - 1))
    def _():
        if packed:
            o_ref[...] = pack_bf16_pair(acc_ref[:, 0:d // 2], acc_ref[:, d // 2:d])
        else:
            o_ref[...] = acc_ref[...].astype(o_ref.dtype)


def ffn_groups(xs, w1, w3, w2, grp_expert, grp_block, grp_rows, *, gsz=1024, tf=256, packed=False, ragged=False):
    e, d, ff = w1.shape
    r, dx = xs.shape
    nf = ff // tf
    ng = grp_expert.shape[0]

    def fmap(f, g, gr):
        return jnp.where(gr[g] > 0, f, nf - 1)

    scratch = [pltpu.VMEM((gsz, d), F32)]
    if packed:
        scratch.append(pltpu.VMEM((gsz, d), BF16))
    grid_spec = pltpu.PrefetchScalarGridSpec(
        num_scalar_prefetch=3,
        grid=(ng, nf),
        in_specs=[
            pl.BlockSpec((gsz, dx), lambda g, f, ge, gb, gr: (gb[g], 0)),
            pl.BlockSpec((None, d, tf), lambda g, f, ge, gb, gr: (ge[g], 0, fmap(f, g, gr))),
            pl.BlockSpec((None, d, tf), lambda g, f, ge, gb, gr: (ge[g], 0, fmap(f, g, gr))),
            pl.BlockSpec((None, tf, d), lambda g, f, ge, gb, gr: (ge[g], fmap(f, g, gr), 0)),
        ],
        out_specs=pl.BlockSpec((gsz, dx), lambda g, f, ge, gb, gr: (gb[g], 0)),
        scratch_shapes=scratch,
    )
    return pl.pallas_call(
        functools.partial(_ffn_kernel, nf=nf, gsz=gsz, packed=packed, ragged=ragged),
        out_shape=jax.ShapeDtypeStruct((r, dx), xs.dtype),
        grid_spec=grid_spec,
        compiler_params=_cparams("arbitrary", "arbitrary"),
        name="ffn_groups",
    )(grp_expert, grp_block, grp_rows, xs, w1, w3, w2)


def _row_copy(src_ref, src_row, dst_ref, dst_row, sem):
    return pltpu.make_async_copy(src_ref.at[pl.ds(src_row, 1), :], dst_ref.at[pl.ds(dst_row, 1), :], sem)


def _dispatch_kernel(pos_ref, h_ref, xs_in_ref, xs_ref, sem):
    del xs_in_ref
    tm = h_ref.shape[0]
    base = pl.program_id(0) * tm

    def issue(i, c):
        for k in range(TOP_K):
            _row_copy(h_ref, i, xs_ref, pos_ref[(base + i) * TOP_K + k], sem).start()
        return c

    lax.fori_loop(0, tm, issue, 0)

    def drain(i, c):
        for k in range(TOP_K):
            _row_copy(h_ref, i, xs_ref, pos_ref[(base + i) * TOP_K + k], sem).wait()
        return c

    lax.fori_loop(0, tm, drain, 0)


def moe_dispatch(hp, pos_flat, n_rows, tm=512):
    t, w = hp.shape
    xs0 = jnp.zeros((n_rows, w), hp.dtype)
    grid_spec = pltpu.PrefetchScalarGridSpec(
        num_scalar_prefetch=1,
        grid=(t // tm,),
        in_specs=[pl.BlockSpec((tm, w), lambda i, pos: (i, 0)), pl.BlockSpec(memory_space=pl.ANY)],
        out_specs=pl.BlockSpec(memory_space=pl.ANY),
        scratch_shapes=[pltpu.SemaphoreType.DMA],
    )
    return pl.pallas_call(
        _dispatch_kernel,
        out_shape=jax.ShapeDtypeStruct((n_rows, w), hp.dtype),
        grid_spec=grid_spec,
        input_output_aliases={2: 0},
        compiler_params=_cparams("arbitrary"),
        name="moe_dispatch",
    )(pos_flat, hp, xs0)


def _t5_bucket_np(dist):
    max_exact = REL_BUCKETS // 2
    d_f = np.maximum(dist, 1).astype(np.float32)
    large = max_exact + (np.log(d_f / np.float32(max_exact)) / np.float32(math.log(REL_MAX_DIST / max_exact))
                         * np.float32(REL_BUCKETS - max_exact)).astype(np.int32)
    large = np.minimum(large, REL_BUCKETS - 1)
    return np.where(dist < max_exact, dist, large)


RES = 4
TQ = ATTN_BLOCK
PIECE = TQ // RES


def _bucket_index_tables():
    assert DILATED_PATTERNS == ((128, 1), (512, 4), (2048, 16)) and RES == 4 and TQ == 128

    def fin(steps, dil, n_back=TQ):
        ok = (steps >= 0) & (steps <= n_back)
        return np.where(ok, _t5_bucket_np(np.clip(steps, 0, n_back) * dil), -1).astype(np.int32)

    i = np.arange(TQ)[:, None]
    t4 = fin(i + TQ - np.arange(2 * TQ)[None, :], 4)
    t16 = fin(i - np.arange(TQ)[None, :], 16)
    rq, mq = i // PIECE, i % PIECE
    jk = np.arange(2 * TQ)[None, :]
    rk, mk = jk // (2 * PIECE), jk % (2 * PIECE)
    t1 = fin(RES * (PIECE + mq - mk) + (rq - rk), 1)
    jk0 = np.arange(TQ)[None, :]
    rk0, mk0 = jk0 // PIECE, jk0 % PIECE
    t1first = fin(RES * (mq - mk0) + (rq - rk0), 1)
    return t4, t1, t1first, t16


def _bias_tab_kernel(rel_ref, *refs):
    n = len(refs) // 2
    h = pl.program_id(0)
    for idx_ref, o_ref in zip(refs[:n], refs[n:]):
        idx = idx_ref[...]
        tab = jnp.full(idx.shape, NEG_INF, F32)
        for b in range(REL_BUCKETS):
            tab = jnp.where(idx == b, rel_ref[b, h], tab)
        o_ref[0] = tab


def attn_bias_tables(rel_bias):
    idx_tabs = [jnp.asarray(t) for t in _bucket_index_tables()]
    nh = rel_bias.shape[1]
    return pl.pallas_call(
        _bias_tab_kernel,
        out_shape=tuple(jax.ShapeDtypeStruct((nh,) + t.shape, F32) for t in idx_tabs),
        grid=(nh,),
        in_specs=[pl.BlockSpec(memory_space=pltpu.SMEM)] + [pl.BlockSpec(t.shape, lambda h: (0, 0)) for t in idx_tabs],
        out_specs=tuple(pl.BlockSpec((1,) + t.shape, lambda h: (h, 0, 0)) for t in idx_tabs),
        compiler_params=_cparams("parallel"),
        name="attn_bias_tables",
    )(rel_bias.astype(F32), *idx_tabs)


def _attn_kernel(q_ref, k_ref, v_ref, t4_ref, t1_ref, t1f_ref, t16_ref, o_ref, acc, m_s, l_s, qs, ks, vs, *, scale):
    nres, lsub, hd = q_ref.shape
    nt = (((1,), (1,)), ((), ()))

    def tile(q, k, v, bias):
        s = lax.dot_general(q, k, nt, preferred_element_type=F32) * scale + bias
        m = jnp.max(s, axis=-1, keepdims=True)
        p = jnp.exp(s - m)
        l = jnp.sum(p, axis=-1, keepdims=True)
        o = jnp.dot(p.astype(BF16), v, preferred_element_type=F32)
        return o, jnp.broadcast_to(m, o.shape), jnp.broadcast_to(l, o.shape)

    def merge(r, rows, o, m, l):
        m0 = m_s[r, rows, :]
        mn = jnp.maximum(m0, m)
        a = jnp.exp(m0 - mn)
        b = jnp.exp(m - mn)
        acc[r, rows, :] = acc[r, rows, :] * a + o * b
        l_s[r, rows, :] = l_s[r, rows, :] * a + l * b
        m_s[r, rows, :] = mn

    for r in range(nres):
        for i in range(lsub // TQ):
            rows = pl.ds(i * TQ, TQ)
            if i == 0:
                o, m, l = tile(q_ref[r, rows, :], k_ref[r, rows, :], v_ref[r, rows, :], t4_ref[0, :, TQ:2 * TQ])
            else:
                krows = pl.ds((i - 1) * TQ, 2 * TQ)
                o, m, l = tile(q_ref[r, rows, :], k_ref[r, krows, :], v_ref[r, krows, :], t4_ref[0])
            acc[r, rows, :] = o
            m_s[r, rows, :] = m
            l_s[r, rows, :] = l

    for j in range(lsub // PIECE):
        q = jnp.concatenate([q_ref[r, pl.ds(j * PIECE, PIECE), :] for r in range(nres)], axis=0)
        if j == 0:
            krows, bias = pl.ds(0, PIECE), t1f_ref[0]
        else:
            krows, bias = pl.ds((j - 1) * PIECE, 2 * PIECE), t1_ref[0]
        k = jnp.concatenate([k_ref[r, krows, :] for r in range(nres)], axis=0)
        v = jnp.concatenate([v_ref[r, krows, :] for r in range(nres)], axis=0)
        o, m, l = tile(q, k, v, bias)
        for r in range(nres):
            part = slice(r * PIECE, (r + 1) * PIECE)
            merge(r, pl.ds(j * PIECE, PIECE), o[part], m[part], l[part])

    for r in range(nres):
        qs[...] = q_ref[r].astype(F32)
        ks[...] = k_ref[r].astype(F32)
        vs[...] = v_ref[r].astype(F32)
        for c in range(lsub // TQ):
            rows = pl.ds(c, TQ, stride=lsub // TQ)
            o, m, l = tile(qs[rows, :].astype(BF16), ks[rows, :].astype(BF16), vs[rows, :].astype(BF16), t16_ref[0])
            merge(r, rows, o, m, l)

    for r in range(nres):
        o_ref[r] = (acc[r] / l_s[r]).astype(o_ref.dtype)


def attention(qkv, tabs, *, bsz, seq, hd):
    dm = ATTN_HEADS * hd
    lsub = seq // RES
    assert hd == V7X_LANES and lsub // TQ == 16 // RES and seq // 16 == TQ
    view = qkv.reshape(bsz * RES, lsub, 3 * dm)
    blk = lambda which: pl.BlockSpec((RES, lsub, hd), lambda b, h: (b, 0, which * ATTN_HEADS + h))
    tab = lambda t: pl.BlockSpec((1,) + t.shape[1:], lambda b, h: (h, 0, 0))
    st = pltpu.VMEM((RES, lsub, hd), F32)
    row = pltpu.VMEM((lsub, hd), F32)
    o = pl.pallas_call(
        functools.partial(_attn_kernel, scale=hd ** -0.5),
        out_shape=jax.ShapeDtypeStruct((bsz * RES, lsub, dm), BF16),
        grid=(bsz, ATTN_HEADS),
        in_specs=[blk(0), blk(1), blk(2)] + [tab(t) for t in tabs],
        out_specs=pl.BlockSpec((RES, lsub, hd), lambda b, h: (b, 0, h)),
        scratch_shapes=[st, st, st, row, row, row],
        compiler_params=_cparams("parallel", "arbitrary"),
        name="attention",
    )(view, view, view, *tabs)
    return o.reshape(bsz * seq, dm)


def _split_bf16(x):
    hi = x.astype(BF16)
    lo = (x - hi.astype(F32)).astype(BF16)
    return hi, lo


def _moe_route_kernel(x_ref, g_ref, wr_ref, h_ref, meta_ref, cnt_ref, carry_ref):
    step = pl.program_id(0)
    tm = x_ref.shape[0]

    @pl.when(step == 0)
    def _():
        carry_ref[...] = jnp.zeros_like(carry_ref)

    h = _rms(x_ref[...], g_ref[...])
    h_hi, h_lo = _split_bf16(h)
    dh = h.shape[1] // 2
    h_ref[...] = pack_bf16_pair(h[:, 0:dh], h[:, dh:2 * dh])
    w_hi, w_lo = _split_bf16(wr_ref[...])
    logits = (jnp.dot(h_hi, w_hi, preferred_element_type=F32) + jnp.dot(h_lo, w_hi, preferred_element_type=F32)
              + jnp.dot(h_hi, w_lo, preferred_element_type=F32))
    lane = lax.broadcasted_iota(jnp.int32, (tm, V7X_LANES), 1).astype(F32)
    lg = jnp.where(lane < N_EXPERTS, logits, NEG_INF)
    m1 = jnp.max(lg, axis=-1, keepdims=True)
    i1 = jnp.min(jnp.where(lg == m1, lane, float(V7X_LANES)), axis=-1, keepdims=True)
    lg2 = jnp.where(lane == i1, NEG_INF, lg)
    m2 = jnp.max(lg2, axis=-1, keepdims=True)
    i2 = jnp.min(jnp.where(lg2 == m2, lane, float(V7X_LANES)), axis=-1, keepdims=True)
    e2 = jnp.exp(m2 - m1)
    g1 = 1.0 / (1.0 + e2)
    g2 = e2 / (1.0 + e2)
    oh = ((lane == i1) | (lane == i2)).astype(BF16)
    r_i = lax.broadcasted_iota(jnp.int32, (tm, tm), 0)
    c_i = lax.broadcasted_iota(jnp.int32, (tm, tm), 1)
    tri = (c_i < r_i).astype(BF16)
    rank = jnp.dot(tri, oh, preferred_element_type=F32) + carry_ref[0:1, :]
    r1 = jnp.sum(jnp.where(lane == i1, rank, 0.0), axis=-1, keepdims=True)
    r2 = jnp.sum(jnp.where(lane == i2, rank, 0.0), axis=-1, keepdims=True)
    carry_ref[0:1, :] = carry_ref[0:1, :] + jnp.sum(oh.astype(F32), axis=0, keepdims=True)
    meta = jnp.where(lane == 0, i1.astype(F32), 0.0)
    meta = jnp.where(lane == 1, i2.astype(F32), meta)
    meta = jnp.where(lane == 2, g1, meta)
    meta = jnp.where(lane == 3, g2, meta)
    meta = jnp.where(lane == 4, r1, meta)
    meta = jnp.where(lane == 5, r2, meta)
    meta_ref[...] = meta
    cnt_ref[...] = jnp.broadcast_to(carry_ref[0:1, :], cnt_ref.shape)


def moe_route(x, g, w_router_p, tm=256):
    t, d = x.shape
    return pl.pallas_call(
        _moe_route_kernel,
        out_shape=(jax.ShapeDtypeStruct((t, d // 2), jnp.uint32), jax.ShapeDtypeStruct((t, V7X_LANES), F32),
                   jax.ShapeDtypeStruct((8, V7X_LANES), F32)),
        grid=(t // tm,),
        in_specs=[pl.BlockSpec((tm, d), lambda i: (i, 0)), pl.BlockSpec((1, d), lambda i: (0, 0)),
                  pl.BlockSpec((d, V7X_LANES), lambda i: (0, 0))],
        out_specs=(pl.BlockSpec((tm, d // 2), lambda i: (i, 0)), pl.BlockSpec((tm, V7X_LANES), lambda i: (i, 0)),
                   pl.BlockSpec((8, V7X_LANES), lambda i: (0, 0))),
        scratch_shapes=[pltpu.VMEM((8, V7X_LANES), F32)],
        compiler_params=_cparams("arbitrary"),
        name="moe_route",
    )(x, g.reshape(1, d), w_router_p)


def _moe_final_kernel(pos_ref, x_ref, meta_ref, g_ref, ys_ref, o_ref, ybuf, o_scr, sem, *, lsub):
    nres, tp, d = x_ref.shape
    dh = d // 2
    b = pl.program_id(0)
    j = pl.program_id(1)

    def token(i):
        r = i // tp
        return (b * nres + r) * lsub + j * tp + (i - r * tp)

    def issue(i, c):
        for k in range(TOP_K):
            _row_copy(ys_ref, pos_ref[token(i) * TOP_K + k], ybuf.at[k], i, sem).start()
        return c

    lax.fori_loop(0, nres * tp, issue, 0)

    def drain(i, c):
        for k in range(TOP_K):
            _row_copy(ys_ref, pos_ref[token(i) * TOP_K + k], ybuf.at[k], i, sem).wait()
        return c

    lax.fori_loop(0, nres * tp, drain, 0)
    for r in range(nres):
        rows = slice(r * tp, (r + 1) * tp)
        meta = meta_ref[r]
        g1 = meta[:, 2:3]
        g2 = meta[:, 3:4]
        a0, b0 = unpack_bf16_pair(ybuf[0, rows, :])
        a1, b1 = unpack_bf16_pair(ybuf[1, rows, :])
        x = x_ref[r]
        xa = x[:, 0:dh] + a0 * g1 + a1 * g2
        xb = x[:, dh:d] + b0 * g1 + b1 * g2
        ms = (jnp.sum(xa * xa, axis=-1, keepdims=True) + jnp.sum(xb * xb, axis=-1, keepdims=True)) * (1.0 / d)
        inv = lax.rsqrt(ms + EPS)
        ya = xa * inv * g_ref[:, 0:dh]
        yb = xb * inv * g_ref[:, dh:d]
        out_rows = pl.ds(r, tp, stride=nres)
        for s in range(dh // V7X_LANES):
            ls = slice(s * V7X_LANES, (s + 1) * V7X_LANES)
            o_scr[s, out_rows, :] = ya[:, ls]
            o_scr[dh // V7X_LANES + s, out_rows, :] = yb[:, ls]
    for s in range(d // V7X_LANES):
        o_ref[:, s * V7X_LANES:(s + 1) * V7X_LANES] = o_scr[s]


def moe_final(x, ys, pos_flat, meta, g, *, bsz, seq, tp=128):
    t, d = x.shape
    lsub = seq // RES
    tm = RES * tp
    xv = x.reshape(bsz * RES, lsub, d)
    mv = meta.reshape(bsz * RES, lsub, V7X_LANES)
    grid_spec = pltpu.PrefetchScalarGridSpec(
        num_scalar_prefetch=1,
        grid=(bsz, lsub // tp),
        in_specs=[pl.BlockSpec((RES, tp, d), lambda b, j, pos: (b, j, 0)),
                  pl.BlockSpec((RES, tp, V7X_LANES), lambda b, j, pos: (b, j, 0)),
                  pl.BlockSpec((1, d), lambda b, j, pos: (0, 0)), pl.BlockSpec(memory_space=pl.ANY)],
        out_specs=pl.BlockSpec((tm, d), lambda b, j, pos: (b * (lsub // tp) + j, 0)),
        scratch_shapes=[pltpu.VMEM((TOP_K, tm, d // 2), jnp.uint32), pltpu.VMEM((d // V7X_LANES, tm, V7X_LANES), F32),
                        pltpu.SemaphoreType.DMA],
    )
    return pl.pallas_call(
        functools.partial(_moe_final_kernel, lsub=lsub),
        out_shape=jax.ShapeDtypeStruct((t, d), F32),
        grid_spec=grid_spec,
        compiler_params=_cparams("arbitrary", "arbitrary"),
        name="moe_final",
    )(pos_flat, xv, mv, g.reshape(1, d), ys)


def _pad_lanes(v, fill=0.0):
    v = v.reshape(1, -1).astype(F32)
    return jnp.pad(v, ((0, 0), (0, V7X_LANES - v.shape[1])), constant_values=fill)


def even_layer(x, h, p, *, bsz, seq):
    t, d = x.shape
    d_ssd = d
    d_conf = d
    heads = d_ssd // SSD_HEAD_DIM
    conv_dim = d_ssd + 2 * SSD_GROUPS * SSD_STATE
    i1 = d_ssd + conv_dim
    i2 = i1 + heads
    w_in_t = jnp.swapaxes(p["w_in"], 0, 1)
    zx = matmul_nt(h, w_in_t, row_start=0, n_cols=i1, tn=1024, name="in_proj_zx")
    dt_raw = matmul_nt(h, w_in_t, row_start=i1, n_cols=V7X_LANES, tn=V7X_LANES, out_dtype=F32, name="in_proj_dt")
    conf = matmul_nt(h, w_in_t, row_start=i2, n_cols=2 * d_conf, tn=1024, name="in_proj_conf")
    xc = ssd_conv(zx, p["conv_w"], p["conv_b"], bsz=bsz, seq=seq, col_off=d_ssd)
    dt, acs, dtt, acst = ssd_prep(dt_raw, _pad_lanes(p["dt_bias"]), _pad_lanes(p["a_log"]), bsz=bsz, seq=seq)
    dskip_e = jnp.repeat(p["d_skip"].astype(F32), SSD_HEAD_DIM).reshape(1, d_ssd)
    y_ssd = ssd_main(xc, zx, dt, acs, dtt, acst, dskip_e, p["ssd_norm_w"], bsz=bsz, seq=seq, d_ssd=d_ssd)
    u = conf_module(conf, p["conf_dw_w"], p["conf_dw_b"], p["conf_ln_w"], p["conf_ln_b"], bsz=bsz, seq=seq)
    return matmul([y_ssd, u], p["w_out"], n_cols=d, tn=512, res=x, out_dtype=F32, name="even_out_proj")


def dense_ffn(h, w1, w3, w2, gsz=1024):
    t = h.shape[0]
    ng = t // gsz
    ge = jnp.zeros((ng,), jnp.int32)
    gb = jnp.arange(ng, dtype=jnp.int32)
    gr = jnp.full((ng,), gsz, jnp.int32)
    return ffn_groups(h, w1[None], w3[None], w2[None], ge, gb, gr, gsz=gsz)


def odd_layer_attn(x, h, p, *, bsz, seq):
    t, d = x.shape
    hd = d // ATTN_HEADS
    qkv = matmul([h], p["w_qkv"], n_cols=3 * d, tn=1024, name="qkv_proj")
    tabs = attn_bias_tables(p["rel_bias"])
    o = attention(qkv, tabs, bsz=bsz, seq=seq, hd=hd)
    return matmul([o], p["w_attn_out"], n_cols=d, tn=1024, res=x, out_dtype=F32, name="attn_out_proj")


def moe_layer(x, norm_g, w_router, w1, w3, w2, final_g, *, bsz, seq, gsz=1280):
    t, d = x.shape
    ne = w1.shape[0]
    w_router_p = jnp.pad(w_router.astype(F32), ((0, 0), (0, V7X_LANES - ne)))
    h, meta, cnt = moe_route(x, norm_g, w_router_p)
    counts = cnt[0, :ne].astype(jnp.int32)
    ngrp_e = (counts + gsz - 1) // gsz
    per_grp = (counts + jnp.maximum(ngrp_e, 1) - 1) // jnp.maximum(ngrp_e, 1)
    rpg_e = jnp.maximum((per_grp + FFN_SUB - 1) // FFN_SUB * FFN_SUB, FFN_SUB)
    grp_end = jnp.cumsum(ngrp_e)
    grp_start = grp_end - ngrp_e
    ng = (t * TOP_K) // gsz + ne
    j = jnp.arange(ng, dtype=jnp.int32)
    total = grp_end[-1]
    last = jnp.maximum(total - 1, 0)
    jj = jnp.minimum(j, last)
    ge = jnp.searchsorted(grp_end, jj, side="right").astype(jnp.int32)
    ge = jnp.minimum(ge, ne - 1)
    rows = jnp.clip(counts[ge] - (jj - grp_start[ge]) * rpg_e[ge], 0, rpg_e[ge])
    gr = jnp.where(j < total, rows, 0).astype(jnp.int32)
    gb = j
    e_idx = meta[:, 0:2].astype(jnp.int32)
    rank = meta[:, 4:6].astype(jnp.int32)
    grp_in_e = rank // rpg_e[e_idx]
    pos = ((grp_start[e_idx] + grp_in_e) * gsz + (rank - grp_in_e * rpg_e[e_idx])).reshape(-1)
    xs = moe_dispatch(h, pos, ng * gsz)
    ys = ffn_groups(xs, w1, w3, w2, ge, gb, gr, gsz=gsz, packed=True, ragged=True)
    return moe_final(x, ys, pos, meta, final_g, bsz=bsz, seq=seq)


def kernel(x, norm_mix, norm_ffn, norm_final, even_w_in, ssd_conv_w, ssd_conv_b, ssd_dt_bias, ssd_a_log, ssd_d, ssd_norm_w, conf_dw_w, conf_dw_b, conf_ln_w, conf_ln_b, even_w_out, ffn_w1, ffn_w3, ffn_w2, attn_w_qkv, attn_w_out, rel_bias, moe_router, moe_w1, moe_w3, moe_w2):
    bsz, seq, d = x.shape
    assert norm_mix.shape[0] == 2, "two-layer trunk: one even (SSD+Conformer/FFN) and one odd (attention/MoE) layer"
    xf = x.reshape(bsz * seq, d)
    h = rmsnorm(xf, norm_mix[0])
    p_even = dict(w_in=even_w_in[0], conv_w=ssd_conv_w[0], conv_b=ssd_conv_b[0], dt_bias=ssd_dt_bias[0],
                  a_log=ssd_a_log[0], d_skip=ssd_d[0], ssd_norm_w=ssd_norm_w[0], conf_dw_w=conf_dw_w[0],
                  conf_dw_b=conf_dw_b[0], conf_ln_w=conf_ln_w[0], conf_ln_b=conf_ln_b[0], w_out=even_w_out[0])
    x1 = even_layer(xf, h, p_even, bsz=bsz, seq=seq)
    h1 = rmsnorm(x1, norm_ffn[0])
    y_ffn = dense_ffn(h1, ffn_w1[0], ffn_w3[0], ffn_w2[0])
    x2, h2 = add_norm_to_residue_major(x1, y_ffn, norm_mix[1], bsz=bsz, seq=seq, nres=RES)
    p_odd = dict(w_qkv=attn_w_qkv[0], w_attn_out=attn_w_out[0], rel_bias=rel_bias)
    x3 = odd_layer_attn(x2, h2, p_odd, bsz=bsz, seq=seq)
    out = moe_layer(x3, norm_ffn[1], moe_router[0], moe_w1[0], moe_w3[0], moe_w2[0], norm_final, bsz=bsz, seq=seq)
    return out.reshape(bsz, seq, d)
```

```python
import functools
import math

import numpy as np
import jax
import jax.numpy as jnp
from jax import lax
from jax.experimental import pallas as pl
from jax.experimental.pallas import tpu as pltpu

SSD_HEAD_DIM = 64
SSD_GROUPS = 4
SSD_STATE = 128
SSD_CONV = 4
SSD_CHUNK = 128
CONF_KERNEL = 31
ATTN_HEADS = 16
DILATED_PATTERNS = ((128, 1), (512, 4), (2048, 16))
ATTN_BLOCK = 128
REL_BUCKETS = 32
REL_MAX_DIST = 2048
N_EXPERTS = 8
TOP_K = 2
EPS = 1e-6

V7X_LANES = 128
V7X_VMEM_BYTES = 64 * 1024 * 1024
VMEM_LIMIT = 56 * 1024 * 1024

F32 = jnp.float32
BF16 = jnp.bfloat16
NEG_INF = float("-inf")


def _cparams(*sem):
    return pltpu.CompilerParams(dimension_semantics=tuple(sem), vmem_limit_bytes=VMEM_LIMIT)


def _rms(x, g):
    ms = jnp.mean(x * x, axis=-1, keepdims=True)
    return x * lax.rsqrt(ms + EPS) * g


def _rmsnorm_kernel(x_ref, g_ref, h_ref):
    h_ref[...] = _rms(x_ref[...], g_ref[...]).astype(h_ref.dtype)


def rmsnorm(x, g, out_dtype=BF16, tm=512):
    t, d = x.shape
    return pl.pallas_call(
        _rmsnorm_kernel,
        out_shape=jax.ShapeDtypeStruct((t, d), out_dtype),
        grid=(t // tm,),
        in_specs=[pl.BlockSpec((tm, d), lambda i: (i, 0)), pl.BlockSpec((1, d), lambda i: (0, 0))],
        out_specs=pl.BlockSpec((tm, d), lambda i: (i, 0)),
        compiler_params=_cparams("parallel"),
        name="rmsnorm",
    )(x, g.reshape(1, d))


def _add_norm_res_kernel(x_ref, y_ref, g_ref, xo_ref, ho_ref, xs_scr, hs_scr):
    nres, tp, d = xo_ref.shape
    xn = x_ref[...] + y_ref[...].astype(F32)
    h = _rms(xn, g_ref[...])
    for s in range(d // V7X_LANES):
        ls = slice(s * V7X_LANES, (s + 1) * V7X_LANES)
        xs_scr[s] = xn[:, ls]
        hs_scr[s] = h[:, ls]
    for r in range(nres):
        rows = pl.ds(r, tp, stride=nres)
        for s in range(d // V7X_LANES):
            ls = slice(s * V7X_LANES, (s + 1) * V7X_LANES)
            xo_ref[r, :, ls] = xs_scr[s, rows, :]
            ho_ref[r, :, ls] = hs_scr[s, rows, :].astype(ho_ref.dtype)


def add_norm_to_residue_major(x, y, g, *, bsz, seq, nres, tp=128):
    t, d = x.shape
    lsub = seq // nres
    tm = nres * tp
    nj = lsub // tp
    slab = pltpu.VMEM((d // V7X_LANES, tm, V7X_LANES), F32)
    xo, ho = pl.pallas_call(
        _add_norm_res_kernel,
        out_shape=(jax.ShapeDtypeStruct((bsz * nres, lsub, d), F32), jax.ShapeDtypeStruct((bsz * nres, lsub, d), BF16)),
        grid=(bsz, nj),
        in_specs=[pl.BlockSpec((tm, d), lambda b, j: (b * nj + j, 0)), pl.BlockSpec((tm, d), lambda b, j: (b * nj + j, 0)),
                  pl.BlockSpec((1, d), lambda b, j: (0, 0))],
        out_specs=(pl.BlockSpec((nres, tp, d), lambda b, j: (b, j, 0)), pl.BlockSpec((nres, tp, d), lambda b, j: (b, j, 0))),
        scratch_shapes=[slab, slab],
        compiler_params=_cparams("parallel", "parallel"),
        name="add_norm_reorder",
    )(x, y, g.reshape(1, d))
    return xo.reshape(t, d), ho.reshape(t, d)


def _mm_kernel(*refs, n_lhs, ks, has_res):
    xs = refs[:n_lhs]
    w_ref = refs[n_lhs]
    r_ref = refs[n_lhs + 1] if has_res else None
    o_ref = refs[-1]
    acc = None
    k0 = 0
    for x_ref, k in zip(xs, ks):
        part = jnp.dot(x_ref[...], w_ref[k0:k0 + k, :].astype(BF16), preferred_element_type=F32)
        acc = part if acc is None else acc + part
        k0 += k
    if has_res:
        acc = acc + r_ref[...]
    o_ref[...] = acc.astype(o_ref.dtype)


def matmul(xs, w, *, n_cols, col_block_off=0, tn, tm=1024, res=None, out_dtype=BF16, name="matmul"):
    t = xs[0].shape[0]
    ks = tuple(x.shape[1] for x in xs)
    ktot = sum(ks)
    assert w.shape[0] == ktot and n_cols % tn == 0 and t % tm == 0
    in_specs = [pl.BlockSpec((tm, k), lambda i, j: (i, 0)) for k in ks]
    in_specs.append(pl.BlockSpec((ktot, tn), lambda i, j: (0, j + col_block_off)))
    args = list(xs) + [w]
    if res is not None:
        in_specs.append(pl.BlockSpec((tm, tn), lambda i, j: (i, j)))
        args.append(res)
    return pl.pallas_call(
        functools.partial(_mm_kernel, n_lhs=len(xs), ks=ks, has_res=res is not None),
        out_shape=jax.ShapeDtypeStruct((t, n_cols), out_dtype),
        grid=(t // tm, n_cols // tn),
        in_specs=in_specs,
        out_specs=pl.BlockSpec((tm, tn), lambda i, j: (i, j)),
        compiler_params=_cparams("parallel", "arbitrary"),
        name=name,
    )(*args)


def _mm_nt_kernel(x_ref, wt_ref, o_ref):
    wt = wt_ref[...].astype(BF16)
    acc = lax.dot_general(x_ref[...], wt, (((1,), (1,)), ((), ())), preferred_element_type=F32)
    o_ref[...] = acc.astype(o_ref.dtype)


def matmul_nt(x, w_t, *, row_start, n_cols, tn, tm=1024, out_dtype=BF16, name="matmul_nt"):
    t, k = x.shape
    assert w_t.shape[1] == k and n_cols % tn == 0 and t % tm == 0
    if row_start % tn == 0:
        w_spec = pl.BlockSpec((tn, k), lambda i, j: (row_start // tn + j, 0))
    else:
        assert row_start % SUBLANES == 0 and tn % SUBLANES == 0
        w_spec = pl.BlockSpec((pl.Element(tn), pl.Element(k)),
                              lambda i, j: ((row_start // SUBLANES + j * (tn // SUBLANES)) * SUBLANES, 0))
    return pl.pallas_call(
        _mm_nt_kernel,
        out_shape=jax.ShapeDtypeStruct((t, n_cols), out_dtype),
        grid=(t // tm, n_cols // tn),
        in_specs=[pl.BlockSpec((tm, k), lambda i, j: (i, 0)), w_spec],
        out_specs=pl.BlockSpec((tm, tn), lambda i, j: (i, j)),
        compiler_params=_cparams("parallel", "arbitrary"),
        name=name,
    )(x, w_t)


HALO_BF16 = 16


def _ssd_conv_kernel(cur_ref, halo_ref, w_ref, b_ref, o_ref, buf_ref, *, ts):
    i = pl.program_id(1)
    halo = halo_ref[...].astype(F32)
    buf_ref[0:HALO_BF16, :] = jnp.where(i > 0, halo, 0.0)
    buf_ref[HALO_BF16:HALO_BF16 + ts, :] = cur_ref[...].astype(F32)
    acc = b_ref[...]
    for k in range(SSD_CONV):
        off = HALO_BF16 - (SSD_CONV - 1) + k
        acc = acc + w_ref[k:k + 1, :] * buf_ref[off:off + ts, :]
    o_ref[...] = (acc * jax.nn.sigmoid(acc)).astype(o_ref.dtype)


def ssd_conv(zx, conv_w, conv_b, *, bsz, seq, col_off, ts=512, tc=512):
    c = conv_w.shape[1]
    t = zx.shape[0]
    nsb = seq // ts
    cb0 = col_off // tc
    hb = ts // HALO_BF16
    return pl.pallas_call(
        functools.partial(_ssd_conv_kernel, ts=ts),
        out_shape=jax.ShapeDtypeStruct((t, c), BF16),
        grid=(bsz, nsb, c // tc),
        in_specs=[
            pl.BlockSpec((ts, tc), lambda b, i, j: (b * nsb + i, cb0 + j)),
            pl.BlockSpec((HALO_BF16, tc), lambda b, i, j: (jnp.maximum((b * nsb + i) * hb - 1, 0), cb0 + j)),
            pl.BlockSpec((SSD_CONV, tc), lambda b, i, j: (0, j)),
            pl.BlockSpec((1, tc), lambda b, i, j: (0, j)),
        ],
        out_specs=pl.BlockSpec((ts, tc), lambda b, i, j: (b * nsb + i, j)),
        scratch_shapes=[pltpu.VMEM((ts + HALO_BF16, tc), F32)],
        compiler_params=_cparams("parallel", "parallel", "parallel"),
        name="ssd_conv",
    )(zx, zx, conv_w, conv_b.reshape(1, c))


def _ssd_prep_kernel(raw_ref, bias_ref, alog_ref, dt_ref, acs_ref, dtt_ref, acst_ref, *, seq):
    raw = raw_ref[...] + bias_ref[...]
    dt = jnp.maximum(raw, 0.0) + jnp.log1p(jnp.exp(-jnp.abs(raw)))
    a = -jnp.exp(alog_ref[...])
    x = dt * a
    row = lax.broadcasted_iota(jnp.int32, x.shape, 0) % SSD_CHUNK
    sh = 1
    while sh < SSD_CHUNK:
        x = x + jnp.where(row >= sh, pltpu.roll(x, sh, 0), 0.0)
        sh *= 2
    dt_ref[...] = dt
    acs_ref[...] = x
    for c in range(seq // SSD_CHUNK):
        sl = slice(c * SSD_CHUNK, (c + 1) * SSD_CHUNK)
        dtt_ref[sl, :] = dt[sl, :].T
        acst_ref[sl, :] = x[sl, :].T


def ssd_prep(dt_raw, dt_bias_p, a_log_p, *, bsz, seq):
    t = dt_raw.shape[0]
    shp = jax.ShapeDtypeStruct((t, V7X_LANES), F32)
    blk = pl.BlockSpec((seq, V7X_LANES), lambda b: (b, 0))
    vec = pl.BlockSpec((1, V7X_LANES), lambda b: (0, 0))
    return pl.pallas_call(
        functools.partial(_ssd_prep_kernel, seq=seq),
        out_shape=(shp, shp, shp, shp),
        grid=(bsz,),
        in_specs=[blk, vec, vec],
        out_specs=(blk, blk, blk, blk),
        compiler_params=_cparams("parallel"),
        name="ssd_prep",
    )(dt_raw, dt_bias_p, a_log_p)


def _ssd_main_kernel(x_ref, b_ref, c_ref, z_ref, dt_ref, acs_ref, dtt_ref, acst_ref, dsk_ref, nw_ref,
                     y_ref, st_ref, *, heads_per_group):
    ci = pl.program_id(1)
    L = SSD_CHUNK
    gw = heads_per_group * SSD_HEAD_DIM
    npair = heads_per_group // 2

    @pl.when(ci == 0)
    def _():
        st_ref[...] = jnp.zeros_like(st_ref)

    dt = dt_ref[...]
    acs = acs_ref[...]
    dtt = dtt_ref[...]
    acst = acst_ref[...]
    row = lax.broadcasted_iota(jnp.int32, (L, L), 0)
    col = lax.broadcasted_iota(jnp.int32, (L, L), 1)
    causal = col <= row
    lane_lo = lax.broadcasted_iota(jnp.int32, (L, 2 * SSD_HEAD_DIM), 1) < SSD_HEAD_DIM
    lane_lo1 = lax.broadcasted_iota(jnp.int32, (1, 2 * SSD_HEAD_DIM), 1) < SSD_HEAD_DIM
    zero_b = jnp.zeros((L, 2 * SSD_HEAD_DIM), BF16)

    for g in range(SSD_GROUPS):
        bm = b_ref[:, g * SSD_STATE:(g + 1) * SSD_STATE]
        cm = c_ref[:, g * SSD_STATE:(g + 1) * SSD_STATE]
        bmt = bm.astype(F32).T.astype(BF16)
        cb = lax.dot_general(cm, bm, (((1,), (1,)), ((), ())), preferred_element_type=F32)
        ys = []
        for q in range(npair):
            c0 = g * gw + q * 2 * SSD_HEAD_DIM
            xp = x_ref[:, c0:c0 + 2 * SSD_HEAD_DIM]
            xpf = xp.astype(F32)
            ms = []
            for hh in range(2):
                h = g * heads_per_group + 2 * q + hh
                seg = acs[:, h:h + 1] - acst[h:h + 1, :]
                dec = jnp.exp(jnp.where(causal, seg, NEG_INF))
                ms.append((cb * dec * dtt[h:h + 1, :]).astype(BF16))
            h0 = g * heads_per_group + 2 * q
            h1 = h0 + 1
            lhs = jnp.concatenate(ms, axis=1)
            rhs = jnp.concatenate([jnp.where(lane_lo, xp, zero_b), jnp.where(lane_lo, zero_b, xp)], axis=0)
            y_diag = jnp.dot(lhs, rhs, preferred_element_type=F32)
            st = st_ref[g, :, q * 2 * SSD_HEAD_DIM:(q + 1) * 2 * SSD_HEAD_DIM]
            e_l = jnp.where(lane_lo, jnp.exp(acs[:, h0:h0 + 1]), jnp.exp(acs[:, h1:h1 + 1]))
            y_off = jnp.dot(cm, st.astype(BF16), preferred_element_type=F32) * e_l
            dsk = dsk_ref[:, c0:c0 + 2 * SSD_HEAD_DIM]
            ys.append(y_diag + y_off + xpf * dsk)
            last0 = acs[L - 1:L, h0:h0 + 1]
            last1 = acs[L - 1:L, h1:h1 + 1]
            w_l = jnp.where(lane_lo, dt[:, h0:h0 + 1] * jnp.exp(last0 - acs[:, h0:h0 + 1]),
                            dt[:, h1:h1 + 1] * jnp.exp(last1 - acs[:, h1:h1 + 1]))
            xw = (xpf * w_l).astype(BF16)
            st_new = jnp.dot(bmt, xw, preferred_element_type=F32)
            cd = jnp.where(lane_lo1, jnp.exp(last0), jnp.exp(last1))
            st_ref[g, :, q * 2 * SSD_HEAD_DIM:(q + 1) * 2 * SSD_HEAD_DIM] = st * cd + st_new
        y = jnp.concatenate(ys, axis=1)
        zg = z_ref[:, g * gw:(g + 1) * gw].astype(F32)
        yg = y * (zg * jax.nn.sigmoid(zg))
        ms_ = jnp.mean(yg * yg, axis=-1, keepdims=True)
        y_ref[:, g * gw:(g + 1) * gw] = (yg * lax.rsqrt(ms_ + EPS) * nw_ref[:, g * gw:(g + 1) * gw]).astype(y_ref.dtype)


def ssd_main(xc, zx, dt, acs, dtt, acst, dskip_e, norm_w, *, bsz, seq, d_ssd):
    t = xc.shape[0]
    nc = seq // SSD_CHUNK
    heads = d_ssd // SSD_HEAD_DIM
    hpg = heads // SSD_GROUPS
    bc_w = SSD_GROUPS * SSD_STATE
    assert d_ssd % bc_w == 0 and hpg % 2 == 0
    rowmap = lambda b, c: (b * nc + c, 0)
    return pl.pallas_call(
        functools.partial(_ssd_main_kernel, heads_per_group=hpg),
        out_shape=jax.ShapeDtypeStruct((t, d_ssd), BF16),
        grid=(bsz, nc),
        in_specs=[
            pl.BlockSpec((SSD_CHUNK, d_ssd), rowmap),
            pl.BlockSpec((SSD_CHUNK, bc_w), lambda b, c: (b * nc + c, d_ssd // bc_w)),
            pl.BlockSpec((SSD_CHUNK, bc_w), lambda b, c: (b * nc + c, d_ssd // bc_w + 1)),
            pl.BlockSpec((SSD_CHUNK, d_ssd), rowmap),
            pl.BlockSpec((SSD_CHUNK, V7X_LANES), rowmap),
            pl.BlockSpec((SSD_CHUNK, V7X_LANES), rowmap),
            pl.BlockSpec((SSD_CHUNK, V7X_LANES), rowmap),
            pl.BlockSpec((SSD_CHUNK, V7X_LANES), rowmap),
            pl.BlockSpec((1, d_ssd), lambda b, c: (0, 0)),
            pl.BlockSpec((1, d_ssd), lambda b, c: (0, 0)),
        ],
        out_specs=pl.BlockSpec((SSD_CHUNK, d_ssd), rowmap),
        scratch_shapes=[pltpu.VMEM((SSD_GROUPS, SSD_STATE, hpg * SSD_HEAD_DIM), F32)],
        compiler_params=_cparams("parallel", "arbitrary"),
        name="ssd_main",
    )(xc, xc, xc, zx, dt, acs, dtt, acst, dskip_e, norm_w.reshape(1, d_ssd))


CONF_HALO = 32


SUBLANES = 8


def _conf_kernel(a_ref, g_ref, ah_ref, gh_ref, w_ref, b_ref, lw_ref, lb_ref, o_ref, buf_ref, sh_ref, *, ts, rc, lt):
    i = pl.program_id(1)
    d = a_ref.shape[1]
    ah = ah_ref[...].astype(F32)
    gh = gh_ref[...].astype(F32)
    buf_ref[0:CONF_HALO, :] = jnp.where(i > 0, ah * jax.nn.sigmoid(gh), 0.0)
    a = a_ref[...].astype(F32)
    g = g_ref[...].astype(F32)
    buf_ref[CONF_HALO:CONF_HALO + ts, :] = a * jax.nn.sigmoid(g)
    span = ts + CONF_HALO - SUBLANES
    for j in range(1, SUBLANES):
        sh_ref[j - 1, 0:span, :] = buf_ref[j:j + span, :]
    base = CONF_HALO - (CONF_KERNEL - 1)
    outs = []
    for r in range(ts // rc):
        cols = []
        for c in range(d // lt):
            ls = slice(c * lt, (c + 1) * lt)
            acc = jnp.broadcast_to(b_ref[:, ls], (rc, lt))
            for k in range(CONF_KERNEL):
                j = (base + k) % SUBLANES
                off = r * rc + (base + k) - j
                src = buf_ref[off:off + rc, ls] if j == 0 else sh_ref[j - 1, off:off + rc, ls]
                acc = acc + w_ref[k:k + 1, ls] * src
            cols.append(acc)
        outs.append(jnp.concatenate(cols, axis=1))
    u = jnp.concatenate(outs, axis=0)
    mu = jnp.mean(u, axis=-1, keepdims=True)
    uc = u - mu
    var = jnp.mean(uc * uc, axis=-1, keepdims=True)
    y = uc * lax.rsqrt(var + EPS) * lw_ref[...] + lb_ref[...]
    o_ref[...] = (y * jax.nn.sigmoid(y)).astype(o_ref.dtype)


def conf_module(conf, dw_w, dw_b, ln_w, ln_b, *, bsz, seq, ts=128, rc=32, lt=512):
    t = conf.shape[0]
    d = dw_w.shape[1]
    nsb = seq // ts
    hb = ts // CONF_HALO
    cur = lambda col: pl.BlockSpec((ts, d), lambda b, i: (b * nsb + i, col))
    halo = lambda col: pl.BlockSpec((CONF_HALO, d), lambda b, i: (jnp.maximum((b * nsb + i) * hb - 1, 0), col))
    vec = pl.BlockSpec((1, d), lambda b, i: (0, 0))
    return pl.pallas_call(
        functools.partial(_conf_kernel, ts=ts, rc=rc, lt=lt),
        out_shape=jax.ShapeDtypeStruct((t, d), BF16),
        grid=(bsz, nsb),
        in_specs=[cur(0), cur(1), halo(0), halo(1), pl.BlockSpec((CONF_KERNEL, d), lambda b, i: (0, 0)),
                  vec, vec, vec],
        out_specs=pl.BlockSpec((ts, d), lambda b, i: (b * nsb + i, 0)),
        scratch_shapes=[pltpu.VMEM((ts + CONF_HALO, d), F32), pltpu.VMEM((SUBLANES - 1, ts + CONF_HALO, d), F32)],
        compiler_params=_cparams("parallel", "parallel"),
        name="conf_module",
    )(conf, conf, conf, conf, dw_w, dw_b.reshape(1, d), ln_w.reshape(1, d), ln_b.reshape(1, d))


FFN_SUB = 256


FFN_NCHUNK = 512


def pack_bf16_pair(a, b):
    ua = lax.bitcast_convert_type(a.astype(BF16).astype(F32), jnp.uint32)
    ub = lax.bitcast_convert_type(b.astype(BF16).astype(F32), jnp.uint32)
    return (ua >> 16) | (ub & jnp.uint32(0xFFFF0000))


def unpack_bf16_pair(u):
    a = lax.bitcast_convert_type(u << 16, F32)
    b = lax.bitcast_convert_type(u & jnp.uint32(0xFFFF0000), F32)
    return a, b


def _ffn_kernel(ge_ref, gb_ref, gr_ref, x_ref, w1_ref, w3_ref, w2_ref, o_ref, acc_ref, *xb_scr, nf, gsz, packed, ragged):
    g = pl.program_id(0)
    f = pl.program_id(1)
    rows = gr_ref[g]
    valid = rows > 0
    d = acc_ref.shape[1]

    @pl.when(valid & (f == 0))
    def _():
        acc_ref[...] = jnp.zeros_like(acc_ref)
        if packed:
            a, b = unpack_bf16_pair(x_ref[...])
            xb_scr[0][:, 0:d // 2] = a.astype(BF16)
            xb_scr[0][:, d // 2:d] = b.astype(BF16)

    xsrc = xb_scr[0] if packed else x_ref

    def slab(r0, nrows, w1, w3, w2):
        xs = xsrc[pl.ds(r0, nrows), :]
        h1 = jnp.dot(xs, w1, preferred_element_type=F32)
        h3 = jnp.dot(xs, w3, preferred_element_type=F32)
        hh = (h1 * jax.nn.sigmoid(h1) * h3).astype(BF16)
        for c in range(d // FFN_NCHUNK):
            cs = slice(c * FFN_NCHUNK, (c + 1) * FFN_NCHUNK)
            acc_ref[pl.ds(r0, nrows), cs] += jnp.dot(hh, w2[:, cs], preferred_element_type=F32)

    nslab = (rows + FFN_SUB - 1) // FFN_SUB
    for ns in (range(1, gsz // FFN_SUB + 1) if ragged else (gsz // FFN_SUB,)):
        @pl.when(nslab == ns)
        def _(ns=ns):
            slab(0, ns * FFN_SUB, w1_ref[...].astype(BF16), w3_ref[...].astype(BF16), w2_ref[...].astype(BF16))

    @pl.when(jnp.logical_not(valid) & (f == 0))
    def _():
        o_ref[...] = jnp.zeros_like(o_ref)

    @pl.when(valid & (f == nf - 1))
    def _():
        if packed:
            o_ref[...] = pack_bf16_pair(acc_ref[:, 0:d // 2], acc_ref[:, d // 2:d])
        else:
            o_ref[...] = acc_ref[...].astype(o_ref.dtype)


def ffn_groups(xs, w1, w3, w2, grp_expert, grp_block, grp_rows, *, gsz=1024, tf=256, packed=False, ragged=False):
    e, d, ff = w1.shape
    r, dx = xs.shape
    nf = ff // tf
    ng = grp_expert.shape[0]

    def fmap(f, g, gr):
        return jnp.where(gr[g] > 0, f, nf - 1)

    scratch = [pltpu.VMEM((gsz, d), F32)]
    if packed:
        scratch.append(pltpu.VMEM((gsz, d), BF16))
    grid_spec = pltpu.PrefetchScalarGridSpec(
        num_scalar_prefetch=3,
        grid=(ng, nf),
        in_specs=[
            pl.BlockSpec((gsz, dx), lambda g, f, ge, gb, gr: (gb[g], 0)),
            pl.BlockSpec((None, d, tf), lambda g, f, ge, gb, gr: (ge[g], 0, fmap(f, g, gr))),
            pl.BlockSpec((None, d, tf), lambda g, f, ge, gb, gr: (ge[g], 0, fmap(f, g, gr))),
            pl.BlockSpec((None, tf, d), lambda g, f, ge, gb, gr: (ge[g], fmap(f, g, gr), 0)),
        ],
        out_specs=pl.BlockSpec((gsz, dx), lambda g, f, ge, gb, gr: (gb[g], 0)),
        scratch_shapes=scratch,
    )
    return pl.pallas_call(
        functools.partial(_ffn_kernel, nf=nf, gsz=gsz, packed=packed, ragged=ragged),
        out_shape=jax.ShapeDtypeStruct((r, dx), xs.dtype),
        grid_spec=grid_spec,
        compiler_params=_cparams("arbitrary", "arbitrary"),
        name="ffn_groups",
    )(grp_expert, grp_block, grp_rows, xs, w1, w3, w2)


def _row_copy(src_ref, src_row, dst_ref, dst_row, sem):
    return pltpu.make_async_copy(src_ref.at[pl.ds(src_row, 1), :], dst_ref.at[pl.ds(dst_row, 1), :], sem)


DMA_UNROLL = 8


def _dispatch_kernel(pos0_ref, pos1_ref, h_ref, xs_in_ref, xs_ref, sem):
    del xs_in_ref
    tm = h_ref.shape[0]
    base = pl.program_id(0) * tm

    def copies(i):
        return [_row_copy(h_ref, i, xs_ref, p[base + i], sem) for p in (pos0_ref, pos1_ref)]

    def issue(i, c):
        for cp in copies(i):
            cp.start()
        return c

    lax.fori_loop(0, tm, issue, 0, unroll=DMA_UNROLL)

    def drain(i, c):
        for cp in copies(i):
            cp.wait()
        return c

    lax.fori_loop(0, tm, drain, 0, unroll=DMA_UNROLL)


def moe_dispatch(hp, pos0, pos1, n_rows, tm=512):
    t, w = hp.shape
    xs0 = jnp.zeros((n_rows, w), hp.dtype)
    grid_spec = pltpu.PrefetchScalarGridSpec(
        num_scalar_prefetch=2,
        grid=(t // tm,),
        in_specs=[pl.BlockSpec((tm, w), lambda i, p0, p1: (i, 0)), pl.BlockSpec(memory_space=pl.ANY)],
        out_specs=pl.BlockSpec(memory_space=pl.ANY),
        scratch_shapes=[pltpu.SemaphoreType.DMA],
    )
    return pl.pallas_call(
        _dispatch_kernel,
        out_shape=jax.ShapeDtypeStruct((n_rows, w), hp.dtype),
        grid_spec=grid_spec,
        input_output_aliases={3: 0},
        compiler_params=_cparams("arbitrary"),
        name="moe_dispatch",
    )(pos0, pos1, hp, xs0)


def _t5_bucket_np(dist):
    max_exact = REL_BUCKETS // 2
    d_f = np.maximum(dist, 1).astype(np.float32)
    large = max_exact + (np.log(d_f / np.float32(max_exact)) / np.float32(math.log(REL_MAX_DIST / max_exact))
                         * np.float32(REL_BUCKETS - max_exact)).astype(np.int32)
    large = np.minimum(large, REL_BUCKETS - 1)
    return np.where(dist < max_exact, dist, large)


RES = 4
TQ = ATTN_BLOCK
PIECE = TQ // RES


def _bucket_index_tables():
    assert DILATED_PATTERNS == ((128, 1), (512, 4), (2048, 16)) and RES == 4 and TQ == 128

    def fin(steps, dil, n_back=TQ):
        ok = (steps >= 0) & (steps <= n_back)
        return np.where(ok, _t5_bucket_np(np.clip(steps, 0, n_back) * dil), -1).astype(np.int32)

    i = np.arange(TQ)[:, None]
    t4 = fin(i + TQ - np.arange(2 * TQ)[None, :], 4)
    t16 = fin(i - np.arange(TQ)[None, :], 16)
    rq, mq = i // PIECE, i % PIECE
    jk = np.arange(2 * TQ)[None, :]
    rk, mk = jk // (2 * PIECE), jk % (2 * PIECE)
    t1 = fin(RES * (PIECE + mq - mk) + (rq - rk), 1)
    t1first = fin(RES * (mq - mk) + (rq - rk), 1)
    return t4, t1, t1first, t16


def _bias_tab_kernel(rel_ref, *refs):
    n = len(refs) // 2
    h = pl.program_id(0)
    for idx_ref, o_ref in zip(refs[:n], refs[n:]):
        idx = idx_ref[...]
        tab = jnp.full(idx.shape, NEG_INF, F32)
        for b in range(REL_BUCKETS):
            tab = jnp.where(idx == b, rel_ref[b, h], tab)
        o_ref[0] = tab


def attn_bias_tables(rel_bias):
    idx_tabs = [jnp.asarray(t) for t in _bucket_index_tables()]
    nh = rel_bias.shape[1]
    return pl.pallas_call(
        _bias_tab_kernel,
        out_shape=tuple(jax.ShapeDtypeStruct((nh,) + t.shape, F32) for t in idx_tabs),
        grid=(nh,),
        in_specs=[pl.BlockSpec(memory_space=pltpu.SMEM)] + [pl.BlockSpec(t.shape, lambda h: (0, 0)) for t in idx_tabs],
        out_specs=tuple(pl.BlockSpec((1,) + t.shape, lambda h: (h, 0, 0)) for t in idx_tabs),
        compiler_params=_cparams("parallel"),
        name="attn_bias_tables",
    )(rel_bias.astype(F32), *idx_tabs)


def _attn_kernel(q_ref, k_ref, v_ref, t4_ref, t1_ref, t1f_ref, t16_ref, o_ref, acc, m_s, l_s, qs, ks, vs, od, md, ld,
                 *, scale):
    nres, lsub, hd = q_ref.shape
    nt = (((1,), (1,)), ((), ()))

    def tile(q, k, v, bias):
        s = lax.dot_general(q, k, nt, preferred_element_type=F32) * scale + bias
        m = jnp.max(s, axis=-1, keepdims=True)
        p = jnp.exp(s - m)
        l = jnp.sum(p, axis=-1, keepdims=True)
        o = jnp.dot(p.astype(BF16), v, preferred_element_type=F32)
        return o, jnp.broadcast_to(m, o.shape), jnp.broadcast_to(l, o.shape)

    def merge(r, rows, o, m, l):
        m0 = m_s[r, rows, :]
        mn = jnp.maximum(m0, m)
        a = jnp.exp(m0 - mn)
        b = jnp.exp(m - mn)
        acc[r, rows, :] = acc[r, rows, :] * a + o * b
        l_s[r, rows, :] = l_s[r, rows, :] * a + l * b
        m_s[r, rows, :] = mn

    nsub = lsub // TQ
    for r in range(nres):
        qs[r] = q_ref[r].astype(F32)
        ks[r] = k_ref[r].astype(F32)
        vs[r] = v_ref[r].astype(F32)
    masked = jnp.full((TQ, TQ), NEG_INF, F32)
    bias16 = (jnp.concatenate([t16_ref[0], masked], axis=1), jnp.concatenate([masked, t16_ref[0]], axis=1))
    for r in range(nres):
        for pair in range(nsub // 2):
            sub = [pl.ds(2 * pair + e, TQ, stride=nsub) for e in range(2)]
            kk = jnp.concatenate([ks[r, s, :] for s in sub], axis=0).astype(BF16)
            vv = jnp.concatenate([vs[r, s, :] for s in sub], axis=0).astype(BF16)
            for e in range(2):
                o, m, l = tile(qs[r, sub[e], :].astype(BF16), kk, vv, bias16[e])
                od[r * nsub + 2 * pair + e] = o
                md[r * nsub + 2 * pair + e] = m
                ld[r * nsub + 2 * pair + e] = l

    for r in range(nres):
        for i in range(lsub // TQ):
            rows = pl.ds(i * TQ, TQ)
            if i == 0:
                krows = pl.ds(0, 2 * TQ)
                bias = jnp.concatenate([t4_ref[0, :, TQ:2 * TQ], masked], axis=1)
            else:
                krows = pl.ds((i - 1) * TQ, 2 * TQ)
                bias = t4_ref[0]
            o, m, l = tile(q_ref[r, rows, :], k_ref[r, krows, :], v_ref[r, krows, :], bias)
            acc[r, rows, :] = o
            m_s[r, rows, :] = m
            l_s[r, rows, :] = l

    for j in range(lsub // PIECE):
        q = jnp.concatenate([q_ref[r, pl.ds(j * PIECE, PIECE), :] for r in range(nres)], axis=0)
        if j == 0:
            krows, bias = pl.ds(0, 2 * PIECE), t1f_ref[0]
        else:
            krows, bias = pl.ds((j - 1) * PIECE, 2 * PIECE), t1_ref[0]
        k = jnp.concatenate([k_ref[r, krows, :] for r in range(nres)], axis=0)
        v = jnp.concatenate([v_ref[r, krows, :] for r in range(nres)], axis=0)
        o, m, l = tile(q, k, v, bias)
        for r in range(nres):
            part = slice(r * PIECE, (r + 1) * PIECE)
            merge(r, pl.ds(j * PIECE, PIECE), o[part], m[part], l[part])

    for r in range(nres):
        for c in range(nsub):
            merge(r, pl.ds(c, TQ, stride=nsub), od[r * nsub + c], md[r * nsub + c], ld[r * nsub + c])

    for r in range(nres):
        o_ref[r] = (acc[r] / l_s[r]).astype(o_ref.dtype)


def attention(qkv, tabs, *, bsz, seq, hd):
    dm = ATTN_HEADS * hd
    lsub = seq // RES
    assert hd == V7X_LANES and lsub // TQ == 16 // RES and seq // 16 == TQ
    view = qkv.reshape(bsz * RES, lsub, 3 * dm)
    blk = lambda which: pl.BlockSpec((RES, lsub, hd), lambda b, h: (b, 0, which * ATTN_HEADS + h))
    tab = lambda t: pl.BlockSpec((1,) + t.shape[1:], lambda b, h: (h, 0, 0))
    st = pltpu.VMEM((RES, lsub, hd), F32)
    sub_f = pltpu.VMEM((RES * lsub // TQ, TQ, hd), F32)
    o = pl.pallas_call(
        functools.partial(_attn_kernel, scale=hd ** -0.5),
        out_shape=jax.ShapeDtypeStruct((bsz * RES, lsub, dm), BF16),
        grid=(bsz, ATTN_HEADS),
        in_specs=[blk(0), blk(1), blk(2)] + [tab(t) for t in tabs],
        out_specs=pl.BlockSpec((RES, lsub, hd), lambda b, h: (b, 0, h)),
        scratch_shapes=[st, st, st, st, st, st, sub_f, sub_f, sub_f],
        compiler_params=_cparams("parallel", "arbitrary"),
        name="attention",
    )(view, view, view, *tabs)
    return o.reshape(bsz * seq, dm)


def _split_bf16(x):
    hi = x.astype(BF16)
    lo = (x - hi.astype(F32)).astype(BF16)
    return hi, lo


def _moe_route_kernel(x_ref, g_ref, wr_ref, h_ref, meta_ref, cnt_ref, carry_ref):
    step = pl.program_id(0)
    tm = x_ref.shape[0]

    @pl.when(step == 0)
    def _():
        carry_ref[...] = jnp.zeros_like(carry_ref)

    h = _rms(x_ref[...], g_ref[...])
    h_hi, h_lo = _split_bf16(h)
    dh = h.shape[1] // 2
    h_ref[...] = pack_bf16_pair(h[:, 0:dh], h[:, dh:2 * dh])
    w_hi, w_lo = _split_bf16(wr_ref[...])
    logits = (jnp.dot(h_hi, w_hi, preferred_element_type=F32) + jnp.dot(h_lo, w_hi, preferred_element_type=F32)
              + jnp.dot(h_hi, w_lo, preferred_element_type=F32))
    lane = lax.broadcasted_iota(jnp.int32, (tm, V7X_LANES), 1).astype(F32)
    lg = jnp.where(lane < N_EXPERTS, logits, NEG_INF)
    m1 = jnp.max(lg, axis=-1, keepdims=True)
    i1 = jnp.min(jnp.where(lg == m1, lane, float(V7X_LANES)), axis=-1, keepdims=True)
    lg2 = jnp.where(lane == i1, NEG_INF, lg)
    m2 = jnp.max(lg2, axis=-1, keepdims=True)
    i2 = jnp.min(jnp.where(lg2 == m2, lane, float(V7X_LANES)), axis=-1, keepdims=True)
    e2 = jnp.exp(m2 - m1)
    g1 = 1.0 / (1.0 + e2)
    g2 = e2 / (1.0 + e2)
    oh = ((lane == i1) | (lane == i2)).astype(BF16)
    r_i = lax.broadcasted_iota(jnp.int32, (tm, tm), 0)
    c_i = lax.broadcasted_iota(jnp.int32, (tm, tm), 1)
    tri = (c_i < r_i).astype(BF16)
    rank = jnp.dot(tri, oh, preferred_element_type=F32) + carry_ref[0:1, :]
    r1 = jnp.sum(jnp.where(lane == i1, rank, 0.0), axis=-1, keepdims=True)
    r2 = jnp.sum(jnp.where(lane == i2, rank, 0.0), axis=-1, keepdims=True)
    carry_ref[0:1, :] = carry_ref[0:1, :] + jnp.sum(oh.astype(F32), axis=0, keepdims=True)
    meta = jnp.where(lane == 0, i1.astype(F32), 0.0)
    meta = jnp.where(lane == 1, i2.astype(F32), meta)
    meta = jnp.where(lane == 2, g1, meta)
    meta = jnp.where(lane == 3, g2, meta)
    meta = jnp.where(lane == 4, r1, meta)
    meta = jnp.where(lane == 5, r2, meta)
    meta_ref[...] = meta
    cnt_ref[...] = jnp.broadcast_to(carry_ref[0:1, :], cnt_ref.shape)


def moe_route(x, g, w_router_p, tm=256):
    t, d = x.shape
    return pl.pallas_call(
        _moe_route_kernel,
        out_shape=(jax.ShapeDtypeStruct((t, d // 2), jnp.uint32), jax.ShapeDtypeStruct((t, V7X_LANES), F32),
                   jax.ShapeDtypeStruct((8, V7X_LANES), F32)),
        grid=(t // tm,),
        in_specs=[pl.BlockSpec((tm, d), lambda i: (i, 0)), pl.BlockSpec((1, d), lambda i: (0, 0)),
                  pl.BlockSpec((d, V7X_LANES), lambda i: (0, 0))],
        out_specs=(pl.BlockSpec((tm, d // 2), lambda i: (i, 0)), pl.BlockSpec((tm, V7X_LANES), lambda i: (i, 0)),
                   pl.BlockSpec((8, V7X_LANES), lambda i: (0, 0))),
        scratch_shapes=[pltpu.VMEM((8, V7X_LANES), F32)],
        compiler_params=_cparams("arbitrary"),
        name="moe_route",
    )(x, g.reshape(1, d), w_router_p)


def _moe_final_kernel(pos0_ref, pos1_ref, x_ref, meta_ref, g_ref, ys_ref, o_ref, ybuf, o_scr, sem, *, lsub):
    nres, tp, d = x_ref.shape
    dh = d // 2
    b = pl.program_id(0)
    j = pl.program_id(1)

    def copies(r, mm):
        tok = (b * nres + r) * lsub + j * tp + mm
        return [_row_copy(ys_ref, p[tok], ybuf.at[k], r * tp + mm, sem) for k, p in enumerate((pos0_ref, pos1_ref))]

    for r in range(nres):
        def issue(mm, c, r=r):
            for cp in copies(r, mm):
                cp.start()
            return c

        lax.fori_loop(0, tp, issue, 0, unroll=DMA_UNROLL)

    for r in range(nres):
        def drain(mm, c, r=r):
            for cp in copies(r, mm):
                cp.wait()
            return c

        lax.fori_loop(0, tp, drain, 0, unroll=DMA_UNROLL)
    for r in range(nres):
        rows = slice(r * tp, (r + 1) * tp)
        meta = meta_ref[r]
        g1 = meta[:, 2:3]
        g2 = meta[:, 3:4]
        a0, b0 = unpack_bf16_pair(ybuf[0, rows, :])
        a1, b1 = unpack_bf16_pair(ybuf[1, rows, :])
        x = x_ref[r]
        xa = x[:, 0:dh] + a0 * g1 + a1 * g2
        xb = x[:, dh:d] + b0 * g1 + b1 * g2
        ms = (jnp.sum(xa * xa, axis=-1, keepdims=True) + jnp.sum(xb * xb, axis=-1, keepdims=True)) * (1.0 / d)
        inv = lax.rsqrt(ms + EPS)
        ya = xa * inv * g_ref[:, 0:dh]
        yb = xb * inv * g_ref[:, dh:d]
        out_rows = pl.ds(r, tp, stride=nres)
        for s in range(dh // V7X_LANES):
            ls = slice(s * V7X_LANES, (s + 1) * V7X_LANES)
            o_scr[s, out_rows, :] = ya[:, ls]
            o_scr[dh // V7X_LANES + s, out_rows, :] = yb[:, ls]
    for s in range(d // V7X_LANES):
        o_ref[:, s * V7X_LANES:(s + 1) * V7X_LANES] = o_scr[s]


def moe_final(x, ys, pos0, pos1, meta, g, *, bsz, seq, tp=128):
    t, d = x.shape
    lsub = seq // RES
    tm = RES * tp
    xv = x.reshape(bsz * RES, lsub, d)
    mv = meta.reshape(bsz * RES, lsub, V7X_LANES)
    grid_spec = pltpu.PrefetchScalarGridSpec(
        num_scalar_prefetch=2,
        grid=(bsz, lsub // tp),
        in_specs=[pl.BlockSpec((RES, tp, d), lambda b, j, p0, p1: (b, j, 0)),
                  pl.BlockSpec((RES, tp, V7X_LANES), lambda b, j, p0, p1: (b, j, 0)),
                  pl.BlockSpec((1, d), lambda b, j, p0, p1: (0, 0)), pl.BlockSpec(memory_space=pl.ANY)],
        out_specs=pl.BlockSpec((tm, d), lambda b, j, p0, p1: (b * (lsub // tp) + j, 0)),
        scratch_shapes=[pltpu.VMEM((TOP_K, tm, d // 2), jnp.uint32), pltpu.VMEM((d // V7X_LANES, tm, V7X_LANES), F32),
                        pltpu.SemaphoreType.DMA],
    )
    return pl.pallas_call(
        functools.partial(_moe_final_kernel, lsub=lsub),
        out_shape=jax.ShapeDtypeStruct((t, d), F32),
        grid_spec=grid_spec,
        compiler_params=_cparams("arbitrary", "arbitrary"),
        name="moe_final",
    )(pos0, pos1, xv, mv, g.reshape(1, d), ys)


def _pad_lanes(v, fill=0.0):
    v = v.reshape(1, -1).astype(F32)
    return jnp.pad(v, ((0, 0), (0, V7X_LANES - v.shape[1])), constant_values=fill)


def even_layer(x, h, p, *, bsz, seq):
    t, d = x.shape
    d_ssd = d
    d_conf = d
    heads = d_ssd // SSD_HEAD_DIM
    conv_dim = d_ssd + 2 * SSD_GROUPS * SSD_STATE
    i1 = d_ssd + conv_dim
    i2 = i1 + heads
    w_in_t = jnp.swapaxes(p["w_in"], 0, 1)
    zx = matmul_nt(h, w_in_t, row_start=0, n_cols=i1, tn=1024, name="in_proj_zx")
    dt_raw = matmul_nt(h, w_in_t, row_start=i1, n_cols=V7X_LANES, tn=V7X_LANES, out_dtype=F32, name="in_proj_dt")
    conf = matmul_nt(h, w_in_t, row_start=i2, n_cols=2 * d_conf, tn=1024, name="in_proj_conf")
    xc = ssd_conv(zx, p["conv_w"], p["conv_b"], bsz=bsz, seq=seq, col_off=d_ssd)
    dt, acs, dtt, acst = ssd_prep(dt_raw, _pad_lanes(p["dt_bias"]), _pad_lanes(p["a_log"]), bsz=bsz, seq=seq)
    dskip_e = jnp.repeat(p["d_skip"].astype(F32), SSD_HEAD_DIM).reshape(1, d_ssd)
    y_ssd = ssd_main(xc, zx, dt, acs, dtt, acst, dskip_e, p["ssd_norm_w"], bsz=bsz, seq=seq, d_ssd=d_ssd)
    u = conf_module(conf, p["conf_dw_w"], p["conf_dw_b"], p["conf_ln_w"], p["conf_ln_b"], bsz=bsz, seq=seq)
    return matmul([y_ssd, u], p["w_out"], n_cols=d, tn=512, res=x, out_dtype=F32, name="even_out_proj")


def dense_ffn(h, w1, w3, w2, gsz=1024):
    t = h.shape[0]
    ng = t // gsz
    ge = jnp.zeros((ng,), jnp.int32)
    gb = jnp.arange(ng, dtype=jnp.int32)
    gr = jnp.full((ng,), gsz, jnp.int32)
    return ffn_groups(h, w1[None], w3[None], w2[None], ge, gb, gr, gsz=gsz)


def odd_layer_attn(x, h, p, *, bsz, seq):
    t, d = x.shape
    hd = d // ATTN_HEADS
    qkv = matmul([h], p["w_qkv"], n_cols=3 * d, tn=1024, name="qkv_proj")
    tabs = attn_bias_tables(p["rel_bias"])
    o = attention(qkv, tabs, bsz=bsz, seq=seq, hd=hd)
    return matmul([o], p["w_attn_out"], n_cols=d, tn=512, tm=2048, res=x, out_dtype=F32, name="attn_out_proj")


def moe_layer(x, norm_g, w_router, w1, w3, w2, final_g, *, bsz, seq, gsz=1280):
    t, d = x.shape
    ne = w1.shape[0]
    w_router_p = jnp.pad(w_router.astype(F32), ((0, 0), (0, V7X_LANES - ne)))
    h, meta, cnt = moe_route(x, norm_g, w_router_p)
    counts = cnt[0, :ne].astype(jnp.int32)
    ngrp_e = (counts + gsz - 1) // gsz
    per_grp = (counts + jnp.maximum(ngrp_e, 1) - 1) // jnp.maximum(ngrp_e, 1)
    rpg_e = jnp.maximum((per_grp + FFN_SUB - 1) // FFN_SUB * FFN_SUB, FFN_SUB)
    grp_end = jnp.cumsum(ngrp_e)
    grp_start = grp_end - ngrp_e
    ng = (t * TOP_K) // gsz + ne
    j = jnp.arange(ng, dtype=jnp.int32)
    total = grp_end[-1]
    last = jnp.maximum(total - 1, 0)
    jj = jnp.minimum(j, last)
    ge = jnp.searchsorted(grp_end, jj, side="right").astype(jnp.int32)
    ge = jnp.minimum(ge, ne - 1)
    rows = jnp.clip(counts[ge] - (jj - grp_start[ge]) * rpg_e[ge], 0, rpg_e[ge])
    gr = jnp.where(j < total, rows, 0).astype(jnp.int32)
    gb = j
    max_grp = (t + gsz - 1) // gsz

    def dest_row(e, rank):
        rpg = jnp.zeros_like(rank)
        gs = jnp.zeros_like(rank)
        for k in range(ne):
            rpg = jnp.where(e == k, rpg_e[k], rpg)
            gs = jnp.where(e == k, grp_start[k], gs)
        g_in = jnp.zeros_like(rank)
        for m in range(1, max_grp):
            g_in = g_in + (rank >= m * rpg).astype(jnp.int32)
        return (gs + g_in) * gsz + rank - g_in * rpg

    pos0 = dest_row(meta[:, 0].astype(jnp.int32), meta[:, 4].astype(jnp.int32))
    pos1 = dest_row(meta[:, 1].astype(jnp.int32), meta[:, 5].astype(jnp.int32))
    xs = moe_dispatch(h, pos0, pos1, ng * gsz)
    ys = ffn_groups(xs, w1, w3, w2, ge, gb, gr, gsz=gsz, packed=True, ragged=True)
    return moe_final(x, ys, pos0, pos1, meta, final_g, bsz=bsz, seq=seq)


def kernel(x, norm_mix, norm_ffn, norm_final, even_w_in, ssd_conv_w, ssd_conv_b, ssd_dt_bias, ssd_a_log, ssd_d, ssd_norm_w, conf_dw_w, conf_dw_b, conf_ln_w, conf_ln_b, even_w_out, ffn_w1, ffn_w3, ffn_w2, attn_w_qkv, attn_w_out, rel_bias, moe_router, moe_w1, moe_w3, moe_w2):
    bsz, seq, d = x.shape
    assert norm_mix.shape[0] == 2, "two-layer trunk: one even (SSD+Conformer/FFN) and one odd (attention/MoE) layer"
    xf = x.reshape(bsz * seq, d)
    h = rmsnorm(xf, norm_mix[0])
    p_even = dict(w_in=even_w_in[0], conv_w=ssd_conv_w[0], conv_b=ssd_conv_b[0], dt_bias=ssd_dt_bias[0],
                  a_log=ssd_a_log[0], d_skip=ssd_d[0], ssd_norm_w=ssd_norm_w[0], conf_dw_w=conf_dw_w[0],
                  conf_dw_b=conf_dw_b[0], conf_ln_w=conf_ln_w[0], conf_ln_b=conf_ln_b[0], w_out=even_w_out[0])
    x1 = even_layer(xf, h, p_even, bsz=bsz, seq=seq)
    h1 = rmsnorm(x1, norm_ffn[0])
    y_ffn = dense_ffn(h1, ffn_w1[0], ffn_w3[0], ffn_w2[0])
    x2, h2 = add_norm_to_residue_major(x1, y_ffn, norm_mix[1], bsz=bsz, seq=seq, nres=RES)
    p_odd = dict(w_qkv=attn_w_qkv[0], w_attn_out=attn_w_out[0], rel_bias=rel_bias)
    x3 = odd_layer_attn(x2, h2, p_odd, bsz=bsz, seq=seq)
    out = moe_layer(x3, norm_ffn[1], moe_router[0], moe_w1[0], moe_w3[0], moe_w2[0], norm_final, bsz=bsz, seq=seq)
    return out.reshape(bsz, seq, d)
```

```python
import functools
import math

import numpy as np
import jax
import jax.numpy as jnp
from jax import lax
from jax.experimental import pallas as pl
from jax.experimental.pallas import tpu as pltpu

SSD_HEAD_DIM = 64
SSD_GROUPS = 4
SSD_STATE = 128
SSD_CONV = 4
SSD_CHUNK = 128
CONF_KERNEL = 31
ATTN_HEADS = 16
DILATED_PATTERNS = ((128, 1), (512, 4), (2048, 16))
ATTN_BLOCK = 128
REL_BUCKETS = 32
REL_MAX_DIST = 2048
N_EXPERTS = 8
TOP_K = 2
EPS = 1e-6

V7X_LANES = 128
V7X_VMEM_BYTES = 64 * 1024 * 1024
VMEM_LIMIT = 56 * 1024 * 1024

F32 = jnp.float32
BF16 = jnp.bfloat16
NEG_INF = float("-inf")


def _cparams(*sem):
    return pltpu.CompilerParams(dimension_semantics=tuple(sem), vmem_limit_bytes=VMEM_LIMIT)


def _rms(x, g):
    ms = jnp.mean(x * x, axis=-1, keepdims=True)
    return x * lax.rsqrt(ms + EPS) * g


def _rmsnorm_kernel(x_ref, g_ref, h_ref):
    h_ref[...] = _rms(x_ref[...], g_ref[...]).astype(h_ref.dtype)


def rmsnorm(x, g, out_dtype=BF16, tm=512):
    t, d = x.shape
    return pl.pallas_call(
        _rmsnorm_kernel,
        out_shape=jax.ShapeDtypeStruct((t, d), out_dtype),
        grid=(t // tm,),
        in_specs=[pl.BlockSpec((tm, d), lambda i: (i, 0)), pl.BlockSpec((1, d), lambda i: (0, 0))],
        out_specs=pl.BlockSpec((tm, d), lambda i: (i, 0)),
        compiler_params=_cparams("parallel"),
        name="rmsnorm",
    )(x, g.reshape(1, d))


def _add_norm_res_kernel(x_ref, y_ref, g_ref, xo_ref, ho_ref, xs_scr, hs_scr):
    nres, tp, d = xo_ref.shape
    xn = x_ref[...] + y_ref[...].astype(F32)
    h = _rms(xn, g_ref[...])
    for s in range(d // V7X_LANES):
        ls = slice(s * V7X_LANES, (s + 1) * V7X_LANES)
        xs_scr[s] = xn[:, ls]
        hs_scr[s] = h[:, ls]
    for r in range(nres):
        rows = pl.ds(r, tp, stride=nres)
        for s in range(d // V7X_LANES):
            ls = slice(s * V7X_LANES, (s + 1) * V7X_LANES)
            xo_ref[r, :, ls] = xs_scr[s, rows, :]
            ho_ref[r, :, ls] = hs_scr[s, rows, :].astype(ho_ref.dtype)


def add_norm_to_residue_major(x, y, g, *, bsz, seq, nres, tp=128):
    t, d = x.shape
    lsub = seq // nres
    tm = nres * tp
    nj = lsub // tp
    slab = pltpu.VMEM((d // V7X_LANES, tm, V7X_LANES), F32)
    xo, ho = pl.pallas_call(
        _add_norm_res_kernel,
        out_shape=(jax.ShapeDtypeStruct((bsz * nres, lsub, d), F32), jax.ShapeDtypeStruct((bsz * nres, lsub, d), BF16)),
        grid=(bsz, nj),
        in_specs=[pl.BlockSpec((tm, d), lambda b, j: (b * nj + j, 0)), pl.BlockSpec((tm, d), lambda b, j: (b * nj + j, 0)),
                  pl.BlockSpec((1, d), lambda b, j: (0, 0))],
        out_specs=(pl.BlockSpec((nres, tp, d), lambda b, j: (b, j, 0)), pl.BlockSpec((nres, tp, d), lambda b, j: (b, j, 0))),
        scratch_shapes=[slab, slab],
        compiler_params=_cparams("parallel", "parallel"),
        name="add_norm_reorder",
    )(x, y, g.reshape(1, d))
    return xo.reshape(t, d), ho.reshape(t, d)


def _mm_kernel(*refs, n_lhs, ks, has_res):
    xs = refs[:n_lhs]
    w_ref = refs[n_lhs]
    r_ref = refs[n_lhs + 1] if has_res else None
    o_ref = refs[-1]
    acc = None
    k0 = 0
    for x_ref, k in zip(xs, ks):
        part = jnp.dot(x_ref[...], w_ref[k0:k0 + k, :].astype(BF16), preferred_element_type=F32)
        acc = part if acc is None else acc + part
        k0 += k
    if has_res:
        acc = acc + r_ref[...]
    o_ref[...] = acc.astype(o_ref.dtype)


def matmul(xs, w, *, n_cols, col_block_off=0, tn, tm=1024, res=None, out_dtype=BF16, name="matmul"):
    t = xs[0].shape[0]
    ks = tuple(x.shape[1] for x in xs)
    ktot = sum(ks)
    assert w.shape[0] == ktot and n_cols % tn == 0 and t % tm == 0
    in_specs = [pl.BlockSpec((tm, k), lambda i, j: (i, 0)) for k in ks]
    in_specs.append(pl.BlockSpec((ktot, tn), lambda i, j: (0, j + col_block_off)))
    args = list(xs) + [w]
    if res is not None:
        in_specs.append(pl.BlockSpec((tm, tn), lambda i, j: (i, j)))
        args.append(res)
    return pl.pallas_call(
        functools.partial(_mm_kernel, n_lhs=len(xs), ks=ks, has_res=res is not None),
        out_shape=jax.ShapeDtypeStruct((t, n_cols), out_dtype),
        grid=(t // tm, n_cols // tn),
        in_specs=in_specs,
        out_specs=pl.BlockSpec((tm, tn), lambda i, j: (i, j)),
        compiler_params=_cparams("parallel", "arbitrary"),
        name=name,
    )(*args)


def _mm_nt_kernel(x_ref, wt_ref, o_ref):
    wt = wt_ref[...].astype(BF16)
    acc = lax.dot_general(x_ref[...], wt, (((1,), (1,)), ((), ())), preferred_element_type=F32)
    o_ref[...] = acc.astype(o_ref.dtype)


def matmul_nt(x, w_t, *, row_start, n_cols, tn, tm=1024, out_dtype=BF16, name="matmul_nt"):
    t, k = x.shape
    assert w_t.shape[1] == k and n_cols % tn == 0 and t % tm == 0
    if row_start % tn == 0:
        w_spec = pl.BlockSpec((tn, k), lambda i, j: (row_start // tn + j, 0))
    else:
        assert row_start % SUBLANES == 0 and tn % SUBLANES == 0
        w_spec = pl.BlockSpec((pl.Element(tn), pl.Element(k)),
                              lambda i, j: ((row_start // SUBLANES + j * (tn // SUBLANES)) * SUBLANES, 0))
    return pl.pallas_call(
        _mm_nt_kernel,
        out_shape=jax.ShapeDtypeStruct((t, n_cols), out_dtype),
        grid=(t // tm, n_cols // tn),
        in_specs=[pl.BlockSpec((tm, k), lambda i, j: (i, 0)), w_spec],
        out_specs=pl.BlockSpec((tm, tn), lambda i, j: (i, j)),
        compiler_params=_cparams("parallel", "arbitrary"),
        name=name,
    )(x, w_t)


HALO_BF16 = 16


def _ssd_conv_kernel(cur_ref, halo_ref, w_ref, b_ref, o_ref, buf_ref, *, ts):
    i = pl.program_id(1)
    halo = halo_ref[...].astype(F32)
    buf_ref[0:HALO_BF16, :] = jnp.where(i > 0, halo, 0.0)
    buf_ref[HALO_BF16:HALO_BF16 + ts, :] = cur_ref[...].astype(F32)
    acc = b_ref[...]
    for k in range(SSD_CONV):
        off = HALO_BF16 - (SSD_CONV - 1) + k
        acc = acc + w_ref[k:k + 1, :] * buf_ref[off:off + ts, :]
    o_ref[...] = (acc * jax.nn.sigmoid(acc)).astype(o_ref.dtype)


def ssd_conv(zx, conv_w, conv_b, *, bsz, seq, col_off, ts=1024, tc=512):
    c = conv_w.shape[1]
    t = zx.shape[0]
    nsb = seq // ts
    cb0 = col_off // tc
    hb = ts // HALO_BF16
    return pl.pallas_call(
        functools.partial(_ssd_conv_kernel, ts=ts),
        out_shape=jax.ShapeDtypeStruct((t, c), BF16),
        grid=(bsz, nsb, c // tc),
        in_specs=[
            pl.BlockSpec((ts, tc), lambda b, i, j: (b * nsb + i, cb0 + j)),
            pl.BlockSpec((HALO_BF16, tc), lambda b, i, j: (jnp.maximum((b * nsb + i) * hb - 1, 0), cb0 + j)),
            pl.BlockSpec((SSD_CONV, tc), lambda b, i, j: (0, j)),
            pl.BlockSpec((1, tc), lambda b, i, j: (0, j)),
        ],
        out_specs=pl.BlockSpec((ts, tc), lambda b, i, j: (b * nsb + i, j)),
        scratch_shapes=[pltpu.VMEM((ts + HALO_BF16, tc), F32)],
        compiler_params=_cparams("parallel", "parallel", "parallel"),
        name="ssd_conv",
    )(zx, zx, conv_w, conv_b.reshape(1, c))


def _ssd_prep_kernel(raw_ref, bias_ref, alog_ref, dt_ref, acs_ref, dtt_ref, acst_ref, *, seq):
    raw = raw_ref[...] + bias_ref[...]
    dt = jnp.maximum(raw, 0.0) + jnp.log1p(jnp.exp(-jnp.abs(raw)))
    a = -jnp.exp(alog_ref[...])
    x = dt * a
    row = lax.broadcasted_iota(jnp.int32, x.shape, 0) % SSD_CHUNK
    sh = 1
    while sh < SSD_CHUNK:
        x = x + jnp.where(row >= sh, pltpu.roll(x, sh, 0), 0.0)
        sh *= 2
    dt_ref[...] = dt
    acs_ref[...] = x
    for c in range(seq // SSD_CHUNK):
        sl = slice(c * SSD_CHUNK, (c + 1) * SSD_CHUNK)
        dtt_ref[sl, :] = dt[sl, :].T
        acst_ref[sl, :] = x[sl, :].T


def ssd_prep(dt_raw, dt_bias_p, a_log_p, *, bsz, seq):
    t = dt_raw.shape[0]
    shp = jax.ShapeDtypeStruct((t, V7X_LANES), F32)
    blk = pl.BlockSpec((seq, V7X_LANES), lambda b: (b, 0))
    vec = pl.BlockSpec((1, V7X_LANES), lambda b: (0, 0))
    return pl.pallas_call(
        functools.partial(_ssd_prep_kernel, seq=seq),
        out_shape=(shp, shp, shp, shp),
        grid=(bsz,),
        in_specs=[blk, vec, vec],
        out_specs=(blk, blk, blk, blk),
        compiler_params=_cparams("parallel"),
        name="ssd_prep",
    )(dt_raw, dt_bias_p, a_log_p)


def _ssd_main_kernel(x_ref, b_ref, c_ref, z_ref, dt_ref, acs_ref, dtt_ref, acst_ref, dsk_ref, nw_ref,
                     y_ref, st_ref, *, heads_per_group):
    ci = pl.program_id(1)
    L = SSD_CHUNK
    gw = heads_per_group * SSD_HEAD_DIM
    npair = heads_per_group // 2

    @pl.when(ci == 0)
    def _():
        st_ref[...] = jnp.zeros_like(st_ref)

    dt = dt_ref[...]
    acs = acs_ref[...]
    dtt = dtt_ref[...]
    acst = acst_ref[...]
    row = lax.broadcasted_iota(jnp.int32, (L, L), 0)
    col = lax.broadcasted_iota(jnp.int32, (L, L), 1)
    causal = col <= row
    lane_lo = lax.broadcasted_iota(jnp.int32, (L, 2 * SSD_HEAD_DIM), 1) < SSD_HEAD_DIM
    lane_lo1 = lax.broadcasted_iota(jnp.int32, (1, 2 * SSD_HEAD_DIM), 1) < SSD_HEAD_DIM
    zero_b = jnp.zeros((L, 2 * SSD_HEAD_DIM), BF16)

    for g in range(SSD_GROUPS):
        bm = b_ref[:, g * SSD_STATE:(g + 1) * SSD_STATE]
        cm = c_ref[:, g * SSD_STATE:(g + 1) * SSD_STATE]
        bmt = bm.astype(F32).T.astype(BF16)
        cb = lax.dot_general(cm, bm, (((1,), (1,)), ((), ())), preferred_element_type=F32)
        ys = []
        for q in range(npair):
            c0 = g * gw + q * 2 * SSD_HEAD_DIM
            xp = x_ref[:, c0:c0 + 2 * SSD_HEAD_DIM]
            xpf = xp.astype(F32)
            ms = []
            for hh in range(2):
                h = g * heads_per_group + 2 * q + hh
                seg = acs[:, h:h + 1] - acst[h:h + 1, :]
                dec = jnp.exp(jnp.where(causal, seg, NEG_INF))
                ms.append((cb * dec * dtt[h:h + 1, :]).astype(BF16))
            h0 = g * heads_per_group + 2 * q
            h1 = h0 + 1
            lhs = jnp.concatenate(ms, axis=1)
            rhs = jnp.concatenate([jnp.where(lane_lo, xp, zero_b), jnp.where(lane_lo, zero_b, xp)], axis=0)
            y_diag = jnp.dot(lhs, rhs, preferred_element_type=F32)
            st = st_ref[g, :, q * 2 * SSD_HEAD_DIM:(q + 1) * 2 * SSD_HEAD_DIM]
            e_l = jnp.where(lane_lo, jnp.exp(acs[:, h0:h0 + 1]), jnp.exp(acs[:, h1:h1 + 1]))
            y_off = jnp.dot(cm, st.astype(BF16), preferred_element_type=F32) * e_l
            dsk = dsk_ref[:, c0:c0 + 2 * SSD_HEAD_DIM]
            ys.append(y_diag + y_off + xpf * dsk)
            last0 = acs[L - 1:L, h0:h0 + 1]
            last1 = acs[L - 1:L, h1:h1 + 1]
            w_l = jnp.where(lane_lo, dt[:, h0:h0 + 1] * jnp.exp(last0 - acs[:, h0:h0 + 1]),
                            dt[:, h1:h1 + 1] * jnp.exp(last1 - acs[:, h1:h1 + 1]))
            xw = (xpf * w_l).astype(BF16)
            st_new = jnp.dot(bmt, xw, preferred_element_type=F32)
            cd = jnp.where(lane_lo1, jnp.exp(last0), jnp.exp(last1))
            st_ref[g, :, q * 2 * SSD_HEAD_DIM:(q + 1) * 2 * SSD_HEAD_DIM] = st * cd + st_new
        y = jnp.concatenate(ys, axis=1)
        zg = z_ref[:, g * gw:(g + 1) * gw].astype(F32)
        yg = y * (zg * jax.nn.sigmoid(zg))
        ms_ = jnp.mean(yg * yg, axis=-1, keepdims=True)
        y_ref[:, g * gw:(g + 1) * gw] = (yg * lax.rsqrt(ms_ + EPS) * nw_ref[:, g * gw:(g + 1) * gw]).astype(y_ref.dtype)


def ssd_main(xc, zx, dt, acs, dtt, acst, dskip_e, norm_w, *, bsz, seq, d_ssd):
    t = xc.shape[0]
    nc = seq // SSD_CHUNK
    heads = d_ssd // SSD_HEAD_DIM
    hpg = heads // SSD_GROUPS
    bc_w = SSD_GROUPS * SSD_STATE
    assert d_ssd % bc_w == 0 and hpg % 2 == 0
    rowmap = lambda b, c: (b * nc + c, 0)
    return pl.pallas_call(
        functools.partial(_ssd_main_kernel, heads_per_group=hpg),
        out_shape=jax.ShapeDtypeStruct((t, d_ssd), BF16),
        grid=(bsz, nc),
        in_specs=[
            pl.BlockSpec((SSD_CHUNK, d_ssd), rowmap),
            pl.BlockSpec((SSD_CHUNK, bc_w), lambda b, c: (b * nc + c, d_ssd // bc_w)),
            pl.BlockSpec((SSD_CHUNK, bc_w), lambda b, c: (b * nc + c, d_ssd // bc_w + 1)),
            pl.BlockSpec((SSD_CHUNK, d_ssd), rowmap),
            pl.BlockSpec((SSD_CHUNK, V7X_LANES), rowmap),
            pl.BlockSpec((SSD_CHUNK, V7X_LANES), rowmap),
            pl.BlockSpec((SSD_CHUNK, V7X_LANES), rowmap),
            pl.BlockSpec((SSD_CHUNK, V7X_LANES), rowmap),
            pl.BlockSpec((1, d_ssd), lambda b, c: (0, 0)),
            pl.BlockSpec((1, d_ssd), lambda b, c: (0, 0)),
        ],
        out_specs=pl.BlockSpec((SSD_CHUNK, d_ssd), rowmap),
        scratch_shapes=[pltpu.VMEM((SSD_GROUPS, SSD_STATE, hpg * SSD_HEAD_DIM), F32)],
        compiler_params=_cparams("parallel", "arbitrary"),
        name="ssd_main",
    )(xc, xc, xc, zx, dt, acs, dtt, acst, dskip_e, norm_w.reshape(1, d_ssd))


CONF_HALO = 32


SUBLANES = 8


def _conf_kernel(a_ref, g_ref, ah_ref, gh_ref, w_ref, b_ref, lw_ref, lb_ref, o_ref, buf_ref, sh_ref, *, ts, rc, lt):
    i = pl.program_id(1)
    d = a_ref.shape[1]
    ah = ah_ref[...].astype(F32)
    gh = gh_ref[...].astype(F32)
    buf_ref[0:CONF_HALO, :] = jnp.where(i > 0, ah * jax.nn.sigmoid(gh), 0.0)
    a = a_ref[...].astype(F32)
    g = g_ref[...].astype(F32)
    buf_ref[CONF_HALO:CONF_HALO + ts, :] = a * jax.nn.sigmoid(g)
    span = ts + CONF_HALO - SUBLANES
    for j in range(1, SUBLANES):
        sh_ref[j - 1, 0:span, :] = buf_ref[j:j + span, :]
    base = CONF_HALO - (CONF_KERNEL - 1)
    outs = []
    for r in range(ts // rc):
        cols = []
        for c in range(d // lt):
            ls = slice(c * lt, (c + 1) * lt)
            acc = jnp.broadcast_to(b_ref[:, ls], (rc, lt))
            for k in range(CONF_KERNEL):
                j = (base + k) % SUBLANES
                off = r * rc + (base + k) - j
                src = buf_ref[off:off + rc, ls] if j == 0 else sh_ref[j - 1, off:off + rc, ls]
                acc = acc + w_ref[k:k + 1, ls] * src
            cols.append(acc)
        outs.append(jnp.concatenate(cols, axis=1))
    u = jnp.concatenate(outs, axis=0)
    mu = jnp.mean(u, axis=-1, keepdims=True)
    uc = u - mu
    var = jnp.mean(uc * uc, axis=-1, keepdims=True)
    y = uc * lax.rsqrt(var + EPS) * lw_ref[...] + lb_ref[...]
    o_ref[...] = (y * jax.nn.sigmoid(y)).astype(o_ref.dtype)


def conf_module(conf, dw_w, dw_b, ln_w, ln_b, *, bsz, seq, ts=256, rc=32, lt=512):
    t = conf.shape[0]
    d = dw_w.shape[1]
    nsb = seq // ts
    hb = ts // CONF_HALO
    cur = lambda col: pl.BlockSpec((ts, d), lambda b, i: (b * nsb + i, col))
    halo = lambda col: pl.BlockSpec((CONF_HALO, d), lambda b, i: (jnp.maximum((b * nsb + i) * hb - 1, 0), col))
    vec = pl.BlockSpec((1, d), lambda b, i: (0, 0))
    return pl.pallas_call(
        functools.partial(_conf_kernel, ts=ts, rc=rc, lt=lt),
        out_shape=jax.ShapeDtypeStruct((t, d), BF16),
        grid=(bsz, nsb),
        in_specs=[cur(0), cur(1), halo(0), halo(1), pl.BlockSpec((CONF_KERNEL, d), lambda b, i: (0, 0)),
                  vec, vec, vec],
        out_specs=pl.BlockSpec((ts, d), lambda b, i: (b * nsb + i, 0)),
        scratch_shapes=[pltpu.VMEM((ts + CONF_HALO, d), F32), pltpu.VMEM((SUBLANES - 1, ts + CONF_HALO, d), F32)],
        compiler_params=_cparams("parallel", "parallel"),
        name="conf_module",
    )(conf, conf, conf, conf, dw_w, dw_b.reshape(1, d), ln_w.reshape(1, d), ln_b.reshape(1, d))


FFN_SUB = 256


FFN_NCHUNK = 512


def pack_bf16_pair(a, b):
    ua = lax.bitcast_convert_type(a.astype(BF16).astype(F32), jnp.uint32)
    ub = lax.bitcast_convert_type(b.astype(BF16).astype(F32), jnp.uint32)
    return (ua >> 16) | (ub & jnp.uint32(0xFFFF0000))


def unpack_bf16_pair(u):
    a = lax.bitcast_convert_type(u << 16, F32)
    b = lax.bitcast_convert_type(u & jnp.uint32(0xFFFF0000), F32)
    return a, b


def _ffn_kernel(ge_ref, gb_ref, gr_ref, x_ref, w1_ref, w3_ref, w2_ref, o_ref, acc_ref, *xb_scr, nf, gsz, packed, ragged):
    g = pl.program_id(0)
    f = pl.program_id(1)
    rows = gr_ref[g]
    valid = rows > 0
    d = acc_ref.shape[1]

    @pl.when(valid & (f == 0))
    def _():
        acc_ref[...] = jnp.zeros_like(acc_ref)
        if packed:
            a, b = unpack_bf16_pair(x_ref[...])
            xb_scr[0][:, 0:d // 2] = a.astype(BF16)
            xb_scr[0][:, d // 2:d] = b.astype(BF16)

    xsrc = xb_scr[0] if packed else x_ref

    def slab(r0, nrows, w1, w3, w2):
        xs = xsrc[pl.ds(r0, nrows), :]
        h1 = jnp.dot(xs, w1, preferred_element_type=F32)
        h3 = jnp.dot(xs, w3, preferred_element_type=F32)
        hh = (h1 * jax.nn.sigmoid(h1) * h3).astype(BF16)
        for c in range(d // FFN_NCHUNK):
            cs = slice(c * FFN_NCHUNK, (c + 1) * FFN_NCHUNK)
            acc_ref[pl.ds(r0, nrows), cs] += jnp.dot(hh, w2[:, cs], preferred_element_type=F32)

    nslab = (rows + FFN_SUB - 1) // FFN_SUB
    for ns in (range(1, gsz // FFN_SUB + 1) if ragged else (gsz // FFN_SUB,)):
        @pl.when(nslab == ns)
        def _(ns=ns):
            slab(0, ns * FFN_SUB, w1_ref[...].astype(BF16), w3_ref[...].astype(BF16), w2_ref[...].astype(BF16))

    @pl.when(jnp.logical_not(valid) & (f == 0))
    def _():
        o_ref[...] = jnp.zeros_like(o_ref)

    @pl.when(valid & (f == nf - 1))
    def _():
        if packed:
            o_ref[...] = pack_bf16_pair(acc_ref[:, 0:d // 2], acc_ref[:, d // 2:d])
        else:
            o_ref[...] = acc_ref[...].astype(o_ref.dtype)


def ffn_groups(xs, w1, w3, w2, grp_expert, grp_block, grp_rows, *, gsz=1024, tf=256, packed=False, ragged=False,
               single_buffer_rows=False):
    e, d, ff = w1.shape
    r, dx = xs.shape
    nf = ff // tf
    ng = grp_expert.shape[0]

    def fmap(f, g, gr):
        return jnp.where(gr[g] > 0, f, nf - 1)

    scratch = [pltpu.VMEM((gsz, d), F32)]
    if packed:
        scratch.append(pltpu.VMEM((gsz, d), BF16))
    row_mode = dict(pipeline_mode=pl.Buffered(1)) if single_buffer_rows else {}
    grid_spec = pltpu.PrefetchScalarGridSpec(
        num_scalar_prefetch=3,
        grid=(ng, nf),
        in_specs=[
            pl.BlockSpec((gsz, dx), lambda g, f, ge, gb, gr: (gb[g], 0), **row_mode),
            pl.BlockSpec((None, d, tf), lambda g, f, ge, gb, gr: (ge[g], 0, fmap(f, g, gr))),
            pl.BlockSpec((None, d, tf), lambda g, f, ge, gb, gr: (ge[g], 0, fmap(f, g, gr))),
            pl.BlockSpec((None, tf, d), lambda g, f, ge, gb, gr: (ge[g], fmap(f, g, gr), 0)),
        ],
        out_specs=pl.BlockSpec((gsz, dx), lambda g, f, ge, gb, gr: (gb[g], 0), **row_mode),
        scratch_shapes=scratch,
    )
    return pl.pallas_call(
        functools.partial(_ffn_kernel, nf=nf, gsz=gsz, packed=packed, ragged=ragged),
        out_shape=jax.ShapeDtypeStruct((r, dx), xs.dtype),
        grid_spec=grid_spec,
        compiler_params=_cparams("arbitrary", "arbitrary"),
        name="ffn_groups",
    )(grp_expert, grp_block, grp_rows, xs, w1, w3, w2)


def _row_copy(src_ref, src_row, dst_ref, dst_row, sem):
    return pltpu.make_async_copy(src_ref.at[pl.ds(src_row, 1), :], dst_ref.at[pl.ds(dst_row, 1), :], sem)


DMA_UNROLL = 8


def _dispatch_kernel(pos0_ref, pos1_ref, h_ref, xs_in_ref, xs_ref, sem):
    del xs_in_ref
    tm = h_ref.shape[0]
    base = pl.program_id(0) * tm

    def copies(i):
        return [_row_copy(h_ref, i, xs_ref, p[base + i], sem) for p in (pos0_ref, pos1_ref)]

    def issue(i, c):
        for cp in copies(i):
            cp.start()
        return c

    lax.fori_loop(0, tm, issue, 0, unroll=DMA_UNROLL)

    def drain(i, c):
        for cp in copies(i):
            cp.wait()
        return c

    lax.fori_loop(0, tm, drain, 0, unroll=DMA_UNROLL)


def moe_dispatch(hp, pos0, pos1, n_rows, tm=512):
    t, w = hp.shape
    xs0 = jnp.zeros((n_rows, w), hp.dtype)
    grid_spec = pltpu.PrefetchScalarGridSpec(
        num_scalar_prefetch=2,
        grid=(t // tm,),
        in_specs=[pl.BlockSpec((tm, w), lambda i, p0, p1: (i, 0)), pl.BlockSpec(memory_space=pl.ANY)],
        out_specs=pl.BlockSpec(memory_space=pl.ANY),
        scratch_shapes=[pltpu.SemaphoreType.DMA],
    )
    return pl.pallas_call(
        _dispatch_kernel,
        out_shape=jax.ShapeDtypeStruct((n_rows, w), hp.dtype),
        grid_spec=grid_spec,
        input_output_aliases={3: 0},
        compiler_params=_cparams("arbitrary"),
        name="moe_dispatch",
    )(pos0, pos1, hp, xs0)


def _t5_bucket_np(dist):
    max_exact = REL_BUCKETS // 2
    d_f = np.maximum(dist, 1).astype(np.float32)
    large = max_exact + (np.log(d_f / np.float32(max_exact)) / np.float32(math.log(REL_MAX_DIST / max_exact))
                         * np.float32(REL_BUCKETS - max_exact)).astype(np.int32)
    large = np.minimum(large, REL_BUCKETS - 1)
    return np.where(dist < max_exact, dist, large)


RES = 4
TQ = ATTN_BLOCK
PIECE = TQ // RES


def _bucket_index_tables():
    assert DILATED_PATTERNS == ((128, 1), (512, 4), (2048, 16)) and RES == 4 and TQ == 128

    def fin(steps, dil, n_back=TQ):
        ok = (steps >= 0) & (steps <= n_back)
        return np.where(ok, _t5_bucket_np(np.clip(steps, 0, n_back) * dil), -1).astype(np.int32)

    i = np.arange(TQ)[:, None]
    t4 = fin(i + TQ - np.arange(2 * TQ)[None, :], 4)
    t16 = fin(i - np.arange(TQ)[None, :], 16)
    rq, mq = i // PIECE, i % PIECE
    jk = np.arange(2 * TQ)[None, :]
    rk, mk = jk // (2 * PIECE), jk % (2 * PIECE)
    t1 = fin(RES * (PIECE + mq - mk) + (rq - rk), 1)
    t1first = fin(RES * (mq - mk) + (rq - rk), 1)
    return t4, t1, t1first, t16


def _bias_tab_kernel(rel_ref, *refs):
    n = len(refs) // 2
    h = pl.program_id(0)
    for idx_ref, o_ref in zip(refs[:n], refs[n:]):
        idx = idx_ref[...]
        tab = jnp.full(idx.shape, NEG_INF, F32)
        for b in range(REL_BUCKETS):
            tab = jnp.where(idx == b, rel_ref[b, h], tab)
        o_ref[0] = tab


def attn_bias_tables(rel_bias):
    idx_tabs = [jnp.asarray(t) for t in _bucket_index_tables()]
    nh = rel_bias.shape[1]
    return pl.pallas_call(
        _bias_tab_kernel,
        out_shape=tuple(jax.ShapeDtypeStruct((nh,) + t.shape, F32) for t in idx_tabs),
        grid=(nh,),
        in_specs=[pl.BlockSpec(memory_space=pltpu.SMEM)] + [pl.BlockSpec(t.shape, lambda h: (0, 0)) for t in idx_tabs],
        out_specs=tuple(pl.BlockSpec((1,) + t.shape, lambda h: (h, 0, 0)) for t in idx_tabs),
        compiler_params=_cparams("parallel"),
        name="attn_bias_tables",
    )(rel_bias.astype(F32), *idx_tabs)


def _attn_kernel(q_ref, k_ref, v_ref, t4_ref, t1_ref, t1f_ref, t16_ref, o_ref, acc, m_s, l_s, qs, ks, vs, od, md, ld,
                 *, scale):
    nres, lsub, hd = q_ref.shape
    nt = (((1,), (1,)), ((), ()))

    def tile(q, k, v, bias):
        s = lax.dot_general(q, k, nt, preferred_element_type=F32) * scale + bias
        m = jnp.max(s, axis=-1, keepdims=True)
        p = jnp.exp(s - m)
        l = jnp.sum(p, axis=-1, keepdims=True)
        o = jnp.dot(p.astype(BF16), v, preferred_element_type=F32)
        return o, jnp.broadcast_to(m, o.shape), jnp.broadcast_to(l, o.shape)

    def merge(r, rows, o, m, l):
        m0 = m_s[r, rows, :]
        mn = jnp.maximum(m0, m)
        a = jnp.exp(m0 - mn)
        b = jnp.exp(m - mn)
        acc[r, rows, :] = acc[r, rows, :] * a + o * b
        l_s[r, rows, :] = l_s[r, rows, :] * a + l * b
        m_s[r, rows, :] = mn

    nsub = lsub // TQ
    for r in range(nres):
        qs[r] = q_ref[r].astype(F32)
        ks[r] = k_ref[r].astype(F32)
        vs[r] = v_ref[r].astype(F32)
    masked = jnp.full((TQ, TQ), NEG_INF, F32)
    bias16 = (jnp.concatenate([t16_ref[0], masked], axis=1), jnp.concatenate([masked, t16_ref[0]], axis=1))
    for r in range(nres):
        for pair in range(nsub // 2):
            sub = [pl.ds(2 * pair + e, TQ, stride=nsub) for e in range(2)]
            kk = jnp.concatenate([ks[r, s, :] for s in sub], axis=0).astype(BF16)
            vv = jnp.concatenate([vs[r, s, :] for s in sub], axis=0).astype(BF16)
            for e in range(2):
                o, m, l = tile(qs[r, sub[e], :].astype(BF16), kk, vv, bias16[e])
                od[r * nsub + 2 * pair + e] = o
                md[r * nsub + 2 * pair + e] = m
                ld[r * nsub + 2 * pair + e] = l

    for r in range(nres):
        for i in range(lsub // TQ):
            rows = pl.ds(i * TQ, TQ)
            if i == 0:
                krows = pl.ds(0, 2 * TQ)
                bias = jnp.concatenate([t4_ref[0, :, TQ:2 * TQ], masked], axis=1)
            else:
                krows = pl.ds((i - 1) * TQ, 2 * TQ)
                bias = t4_ref[0]
            o, m, l = tile(q_ref[r, rows, :], k_ref[r, krows, :], v_ref[r, krows, :], bias)
            acc[r, rows, :] = o
            m_s[r, rows, :] = m
            l_s[r, rows, :] = l

    for j in range(lsub // PIECE):
        q = jnp.concatenate([q_ref[r, pl.ds(j * PIECE, PIECE), :] for r in range(nres)], axis=0)
        if j == 0:
            krows, bias = pl.ds(0, 2 * PIECE), t1f_ref[0]
        else:
            krows, bias = pl.ds((j - 1) * PIECE, 2 * PIECE), t1_ref[0]
        k = jnp.concatenate([k_ref[r, krows, :] for r in range(nres)], axis=0)
        v = jnp.concatenate([v_ref[r, krows, :] for r in range(nres)], axis=0)
        o, m, l = tile(q, k, v, bias)
        for r in range(nres):
            part = slice(r * PIECE, (r + 1) * PIECE)
            merge(r, pl.ds(j * PIECE, PIECE), o[part], m[part], l[part])

    for r in range(nres):
        for c in range(nsub):
            merge(r, pl.ds(c, TQ, stride=nsub), od[r * nsub + c], md[r * nsub + c], ld[r * nsub + c])

    for r in range(nres):
        o_ref[r] = (acc[r] / l_s[r]).astype(o_ref.dtype)


def attention(qkv, tabs, *, bsz, seq, hd):
    dm = ATTN_HEADS * hd
    lsub = seq // RES
    assert hd == V7X_LANES and lsub // TQ == 16 // RES and seq // 16 == TQ
    view = qkv.reshape(bsz * RES, lsub, 3 * dm)
    blk = lambda which: pl.BlockSpec((RES, lsub, hd), lambda b, h: (b, 0, which * ATTN_HEADS + h))
    tab = lambda t: pl.BlockSpec((1,) + t.shape[1:], lambda b, h: (h, 0, 0))
    st = pltpu.VMEM((RES, lsub, hd), F32)
    sub_f = pltpu.VMEM((RES * lsub // TQ, TQ, hd), F32)
    o = pl.pallas_call(
        functools.partial(_attn_kernel, scale=hd ** -0.5),
        out_shape=jax.ShapeDtypeStruct((bsz * RES, lsub, dm), BF16),
        grid=(bsz, ATTN_HEADS),
        in_specs=[blk(0), blk(1), blk(2)] + [tab(t) for t in tabs],
        out_specs=pl.BlockSpec((RES, lsub, hd), lambda b, h: (b, 0, h)),
        scratch_shapes=[st, st, st, st, st, st, sub_f, sub_f, sub_f],
        compiler_params=_cparams("parallel", "arbitrary"),
        name="attention",
    )(view, view, view, *tabs)
    return o.reshape(bsz * seq, dm)


def _split_bf16(x):
    hi = x.astype(BF16)
    lo = (x - hi.astype(F32)).astype(BF16)
    return hi, lo


def _moe_route_kernel(x_ref, g_ref, wr_ref, h_ref, meta_ref, cnt_ref, carry_ref):
    step = pl.program_id(0)
    tm = x_ref.shape[0]

    @pl.when(step == 0)
    def _():
        carry_ref[...] = jnp.zeros_like(carry_ref)

    h = _rms(x_ref[...], g_ref[...])
    h_hi, h_lo = _split_bf16(h)
    dh = h.shape[1] // 2
    h_ref[...] = pack_bf16_pair(h[:, 0:dh], h[:, dh:2 * dh])
    w_hi, w_lo = _split_bf16(wr_ref[...])
    logits = (jnp.dot(h_hi, w_hi, preferred_element_type=F32) + jnp.dot(h_lo, w_hi, preferred_element_type=F32)
              + jnp.dot(h_hi, w_lo, preferred_element_type=F32))
    lane = lax.broadcasted_iota(jnp.int32, (tm, V7X_LANES), 1).astype(F32)
    lg = jnp.where(lane < N_EXPERTS, logits, NEG_INF)
    m1 = jnp.max(lg, axis=-1, keepdims=True)
    i1 = jnp.min(jnp.where(lg == m1, lane, float(V7X_LANES)), axis=-1, keepdims=True)
    lg2 = jnp.where(lane == i1, NEG_INF, lg)
    m2 = jnp.max(lg2, axis=-1, keepdims=True)
    i2 = jnp.min(jnp.where(lg2 == m2, lane, float(V7X_LANES)), axis=-1, keepdims=True)
    e2 = jnp.exp(m2 - m1)
    g1 = 1.0 / (1.0 + e2)
    g2 = e2 / (1.0 + e2)
    oh = ((lane == i1) | (lane == i2)).astype(BF16)
    r_i = lax.broadcasted_iota(jnp.int32, (tm, tm), 0)
    c_i = lax.broadcasted_iota(jnp.int32, (tm, tm), 1)
    tri = (c_i < r_i).astype(BF16)
    rank = jnp.dot(tri, oh, preferred_element_type=F32) + carry_ref[0:1, :]
    r1 = jnp.sum(jnp.where(lane == i1, rank, 0.0), axis=-1, keepdims=True)
    r2 = jnp.sum(jnp.where(lane == i2, rank, 0.0), axis=-1, keepdims=True)
    carry_ref[0:1, :] = carry_ref[0:1, :] + jnp.sum(oh.astype(F32), axis=0, keepdims=True)
    meta = jnp.where(lane == 0, i1.astype(F32), 0.0)
    meta = jnp.where(lane == 1, i2.astype(F32), meta)
    meta = jnp.where(lane == 2, g1, meta)
    meta = jnp.where(lane == 3, g2, meta)
    meta = jnp.where(lane == 4, r1, meta)
    meta = jnp.where(lane == 5, r2, meta)
    meta_ref[...] = meta
    cnt_ref[...] = jnp.broadcast_to(carry_ref[0:1, :], cnt_ref.shape)


def moe_route(x, g, w_router_p, tm=256):
    t, d = x.shape
    return pl.pallas_call(
        _moe_route_kernel,
        out_shape=(jax.ShapeDtypeStruct((t, d // 2), jnp.uint32), jax.ShapeDtypeStruct((t, V7X_LANES), F32),
                   jax.ShapeDtypeStruct((8, V7X_LANES), F32)),
        grid=(t // tm,),
        in_specs=[pl.BlockSpec((tm, d), lambda i: (i, 0)), pl.BlockSpec((1, d), lambda i: (0, 0)),
                  pl.BlockSpec((d, V7X_LANES), lambda i: (0, 0))],
        out_specs=(pl.BlockSpec((tm, d // 2), lambda i: (i, 0)), pl.BlockSpec((tm, V7X_LANES), lambda i: (i, 0)),
                   pl.BlockSpec((8, V7X_LANES), lambda i: (0, 0))),
        scratch_shapes=[pltpu.VMEM((8, V7X_LANES), F32)],
        compiler_params=_cparams("arbitrary"),
        name="moe_route",
    )(x, g.reshape(1, d), w_router_p)


def _moe_final_kernel(pos0_ref, pos1_ref, x_ref, meta_ref, g_ref, ys_ref, o_ref, ybuf, o_scr, sem, *, lsub):
    nres, tp, d = x_ref.shape
    dh = d // 2
    b = pl.program_id(0)
    j = pl.program_id(1)

    def copies(r, mm):
        tok = (b * nres + r) * lsub + j * tp + mm
        return [_row_copy(ys_ref, p[tok], ybuf.at[k], r * tp + mm, sem) for k, p in enumerate((pos0_ref, pos1_ref))]

    for r in range(nres):
        def issue(mm, c, r=r):
            for cp in copies(r, mm):
                cp.start()
            return c

        lax.fori_loop(0, tp, issue, 0, unroll=DMA_UNROLL)

    for r in range(nres):
        def drain(mm, c, r=r):
            for cp in copies(r, mm):
                cp.wait()
            return c

        lax.fori_loop(0, tp, drain, 0, unroll=DMA_UNROLL)
    for r in range(nres):
        rows = slice(r * tp, (r + 1) * tp)
        meta = meta_ref[r]
        g1 = meta[:, 2:3]
        g2 = meta[:, 3:4]
        a0, b0 = unpack_bf16_pair(ybuf[0, rows, :])
        a1, b1 = unpack_bf16_pair(ybuf[1, rows, :])
        x = x_ref[r]
        xa = x[:, 0:dh] + a0 * g1 + a1 * g2
        xb = x[:, dh:d] + b0 * g1 + b1 * g2
        ms = (jnp.sum(xa * xa, axis=-1, keepdims=True) + jnp.sum(xb * xb, axis=-1, keepdims=True)) * (1.0 / d)
        inv = lax.rsqrt(ms + EPS)
        ya = xa * inv * g_ref[:, 0:dh]
        yb = xb * inv * g_ref[:, dh:d]
        out_rows = pl.ds(r, tp, stride=nres)
        for s in range(dh // V7X_LANES):
            ls = slice(s * V7X_LANES, (s + 1) * V7X_LANES)
            o_scr[s, out_rows, :] = ya[:, ls]
            o_scr[dh // V7X_LANES + s, out_rows, :] = yb[:, ls]
    for s in range(d // V7X_LANES):
        o_ref[:, s * V7X_LANES:(s + 1) * V7X_LANES] = o_scr[s]


def moe_final(x, ys, pos0, pos1, meta, g, *, bsz, seq, tp=128):
    t, d = x.shape
    lsub = seq // RES
    tm = RES * tp
    xv = x.reshape(bsz * RES, lsub, d)
    mv = meta.reshape(bsz * RES, lsub, V7X_LANES)
    grid_spec = pltpu.PrefetchScalarGridSpec(
        num_scalar_prefetch=2,
        grid=(bsz, lsub // tp),
        in_specs=[pl.BlockSpec((RES, tp, d), lambda b, j, p0, p1: (b, j, 0)),
                  pl.BlockSpec((RES, tp, V7X_LANES), lambda b, j, p0, p1: (b, j, 0)),
                  pl.BlockSpec((1, d), lambda b, j, p0, p1: (0, 0)), pl.BlockSpec(memory_space=pl.ANY)],
        out_specs=pl.BlockSpec((tm, d), lambda b, j, p0, p1: (b * (lsub // tp) + j, 0)),
        scratch_shapes=[pltpu.VMEM((TOP_K, tm, d // 2), jnp.uint32), pltpu.VMEM((d // V7X_LANES, tm, V7X_LANES), F32),
                        pltpu.SemaphoreType.DMA],
    )
    return pl.pallas_call(
        functools.partial(_moe_final_kernel, lsub=lsub),
        out_shape=jax.ShapeDtypeStruct((t, d), F32),
        grid_spec=grid_spec,
        compiler_params=_cparams("arbitrary", "arbitrary"),
        name="moe_final",
    )(pos0, pos1, xv, mv, g.reshape(1, d), ys)


def _pad_lanes(v, fill=0.0):
    v = v.reshape(1, -1).astype(F32)
    return jnp.pad(v, ((0, 0), (0, V7X_LANES - v.shape[1])), constant_values=fill)


def even_layer(x, h, p, *, bsz, seq):
    t, d = x.shape
    d_ssd = d
    d_conf = d
    heads = d_ssd // SSD_HEAD_DIM
    conv_dim = d_ssd + 2 * SSD_GROUPS * SSD_STATE
    i1 = d_ssd + conv_dim
    i2 = i1 + heads
    w_in_t = jnp.swapaxes(p["w_in"], 0, 1)
    zx = matmul_nt(h, w_in_t, row_start=0, n_cols=i1, tn=1024, name="in_proj_zx")
    dt_raw = matmul_nt(h, w_in_t, row_start=i1, n_cols=V7X_LANES, tn=V7X_LANES, out_dtype=F32, name="in_proj_dt")
    conf = matmul_nt(h, w_in_t, row_start=i2, n_cols=2 * d_conf, tn=1024, name="in_proj_conf")
    u = conf_module(conf, p["conf_dw_w"], p["conf_dw_b"], p["conf_ln_w"], p["conf_ln_b"], bsz=bsz, seq=seq)
    xc = ssd_conv(zx, p["conv_w"], p["conv_b"], bsz=bsz, seq=seq, col_off=d_ssd)
    dt, acs, dtt, acst = ssd_prep(dt_raw, _pad_lanes(p["dt_bias"]), _pad_lanes(p["a_log"]), bsz=bsz, seq=seq)
    dskip_e = jnp.repeat(p["d_skip"].astype(F32), SSD_HEAD_DIM).reshape(1, d_ssd)
    y_ssd = ssd_main(xc, zx, dt, acs, dtt, acst, dskip_e, p["ssd_norm_w"], bsz=bsz, seq=seq, d_ssd=d_ssd)
    return matmul([y_ssd, u], p["w_out"], n_cols=d, tn=512, res=x, out_dtype=F32, name="even_out_proj")


def dense_ffn(h, w1, w3, w2, gsz=1024):
    t = h.shape[0]
    ng = t // gsz
    ge = jnp.zeros((ng,), jnp.int32)
    gb = jnp.arange(ng, dtype=jnp.int32)
    gr = jnp.full((ng,), gsz, jnp.int32)
    return ffn_groups(h, w1[None], w3[None], w2[None], ge, gb, gr, gsz=gsz, tf=512, single_buffer_rows=True)


def odd_layer_attn(x, h, p, *, bsz, seq):
    t, d = x.shape
    hd = d // ATTN_HEADS
    qkv = matmul([h], p["w_qkv"], n_cols=3 * d, tn=1024, name="qkv_proj")
    tabs = attn_bias_tables(p["rel_bias"])
    o = attention(qkv, tabs, bsz=bsz, seq=seq, hd=hd)
    return matmul([o], p["w_attn_out"], n_cols=d, tn=512, tm=2048, res=x, out_dtype=F32, name="attn_out_proj")


def moe_layer(x, norm_g, w_router, w1, w3, w2, final_g, *, bsz, seq, gsz=1280):
    t, d = x.shape
    ne = w1.shape[0]
    w_router_p = jnp.pad(w_router.astype(F32), ((0, 0), (0, V7X_LANES - ne)))
    h, meta, cnt = moe_route(x, norm_g, w_router_p)
    counts = cnt[0, :ne].astype(jnp.int32)
    ngrp_e = (counts + gsz - 1) // gsz
    per_grp = (counts + jnp.maximum(ngrp_e, 1) - 1) // jnp.maximum(ngrp_e, 1)
    rpg_e = jnp.maximum((per_grp + FFN_SUB - 1) // FFN_SUB * FFN_SUB, FFN_SUB)
    grp_end = jnp.cumsum(ngrp_e)
    grp_start = grp_end - ngrp_e
    ng = (t * TOP_K) // gsz + ne
    j = jnp.arange(ng, dtype=jnp.int32)
    total = grp_end[-1]
    last = jnp.maximum(total - 1, 0)
    jj = jnp.minimum(j, last)
    ge = jnp.searchsorted(grp_end, jj, side="right").astype(jnp.int32)
    ge = jnp.minimum(ge, ne - 1)
    rows = jnp.clip(counts[ge] - (jj - grp_start[ge]) * rpg_e[ge], 0, rpg_e[ge])
    gr = jnp.where(j < total, rows, 0).astype(jnp.int32)
    gb = j
    max_grp = (t + gsz - 1) // gsz

    def dest_row(e, rank):
        rpg = jnp.zeros_like(rank)
        gs = jnp.zeros_like(rank)
        for k in range(ne):
            rpg = jnp.where(e == k, rpg_e[k], rpg)
            gs = jnp.where(e == k, grp_start[k], gs)
        g_in = jnp.zeros_like(rank)
        for m in range(1, max_grp):
            g_in = g_in + (rank >= m * rpg).astype(jnp.int32)
        return (gs + g_in) * gsz + rank - g_in * rpg

    pos0 = dest_row(meta[:, 0].astype(jnp.int32), meta[:, 4].astype(jnp.int32))
    pos1 = dest_row(meta[:, 1].astype(jnp.int32), meta[:, 5].astype(jnp.int32))
    xs = moe_dispatch(h, pos0, pos1, ng * gsz)
    ys = ffn_groups(xs, w1, w3, w2, ge, gb, gr, gsz=gsz, packed=True, ragged=True)
    return moe_final(x, ys, pos0, pos1, meta, final_g, bsz=bsz, seq=seq)


def kernel(x, norm_mix, norm_ffn, norm_final, even_w_in, ssd_conv_w, ssd_conv_b, ssd_dt_bias, ssd_a_log, ssd_d, ssd_norm_w, conf_dw_w, conf_dw_b, conf_ln_w, conf_ln_b, even_w_out, ffn_w1, ffn_w3, ffn_w2, attn_w_qkv, attn_w_out, rel_bias, moe_router, moe_w1, moe_w3, moe_w2):
    bsz, seq, d = x.shape
    assert norm_mix.shape[0] == 2, "two-layer trunk: one even (SSD+Conformer/FFN) and one odd (attention/MoE) layer"
    xf = x.reshape(bsz * seq, d)
    h = rmsnorm(xf, norm_mix[0])
    p_even = dict(w_in=even_w_in[0], conv_w=ssd_conv_w[0], conv_b=ssd_conv_b[0], dt_bias=ssd_dt_bias[0],
                  a_log=ssd_a_log[0], d_skip=ssd_d[0], ssd_norm_w=ssd_norm_w[0], conf_dw_w=conf_dw_w[0],
                  conf_dw_b=conf_dw_b[0], conf_ln_w=conf_ln_w[0], conf_ln_b=conf_ln_b[0], w_out=even_w_out[0])
    x1 = even_layer(xf, h, p_even, bsz=bsz, seq=seq)
    h1 = rmsnorm(x1, norm_ffn[0])
    y_ffn = dense_ffn(h1, ffn_w1[0], ffn_w3[0], ffn_w2[0])
    x2, h2 = add_norm_to_residue_major(x1, y_ffn, norm_mix[1], bsz=bsz, seq=seq, nres=RES)
    p_odd = dict(w_qkv=attn_w_qkv[0], w_attn_out=attn_w_out[0], rel_bias=rel_bias)
    x3 = odd_layer_attn(x2, h2, p_odd, bsz=bsz, seq=seq)
    out = moe_layer(x3, norm_ffn[1], moe_router[0], moe_w1[0], moe_w3[0], moe_w2[0], norm_final, bsz=bsz, seq=seq)
    return out.reshape(bsz, seq, d)
```

```python
import functools
import math

import numpy as np
import jax
import jax.numpy as jnp
from jax import lax
from jax.experimental import pallas as pl
from jax.experimental.pallas import tpu as pltpu

SSD_HEAD_DIM = 64
SSD_GROUPS = 4
SSD_STATE = 128
SSD_CONV = 4
SSD_CHUNK = 128
CONF_KERNEL = 31
ATTN_HEADS = 16
DILATED_PATTERNS = ((128, 1), (512, 4), (2048, 16))
ATTN_BLOCK = 128
REL_BUCKETS = 32
REL_MAX_DIST = 2048
N_EXPERTS = 8
TOP_K = 2
EPS = 1e-6

V7X_LANES = 128
V7X_VMEM_BYTES = 64 * 1024 * 1024
VMEM_LIMIT = 56 * 1024 * 1024

F32 = jnp.float32
BF16 = jnp.bfloat16
NEG_INF = float("-inf")


def _cparams(*sem):
    return pltpu.CompilerParams(dimension_semantics=tuple(sem), vmem_limit_bytes=VMEM_LIMIT)


def _rms(x, g):
    ms = jnp.mean(x * x, axis=-1, keepdims=True)
    return x * lax.rsqrt(ms + EPS) * g


def _rmsnorm_kernel(x_ref, g_ref, h_ref):
    h_ref[...] = _rms(x_ref[...], g_ref[...]).astype(h_ref.dtype)


def rmsnorm(x, g, out_dtype=BF16, tm=512):
    t, d = x.shape
    return pl.pallas_call(
        _rmsnorm_kernel,
        out_shape=jax.ShapeDtypeStruct((t, d), out_dtype),
        grid=(t // tm,),
        in_specs=[pl.BlockSpec((tm, d), lambda i: (i, 0)), pl.BlockSpec((1, d), lambda i: (0, 0))],
        out_specs=pl.BlockSpec((tm, d), lambda i: (i, 0)),
        compiler_params=_cparams("parallel"),
        name="rmsnorm",
    )(x, g.reshape(1, d))


def _add_norm_res_kernel(x_ref, y_ref, g_ref, xo_ref, ho_ref, xs_scr, hs_scr):
    nres, tp, d = xo_ref.shape
    xn = x_ref[...] + y_ref[...].astype(F32)
    h = _rms(xn, g_ref[...])
    for s in range(d // V7X_LANES):
        ls = slice(s * V7X_LANES, (s + 1) * V7X_LANES)
        xs_scr[s] = xn[:, ls]
        hs_scr[s] = h[:, ls]
    for r in range(nres):
        rows = pl.ds(r, tp, stride=nres)
        for s in range(d // V7X_LANES):
            ls = slice(s * V7X_LANES, (s + 1) * V7X_LANES)
            xo_ref[r, :, ls] = xs_scr[s, rows, :]
            ho_ref[r, :, ls] = hs_scr[s, rows, :].astype(ho_ref.dtype)


def add_norm_to_residue_major(x, y, g, *, bsz, seq, nres, tp=128):
    t, d = x.shape
    lsub = seq // nres
    tm = nres * tp
    nj = lsub // tp
    slab = pltpu.VMEM((d // V7X_LANES, tm, V7X_LANES), F32)
    xo, ho = pl.pallas_call(
        _add_norm_res_kernel,
        out_shape=(jax.ShapeDtypeStruct((bsz * nres, lsub, d), F32), jax.ShapeDtypeStruct((bsz * nres, lsub, d), BF16)),
        grid=(bsz, nj),
        in_specs=[pl.BlockSpec((tm, d), lambda b, j: (b * nj + j, 0)), pl.BlockSpec((tm, d), lambda b, j: (b * nj + j, 0)),
                  pl.BlockSpec((1, d), lambda b, j: (0, 0))],
        out_specs=(pl.BlockSpec((nres, tp, d), lambda b, j: (b, j, 0)), pl.BlockSpec((nres, tp, d), lambda b, j: (b, j, 0))),
        scratch_shapes=[slab, slab],
        compiler_params=_cparams("parallel", "parallel"),
        name="add_norm_reorder",
    )(x, y, g.reshape(1, d))
    return xo.reshape(t, d), ho.reshape(t, d)


def _mm_kernel(*refs, n_lhs, ks, has_res):
    xs = refs[:n_lhs]
    w_ref = refs[n_lhs]
    r_ref = refs[n_lhs + 1] if has_res else None
    o_ref = refs[-1]
    acc = None
    k0 = 0
    for x_ref, k in zip(xs, ks):
        part = jnp.dot(x_ref[...], w_ref[k0:k0 + k, :].astype(BF16), preferred_element_type=F32)
        acc = part if acc is None else acc + part
        k0 += k
    if has_res:
        acc = acc + r_ref[...]
    o_ref[...] = acc.astype(o_ref.dtype)


def matmul(xs, w, *, n_cols, col_block_off=0, tn, tm=1024, res=None, out_dtype=BF16, name="matmul"):
    t = xs[0].shape[0]
    ks = tuple(x.shape[1] for x in xs)
    ktot = sum(ks)
    assert w.shape[0] == ktot and n_cols % tn == 0 and t % tm == 0
    in_specs = [pl.BlockSpec((tm, k), lambda i, j: (i, 0)) for k in ks]
    in_specs.append(pl.BlockSpec((ktot, tn), lambda i, j: (0, j + col_block_off)))
    args = list(xs) + [w]
    if res is not None:
        in_specs.append(pl.BlockSpec((tm, tn), lambda i, j: (i, j)))
        args.append(res)
    return pl.pallas_call(
        functools.partial(_mm_kernel, n_lhs=len(xs), ks=ks, has_res=res is not None),
        out_shape=jax.ShapeDtypeStruct((t, n_cols), out_dtype),
        grid=(t // tm, n_cols // tn),
        in_specs=in_specs,
        out_specs=pl.BlockSpec((tm, tn), lambda i, j: (i, j)),
        compiler_params=_cparams("parallel", "arbitrary"),
        name=name,
    )(*args)


def _mm_nt_kernel(x_ref, wt_ref, o_ref):
    wt = wt_ref[...].astype(BF16)
    acc = lax.dot_general(x_ref[...], wt, (((1,), (1,)), ((), ())), preferred_element_type=F32)
    o_ref[...] = acc.astype(o_ref.dtype)


def matmul_nt(x, w_t, *, row_start, n_cols, tn, tm=1024, out_dtype=BF16, name="matmul_nt"):
    t, k = x.shape
    assert w_t.shape[1] == k and n_cols % tn == 0 and t % tm == 0
    if row_start % tn == 0:
        w_spec = pl.BlockSpec((tn, k), lambda i, j: (row_start // tn + j, 0))
    else:
        assert row_start % SUBLANES == 0 and tn % SUBLANES == 0
        w_spec = pl.BlockSpec((pl.Element(tn), pl.Element(k)),
                              lambda i, j: ((row_start // SUBLANES + j * (tn // SUBLANES)) * SUBLANES, 0))
    return pl.pallas_call(
        _mm_nt_kernel,
        out_shape=jax.ShapeDtypeStruct((t, n_cols), out_dtype),
        grid=(t // tm, n_cols // tn),
        in_specs=[pl.BlockSpec((tm, k), lambda i, j: (i, 0)), w_spec],
        out_specs=pl.BlockSpec((tm, tn), lambda i, j: (i, j)),
        compiler_params=_cparams("parallel", "arbitrary"),
        name=name,
    )(x, w_t)


HALO_BF16 = 16


def _ssd_conv_kernel(cur_ref, halo_ref, w_ref, b_ref, o_ref, buf_ref, *, ts):
    i = pl.program_id(1)
    halo = halo_ref[...].astype(F32)
    buf_ref[0:HALO_BF16, :] = jnp.where(i > 0, halo, 0.0)
    buf_ref[HALO_BF16:HALO_BF16 + ts, :] = cur_ref[...].astype(F32)
    acc = b_ref[...]
    for k in range(SSD_CONV):
        off = HALO_BF16 - (SSD_CONV - 1) + k
        acc = acc + w_ref[k:k + 1, :] * buf_ref[off:off + ts, :]
    o_ref[...] = (acc * jax.nn.sigmoid(acc)).astype(o_ref.dtype)


def ssd_conv(zx, conv_w, conv_b, *, bsz, seq, col_off, ts=1024, tc=512):
    c = conv_w.shape[1]
    t = zx.shape[0]
    nsb = seq // ts
    cb0 = col_off // tc
    hb = ts // HALO_BF16
    return pl.pallas_call(
        functools.partial(_ssd_conv_kernel, ts=ts),
        out_shape=jax.ShapeDtypeStruct((t, c), BF16),
        grid=(bsz, nsb, c // tc),
        in_specs=[
            pl.BlockSpec((ts, tc), lambda b, i, j: (b * nsb + i, cb0 + j)),
            pl.BlockSpec((HALO_BF16, tc), lambda b, i, j: (jnp.maximum((b * nsb + i) * hb - 1, 0), cb0 + j)),
            pl.BlockSpec((SSD_CONV, tc), lambda b, i, j: (0, j)),
            pl.BlockSpec((1, tc), lambda b, i, j: (0, j)),
        ],
        out_specs=pl.BlockSpec((ts, tc), lambda b, i, j: (b * nsb + i, j)),
        scratch_shapes=[pltpu.VMEM((ts + HALO_BF16, tc), F32)],
        compiler_params=_cparams("parallel", "parallel", "parallel"),
        name="ssd_conv",
    )(zx, zx, conv_w, conv_b.reshape(1, c))


def _ssd_prep_kernel(raw_ref, bias_ref, alog_ref, dt_ref, acs_ref, dtt_ref, acst_ref, *, seq):
    raw = raw_ref[...] + bias_ref[...]
    dt = jnp.maximum(raw, 0.0) + jnp.log1p(jnp.exp(-jnp.abs(raw)))
    a = -jnp.exp(alog_ref[...])
    x = dt * a
    row = lax.broadcasted_iota(jnp.int32, x.shape, 0) % SSD_CHUNK
    sh = 1
    while sh < SSD_CHUNK:
        x = x + jnp.where(row >= sh, pltpu.roll(x, sh, 0), 0.0)
        sh *= 2
    dt_ref[...] = dt
    acs_ref[...] = x
    for c in range(seq // SSD_CHUNK):
        sl = slice(c * SSD_CHUNK, (c + 1) * SSD_CHUNK)
        dtt_ref[sl, :] = dt[sl, :].T
        acst_ref[sl, :] = x[sl, :].T


def ssd_prep(dt_raw, dt_bias_p, a_log_p, *, bsz, seq):
    t = dt_raw.shape[0]
    shp = jax.ShapeDtypeStruct((t, V7X_LANES), F32)
    blk = pl.BlockSpec((seq, V7X_LANES), lambda b: (b, 0))
    vec = pl.BlockSpec((1, V7X_LANES), lambda b: (0, 0))
    return pl.pallas_call(
        functools.partial(_ssd_prep_kernel, seq=seq),
        out_shape=(shp, shp, shp, shp),
        grid=(bsz,),
        in_specs=[blk, vec, vec],
        out_specs=(blk, blk, blk, blk),
        compiler_params=_cparams("parallel"),
        name="ssd_prep",
    )(dt_raw, dt_bias_p, a_log_p)


def _conv_silu(cur_ref, halo_ref, w_ref, b_ref, buf_ref, col0, first):
    rows, c = cur_ref.shape
    cols = slice(col0, col0 + c)
    buf_ref[0:HALO_BF16, :] = jnp.where(first, 0.0, halo_ref[...].astype(F32))
    buf_ref[HALO_BF16:HALO_BF16 + rows, :] = cur_ref[...].astype(F32)
    acc = b_ref[:, cols]
    for k in range(SSD_CONV):
        off = HALO_BF16 - (SSD_CONV - 1) + k
        acc = acc + w_ref[k:k + 1, cols] * buf_ref[off:off + rows, :]
    return acc * jax.nn.sigmoid(acc)


def _ssd_main_kernel(xr_ref, xh_ref, br_ref, bh_ref, cr_ref, ch_ref, cw_ref, cbias_ref,
                     z_ref, dt_ref, acs_ref, dtt_ref, acst_ref, dsk_ref, nw_ref,
                     y_ref, st_ref, xbuf, bbuf, cbuf, *, heads_per_group):
    ci = pl.program_id(1)
    L = SSD_CHUNK
    gw = heads_per_group * SSD_HEAD_DIM
    npair = heads_per_group // 2
    d_ssd = xr_ref.shape[1]
    bc_w = br_ref.shape[1]

    @pl.when(ci == 0)
    def _():
        st_ref[...] = jnp.zeros_like(st_ref)

    first = ci == 0
    x_all = _conv_silu(xr_ref, xh_ref, cw_ref, cbias_ref, xbuf, 0, first)
    b_all = _conv_silu(br_ref, bh_ref, cw_ref, cbias_ref, bbuf, d_ssd, first)
    c_all = _conv_silu(cr_ref, ch_ref, cw_ref, cbias_ref, cbuf, d_ssd + bc_w, first)
    dt = dt_ref[...]
    acs = acs_ref[...]
    dtt = dtt_ref[...]
    acst = acst_ref[...]
    row = lax.broadcasted_iota(jnp.int32, (L, L), 0)
    col = lax.broadcasted_iota(jnp.int32, (L, L), 1)
    causal = col <= row
    lane_lo = lax.broadcasted_iota(jnp.int32, (L, 2 * SSD_HEAD_DIM), 1) < SSD_HEAD_DIM
    lane_lo1 = lax.broadcasted_iota(jnp.int32, (1, 2 * SSD_HEAD_DIM), 1) < SSD_HEAD_DIM
    zero_b = jnp.zeros((L, 2 * SSD_HEAD_DIM), BF16)

    for g in range(SSD_GROUPS):
        bmf = b_all[:, g * SSD_STATE:(g + 1) * SSD_STATE]
        bm = bmf.astype(BF16)
        cm = c_all[:, g * SSD_STATE:(g + 1) * SSD_STATE].astype(BF16)
        bmt = bmf.T.astype(BF16)
        cb = lax.dot_general(cm, bm, (((1,), (1,)), ((), ())), preferred_element_type=F32)
        xg = x_all[:, g * gw:(g + 1) * gw]
        y_diags, e_ls, w_ls, cds = [], [], [], []
        for q in range(npair):
            xp = xg[:, q * 2 * SSD_HEAD_DIM:(q + 1) * 2 * SSD_HEAD_DIM].astype(BF16)
            ms = []
            for hh in range(2):
                h = g * heads_per_group + 2 * q + hh
                seg = acs[:, h:h + 1] - acst[h:h + 1, :]
                dec = jnp.exp(jnp.where(causal, seg, NEG_INF))
                ms.append((cb * dec * dtt[h:h + 1, :]).astype(BF16))
            h0 = g * heads_per_group + 2 * q
            h1 = h0 + 1
            lhs = jnp.concatenate(ms, axis=1)
            rhs = jnp.concatenate([jnp.where(lane_lo, xp, zero_b), jnp.where(lane_lo, zero_b, xp)], axis=0)
            y_diags.append(jnp.dot(lhs, rhs, preferred_element_type=F32))
            last0 = acs[L - 1:L, h0:h0 + 1]
            last1 = acs[L - 1:L, h1:h1 + 1]
            e_ls.append(jnp.where(lane_lo, jnp.exp(acs[:, h0:h0 + 1]), jnp.exp(acs[:, h1:h1 + 1])))
            w_ls.append(jnp.where(lane_lo, dt[:, h0:h0 + 1] * jnp.exp(last0 - acs[:, h0:h0 + 1]),
                                  dt[:, h1:h1 + 1] * jnp.exp(last1 - acs[:, h1:h1 + 1])))
            cds.append(jnp.where(lane_lo1, jnp.exp(last0), jnp.exp(last1)))
        st = st_ref[g]
        y_off = jnp.dot(cm, st.astype(BF16), preferred_element_type=F32) * jnp.concatenate(e_ls, axis=1)
        xw = (xg * jnp.concatenate(w_ls, axis=1)).astype(BF16)
        st_ref[g] = st * jnp.concatenate(cds, axis=1) + jnp.dot(bmt, xw, preferred_element_type=F32)
        y = jnp.concatenate(y_diags, axis=1) + y_off + xg * dsk_ref[:, g * gw:(g + 1) * gw]
        zg = z_ref[:, g * gw:(g + 1) * gw].astype(F32)
        yg = y * (zg * jax.nn.sigmoid(zg))
        ms_ = jnp.mean(yg * yg, axis=-1, keepdims=True)
        y_ref[:, g * gw:(g + 1) * gw] = (yg * lax.rsqrt(ms_ + EPS) * nw_ref[:, g * gw:(g + 1) * gw]).astype(y_ref.dtype)


def ssd_main(zx, conv_w, conv_b, dt, acs, dtt, acst, dskip_e, norm_w, *, bsz, seq, d_ssd):
    t = zx.shape[0]
    nc = seq // SSD_CHUNK
    heads = d_ssd // SSD_HEAD_DIM
    hpg = heads // SSD_GROUPS
    bc_w = SSD_GROUPS * SSD_STATE
    conv_dim = conv_w.shape[1]
    assert d_ssd % bc_w == 0 and hpg % 2 == 0 and conv_dim == d_ssd + 2 * bc_w
    rowmap = lambda b, c: (b * nc + c, 0)
    hb = SSD_CHUNK // HALO_BF16

    def cur(width, col_block):
        return pl.BlockSpec((SSD_CHUNK, width), lambda b, c: (b * nc + c, col_block))

    def halo(width, col_block):
        return pl.BlockSpec((HALO_BF16, width), lambda b, c: (jnp.maximum((b * nc + c) * hb - 1, 0), col_block))

    xcol, bcol, ccol = 1, 2 * d_ssd // bc_w, 2 * d_ssd // bc_w + 1
    return pl.pallas_call(
        functools.partial(_ssd_main_kernel, heads_per_group=hpg),
        out_shape=jax.ShapeDtypeStruct((t, d_ssd), BF16),
        grid=(bsz, nc),
        in_specs=[
            cur(d_ssd, xcol), halo(d_ssd, xcol), cur(bc_w, bcol), halo(bc_w, bcol), cur(bc_w, ccol), halo(bc_w, ccol),
            pl.BlockSpec((SSD_CONV, conv_dim), lambda b, c: (0, 0)),
            pl.BlockSpec((1, conv_dim), lambda b, c: (0, 0)),
            pl.BlockSpec((SSD_CHUNK, d_ssd), rowmap),
            pl.BlockSpec((SSD_CHUNK, V7X_LANES), rowmap),
            pl.BlockSpec((SSD_CHUNK, V7X_LANES), rowmap),
            pl.BlockSpec((SSD_CHUNK, V7X_LANES), rowmap),
            pl.BlockSpec((SSD_CHUNK, V7X_LANES), rowmap),
            pl.BlockSpec((1, d_ssd), lambda b, c: (0, 0)),
            pl.BlockSpec((1, d_ssd), lambda b, c: (0, 0)),
        ],
        out_specs=pl.BlockSpec((SSD_CHUNK, d_ssd), rowmap),
        scratch_shapes=[pltpu.VMEM((SSD_GROUPS, SSD_STATE, hpg * SSD_HEAD_DIM), F32),
                        pltpu.VMEM((SSD_CHUNK + HALO_BF16, d_ssd), F32),
                        pltpu.VMEM((SSD_CHUNK + HALO_BF16, bc_w), F32),
                        pltpu.VMEM((SSD_CHUNK + HALO_BF16, bc_w), F32)],
        compiler_params=_cparams("parallel", "arbitrary"),
        name="ssd_main",
    )(zx, zx, zx, zx, zx, zx, conv_w, conv_b.reshape(1, conv_dim), zx, dt, acs, dtt, acst, dskip_e,
      norm_w.reshape(1, d_ssd))


CONF_HALO = 32


SUBLANES = 8


def _conf_kernel(a_ref, g_ref, ah_ref, gh_ref, w_ref, b_ref, lw_ref, lb_ref, o_ref, buf_ref, sh_ref, *, ts, rc, lt):
    i = pl.program_id(1)
    d = a_ref.shape[1]
    ah = ah_ref[...].astype(F32)
    gh = gh_ref[...].astype(F32)
    buf_ref[0:CONF_HALO, :] = jnp.where(i > 0, ah * jax.nn.sigmoid(gh), 0.0)
    a = a_ref[...].astype(F32)
    g = g_ref[...].astype(F32)
    buf_ref[CONF_HALO:CONF_HALO + ts, :] = a * jax.nn.sigmoid(g)
    span = ts + CONF_HALO - SUBLANES
    for j in range(1, SUBLANES):
        sh_ref[j - 1, 0:span, :] = buf_ref[j:j + span, :]
    base = CONF_HALO - (CONF_KERNEL - 1)
    outs = []
    for r in range(ts // rc):
        cols = []
        for c in range(d // lt):
            ls = slice(c * lt, (c + 1) * lt)
            acc = jnp.broadcast_to(b_ref[:, ls], (rc, lt))
            for k in range(CONF_KERNEL):
                j = (base + k) % SUBLANES
                off = r * rc + (base + k) - j
                src = buf_ref[off:off + rc, ls] if j == 0 else sh_ref[j - 1, off:off + rc, ls]
                acc = acc + w_ref[k:k + 1, ls] * src
            cols.append(acc)
        outs.append(jnp.concatenate(cols, axis=1))
    u = jnp.concatenate(outs, axis=0)
    mu = jnp.mean(u, axis=-1, keepdims=True)
    uc = u - mu
    var = jnp.mean(uc * uc, axis=-1, keepdims=True)
    y = uc * lax.rsqrt(var + EPS) * lw_ref[...] + lb_ref[...]
    o_ref[...] = (y * jax.nn.sigmoid(y)).astype(o_ref.dtype)


def conf_module(conf, dw_w, dw_b, ln_w, ln_b, *, bsz, seq, ts=256, rc=32, lt=512):
    t = conf.shape[0]
    d = dw_w.shape[1]
    nsb = seq // ts
    hb = ts // CONF_HALO
    cur = lambda col: pl.BlockSpec((ts, d), lambda b, i: (b * nsb + i, col))
    halo = lambda col: pl.BlockSpec((CONF_HALO, d), lambda b, i: (jnp.maximum((b * nsb + i) * hb - 1, 0), col))
    vec = pl.BlockSpec((1, d), lambda b, i: (0, 0))
    return pl.pallas_call(
        functools.partial(_conf_kernel, ts=ts, rc=rc, lt=lt),
        out_shape=jax.ShapeDtypeStruct((t, d), BF16),
        grid=(bsz, nsb),
        in_specs=[cur(0), cur(1), halo(0), halo(1), pl.BlockSpec((CONF_KERNEL, d), lambda b, i: (0, 0)),
                  vec, vec, vec],
        out_specs=pl.BlockSpec((ts, d), lambda b, i: (b * nsb + i, 0)),
        scratch_shapes=[pltpu.VMEM((ts + CONF_HALO, d), F32), pltpu.VMEM((SUBLANES - 1, ts + CONF_HALO, d), F32)],
        compiler_params=_cparams("parallel", "parallel"),
        name="conf_module",
    )(conf, conf, conf, conf, dw_w, dw_b.reshape(1, d), ln_w.reshape(1, d), ln_b.reshape(1, d))


FFN_SUB = 256


FFN_NCHUNK = 512


def pack_bf16_pair(a, b):
    ua = lax.bitcast_convert_type(a.astype(BF16).astype(F32), jnp.uint32)
    ub = lax.bitcast_convert_type(b.astype(BF16).astype(F32), jnp.uint32)
    return (ua >> 16) | (ub & jnp.uint32(0xFFFF0000))


def unpack_bf16_pair(u):
    a = lax.bitcast_convert_type(u << 16, F32)
    b = lax.bitcast_convert_type(u & jnp.uint32(0xFFFF0000), F32)
    return a, b


def _ffn_kernel(ge_ref, gb_ref, gr_ref, x_ref, w1_ref, w3_ref, w2_ref, o_ref, acc_ref, *xb_scr, nf, gsz, packed, ragged):
    g = pl.program_id(0)
    f = pl.program_id(1)
    rows = gr_ref[g]
    valid = rows > 0
    d = acc_ref.shape[1]

    @pl.when(valid & (f == 0))
    def _():
        acc_ref[...] = jnp.zeros_like(acc_ref)
        if packed:
            a, b = unpack_bf16_pair(x_ref[...])
            xb_scr[0][:, 0:d // 2] = a.astype(BF16)
            xb_scr[0][:, d // 2:d] = b.astype(BF16)

    xsrc = xb_scr[0] if packed else x_ref

    def slab(r0, nrows, w1, w3, w2):
        xs = xsrc[pl.ds(r0, nrows), :]
        h1 = jnp.dot(xs, w1, preferred_element_type=F32)
        h3 = jnp.dot(xs, w3, preferred_element_type=F32)
        hh = (h1 * jax.nn.sigmoid(h1) * h3).astype(BF16)
        for c in range(d // FFN_NCHUNK):
            cs = slice(c * FFN_NCHUNK, (c + 1) * FFN_NCHUNK)
            acc_ref[pl.ds(r0, nrows), cs] += jnp.dot(hh, w2[:, cs], preferred_element_type=F32)

    nslab = (rows + FFN_SUB - 1) // FFN_SUB
    for ns in (range(1, gsz // FFN_SUB + 1) if ragged else (gsz // FFN_SUB,)):
        @pl.when(nslab == ns)
        def _(ns=ns):
            slab(0, ns * FFN_SUB, w1_ref[...].astype(BF16), w3_ref[...].astype(BF16), w2_ref[...].astype(BF16))

    @pl.when(jnp.logical_not(valid) & (f == 0))
    def _():
        o_ref[...] = jnp.zeros_like(o_ref)

    @pl.when(valid & (f == nf - 1))
    def _():
        if packed:
            o_ref[...] = pack_bf16_pair(acc_ref[:, 0:d // 2], acc_ref[:, d // 2:d])
        else:
            o_ref[...] = acc_ref[...].astype(o_ref.dtype)


def ffn_groups(xs, w1, w3, w2, grp_expert, grp_block, grp_rows, *, gsz=1024, tf=256, packed=False, ragged=False,
               single_buffer_rows=False):
    e, d, ff = w1.shape
    r, dx = xs.shape
    nf = ff // tf
    ng = grp_expert.shape[0]

    def fmap(f, g, gr):
        return jnp.where(gr[g] > 0, f, nf - 1)

    scratch = [pltpu.VMEM((gsz, d), F32)]
    if packed:
        scratch.append(pltpu.VMEM((gsz, d), BF16))
    row_mode = dict(pipeline_mode=pl.Buffered(1)) if single_buffer_rows else {}
    grid_spec = pltpu.PrefetchScalarGridSpec(
        num_scalar_prefetch=3,
        grid=(ng, nf),
        in_specs=[
            pl.BlockSpec((gsz, dx), lambda g, f, ge, gb, gr: (gb[g], 0), **row_mode),
            pl.BlockSpec((None, d, tf), lambda g, f, ge, gb, gr: (ge[g], 0, fmap(f, g, gr))),
            pl.BlockSpec((None, d, tf), lambda g, f, ge, gb, gr: (ge[g], 0, fmap(f, g, gr))),
            pl.BlockSpec((None, tf, d), lambda g, f, ge, gb, gr: (ge[g], fmap(f, g, gr), 0)),
        ],
        out_specs=pl.BlockSpec((gsz, dx), lambda g, f, ge, gb, gr: (gb[g], 0), **row_mode),
        scratch_shapes=scratch,
    )
    return pl.pallas_call(
        functools.partial(_ffn_kernel, nf=nf, gsz=gsz, packed=packed, ragged=ragged),
        out_shape=jax.ShapeDtypeStruct((r, dx), xs.dtype),
        grid_spec=grid_spec,
        compiler_params=_cparams("arbitrary", "arbitrary"),
        name="ffn_groups",
    )(grp_expert, grp_block, grp_rows, xs, w1, w3, w2)


def _row_copy(src_ref, src_row, dst_ref, dst_row, sem):
    return pltpu.make_async_copy(src_ref.at[pl.ds(src_row, 1), :], dst_ref.at[pl.ds(dst_row, 1), :], sem)


DMA_UNROLL = 8


def _dispatch_kernel(pos0_ref, pos1_ref, h_ref, xs_in_ref, xs_ref, sem):
    del xs_in_ref
    tm = h_ref.shape[0]
    base = pl.program_id(0) * tm

    def copies(i):
        return [_row_copy(h_ref, i, xs_ref, p[base + i], sem) for p in (pos0_ref, pos1_ref)]

    def issue(i, c):
        for cp in copies(i):
            cp.start()
        return c

    lax.fori_loop(0, tm, issue, 0, unroll=DMA_UNROLL)

    def drain(i, c):
        for cp in copies(i):
            cp.wait()
        return c

    lax.fori_loop(0, tm, drain, 0, unroll=DMA_UNROLL)


def moe_dispatch(hp, pos0, pos1, n_rows, tm=512):
    t, w = hp.shape
    xs0 = jnp.zeros((n_rows, w), hp.dtype)
    grid_spec = pltpu.PrefetchScalarGridSpec(
        num_scalar_prefetch=2,
        grid=(t // tm,),
        in_specs=[pl.BlockSpec((tm, w), lambda i, p0, p1: (i, 0)), pl.BlockSpec(memory_space=pl.ANY)],
        out_specs=pl.BlockSpec(memory_space=pl.ANY),
        scratch_shapes=[pltpu.SemaphoreType.DMA],
    )
    return pl.pallas_call(
        _dispatch_kernel,
        out_shape=jax.ShapeDtypeStruct((n_rows, w), hp.dtype),
        grid_spec=grid_spec,
        input_output_aliases={3: 0},
        compiler_params=_cparams("arbitrary"),
        name="moe_dispatch",
    )(pos0, pos1, hp, xs0)


def _t5_bucket_np(dist):
    max_exact = REL_BUCKETS // 2
    d_f = np.maximum(dist, 1).astype(np.float32)
    large = max_exact + (np.log(d_f / np.float32(max_exact)) / np.float32(math.log(REL_MAX_DIST / max_exact))
                         * np.float32(REL_BUCKETS - max_exact)).astype(np.int32)
    large = np.minimum(large, REL_BUCKETS - 1)
    return np.where(dist < max_exact, dist, large)


RES = 4
TQ = ATTN_BLOCK
PIECE = TQ // RES


def _bucket_index_tables():
    assert DILATED_PATTERNS == ((128, 1), (512, 4), (2048, 16)) and RES == 4 and TQ == 128

    def fin(steps, dil, n_back=TQ):
        ok = (steps >= 0) & (steps <= n_back)
        return np.where(ok, _t5_bucket_np(np.clip(steps, 0, n_back) * dil), -1).astype(np.int32)

    i = np.arange(TQ)[:, None]
    t4 = fin(i + TQ - np.arange(2 * TQ)[None, :], 4)
    t16 = fin(i - np.arange(TQ)[None, :], 16)
    rq, mq = i // PIECE, i % PIECE
    jk = np.arange(2 * TQ)[None, :]
    rk, mk = jk // (2 * PIECE), jk % (2 * PIECE)
    t1 = fin(RES * (PIECE + mq - mk) + (rq - rk), 1)
    t1first = fin(RES * (mq - mk) + (rq - rk), 1)
    return t4, t1, t1first, t16


def _bias_tab_kernel(rel_ref, *refs):
    n = len(refs) // 2
    h = pl.program_id(0)
    for idx_ref, o_ref in zip(refs[:n], refs[n:]):
        idx = idx_ref[...]
        tab = jnp.full(idx.shape, NEG_INF, F32)
        for b in range(REL_BUCKETS):
            tab = jnp.where(idx == b, rel_ref[b, h], tab)
        o_ref[0] = tab


def attn_bias_tables(rel_bias):
    idx_tabs = [jnp.asarray(t) for t in _bucket_index_tables()]
    nh = rel_bias.shape[1]
    return pl.pallas_call(
        _bias_tab_kernel,
        out_shape=tuple(jax.ShapeDtypeStruct((nh,) + t.shape, F32) for t in idx_tabs),
        grid=(nh,),
        in_specs=[pl.BlockSpec(memory_space=pltpu.SMEM)] + [pl.BlockSpec(t.shape, lambda h: (0, 0)) for t in idx_tabs],
        out_specs=tuple(pl.BlockSpec((1,) + t.shape, lambda h: (h, 0, 0)) for t in idx_tabs),
        compiler_params=_cparams("parallel"),
        name="attn_bias_tables",
    )(rel_bias.astype(F32), *idx_tabs)


def _attn_kernel(q_ref, k_ref, v_ref, t4_ref, t1_ref, t1f_ref, t16_ref, o_ref, acc, m_s, l_s, qs, ks, vs, od, md, ld,
                 *, scale):
    nres, lsub, hd = q_ref.shape
    nt = (((1,), (1,)), ((), ()))

    def tile(q, k, v, bias):
        s = lax.dot_general(q, k, nt, preferred_element_type=F32) * scale + bias
        m = jnp.max(s, axis=-1, keepdims=True)
        p = jnp.exp(s - m)
        l = jnp.sum(p, axis=-1, keepdims=True)
        o = jnp.dot(p.astype(BF16), v, preferred_element_type=F32)
        return o, jnp.broadcast_to(m, o.shape), jnp.broadcast_to(l, o.shape)

    def merge(r, rows, o, m, l):
        m0 = m_s[r, rows, :]
        mn = jnp.maximum(m0, m)
        a = jnp.exp(m0 - mn)
        b = jnp.exp(m - mn)
        acc[r, rows, :] = acc[r, rows, :] * a + o * b
        l_s[r, rows, :] = l_s[r, rows, :] * a + l * b
        m_s[r, rows, :] = mn

    nsub = lsub // TQ
    for r in range(nres):
        qs[r] = q_ref[r].astype(F32)
        ks[r] = k_ref[r].astype(F32)
        vs[r] = v_ref[r].astype(F32)
    masked = jnp.full((TQ, TQ), NEG_INF, F32)
    bias16 = (jnp.concatenate([t16_ref[0], masked], axis=1), jnp.concatenate([masked, t16_ref[0]], axis=1))
    for r in range(nres):
        for pair in range(nsub // 2):
            sub = [pl.ds(2 * pair + e, TQ, stride=nsub) for e in range(2)]
            kk = jnp.concatenate([ks[r, s, :] for s in sub], axis=0).astype(BF16)
            vv = jnp.concatenate([vs[r, s, :] for s in sub], axis=0).astype(BF16)
            for e in range(2):
                o, m, l = tile(qs[r, sub[e], :].astype(BF16), kk, vv, bias16[e])
                od[r * nsub + 2 * pair + e] = o
                md[r * nsub + 2 * pair + e] = m
                ld[r * nsub + 2 * pair + e] = l

    for r in range(nres):
        for i in range(lsub // TQ):
            rows = pl.ds(i * TQ, TQ)
            if i == 0:
                krows = pl.ds(0, 2 * TQ)
                bias = jnp.concatenate([t4_ref[0, :, TQ:2 * TQ], masked], axis=1)
            else:
                krows = pl.ds((i - 1) * TQ, 2 * TQ)
                bias = t4_ref[0]
            o, m, l = tile(q_ref[r, rows, :], k_ref[r, krows, :], v_ref[r, krows, :], bias)
            acc[r, rows, :] = o
            m_s[r, rows, :] = m
            l_s[r, rows, :] = l

    for j in range(lsub // PIECE):
        q = jnp.concatenate([q_ref[r, pl.ds(j * PIECE, PIECE), :] for r in range(nres)], axis=0)
        if j == 0:
            krows, bias = pl.ds(0, 2 * PIECE), t1f_ref[0]
        else:
            krows, bias = pl.ds((j - 1) * PIECE, 2 * PIECE), t1_ref[0]
        k = jnp.concatenate([k_ref[r, krows, :] for r in range(nres)], axis=0)
        v = jnp.concatenate([v_ref[r, krows, :] for r in range(nres)], axis=0)
        o, m, l = tile(q, k, v, bias)
        for r in range(nres):
            part = slice(r * PIECE, (r + 1) * PIECE)
            merge(r, pl.ds(j * PIECE, PIECE), o[part], m[part], l[part])

    for r in range(nres):
        for c in range(nsub):
            merge(r, pl.ds(c, TQ, stride=nsub), od[r * nsub + c], md[r * nsub + c], ld[r * nsub + c])

    for r in range(nres):
        o_ref[r] = (acc[r] / l_s[r]).astype(o_ref.dtype)


def attention(qkv, tabs, *, bsz, seq, hd):
    dm = ATTN_HEADS * hd
    lsub = seq // RES
    assert hd == V7X_LANES and lsub // TQ == 16 // RES and seq // 16 == TQ
    view = qkv.reshape(bsz * RES, lsub, 3 * dm)
    blk = lambda which: pl.BlockSpec((RES, lsub, hd), lambda b, h: (b, 0, which * ATTN_HEADS + h))
    tab = lambda t: pl.BlockSpec((1,) + t.shape[1:], lambda b, h: (h, 0, 0))
    st = pltpu.VMEM((RES, lsub, hd), F32)
    sub_f = pltpu.VMEM((RES * lsub // TQ, TQ, hd), F32)
    o = pl.pallas_call(
        functools.partial(_attn_kernel, scale=hd ** -0.5),
        out_shape=jax.ShapeDtypeStruct((bsz * RES, lsub, dm), BF16),
        grid=(bsz, ATTN_HEADS),
        in_specs=[blk(0), blk(1), blk(2)] + [tab(t) for t in tabs],
        out_specs=pl.BlockSpec((RES, lsub, hd), lambda b, h: (b, 0, h)),
        scratch_shapes=[st, st, st, st, st, st, sub_f, sub_f, sub_f],
        compiler_params=_cparams("parallel", "arbitrary"),
        name="attention",
    )(view, view, view, *tabs)
    return o.reshape(bsz * seq, dm)


def _split_bf16(x):
    hi = x.astype(BF16)
    lo = (x - hi.astype(F32)).astype(BF16)
    return hi, lo


def _moe_route_kernel(x_ref, g_ref, wr_ref, h_ref, meta_ref, metat_ref, cnt_ref, carry_ref):
    step = pl.program_id(0)
    tm = x_ref.shape[0]

    @pl.when(step == 0)
    def _():
        carry_ref[...] = jnp.zeros_like(carry_ref)

    h = _rms(x_ref[...], g_ref[...])
    h_hi, h_lo = _split_bf16(h)
    dh = h.shape[1] // 2
    h_ref[...] = pack_bf16_pair(h[:, 0:dh], h[:, dh:2 * dh])
    w_hi, w_lo = _split_bf16(wr_ref[...])
    logits = (jnp.dot(h_hi, w_hi, preferred_element_type=F32) + jnp.dot(h_lo, w_hi, preferred_element_type=F32)
              + jnp.dot(h_hi, w_lo, preferred_element_type=F32))
    lane = lax.broadcasted_iota(jnp.int32, (tm, V7X_LANES), 1).astype(F32)
    lg = jnp.where(lane < N_EXPERTS, logits, NEG_INF)
    m1 = jnp.max(lg, axis=-1, keepdims=True)
    i1 = jnp.min(jnp.where(lg == m1, lane, float(V7X_LANES)), axis=-1, keepdims=True)
    lg2 = jnp.where(lane == i1, NEG_INF, lg)
    m2 = jnp.max(lg2, axis=-1, keepdims=True)
    i2 = jnp.min(jnp.where(lg2 == m2, lane, float(V7X_LANES)), axis=-1, keepdims=True)
    e2 = jnp.exp(m2 - m1)
    g1 = 1.0 / (1.0 + e2)
    g2 = e2 / (1.0 + e2)
    oh = ((lane == i1) | (lane == i2)).astype(BF16)
    r_i = lax.broadcasted_iota(jnp.int32, (tm, tm), 0)
    c_i = lax.broadcasted_iota(jnp.int32, (tm, tm), 1)
    tri = (c_i < r_i).astype(BF16)
    rank = jnp.dot(tri, oh, preferred_element_type=F32) + carry_ref[0:1, :]
    r1 = jnp.sum(jnp.where(lane == i1, rank, 0.0), axis=-1, keepdims=True)
    r2 = jnp.sum(jnp.where(lane == i2, rank, 0.0), axis=-1, keepdims=True)
    carry_ref[0:1, :] = carry_ref[0:1, :] + jnp.sum(oh.astype(F32), axis=0, keepdims=True)
    meta = jnp.where(lane == 0, i1.astype(F32), 0.0)
    meta = jnp.where(lane == 1, i2.astype(F32), meta)
    meta = jnp.where(lane == 2, g1, meta)
    meta = jnp.where(lane == 3, g2, meta)
    meta = jnp.where(lane == 4, r1, meta)
    meta = jnp.where(lane == 5, r2, meta)
    meta_ref[...] = meta
    for c in range(tm // V7X_LANES):
        blk = meta[c * V7X_LANES:(c + 1) * V7X_LANES, :].T
        metat_ref[:, c * V7X_LANES:(c + 1) * V7X_LANES] = blk[0:SUBLANES, :]
    cnt_ref[...] = jnp.broadcast_to(carry_ref[0:1, :], cnt_ref.shape)


def moe_route(x, g, w_router_p, tm=256):
    t, d = x.shape
    return pl.pallas_call(
        _moe_route_kernel,
        out_shape=(jax.ShapeDtypeStruct((t, d // 2), jnp.uint32), jax.ShapeDtypeStruct((t, V7X_LANES), F32),
                   jax.ShapeDtypeStruct((SUBLANES, t), F32), jax.ShapeDtypeStruct((8, V7X_LANES), F32)),
        grid=(t // tm,),
        in_specs=[pl.BlockSpec((tm, d), lambda i: (i, 0)), pl.BlockSpec((1, d), lambda i: (0, 0)),
                  pl.BlockSpec((d, V7X_LANES), lambda i: (0, 0))],
        out_specs=(pl.BlockSpec((tm, d // 2), lambda i: (i, 0)), pl.BlockSpec((tm, V7X_LANES), lambda i: (i, 0)),
                   pl.BlockSpec((SUBLANES, tm), lambda i: (0, i)), pl.BlockSpec((8, V7X_LANES), lambda i: (0, 0))),
        scratch_shapes=[pltpu.VMEM((8, V7X_LANES), F32)],
        compiler_params=_cparams("arbitrary"),
        name="moe_route",
    )(x, g.reshape(1, d), w_router_p)


def _moe_final_kernel(pos0_ref, pos1_ref, x_ref, meta_ref, g_ref, ys_ref, o_ref, ybuf, o_scr, sem, *, lsub):
    nres, tp, d = x_ref.shape
    dh = d // 2
    b = pl.program_id(0)
    j = pl.program_id(1)

    def copies(r, mm):
        tok = (b * nres + r) * lsub + j * tp + mm
        return [_row_copy(ys_ref, p[tok], ybuf.at[k], r * tp + mm, sem) for k, p in enumerate((pos0_ref, pos1_ref))]

    for r in range(nres):
        def issue(mm, c, r=r):
            for cp in copies(r, mm):
                cp.start()
            return c

        lax.fori_loop(0, tp, issue, 0, unroll=DMA_UNROLL)

    for r in range(nres):
        def drain(mm, c, r=r):
            for cp in copies(r, mm):
                cp.wait()
            return c

        lax.fori_loop(0, tp, drain, 0, unroll=DMA_UNROLL)
    for r in range(nres):
        rows = slice(r * tp, (r + 1) * tp)
        meta = meta_ref[r]
        g1 = meta[:, 2:3]
        g2 = meta[:, 3:4]
        a0, b0 = unpack_bf16_pair(ybuf[0, rows, :])
        a1, b1 = unpack_bf16_pair(ybuf[1, rows, :])
        x = x_ref[r]
        xa = x[:, 0:dh] + a0 * g1 + a1 * g2
        xb = x[:, dh:d] + b0 * g1 + b1 * g2
        ms = (jnp.sum(xa * xa, axis=-1, keepdims=True) + jnp.sum(xb * xb, axis=-1, keepdims=True)) * (1.0 / d)
        inv = lax.rsqrt(ms + EPS)
        ya = xa * inv * g_ref[:, 0:dh]
        yb = xb * inv * g_ref[:, dh:d]
        out_rows = pl.ds(r, tp, stride=nres)
        for s in range(dh // V7X_LANES):
            ls = slice(s * V7X_LANES, (s + 1) * V7X_LANES)
            o_scr[s, out_rows, :] = ya[:, ls]
            o_scr[dh // V7X_LANES + s, out_rows, :] = yb[:, ls]
    for s in range(d // V7X_LANES):
        o_ref[:, s * V7X_LANES:(s + 1) * V7X_LANES] = o_scr[s]


def moe_final(x, ys, pos0, pos1, meta, g, *, bsz, seq, tp=128):
    t, d = x.shape
    lsub = seq // RES
    tm = RES * tp
    xv = x.reshape(bsz * RES, lsub, d)
    mv = meta.reshape(bsz * RES, lsub, V7X_LANES)
    grid_spec = pltpu.PrefetchScalarGridSpec(
        num_scalar_prefetch=2,
        grid=(bsz, lsub // tp),
        in_specs=[pl.BlockSpec((RES, tp, d), lambda b, j, p0, p1: (b, j, 0)),
                  pl.BlockSpec((RES, tp, V7X_LANES), lambda b, j, p0, p1: (b, j, 0)),
                  pl.BlockSpec((1, d), lambda b, j, p0, p1: (0, 0)), pl.BlockSpec(memory_space=pl.ANY)],
        out_specs=pl.BlockSpec((tm, d), lambda b, j, p0, p1: (b * (lsub // tp) + j, 0)),
        scratch_shapes=[pltpu.VMEM((TOP_K, tm, d // 2), jnp.uint32), pltpu.VMEM((d // V7X_LANES, tm, V7X_LANES), F32),
                        pltpu.SemaphoreType.DMA],
    )
    return pl.pallas_call(
        functools.partial(_moe_final_kernel, lsub=lsub),
        out_shape=jax.ShapeDtypeStruct((t, d), F32),
        grid_spec=grid_spec,
        compiler_params=_cparams("arbitrary", "arbitrary"),
        name="moe_final",
    )(pos0, pos1, xv, mv, g.reshape(1, d), ys)


def _pad_lanes(v, fill=0.0):
    v = v.reshape(1, -1).astype(F32)
    return jnp.pad(v, ((0, 0), (0, V7X_LANES - v.shape[1])), constant_values=fill)


def even_layer(x, h, p, *, bsz, seq):
    t, d = x.shape
    d_ssd = d
    d_conf = d
    heads = d_ssd // SSD_HEAD_DIM
    conv_dim = d_ssd + 2 * SSD_GROUPS * SSD_STATE
    i1 = d_ssd + conv_dim
    i2 = i1 + heads
    w_in_t = jnp.swapaxes(p["w_in"], 0, 1)
    zx = matmul_nt(h, w_in_t, row_start=0, n_cols=i1, tn=1024, name="in_proj_zx")
    dt_raw = matmul_nt(h, w_in_t, row_start=i1, n_cols=V7X_LANES, tn=V7X_LANES, out_dtype=F32, name="in_proj_dt")
    conf = matmul_nt(h, w_in_t, row_start=i2, n_cols=2 * d_conf, tn=1024, name="in_proj_conf")
    u = conf_module(conf, p["conf_dw_w"], p["conf_dw_b"], p["conf_ln_w"], p["conf_ln_b"], bsz=bsz, seq=seq)
    dt, acs, dtt, acst = ssd_prep(dt_raw, _pad_lanes(p["dt_bias"]), _pad_lanes(p["a_log"]), bsz=bsz, seq=seq)
    dskip_e = jnp.repeat(p["d_skip"].astype(F32), SSD_HEAD_DIM).reshape(1, d_ssd)
    y_ssd = ssd_main(zx, p["conv_w"], p["conv_b"], dt, acs, dtt, acst, dskip_e, p["ssd_norm_w"],
                     bsz=bsz, seq=seq, d_ssd=d_ssd)
    return matmul([y_ssd, u], p["w_out"], n_cols=d, tn=512, res=x, out_dtype=F32, name="even_out_proj")


def dense_ffn(h, w1, w3, w2, gsz=1024):
    t = h.shape[0]
    ng = t // gsz
    ge = jnp.zeros((ng,), jnp.int32)
    gb = jnp.arange(ng, dtype=jnp.int32)
    gr = jnp.full((ng,), gsz, jnp.int32)
    return ffn_groups(h, w1[None], w3[None], w2[None], ge, gb, gr, gsz=gsz)


def odd_layer_attn(x, h, p, *, bsz, seq):
    t, d = x.shape
    hd = d // ATTN_HEADS
    qkv = matmul([h], p["w_qkv"], n_cols=3 * d, tn=1024, name="qkv_proj")
    tabs = attn_bias_tables(p["rel_bias"])
    o = attention(qkv, tabs, bsz=bsz, seq=seq, hd=hd)
    return matmul([o], p["w_attn_out"], n_cols=d, tn=512, tm=2048, res=x, out_dtype=F32, name="attn_out_proj")


def moe_layer(x, norm_g, w_router, w1, w3, w2, final_g, *, bsz, seq, gsz=1280):
    t, d = x.shape
    ne = w1.shape[0]
    w_router_p = jnp.pad(w_router.astype(F32), ((0, 0), (0, V7X_LANES - ne)))
    h, meta, meta_t, cnt = moe_route(x, norm_g, w_router_p)
    counts = cnt[0, :ne].astype(jnp.int32)
    ngrp_e = (counts + gsz - 1) // gsz
    grp_end = jnp.cumsum(ngrp_e)
    grp_start = grp_end - ngrp_e
    ng = (t * TOP_K) // gsz + ne
    j = jnp.arange(ng, dtype=jnp.int32)
    total = grp_end[-1]
    last = jnp.maximum(total - 1, 0)
    jj = jnp.minimum(j, last)
    ge = jnp.searchsorted(grp_end, jj, side="right").astype(jnp.int32)
    ge = jnp.minimum(ge, ne - 1)
    rows = jnp.clip(counts[ge] - (jj - grp_start[ge]) * gsz, 0, gsz)
    gr = jnp.where(j < total, rows, 0).astype(jnp.int32)
    gb = j

    def dest_row(e, rank):
        off = jnp.zeros_like(rank)
        for k in range(ne):
            off = jnp.where(e == k, grp_start[k] * gsz, off)
        return off + rank

    pos0 = dest_row(meta_t[0].astype(jnp.int32), meta_t[4].astype(jnp.int32))
    pos1 = dest_row(meta_t[1].astype(jnp.int32), meta_t[5].astype(jnp.int32))
    xs = moe_dispatch(h, pos0, pos1, ng * gsz)
    ys = ffn_groups(xs, w1, w3, w2, ge, gb, gr, gsz=gsz, packed=True, ragged=True)
    return moe_final(x, ys, pos0, pos1, meta, final_g, bsz=bsz, seq=seq)


def kernel(x, norm_mix, norm_ffn, norm_final, even_w_in, ssd_conv_w, ssd_conv_b, ssd_dt_bias, ssd_a_log, ssd_d, ssd_norm_w, conf_dw_w, conf_dw_b, conf_ln_w, conf_ln_b, even_w_out, ffn_w1, ffn_w3, ffn_w2, attn_w_qkv, attn_w_out, rel_bias, moe_router, moe_w1, moe_w3, moe_w2):
    bsz, seq, d = x.shape
    assert norm_mix.shape[0] == 2, "two-layer trunk: one even (SSD+Conformer/FFN) and one odd (attention/MoE) layer"
    xf = x.reshape(bsz * seq, d)
    h = rmsnorm(xf, norm_mix[0])
    p_even = dict(w_in=even_w_in[0], conv_w=ssd_conv_w[0], conv_b=ssd_conv_b[0], dt_bias=ssd_dt_bias[0],
                  a_log=ssd_a_log[0], d_skip=ssd_d[0], ssd_norm_w=ssd_norm_w[0], conf_dw_w=conf_dw_w[0],
                  conf_dw_b=conf_dw_b[0], conf_ln_w=conf_ln_w[0], conf_ln_b=conf_ln_b[0], w_out=even_w_out[0])
    x1 = even_layer(xf, h, p_even, bsz=bsz, seq=seq)
    h1 = rmsnorm(x1, norm_ffn[0])
    y_ffn = dense_ffn(h1, ffn_w1[0], ffn_w3[0], ffn_w2[0])
    x2, h2 = add_norm_to_residue_major(x1, y_ffn, norm_mix[1], bsz=bsz, seq=seq, nres=RES)
    p_odd = dict(w_qkv=attn_w_qkv[0], w_attn_out=attn_w_out[0], rel_bias=rel_bias)
    x3 = odd_layer_attn(x2, h2, p_odd, bsz=bsz, seq=seq)
    out = moe_layer(x3, norm_ffn[1], moe_router[0], moe_w1[0], moe_w3[0], moe_w2[0], norm_final, bsz=bsz, seq=seq)
    return out.reshape(bsz, seq, d)
```

```python
import functools
import math

import numpy as np
import jax
import jax.numpy as jnp
from jax import lax
from jax.experimental import pallas as pl
from jax.experimental.pallas import tpu as pltpu

SSD_HEAD_DIM = 64
SSD_GROUPS = 4
SSD_STATE = 128
SSD_CONV = 4
SSD_CHUNK = 128
CONF_KERNEL = 31
ATTN_HEADS = 16
DILATED_PATTERNS = ((128, 1), (512, 4), (2048, 16))
ATTN_BLOCK = 128
REL_BUCKETS = 32
REL_MAX_DIST = 2048
N_EXPERTS = 8
TOP_K = 2
EPS = 1e-6

V7X_LANES = 128
V7X_VMEM_BYTES = 64 * 1024 * 1024
VMEM_LIMIT = 56 * 1024 * 1024

F32 = jnp.float32
BF16 = jnp.bfloat16
NEG_INF = float("-inf")


def _cparams(*sem):
    return pltpu.CompilerParams(dimension_semantics=tuple(sem), vmem_limit_bytes=VMEM_LIMIT)


def _rms(x, g):
    ms = jnp.mean(x * x, axis=-1, keepdims=True)
    return x * lax.rsqrt(ms + EPS) * g


def _rmsnorm_kernel(x_ref, g_ref, h_ref):
    h_ref[...] = _rms(x_ref[...], g_ref[...]).astype(h_ref.dtype)


def rmsnorm(x, g, out_dtype=BF16, tm=512):
    t, d = x.shape
    return pl.pallas_call(
        _rmsnorm_kernel,
        out_shape=jax.ShapeDtypeStruct((t, d), out_dtype),
        grid=(t // tm,),
        in_specs=[pl.BlockSpec((tm, d), lambda i: (i, 0)), pl.BlockSpec((1, d), lambda i: (0, 0))],
        out_specs=pl.BlockSpec((tm, d), lambda i: (i, 0)),
        compiler_params=_cparams("parallel"),
        name="rmsnorm",
    )(x, g.reshape(1, d))


def _add_norm_res_kernel(x_ref, y_ref, g_ref, xo_ref, ho_ref, xs_scr, hs_scr):
    nres, tp, d = xo_ref.shape
    xn = x_ref[...] + y_ref[...].astype(F32)
    h = _rms(xn, g_ref[...])
    for s in range(d // V7X_LANES):
        ls = slice(s * V7X_LANES, (s + 1) * V7X_LANES)
        xs_scr[s] = xn[:, ls]
        hs_scr[s] = h[:, ls]
    for r in range(nres):
        rows = pl.ds(r, tp, stride=nres)
        for s in range(d // V7X_LANES):
            ls = slice(s * V7X_LANES, (s + 1) * V7X_LANES)
            xo_ref[r, :, ls] = xs_scr[s, rows, :]
            ho_ref[r, :, ls] = hs_scr[s, rows, :].astype(ho_ref.dtype)


def add_norm_to_residue_major(x, y, g, *, bsz, seq, nres, tp=128):
    t, d = x.shape
    lsub = seq // nres
    tm = nres * tp
    nj = lsub // tp
    slab = pltpu.VMEM((d // V7X_LANES, tm, V7X_LANES), F32)
    xo, ho = pl.pallas_call(
        _add_norm_res_kernel,
        out_shape=(jax.ShapeDtypeStruct((bsz * nres, lsub, d), F32), jax.ShapeDtypeStruct((bsz * nres, lsub, d), BF16)),
        grid=(bsz, nj),
        in_specs=[pl.BlockSpec((tm, d), lambda b, j: (b * nj + j, 0)), pl.BlockSpec((tm, d), lambda b, j: (b * nj + j, 0)),
                  pl.BlockSpec((1, d), lambda b, j: (0, 0))],
        out_specs=(pl.BlockSpec((nres, tp, d), lambda b, j: (b, j, 0)), pl.BlockSpec((nres, tp, d), lambda b, j: (b, j, 0))),
        scratch_shapes=[slab, slab],
        compiler_params=_cparams("parallel", "parallel"),
        name="add_norm_reorder",
    )(x, y, g.reshape(1, d))
    return xo.reshape(t, d), ho.reshape(t, d)


def _mm_kernel(*refs, n_lhs, ks, has_res):
    xs = refs[:n_lhs]
    w_ref = refs[n_lhs]
    r_ref = refs[n_lhs + 1] if has_res else None
    o_ref = refs[-1]
    acc = None
    k0 = 0
    for x_ref, k in zip(xs, ks):
        part = jnp.dot(x_ref[...], w_ref[k0:k0 + k, :].astype(BF16), preferred_element_type=F32)
        acc = part if acc is None else acc + part
        k0 += k
    if has_res:
        acc = acc + r_ref[...]
    o_ref[...] = acc.astype(o_ref.dtype)


def matmul(xs, w, *, n_cols, col_block_off=0, tn, tm=1024, res=None, out_dtype=BF16, name="matmul"):
    t = xs[0].shape[0]
    ks = tuple(x.shape[1] for x in xs)
    ktot = sum(ks)
    assert w.shape[0] == ktot and n_cols % tn == 0 and t % tm == 0
    in_specs = [pl.BlockSpec((tm, k), lambda i, j: (i, 0)) for k in ks]
    in_specs.append(pl.BlockSpec((ktot, tn), lambda i, j: (0, j + col_block_off)))
    args = list(xs) + [w]
    if res is not None:
        in_specs.append(pl.BlockSpec((tm, tn), lambda i, j: (i, j)))
        args.append(res)
    return pl.pallas_call(
        functools.partial(_mm_kernel, n_lhs=len(xs), ks=ks, has_res=res is not None),
        out_shape=jax.ShapeDtypeStruct((t, n_cols), out_dtype),
        grid=(t // tm, n_cols // tn),
        in_specs=in_specs,
        out_specs=pl.BlockSpec((tm, tn), lambda i, j: (i, j)),
        compiler_params=_cparams("parallel", "arbitrary"),
        name=name,
    )(*args)


def _mm_nt_kernel(x_ref, wt_ref, o_ref):
    wt = wt_ref[...].astype(BF16)
    acc = lax.dot_general(x_ref[...], wt, (((1,), (1,)), ((), ())), preferred_element_type=F32)
    o_ref[...] = acc.astype(o_ref.dtype)


def matmul_nt(x, w_t, *, row_start, n_cols, tn, tm=1024, out_dtype=BF16, name="matmul_nt"):
    t, k = x.shape
    assert w_t.shape[1] == k and n_cols % tn == 0 and t % tm == 0
    if row_start % tn == 0:
        w_spec = pl.BlockSpec((tn, k), lambda i, j: (row_start // tn + j, 0))
    else:
        assert row_start % SUBLANES == 0 and tn % SUBLANES == 0
        w_spec = pl.BlockSpec((pl.Element(tn), pl.Element(k)),
                              lambda i, j: ((row_start // SUBLANES + j * (tn // SUBLANES)) * SUBLANES, 0))
    return pl.pallas_call(
        _mm_nt_kernel,
        out_shape=jax.ShapeDtypeStruct((t, n_cols), out_dtype),
        grid=(t // tm, n_cols // tn),
        in_specs=[pl.BlockSpec((tm, k), lambda i, j: (i, 0)), w_spec],
        out_specs=pl.BlockSpec((tm, tn), lambda i, j: (i, j)),
        compiler_params=_cparams("parallel", "arbitrary"),
        name=name,
    )(x, w_t)


HALO_BF16 = 16


def _ssd_conv_kernel(cur_ref, halo_ref, w_ref, b_ref, o_ref, buf_ref, *, ts):
    i = pl.program_id(1)
    halo = halo_ref[...].astype(F32)
    buf_ref[0:HALO_BF16, :] = jnp.where(i > 0, halo, 0.0)
    buf_ref[HALO_BF16:HALO_BF16 + ts, :] = cur_ref[...].astype(F32)
    acc = b_ref[...]
    for k in range(SSD_CONV):
        off = HALO_BF16 - (SSD_CONV - 1) + k
        acc = acc + w_ref[k:k + 1, :] * buf_ref[off:off + ts, :]
    o_ref[...] = (acc * jax.nn.sigmoid(acc)).astype(o_ref.dtype)


def ssd_conv(zx, conv_w, conv_b, *, bsz, seq, col_off, ts=1024, tc=512):
    c = conv_w.shape[1]
    t = zx.shape[0]
    nsb = seq // ts
    cb0 = col_off // tc
    hb = ts // HALO_BF16
    return pl.pallas_call(
        functools.partial(_ssd_conv_kernel, ts=ts),
        out_shape=jax.ShapeDtypeStruct((t, c), BF16),
        grid=(bsz, nsb, c // tc),
        in_specs=[
            pl.BlockSpec((ts, tc), lambda b, i, j: (b * nsb + i, cb0 + j)),
            pl.BlockSpec((HALO_BF16, tc), lambda b, i, j: (jnp.maximum((b * nsb + i) * hb - 1, 0), cb0 + j)),
            pl.BlockSpec((SSD_CONV, tc), lambda b, i, j: (0, j)),
            pl.BlockSpec((1, tc), lambda b, i, j: (0, j)),
        ],
        out_specs=pl.BlockSpec((ts, tc), lambda b, i, j: (b * nsb + i, j)),
        scratch_shapes=[pltpu.VMEM((ts + HALO_BF16, tc), F32)],
        compiler_params=_cparams("parallel", "parallel", "parallel"),
        name="ssd_conv",
    )(zx, zx, conv_w, conv_b.reshape(1, c))


def _ssd_prep_kernel(raw_ref, bias_ref, alog_ref, dt_ref, acs_ref, dtt_ref, acst_ref, *, seq):
    raw = raw_ref[...] + bias_ref[...]
    dt = jnp.maximum(raw, 0.0) + jnp.log1p(jnp.exp(-jnp.abs(raw)))
    a = -jnp.exp(alog_ref[...])
    x = dt * a
    row = lax.broadcasted_iota(jnp.int32, x.shape, 0) % SSD_CHUNK
    sh = 1
    while sh < SSD_CHUNK:
        x = x + jnp.where(row >= sh, pltpu.roll(x, sh, 0), 0.0)
        sh *= 2
    dt_ref[...] = dt
    acs_ref[...] = x
    for c in range(seq // SSD_CHUNK):
        sl = slice(c * SSD_CHUNK, (c + 1) * SSD_CHUNK)
        dtt_ref[sl, :] = dt[sl, :].T
        acst_ref[sl, :] = x[sl, :].T


def ssd_prep(dt_raw, dt_bias_p, a_log_p, *, bsz, seq):
    t = dt_raw.shape[0]
    shp = jax.ShapeDtypeStruct((t, V7X_LANES), F32)
    blk = pl.BlockSpec((seq, V7X_LANES), lambda b: (b, 0))
    vec = pl.BlockSpec((1, V7X_LANES), lambda b: (0, 0))
    return pl.pallas_call(
        functools.partial(_ssd_prep_kernel, seq=seq),
        out_shape=(shp, shp, shp, shp),
        grid=(bsz,),
        in_specs=[blk, vec, vec],
        out_specs=(blk, blk, blk, blk),
        compiler_params=_cparams("parallel"),
        name="ssd_prep",
    )(dt_raw, dt_bias_p, a_log_p)


def _conv_silu(cur_ref, halo_ref, w_ref, b_ref, buf_ref, col0, first):
    rows, c = cur_ref.shape
    cols = slice(col0, col0 + c)
    buf_ref[0:HALO_BF16, :] = jnp.where(first, 0.0, halo_ref[...].astype(F32))
    buf_ref[HALO_BF16:HALO_BF16 + rows, :] = cur_ref[...].astype(F32)
    acc = b_ref[:, cols]
    for k in range(SSD_CONV):
        off = HALO_BF16 - (SSD_CONV - 1) + k
        acc = acc + w_ref[k:k + 1, cols] * buf_ref[off:off + rows, :]
    return acc * jax.nn.sigmoid(acc)


def _ssd_main_kernel(xr_ref, xh_ref, br_ref, bh_ref, cr_ref, ch_ref, cw_ref, cbias_ref,
                     z_ref, dt_ref, acs_ref, dtt_ref, acst_ref, dsk_ref, nw_ref,
                     y_ref, st_ref, xbuf, bbuf, cbuf, *, heads_per_group):
    ci = pl.program_id(1)
    L = SSD_CHUNK
    gw = heads_per_group * SSD_HEAD_DIM
    npair = heads_per_group // 2
    d_ssd = xr_ref.shape[1]
    bc_w = br_ref.shape[1]

    @pl.when(ci == 0)
    def _():
        st_ref[...] = jnp.zeros_like(st_ref)

    first = ci == 0
    x_all = _conv_silu(xr_ref, xh_ref, cw_ref, cbias_ref, xbuf, 0, first)
    b_all = _conv_silu(br_ref, bh_ref, cw_ref, cbias_ref, bbuf, d_ssd, first)
    c_all = _conv_silu(cr_ref, ch_ref, cw_ref, cbias_ref, cbuf, d_ssd + bc_w, first)
    dt = dt_ref[...]
    acs = acs_ref[...]
    dtt = dtt_ref[...]
    acst = acst_ref[...]
    row = lax.broadcasted_iota(jnp.int32, (L, L), 0)
    col = lax.broadcasted_iota(jnp.int32, (L, L), 1)
    causal = col <= row
    lane_lo = lax.broadcasted_iota(jnp.int32, (L, 2 * SSD_HEAD_DIM), 1) < SSD_HEAD_DIM
    lane_lo1 = lax.broadcasted_iota(jnp.int32, (1, 2 * SSD_HEAD_DIM), 1) < SSD_HEAD_DIM
    zero_b = jnp.zeros((L, 2 * SSD_HEAD_DIM), BF16)

    for g in range(SSD_GROUPS):
        bmf = b_all[:, g * SSD_STATE:(g + 1) * SSD_STATE]
        bm = bmf.astype(BF16)
        cm = c_all[:, g * SSD_STATE:(g + 1) * SSD_STATE].astype(BF16)
        bmt = bmf.T.astype(BF16)
        cb = lax.dot_general(cm, bm, (((1,), (1,)), ((), ())), preferred_element_type=F32)
        xg = x_all[:, g * gw:(g + 1) * gw]
        y_diags, e_ls, w_ls, cds = [], [], [], []
        for q in range(npair):
            xp = xg[:, q * 2 * SSD_HEAD_DIM:(q + 1) * 2 * SSD_HEAD_DIM].astype(BF16)
            ms = []
            for hh in range(2):
                h = g * heads_per_group + 2 * q + hh
                seg = acs[:, h:h + 1] - acst[h:h + 1, :]
                dec = jnp.exp(jnp.where(causal, seg, NEG_INF))
                ms.append((cb * dec * dtt[h:h + 1, :]).astype(BF16))
            h0 = g * heads_per_group + 2 * q
            h1 = h0 + 1
            lhs = jnp.concatenate(ms, axis=1)
            rhs = jnp.concatenate([jnp.where(lane_lo, xp, zero_b), jnp.where(lane_lo, zero_b, xp)], axis=0)
            y_diags.append(jnp.dot(lhs, rhs, preferred_element_type=F32))
            last0 = acs[L - 1:L, h0:h0 + 1]
            last1 = acs[L - 1:L, h1:h1 + 1]
            e_ls.append(jnp.where(lane_lo, jnp.exp(acs[:, h0:h0 + 1]), jnp.exp(acs[:, h1:h1 + 1])))
            w_ls.append(jnp.where(lane_lo, dt[:, h0:h0 + 1] * jnp.exp(last0 - acs[:, h0:h0 + 1]),
                                  dt[:, h1:h1 + 1] * jnp.exp(last1 - acs[:, h1:h1 + 1])))
            cds.append(jnp.where(lane_lo1, jnp.exp(last0), jnp.exp(last1)))
        st = st_ref[g]
        y_off = jnp.dot(cm, st.astype(BF16), preferred_element_type=F32) * jnp.concatenate(e_ls, axis=1)
        xw = (xg * jnp.concatenate(w_ls, axis=1)).astype(BF16)
        st_ref[g] = st * jnp.concatenate(cds, axis=1) + jnp.dot(bmt, xw, preferred_element_type=F32)
        y = jnp.concatenate(y_diags, axis=1) + y_off + xg * dsk_ref[:, g * gw:(g + 1) * gw]
        zg = z_ref[:, g * gw:(g + 1) * gw].astype(F32)
        yg = y * (zg * jax.nn.sigmoid(zg))
        ms_ = jnp.mean(yg * yg, axis=-1, keepdims=True)
        y_ref[:, g * gw:(g + 1) * gw] = (yg * lax.rsqrt(ms_ + EPS) * nw_ref[:, g * gw:(g + 1) * gw]).astype(y_ref.dtype)


def ssd_main(zx, conv_w, conv_b, dt, acs, dtt, acst, dskip_e, norm_w, *, bsz, seq, d_ssd):
    t = zx.shape[0]
    nc = seq // SSD_CHUNK
    heads = d_ssd // SSD_HEAD_DIM
    hpg = heads // SSD_GROUPS
    bc_w = SSD_GROUPS * SSD_STATE
    conv_dim = conv_w.shape[1]
    assert d_ssd % bc_w == 0 and hpg % 2 == 0 and conv_dim == d_ssd + 2 * bc_w
    rowmap = lambda b, c: (b * nc + c, 0)
    hb = SSD_CHUNK // HALO_BF16

    def cur(width, col_block):
        return pl.BlockSpec((SSD_CHUNK, width), lambda b, c: (b * nc + c, col_block))

    def halo(width, col_block):
        return pl.BlockSpec((HALO_BF16, width), lambda b, c: (jnp.maximum((b * nc + c) * hb - 1, 0), col_block))

    xcol, bcol, ccol = 1, 2 * d_ssd // bc_w, 2 * d_ssd // bc_w + 1
    return pl.pallas_call(
        functools.partial(_ssd_main_kernel, heads_per_group=hpg),
        out_shape=jax.ShapeDtypeStruct((t, d_ssd), BF16),
        grid=(bsz, nc),
        in_specs=[
            cur(d_ssd, xcol), halo(d_ssd, xcol), cur(bc_w, bcol), halo(bc_w, bcol), cur(bc_w, ccol), halo(bc_w, ccol),
            pl.BlockSpec((SSD_CONV, conv_dim), lambda b, c: (0, 0)),
            pl.BlockSpec((1, conv_dim), lambda b, c: (0, 0)),
            pl.BlockSpec((SSD_CHUNK, d_ssd), rowmap),
            pl.BlockSpec((SSD_CHUNK, V7X_LANES), rowmap),
            pl.BlockSpec((SSD_CHUNK, V7X_LANES), rowmap),
            pl.BlockSpec((SSD_CHUNK, V7X_LANES), rowmap),
            pl.BlockSpec((SSD_CHUNK, V7X_LANES), rowmap),
            pl.BlockSpec((1, d_ssd), lambda b, c: (0, 0)),
            pl.BlockSpec((1, d_ssd), lambda b, c: (0, 0)),
        ],
        out_specs=pl.BlockSpec((SSD_CHUNK, d_ssd), rowmap),
        scratch_shapes=[pltpu.VMEM((SSD_GROUPS, SSD_STATE, hpg * SSD_HEAD_DIM), F32),
                        pltpu.VMEM((SSD_CHUNK + HALO_BF16, d_ssd), F32),
                        pltpu.VMEM((SSD_CHUNK + HALO_BF16, bc_w), F32),
                        pltpu.VMEM((SSD_CHUNK + HALO_BF16, bc_w), F32)],
        compiler_params=_cparams("parallel", "arbitrary"),
        name="ssd_main",
    )(zx, zx, zx, zx, zx, zx, conv_w, conv_b.reshape(1, conv_dim), zx, dt, acs, dtt, acst, dskip_e,
      norm_w.reshape(1, d_ssd))


CONF_HALO = 32


SUBLANES = 8


def _conf_kernel(a_ref, g_ref, ah_ref, gh_ref, w_ref, b_ref, lw_ref, lb_ref, o_ref, buf_ref, sh_ref, *, ts, rc, lt):
    i = pl.program_id(1)
    d = a_ref.shape[1]
    ah = ah_ref[...].astype(F32)
    gh = gh_ref[...].astype(F32)
    buf_ref[0:CONF_HALO, :] = jnp.where(i > 0, ah * jax.nn.sigmoid(gh), 0.0)
    a = a_ref[...].astype(F32)
    g = g_ref[...].astype(F32)
    buf_ref[CONF_HALO:CONF_HALO + ts, :] = a * jax.nn.sigmoid(g)
    span = ts + CONF_HALO - SUBLANES
    for j in range(1, SUBLANES):
        sh_ref[j - 1, 0:span, :] = buf_ref[j:j + span, :]
    base = CONF_HALO - (CONF_KERNEL - 1)
    outs = []
    for r in range(ts // rc):
        cols = []
        for c in range(d // lt):
            ls = slice(c * lt, (c + 1) * lt)
            acc = jnp.broadcast_to(b_ref[:, ls], (rc, lt))
            for k in range(CONF_KERNEL):
                j = (base + k) % SUBLANES
                off = r * rc + (base + k) - j
                src = buf_ref[off:off + rc, ls] if j == 0 else sh_ref[j - 1, off:off + rc, ls]
                acc = acc + w_ref[k:k + 1, ls] * src
            cols.append(acc)
        outs.append(jnp.concatenate(cols, axis=1))
    u = jnp.concatenate(outs, axis=0)
    mu = jnp.mean(u, axis=-1, keepdims=True)
    uc = u - mu
    var = jnp.mean(uc * uc, axis=-1, keepdims=True)
    y = uc * lax.rsqrt(var + EPS) * lw_ref[...] + lb_ref[...]
    o_ref[...] = (y * jax.nn.sigmoid(y)).astype(o_ref.dtype)


def conf_module(conf, dw_w, dw_b, ln_w, ln_b, *, bsz, seq, ts=256, rc=32, lt=512):
    t = conf.shape[0]
    d = dw_w.shape[1]
    nsb = seq // ts
    hb = ts // CONF_HALO
    cur = lambda col: pl.BlockSpec((ts, d), lambda b, i: (b * nsb + i, col))
    halo = lambda col: pl.BlockSpec((CONF_HALO, d), lambda b, i: (jnp.maximum((b * nsb + i) * hb - 1, 0), col))
    vec = pl.BlockSpec((1, d), lambda b, i: (0, 0))
    return pl.pallas_call(
        functools.partial(_conf_kernel, ts=ts, rc=rc, lt=lt),
        out_shape=jax.ShapeDtypeStruct((t, d), BF16),
        grid=(bsz, nsb),
        in_specs=[cur(0), cur(1), halo(0), halo(1), pl.BlockSpec((CONF_KERNEL, d), lambda b, i: (0, 0)),
                  vec, vec, vec],
        out_specs=pl.BlockSpec((ts, d), lambda b, i: (b * nsb + i, 0)),
        scratch_shapes=[pltpu.VMEM((ts + CONF_HALO, d), F32), pltpu.VMEM((SUBLANES - 1, ts + CONF_HALO, d), F32)],
        compiler_params=_cparams("parallel", "parallel"),
        name="conf_module",
    )(conf, conf, conf, conf, dw_w, dw_b.reshape(1, d), ln_w.reshape(1, d), ln_b.reshape(1, d))


FFN_SUB = 128


FFN_NCHUNK = 512


def pack_bf16_pair(a, b):
    ua = lax.bitcast_convert_type(a.astype(BF16).astype(F32), jnp.uint32)
    ub = lax.bitcast_convert_type(b.astype(BF16).astype(F32), jnp.uint32)
    return (ua >> 16) | (ub & jnp.uint32(0xFFFF0000))


def unpack_bf16_pair(u):
    a = lax.bitcast_convert_type(u << 16, F32)
    b = lax.bitcast_convert_type(u & jnp.uint32(0xFFFF0000), F32)
    return a, b


def _ffn_kernel(ge_ref, gb_ref, gr_ref, x_ref, w1_ref, w3_ref, w2_ref, o_ref, acc_ref, *xb_scr, nf, gsz, packed, ragged):
    g = pl.program_id(0)
    f = pl.program_id(1)
    rows = gr_ref[g]
    valid = rows > 0
    d = acc_ref.shape[1]

    @pl.when(valid & (f == 0))
    def _():
        acc_ref[...] = jnp.zeros_like(acc_ref)
        if packed:
            a, b = unpack_bf16_pair(x_ref[...])
            xb_scr[0][:, 0:d // 2] = a.astype(BF16)
            xb_scr[0][:, d // 2:d] = b.astype(BF16)

    xsrc = xb_scr[0] if packed else x_ref

    def slab(r0, nrows, w1, w3, w2):
        xs = xsrc[pl.ds(r0, nrows), :]
        h1 = jnp.dot(xs, w1, preferred_element_type=F32)
        h3 = jnp.dot(xs, w3, preferred_element_type=F32)
        hh = (h1 * jax.nn.sigmoid(h1) * h3).astype(BF16)
        for c in range(d // FFN_NCHUNK):
            cs = slice(c * FFN_NCHUNK, (c + 1) * FFN_NCHUNK)
            acc_ref[pl.ds(r0, nrows), cs] += jnp.dot(hh, w2[:, cs], preferred_element_type=F32)

    nslab = (rows + FFN_SUB - 1) // FFN_SUB
    for ns in (range(1, gsz // FFN_SUB + 1) if ragged else (gsz // FFN_SUB,)):
        @pl.when(nslab == ns)
        def _(ns=ns):
            slab(0, ns * FFN_SUB, w1_ref[...].astype(BF16), w3_ref[...].astype(BF16), w2_ref[...].astype(BF16))

    @pl.when(jnp.logical_not(valid) & (f == 0))
    def _():
        o_ref[...] = jnp.zeros_like(o_ref)

    @pl.when(valid & (f == nf - 1))
    def _():
        if packed:
            o_ref[...] = pack_bf16_pair(acc_ref[:, 0:d // 2], acc_ref[:, d // 2:d])
        else:
            o_ref[...] = acc_ref[...].astype(o_ref.dtype)


def ffn_groups(xs, w1, w3, w2, grp_expert, grp_block, grp_rows, *, gsz=1024, tf=256, packed=False, ragged=False,
               single_buffer_rows=False):
    e, d, ff = w1.shape
    r, dx = xs.shape
    nf = ff // tf
    ng = grp_expert.shape[0]

    def fmap(f, g, gr):
        return jnp.where(gr[g] > 0, f, nf - 1)

    scratch = [pltpu.VMEM((gsz, d), F32)]
    if packed:
        scratch.append(pltpu.VMEM((gsz, d), BF16))
    row_mode = dict(pipeline_mode=pl.Buffered(1)) if single_buffer_rows else {}
    grid_spec = pltpu.PrefetchScalarGridSpec(
        num_scalar_prefetch=3,
        grid=(ng, nf),
        in_specs=[
            pl.BlockSpec((gsz, dx), lambda g, f, ge, gb, gr: (gb[g], 0), **row_mode),
            pl.BlockSpec((None, d, tf), lambda g, f, ge, gb, gr: (ge[g], 0, fmap(f, g, gr))),
            pl.BlockSpec((None, d, tf), lambda g, f, ge, gb, gr: (ge[g], 0, fmap(f, g, gr))),
            pl.BlockSpec((None, tf, d), lambda g, f, ge, gb, gr: (ge[g], fmap(f, g, gr), 0)),
        ],
        out_specs=pl.BlockSpec((gsz, dx), lambda g, f, ge, gb, gr: (gb[g], 0), **row_mode),
        scratch_shapes=scratch,
    )
    return pl.pallas_call(
        functools.partial(_ffn_kernel, nf=nf, gsz=gsz, packed=packed, ragged=ragged),
        out_shape=jax.ShapeDtypeStruct((r, dx), xs.dtype),
        grid_spec=grid_spec,
        compiler_params=_cparams("arbitrary", "arbitrary"),
        name="ffn_groups",
    )(grp_expert, grp_block, grp_rows, xs, w1, w3, w2)


def _row_copy(src_ref, src_row, dst_ref, dst_row, sem):
    return pltpu.make_async_copy(src_ref.at[pl.ds(src_row, 1), :], dst_ref.at[pl.ds(dst_row, 1), :], sem)


DMA_UNROLL = 8


def _dispatch_kernel(pos0_ref, pos1_ref, h_ref, xs_in_ref, xs_ref, sem):
    del xs_in_ref
    tm = h_ref.shape[0]
    base = pl.program_id(0) * tm

    def copies(i):
        return [_row_copy(h_ref, i, xs_ref, p[base + i], sem) for p in (pos0_ref, pos1_ref)]

    def issue(i, c):
        for cp in copies(i):
            cp.start()
        return c

    lax.fori_loop(0, tm, issue, 0, unroll=DMA_UNROLL)

    def drain(i, c):
        for cp in copies(i):
            cp.wait()
        return c

    lax.fori_loop(0, tm, drain, 0, unroll=DMA_UNROLL)


def moe_dispatch(hp, pos0, pos1, n_rows, tm=512):
    t, w = hp.shape
    xs0 = jnp.zeros((n_rows, w), hp.dtype)
    grid_spec = pltpu.PrefetchScalarGridSpec(
        num_scalar_prefetch=2,
        grid=(t // tm,),
        in_specs=[pl.BlockSpec((tm, w), lambda i, p0, p1: (i, 0)), pl.BlockSpec(memory_space=pl.ANY)],
        out_specs=pl.BlockSpec(memory_space=pl.ANY),
        scratch_shapes=[pltpu.SemaphoreType.DMA],
    )
    return pl.pallas_call(
        _dispatch_kernel,
        out_shape=jax.ShapeDtypeStruct((n_rows, w), hp.dtype),
        grid_spec=grid_spec,
        input_output_aliases={3: 0},
        compiler_params=_cparams("arbitrary"),
        name="moe_dispatch",
    )(pos0, pos1, hp, xs0)


def _t5_bucket_np(dist):
    max_exact = REL_BUCKETS // 2
    d_f = np.maximum(dist, 1).astype(np.float32)
    large = max_exact + (np.log(d_f / np.float32(max_exact)) / np.float32(math.log(REL_MAX_DIST / max_exact))
                         * np.float32(REL_BUCKETS - max_exact)).astype(np.int32)
    large = np.minimum(large, REL_BUCKETS - 1)
    return np.where(dist < max_exact, dist, large)


RES = 4
TQ = ATTN_BLOCK
PIECE = TQ // RES


def _bucket_index_tables():
    assert DILATED_PATTERNS == ((128, 1), (512, 4), (2048, 16)) and RES == 4 and TQ == 128

    def fin(steps, dil, n_back=TQ):
        ok = (steps >= 0) & (steps <= n_back)
        return np.where(ok, _t5_bucket_np(np.clip(steps, 0, n_back) * dil), -1).astype(np.int32)

    i = np.arange(TQ)[:, None]
    t4 = fin(i + TQ - np.arange(2 * TQ)[None, :], 4)
    t16 = fin(i - np.arange(TQ)[None, :], 16)
    rq, mq = i // PIECE, i % PIECE
    jk = np.arange(2 * TQ)[None, :]
    rk, mk = jk // (2 * PIECE), jk % (2 * PIECE)
    t1 = fin(RES * (PIECE + mq - mk) + (rq - rk), 1)
    t1first = fin(RES * (mq - mk) + (rq - rk), 1)
    return t4, t1, t1first, t16


def _bias_tab_kernel(rel_ref, *refs):
    n = len(refs) // 2
    h = pl.program_id(0)
    for idx_ref, o_ref in zip(refs[:n], refs[n:]):
        idx = idx_ref[...]
        tab = jnp.full(idx.shape, NEG_INF, F32)
        for b in range(REL_BUCKETS):
            tab = jnp.where(idx == b, rel_ref[b, h], tab)
        o_ref[0] = tab


def attn_bias_tables(rel_bias):
    idx_tabs = [jnp.asarray(t) for t in _bucket_index_tables()]
    nh = rel_bias.shape[1]
    return pl.pallas_call(
        _bias_tab_kernel,
        out_shape=tuple(jax.ShapeDtypeStruct((nh,) + t.shape, F32) for t in idx_tabs),
        grid=(nh,),
        in_specs=[pl.BlockSpec(memory_space=pltpu.SMEM)] + [pl.BlockSpec(t.shape, lambda h: (0, 0)) for t in idx_tabs],
        out_specs=tuple(pl.BlockSpec((1,) + t.shape, lambda h: (h, 0, 0)) for t in idx_tabs),
        compiler_params=_cparams("parallel"),
        name="attn_bias_tables",
    )(rel_bias.astype(F32), *idx_tabs)


def _attn_kernel(q_ref, k_ref, v_ref, t4_ref, t1_ref, t1f_ref, t16_ref, o_ref, acc, m_s, l_s, qs, ks, vs, od, md, ld,
                 *, scale):
    nres, lsub, hd = q_ref.shape
    nt = (((1,), (1,)), ((), ()))

    def tile(q, k, v, bias):
        s = lax.dot_general(q, k, nt, preferred_element_type=F32) * scale + bias
        m = jnp.max(s, axis=-1, keepdims=True)
        p = jnp.exp(s - m)
        l = jnp.sum(p, axis=-1, keepdims=True)
        o = jnp.dot(p.astype(BF16), v, preferred_element_type=F32)
        return o, jnp.broadcast_to(m, o.shape), jnp.broadcast_to(l, o.shape)

    def merge(r, rows, o, m, l):
        m0 = m_s[r, rows, :]
        mn = jnp.maximum(m0, m)
        a = jnp.exp(m0 - mn)
        b = jnp.exp(m - mn)
        acc[r, rows, :] = acc[r, rows, :] * a + o * b
        l_s[r, rows, :] = l_s[r, rows, :] * a + l * b
        m_s[r, rows, :] = mn

    nsub = lsub // TQ
    for r in range(nres):
        qs[r] = q_ref[r].astype(F32)
        ks[r] = k_ref[r].astype(F32)
        vs[r] = v_ref[r].astype(F32)
    masked = jnp.full((TQ, TQ), NEG_INF, F32)
    bias16 = (jnp.concatenate([t16_ref[0], masked], axis=1), jnp.concatenate([masked, t16_ref[0]], axis=1))
    for r in range(nres):
        for pair in range(nsub // 2):
            sub = [pl.ds(2 * pair + e, TQ, stride=nsub) for e in range(2)]
            kk = jnp.concatenate([ks[r, s, :] for s in sub], axis=0).astype(BF16)
            vv = jnp.concatenate([vs[r, s, :] for s in sub], axis=0).astype(BF16)
            for e in range(2):
                o, m, l = tile(qs[r, sub[e], :].astype(BF16), kk, vv, bias16[e])
                od[r * nsub + 2 * pair + e] = o
                md[r * nsub + 2 * pair + e] = m
                ld[r * nsub + 2 * pair + e] = l

    for r in range(nres):
        for i in range(lsub // TQ):
            rows = pl.ds(i * TQ, TQ)
            if i == 0:
                krows = pl.ds(0, 2 * TQ)
                bias = jnp.concatenate([t4_ref[0, :, TQ:2 * TQ], masked], axis=1)
            else:
                krows = pl.ds((i - 1) * TQ, 2 * TQ)
                bias = t4_ref[0]
            o, m, l = tile(q_ref[r, rows, :], k_ref[r, krows, :], v_ref[r, krows, :], bias)
            acc[r, rows, :] = o
            m_s[r, rows, :] = m
            l_s[r, rows, :] = l

    for j in range(lsub // PIECE):
        q = jnp.concatenate([q_ref[r, pl.ds(j * PIECE, PIECE), :] for r in range(nres)], axis=0)
        if j == 0:
            krows, bias = pl.ds(0, 2 * PIECE), t1f_ref[0]
        else:
            krows, bias = pl.ds((j - 1) * PIECE, 2 * PIECE), t1_ref[0]
        k = jnp.concatenate([k_ref[r, krows, :] for r in range(nres)], axis=0)
        v = jnp.concatenate([v_ref[r, krows, :] for r in range(nres)], axis=0)
        o, m, l = tile(q, k, v, bias)
        for r in range(nres):
            part = slice(r * PIECE, (r + 1) * PIECE)
            merge(r, pl.ds(j * PIECE, PIECE), o[part], m[part], l[part])

    for r in range(nres):
        for c in range(nsub):
            merge(r, pl.ds(c, TQ, stride=nsub), od[r * nsub + c], md[r * nsub + c], ld[r * nsub + c])

    for r in range(nres):
        o_ref[r] = (acc[r] / l_s[r]).astype(o_ref.dtype)


def attention(qkv, tabs, *, bsz, seq, hd):
    dm = ATTN_HEADS * hd
    lsub = seq // RES
    assert hd == V7X_LANES and lsub // TQ == 16 // RES and seq // 16 == TQ
    view = qkv.reshape(bsz * RES, lsub, 3 * dm)
    blk = lambda which: pl.BlockSpec((RES, lsub, hd), lambda b, h: (b, 0, which * ATTN_HEADS + h))
    tab = lambda t: pl.BlockSpec((1,) + t.shape[1:], lambda b, h: (h, 0, 0))
    st = pltpu.VMEM((RES, lsub, hd), F32)
    sub_f = pltpu.VMEM((RES * lsub // TQ, TQ, hd), F32)
    o = pl.pallas_call(
        functools.partial(_attn_kernel, scale=hd ** -0.5),
        out_shape=jax.ShapeDtypeStruct((bsz * RES, lsub, dm), BF16),
        grid=(bsz, ATTN_HEADS),
        in_specs=[blk(0), blk(1), blk(2)] + [tab(t) for t in tabs],
        out_specs=pl.BlockSpec((RES, lsub, hd), lambda b, h: (b, 0, h)),
        scratch_shapes=[st, st, st, st, st, st, sub_f, sub_f, sub_f],
        compiler_params=_cparams("parallel", "arbitrary"),
        name="attention",
    )(view, view, view, *tabs)
    return o.reshape(bsz * seq, dm)


def _split_bf16(x):
    hi = x.astype(BF16)
    lo = (x - hi.astype(F32)).astype(BF16)
    return hi, lo


def _moe_route_kernel(x_ref, g_ref, wr_ref, h_ref, meta_ref, metat_ref, cnt_ref, carry_ref):
    step = pl.program_id(0)
    tm = x_ref.shape[0]

    @pl.when(step == 0)
    def _():
        carry_ref[...] = jnp.zeros_like(carry_ref)

    h = _rms(x_ref[...], g_ref[...])
    h_hi, h_lo = _split_bf16(h)
    dh = h.shape[1] // 2
    h_ref[...] = pack_bf16_pair(h[:, 0:dh], h[:, dh:2 * dh])
    w_hi, w_lo = _split_bf16(wr_ref[...])
    logits = (jnp.dot(h_hi, w_hi, preferred_element_type=F32) + jnp.dot(h_lo, w_hi, preferred_element_type=F32)
              + jnp.dot(h_hi, w_lo, preferred_element_type=F32))
    lane = lax.broadcasted_iota(jnp.int32, (tm, V7X_LANES), 1).astype(F32)
    lg = jnp.where(lane < N_EXPERTS, logits, NEG_INF)
    m1 = jnp.max(lg, axis=-1, keepdims=True)
    i1 = jnp.min(jnp.where(lg == m1, lane, float(V7X_LANES)), axis=-1, keepdims=True)
    lg2 = jnp.where(lane == i1, NEG_INF, lg)
    m2 = jnp.max(lg2, axis=-1, keepdims=True)
    i2 = jnp.min(jnp.where(lg2 == m2, lane, float(V7X_LANES)), axis=-1, keepdims=True)
    e2 = jnp.exp(m2 - m1)
    g1 = 1.0 / (1.0 + e2)
    g2 = e2 / (1.0 + e2)
    oh = ((lane == i1) | (lane == i2)).astype(BF16)
    r_i = lax.broadcasted_iota(jnp.int32, (tm, tm), 0)
    c_i = lax.broadcasted_iota(jnp.int32, (tm, tm), 1)
    tri = (c_i < r_i).astype(BF16)
    rank = jnp.dot(tri, oh, preferred_element_type=F32) + carry_ref[0:1, :]
    r1 = jnp.sum(jnp.where(lane == i1, rank, 0.0), axis=-1, keepdims=True)
    r2 = jnp.sum(jnp.where(lane == i2, rank, 0.0), axis=-1, keepdims=True)
    carry_ref[0:1, :] = carry_ref[0:1, :] + jnp.sum(oh.astype(F32), axis=0, keepdims=True)
    meta = jnp.where(lane == 0, i1.astype(F32), 0.0)
    meta = jnp.where(lane == 1, i2.astype(F32), meta)
    meta = jnp.where(lane == 2, g1, meta)
    meta = jnp.where(lane == 3, g2, meta)
    meta = jnp.where(lane == 4, r1, meta)
    meta = jnp.where(lane == 5, r2, meta)
    meta_ref[...] = meta
    for c in range(tm // V7X_LANES):
        blk = meta[c * V7X_LANES:(c + 1) * V7X_LANES, :].T
        metat_ref[:, c * V7X_LANES:(c + 1) * V7X_LANES] = blk[0:SUBLANES, :]
    cnt_ref[...] = jnp.broadcast_to(carry_ref[0:1, :], cnt_ref.shape)


def moe_route(x, g, w_router_p, tm=256):
    t, d = x.shape
    return pl.pallas_call(
        _moe_route_kernel,
        out_shape=(jax.ShapeDtypeStruct((t, d // 2), jnp.uint32), jax.ShapeDtypeStruct((t, V7X_LANES), F32),
                   jax.ShapeDtypeStruct((SUBLANES, t), F32), jax.ShapeDtypeStruct((8, V7X_LANES), F32)),
        grid=(t // tm,),
        in_specs=[pl.BlockSpec((tm, d), lambda i: (i, 0)), pl.BlockSpec((1, d), lambda i: (0, 0)),
                  pl.BlockSpec((d, V7X_LANES), lambda i: (0, 0))],
        out_specs=(pl.BlockSpec((tm, d // 2), lambda i: (i, 0)), pl.BlockSpec((tm, V7X_LANES), lambda i: (i, 0)),
                   pl.BlockSpec((SUBLANES, tm), lambda i: (0, i)), pl.BlockSpec((8, V7X_LANES), lambda i: (0, 0))),
        scratch_shapes=[pltpu.VMEM((8, V7X_LANES), F32)],
        compiler_params=_cparams("arbitrary"),
        name="moe_route",
    )(x, g.reshape(1, d), w_router_p)


def _moe_final_kernel(pos0_ref, pos1_ref, x_ref, meta_ref, g_ref, ys_ref, o_ref, ybuf, o_scr, sem, *, lsub):
    nres, tp, d = x_ref.shape
    dh = d // 2
    b = pl.program_id(0)
    j = pl.program_id(1)

    def copies(r, mm):
        tok = (b * nres + r) * lsub + j * tp + mm
        return [_row_copy(ys_ref, p[tok], ybuf.at[k], r * tp + mm, sem) for k, p in enumerate((pos0_ref, pos1_ref))]

    for r in range(nres):
        def issue(mm, c, r=r):
            for cp in copies(r, mm):
                cp.start()
            return c

        lax.fori_loop(0, tp, issue, 0, unroll=DMA_UNROLL)

    for r in range(nres):
        def drain(mm, c, r=r):
            for cp in copies(r, mm):
                cp.wait()
            return c

        lax.fori_loop(0, tp, drain, 0, unroll=DMA_UNROLL)
    for r in range(nres):
        rows = slice(r * tp, (r + 1) * tp)
        meta = meta_ref[r]
        g1 = meta[:, 2:3]
        g2 = meta[:, 3:4]
        a0, b0 = unpack_bf16_pair(ybuf[0, rows, :])
        a1, b1 = unpack_bf16_pair(ybuf[1, rows, :])
        x = x_ref[r]
        xa = x[:, 0:dh] + a0 * g1 + a1 * g2
        xb = x[:, dh:d] + b0 * g1 + b1 * g2
        ms = (jnp.sum(xa * xa, axis=-1, keepdims=True) + jnp.sum(xb * xb, axis=-1, keepdims=True)) * (1.0 / d)
        inv = lax.rsqrt(ms + EPS)
        ya = xa * inv * g_ref[:, 0:dh]
        yb = xb * inv * g_ref[:, dh:d]
        out_rows = pl.ds(r, tp, stride=nres)
        for s in range(dh // V7X_LANES):
            ls = slice(s * V7X_LANES, (s + 1) * V7X_LANES)
            o_scr[s, out_rows, :] = ya[:, ls]
            o_scr[dh // V7X_LANES + s, out_rows, :] = yb[:, ls]
    for s in range(d // V7X_LANES):
        o_ref[:, s * V7X_LANES:(s + 1) * V7X_LANES] = o_scr[s]


def moe_final(x, ys, pos0, pos1, meta, g, *, bsz, seq, tp=128):
    t, d = x.shape
    lsub = seq // RES
    tm = RES * tp
    xv = x.reshape(bsz * RES, lsub, d)
    mv = meta.reshape(bsz * RES, lsub, V7X_LANES)
    grid_spec = pltpu.PrefetchScalarGridSpec(
        num_scalar_prefetch=2,
        grid=(bsz, lsub // tp),
        in_specs=[pl.BlockSpec((RES, tp, d), lambda b, j, p0, p1: (b, j, 0)),
                  pl.BlockSpec((RES, tp, V7X_LANES), lambda b, j, p0, p1: (b, j, 0)),
                  pl.BlockSpec((1, d), lambda b, j, p0, p1: (0, 0)), pl.BlockSpec(memory_space=pl.ANY)],
        out_specs=pl.BlockSpec((tm, d), lambda b, j, p0, p1: (b * (lsub // tp) + j, 0)),
        scratch_shapes=[pltpu.VMEM((TOP_K, tm, d // 2), jnp.uint32), pltpu.VMEM((d // V7X_LANES, tm, V7X_LANES), F32),
                        pltpu.SemaphoreType.DMA],
    )
    return pl.pallas_call(
        functools.partial(_moe_final_kernel, lsub=lsub),
        out_shape=jax.ShapeDtypeStruct((t, d), F32),
        grid_spec=grid_spec,
        compiler_params=_cparams("arbitrary", "arbitrary"),
        name="moe_final",
    )(pos0, pos1, xv, mv, g.reshape(1, d), ys)


def _pad_lanes(v, fill=0.0):
    v = v.reshape(1, -1).astype(F32)
    return jnp.pad(v, ((0, 0), (0, V7X_LANES - v.shape[1])), constant_values=fill)


def even_layer(x, h, p, *, bsz, seq):
    t, d = x.shape
    d_ssd = d
    d_conf = d
    heads = d_ssd // SSD_HEAD_DIM
    conv_dim = d_ssd + 2 * SSD_GROUPS * SSD_STATE
    i1 = d_ssd + conv_dim
    i2 = i1 + heads
    w_in_t = jnp.swapaxes(p["w_in"], 0, 1)
    zx = matmul_nt(h, w_in_t, row_start=0, n_cols=i1, tn=1024, name="in_proj_zx")
    dt_raw = matmul_nt(h, w_in_t, row_start=i1, n_cols=V7X_LANES, tn=V7X_LANES, out_dtype=F32, name="in_proj_dt")
    conf = matmul_nt(h, w_in_t, row_start=i2, n_cols=2 * d_conf, tn=1024, name="in_proj_conf")
    u = conf_module(conf, p["conf_dw_w"], p["conf_dw_b"], p["conf_ln_w"], p["conf_ln_b"], bsz=bsz, seq=seq)
    dt, acs, dtt, acst = ssd_prep(dt_raw, _pad_lanes(p["dt_bias"]), _pad_lanes(p["a_log"]), bsz=bsz, seq=seq)
    dskip_e = jnp.repeat(p["d_skip"].astype(F32), SSD_HEAD_DIM).reshape(1, d_ssd)
    y_ssd = ssd_main(zx, p["conv_w"], p["conv_b"], dt, acs, dtt, acst, dskip_e, p["ssd_norm_w"],
                     bsz=bsz, seq=seq, d_ssd=d_ssd)
    return matmul([y_ssd, u], p["w_out"], n_cols=d, tn=512, res=x, out_dtype=F32, name="even_out_proj")


def dense_ffn(h, w1, w3, w2, gsz=1024):
    t = h.shape[0]
    ng = t // gsz
    ge = jnp.zeros((ng,), jnp.int32)
    gb = jnp.arange(ng, dtype=jnp.int32)
    gr = jnp.full((ng,), gsz, jnp.int32)
    return ffn_groups(h, w1[None], w3[None], w2[None], ge, gb, gr, gsz=gsz)


def odd_layer_attn(x, h, p, *, bsz, seq):
    t, d = x.shape
    hd = d // ATTN_HEADS
    qkv = matmul([h], p["w_qkv"], n_cols=3 * d, tn=1024, name="qkv_proj")
    tabs = attn_bias_tables(p["rel_bias"])
    o = attention(qkv, tabs, bsz=bsz, seq=seq, hd=hd)
    return matmul([o], p["w_attn_out"], n_cols=d, tn=512, tm=2048, res=x, out_dtype=F32, name="attn_out_proj")


def moe_layer(x, norm_g, w_router, w1, w3, w2, final_g, *, bsz, seq, gsz=1280):
    t, d = x.shape
    ne = w1.shape[0]
    w_router_p = jnp.pad(w_router.astype(F32), ((0, 0), (0, V7X_LANES - ne)))
    h, meta, meta_t, cnt = moe_route(x, norm_g, w_router_p)
    counts = cnt[0, :ne].astype(jnp.int32)
    ngrp_e = (counts + gsz - 1) // gsz
    grp_end = jnp.cumsum(ngrp_e)
    grp_start = grp_end - ngrp_e
    ng = (t * TOP_K) // gsz + ne
    j = jnp.arange(ng, dtype=jnp.int32)
    total = grp_end[-1]
    last = jnp.maximum(total - 1, 0)
    jj = jnp.minimum(j, last)
    ge = jnp.searchsorted(grp_end, jj, side="right").astype(jnp.int32)
    ge = jnp.minimum(ge, ne - 1)
    rows = jnp.clip(counts[ge] - (jj - grp_start[ge]) * gsz, 0, gsz)
    gr = jnp.where(j < total, rows, 0).astype(jnp.int32)
    gb = j

    def dest_row(e, rank):
        off = jnp.zeros_like(rank)
        for k in range(ne):
            off = jnp.where(e == k, grp_start[k] * gsz, off)
        return off + rank

    pos0 = dest_row(meta_t[0].astype(jnp.int32), meta_t[4].astype(jnp.int32))
    pos1 = dest_row(meta_t[1].astype(jnp.int32), meta_t[5].astype(jnp.int32))
    xs = moe_dispatch(h, pos0, pos1, ng * gsz)
    ys = ffn_groups(xs, w1, w3, w2, ge, gb, gr, gsz=gsz, packed=True, ragged=True)
    return moe_final(x, ys, pos0, pos1, meta, final_g, bsz=bsz, seq=seq)


def kernel(x, norm_mix, norm_ffn, norm_final, even_w_in, ssd_conv_w, ssd_conv_b, ssd_dt_bias, ssd_a_log, ssd_d, ssd_norm_w, conf_dw_w, conf_dw_b, conf_ln_w, conf_ln_b, even_w_out, ffn_w1, ffn_w3, ffn_w2, attn_w_qkv, attn_w_out, rel_bias, moe_router, moe_w1, moe_w3, moe_w2):
    bsz, seq, d = x.shape
    assert norm_mix.shape[0] == 2, "two-layer trunk: one even (SSD+Conformer/FFN) and one odd (attention/MoE) layer"
    xf = x.reshape(bsz * seq, d)
    h = rmsnorm(xf, norm_mix[0])
    p_even = dict(w_in=even_w_in[0], conv_w=ssd_conv_w[0], conv_b=ssd_conv_b[0], dt_bias=ssd_dt_bias[0],
                  a_log=ssd_a_log[0], d_skip=ssd_d[0], ssd_norm_w=ssd_norm_w[0], conf_dw_w=conf_dw_w[0],
                  conf_dw_b=conf_dw_b[0], conf_ln_w=conf_ln_w[0], conf_ln_b=conf_ln_b[0], w_out=even_w_out[0])
    x1 = even_layer(xf, h, p_even, bsz=bsz, seq=seq)
    h1 = rmsnorm(x1, norm_ffn[0])
    y_ffn = dense_ffn(h1, ffn_w1[0], ffn_w3[0], ffn_w2[0])
    x2, h2 = add_norm_to_residue_major(x1, y_ffn, norm_mix[1], bsz=bsz, seq=seq, nres=RES)
    p_odd = dict(w_qkv=attn_w_qkv[0], w_attn_out=attn_w_out[0], rel_bias=rel_bias)
    x3 = odd_layer_attn(x2, h2, p_odd, bsz=bsz, seq=seq)
    out = moe_layer(x3, norm_ffn[1], moe_router[0], moe_w1[0], moe_w3[0], moe_w2[0], norm_final, bsz=bsz, seq=seq)
    return out.reshape(bsz, seq, d)
```

```python
import functools
import math

import numpy as np
import jax
import jax.numpy as jnp
from jax import lax
from jax.experimental import pallas as pl
from jax.experimental.pallas import tpu as pltpu

SSD_HEAD_DIM = 64
SSD_GROUPS = 4
SSD_STATE = 128
SSD_CONV = 4
SSD_CHUNK = 128
CONF_KERNEL = 31
ATTN_HEADS = 16
DILATED_PATTERNS = ((128, 1), (512, 4), (2048, 16))
ATTN_BLOCK = 128
REL_BUCKETS = 32
REL_MAX_DIST = 2048
N_EXPERTS = 8
TOP_K = 2
EPS = 1e-6

V7X_LANES = 128
V7X_VMEM_BYTES = 64 * 1024 * 1024
VMEM_LIMIT = 56 * 1024 * 1024

F32 = jnp.float32
BF16 = jnp.bfloat16
NEG_INF = float("-inf")


def _cparams(*sem):
    return pltpu.CompilerParams(dimension_semantics=tuple(sem), vmem_limit_bytes=VMEM_LIMIT)


def _rms(x, g):
    ms = jnp.mean(x * x, axis=-1, keepdims=True)
    return x * lax.rsqrt(ms + EPS) * g


def _rmsnorm_kernel(x_ref, g_ref, h_ref):
    h_ref[...] = _rms(x_ref[...], g_ref[...]).astype(h_ref.dtype)


def rmsnorm(x, g, out_dtype=BF16, tm=512):
    t, d = x.shape
    return pl.pallas_call(
        _rmsnorm_kernel,
        out_shape=jax.ShapeDtypeStruct((t, d), out_dtype),
        grid=(t // tm,),
        in_specs=[pl.BlockSpec((tm, d), lambda i: (i, 0)), pl.BlockSpec((1, d), lambda i: (0, 0))],
        out_specs=pl.BlockSpec((tm, d), lambda i: (i, 0)),
        compiler_params=_cparams("parallel"),
        name="rmsnorm",
    )(x, g.reshape(1, d))


def _add_norm_res_kernel(x_ref, y_ref, g_ref, xo_ref, ho_ref, xs_scr, hs_scr):
    nres, tp, d = xo_ref.shape
    xn = x_ref[...] + y_ref[...].astype(F32)
    h = _rms(xn, g_ref[...])
    for s in range(d // V7X_LANES):
        ls = slice(s * V7X_LANES, (s + 1) * V7X_LANES)
        xs_scr[s] = xn[:, ls]
        hs_scr[s] = h[:, ls]
    for r in range(nres):
        rows = pl.ds(r, tp, stride=nres)
        for s in range(d // V7X_LANES):
            ls = slice(s * V7X_LANES, (s + 1) * V7X_LANES)
            xo_ref[r, :, ls] = xs_scr[s, rows, :]
            ho_ref[r, :, ls] = hs_scr[s, rows, :].astype(ho_ref.dtype)


def add_norm_to_residue_major(x, y, g, *, bsz, seq, nres, tp=128):
    t, d = x.shape
    lsub = seq // nres
    tm = nres * tp
    nj = lsub // tp
    slab = pltpu.VMEM((d // V7X_LANES, tm, V7X_LANES), F32)
    xo, ho = pl.pallas_call(
        _add_norm_res_kernel,
        out_shape=(jax.ShapeDtypeStruct((bsz * nres, lsub, d), F32), jax.ShapeDtypeStruct((bsz * nres, lsub, d), BF16)),
        grid=(bsz, nj),
        in_specs=[pl.BlockSpec((tm, d), lambda b, j: (b * nj + j, 0)), pl.BlockSpec((tm, d), lambda b, j: (b * nj + j, 0)),
                  pl.BlockSpec((1, d), lambda b, j: (0, 0))],
        out_specs=(pl.BlockSpec((nres, tp, d), lambda b, j: (b, j, 0)), pl.BlockSpec((nres, tp, d), lambda b, j: (b, j, 0))),
        scratch_shapes=[slab, slab],
        compiler_params=_cparams("parallel", "parallel"),
        name="add_norm_reorder",
    )(x, y, g.reshape(1, d))
    return xo.reshape(t, d), ho.reshape(t, d)


def _mm_kernel(*refs, n_lhs, ks, has_res):
    xs = refs[:n_lhs]
    w_ref = refs[n_lhs]
    r_ref = refs[n_lhs + 1] if has_res else None
    o_ref = refs[-1]
    acc = None
    k0 = 0
    for x_ref, k in zip(xs, ks):
        part = jnp.dot(x_ref[...], w_ref[k0:k0 + k, :].astype(BF16), preferred_element_type=F32)
        acc = part if acc is None else acc + part
        k0 += k
    if has_res:
        acc = acc + r_ref[...]
    o_ref[...] = acc.astype(o_ref.dtype)


def matmul(xs, w, *, n_cols, col_block_off=0, tn, tm=1024, res=None, out_dtype=BF16, name="matmul"):
    t = xs[0].shape[0]
    ks = tuple(x.shape[1] for x in xs)
    ktot = sum(ks)
    assert w.shape[0] == ktot and n_cols % tn == 0 and t % tm == 0
    in_specs = [pl.BlockSpec((tm, k), lambda i, j: (i, 0)) for k in ks]
    in_specs.append(pl.BlockSpec((ktot, tn), lambda i, j: (0, j + col_block_off)))
    args = list(xs) + [w]
    if res is not None:
        in_specs.append(pl.BlockSpec((tm, tn), lambda i, j: (i, j)))
        args.append(res)
    return pl.pallas_call(
        functools.partial(_mm_kernel, n_lhs=len(xs), ks=ks, has_res=res is not None),
        out_shape=jax.ShapeDtypeStruct((t, n_cols), out_dtype),
        grid=(t // tm, n_cols // tn),
        in_specs=in_specs,
        out_specs=pl.BlockSpec((tm, tn), lambda i, j: (i, j)),
        compiler_params=_cparams("parallel", "arbitrary"),
        name=name,
    )(*args)


def _mm_nt_kernel(x_ref, wt_ref, o_ref):
    wt = wt_ref[...].astype(BF16)
    acc = lax.dot_general(x_ref[...], wt, (((1,), (1,)), ((), ())), preferred_element_type=F32)
    o_ref[...] = acc.astype(o_ref.dtype)


def matmul_nt(x, w_t, *, row_start, n_cols, tn, tm=1024, out_dtype=BF16, name="matmul_nt"):
    t, k = x.shape
    assert w_t.shape[1] == k and n_cols % tn == 0 and t % tm == 0
    if row_start % tn == 0:
        w_spec = pl.BlockSpec((tn, k), lambda i, j: (row_start // tn + j, 0))
    else:
        assert row_start % SUBLANES == 0 and tn % SUBLANES == 0
        w_spec = pl.BlockSpec((pl.Element(tn), pl.Element(k)),
                              lambda i, j: ((row_start // SUBLANES + j * (tn // SUBLANES)) * SUBLANES, 0))
    return pl.pallas_call(
        _mm_nt_kernel,
        out_shape=jax.ShapeDtypeStruct((t, n_cols), out_dtype),
        grid=(t // tm, n_cols // tn),
        in_specs=[pl.BlockSpec((tm, k), lambda i, j: (i, 0)), w_spec],
        out_specs=pl.BlockSpec((tm, tn), lambda i, j: (i, j)),
        compiler_params=_cparams("parallel", "arbitrary"),
        name=name,
    )(x, w_t)


HALO_BF16 = 16


def _ssd_conv_kernel(cur_ref, halo_ref, w_ref, b_ref, o_ref, buf_ref, *, ts):
    i = pl.program_id(1)
    halo = halo_ref[...].astype(F32)
    buf_ref[0:HALO_BF16, :] = jnp.where(i > 0, halo, 0.0)
    buf_ref[HALO_BF16:HALO_BF16 + ts, :] = cur_ref[...].astype(F32)
    acc = b_ref[...]
    for k in range(SSD_CONV):
        off = HALO_BF16 - (SSD_CONV - 1) + k
        acc = acc + w_ref[k:k + 1, :] * buf_ref[off:off + ts, :]
    o_ref[...] = (acc * jax.nn.sigmoid(acc)).astype(o_ref.dtype)


def ssd_conv(zx, conv_w, conv_b, *, bsz, seq, col_off, ts=1024, tc=512):
    c = conv_w.shape[1]
    t = zx.shape[0]
    nsb = seq // ts
    cb0 = col_off // tc
    hb = ts // HALO_BF16
    return pl.pallas_call(
        functools.partial(_ssd_conv_kernel, ts=ts),
        out_shape=jax.ShapeDtypeStruct((t, c), BF16),
        grid=(bsz, nsb, c // tc),
        in_specs=[
            pl.BlockSpec((ts, tc), lambda b, i, j: (b * nsb + i, cb0 + j)),
            pl.BlockSpec((HALO_BF16, tc), lambda b, i, j: (jnp.maximum((b * nsb + i) * hb - 1, 0), cb0 + j)),
            pl.BlockSpec((SSD_CONV, tc), lambda b, i, j: (0, j)),
            pl.BlockSpec((1, tc), lambda b, i, j: (0, j)),
        ],
        out_specs=pl.BlockSpec((ts, tc), lambda b, i, j: (b * nsb + i, j)),
        scratch_shapes=[pltpu.VMEM((ts + HALO_BF16, tc), F32)],
        compiler_params=_cparams("parallel", "parallel", "parallel"),
        name="ssd_conv",
    )(zx, zx, conv_w, conv_b.reshape(1, c))


def _ssd_prep_kernel(raw_ref, bias_ref, alog_ref, dt_ref, acs_ref, dtt_ref, acst_ref, *, seq):
    raw = raw_ref[...] + bias_ref[...]
    dt = jnp.maximum(raw, 0.0) + jnp.log1p(jnp.exp(-jnp.abs(raw)))
    a = -jnp.exp(alog_ref[...])
    x = dt * a
    row = lax.broadcasted_iota(jnp.int32, x.shape, 0) % SSD_CHUNK
    sh = 1
    while sh < SSD_CHUNK:
        x = x + jnp.where(row >= sh, pltpu.roll(x, sh, 0), 0.0)
        sh *= 2
    dt_ref[...] = dt
    acs_ref[...] = x
    for c in range(seq // SSD_CHUNK):
        sl = slice(c * SSD_CHUNK, (c + 1) * SSD_CHUNK)
        dtt_ref[sl, :] = dt[sl, :].T
        acst_ref[sl, :] = x[sl, :].T


def ssd_prep(dt_raw, dt_bias_p, a_log_p, *, bsz, seq):
    t = dt_raw.shape[0]
    shp = jax.ShapeDtypeStruct((t, V7X_LANES), F32)
    blk = pl.BlockSpec((seq, V7X_LANES), lambda b: (b, 0))
    vec = pl.BlockSpec((1, V7X_LANES), lambda b: (0, 0))
    return pl.pallas_call(
        functools.partial(_ssd_prep_kernel, seq=seq),
        out_shape=(shp, shp, shp, shp),
        grid=(bsz,),
        in_specs=[blk, vec, vec],
        out_specs=(blk, blk, blk, blk),
        compiler_params=_cparams("parallel"),
        name="ssd_prep",
    )(dt_raw, dt_bias_p, a_log_p)


def _conv_silu(cur_ref, halo_ref, w_ref, b_ref, buf_ref, col0, first):
    rows, c = cur_ref.shape
    cols = slice(col0, col0 + c)
    buf_ref[0:HALO_BF16, :] = jnp.where(first, 0.0, halo_ref[...].astype(F32))
    buf_ref[HALO_BF16:HALO_BF16 + rows, :] = cur_ref[...].astype(F32)
    acc = b_ref[:, cols]
    for k in range(SSD_CONV):
        off = HALO_BF16 - (SSD_CONV - 1) + k
        acc = acc + w_ref[k:k + 1, cols] * buf_ref[off:off + rows, :]
    return acc * jax.nn.sigmoid(acc)


def _ssd_main_kernel(xr_ref, xh_ref, br_ref, bh_ref, cr_ref, ch_ref, cw_ref, cbias_ref,
                     z_ref, dt_ref, acs_ref, dtt_ref, acst_ref, dsk_ref, nw_ref,
                     y_ref, st_ref, xbuf, bbuf, cbuf, *, heads_per_group):
    ci = pl.program_id(1)
    L = SSD_CHUNK
    gw = heads_per_group * SSD_HEAD_DIM
    npair = heads_per_group // 2
    d_ssd = xr_ref.shape[1]
    bc_w = br_ref.shape[1]

    @pl.when(ci == 0)
    def _():
        st_ref[...] = jnp.zeros_like(st_ref)

    first = ci == 0
    x_all = _conv_silu(xr_ref, xh_ref, cw_ref, cbias_ref, xbuf, 0, first)
    b_all = _conv_silu(br_ref, bh_ref, cw_ref, cbias_ref, bbuf, d_ssd, first)
    c_all = _conv_silu(cr_ref, ch_ref, cw_ref, cbias_ref, cbuf, d_ssd + bc_w, first)
    dt = dt_ref[...]
    acs = acs_ref[...]
    dtt = dtt_ref[...]
    acst = acst_ref[...]
    row = lax.broadcasted_iota(jnp.int32, (L, L), 0)
    col = lax.broadcasted_iota(jnp.int32, (L, L), 1)
    causal = col <= row
    lane_lo = lax.broadcasted_iota(jnp.int32, (L, 2 * SSD_HEAD_DIM), 1) < SSD_HEAD_DIM
    lane_lo1 = lax.broadcasted_iota(jnp.int32, (1, 2 * SSD_HEAD_DIM), 1) < SSD_HEAD_DIM
    zero_b = jnp.zeros((L, 2 * SSD_HEAD_DIM), BF16)

    for g in range(SSD_GROUPS):
        bmf = b_all[:, g * SSD_STATE:(g + 1) * SSD_STATE]
        bm = bmf.astype(BF16)
        cm = c_all[:, g * SSD_STATE:(g + 1) * SSD_STATE].astype(BF16)
        bmt = bmf.T.astype(BF16)
        cb = lax.dot_general(cm, bm, (((1,), (1,)), ((), ())), preferred_element_type=F32)
        xg = x_all[:, g * gw:(g + 1) * gw]
        y_diags, e_ls, w_ls, cds = [], [], [], []
        for q in range(npair):
            xp = xg[:, q * 2 * SSD_HEAD_DIM:(q + 1) * 2 * SSD_HEAD_DIM].astype(BF16)
            ms = []
            for hh in range(2):
                h = g * heads_per_group + 2 * q + hh
                seg = acs[:, h:h + 1] - acst[h:h + 1, :]
                dec = jnp.exp(jnp.where(causal, seg, NEG_INF))
                ms.append((cb * dec * dtt[h:h + 1, :]).astype(BF16))
            h0 = g * heads_per_group + 2 * q
            h1 = h0 + 1
            lhs = jnp.concatenate(ms, axis=1)
            rhs = jnp.concatenate([jnp.where(lane_lo, xp, zero_b), jnp.where(lane_lo, zero_b, xp)], axis=0)
            y_diags.append(jnp.dot(lhs, rhs, preferred_element_type=F32))
            last0 = acs[L - 1:L, h0:h0 + 1]
            last1 = acs[L - 1:L, h1:h1 + 1]
            e_ls.append(jnp.where(lane_lo, jnp.exp(acs[:, h0:h0 + 1]), jnp.exp(acs[:, h1:h1 + 1])))
            w_ls.append(jnp.where(lane_lo, dt[:, h0:h0 + 1] * jnp.exp(last0 - acs[:, h0:h0 + 1]),
                                  dt[:, h1:h1 + 1] * jnp.exp(last1 - acs[:, h1:h1 + 1])))
            cds.append(jnp.where(lane_lo1, jnp.exp(last0), jnp.exp(last1)))
        st = st_ref[g]
        y_off = jnp.dot(cm, st.astype(BF16), preferred_element_type=F32) * jnp.concatenate(e_ls, axis=1)
        xw = (xg * jnp.concatenate(w_ls, axis=1)).astype(BF16)
        st_ref[g] = st * jnp.concatenate(cds, axis=1) + jnp.dot(bmt, xw, preferred_element_type=F32)
        y = jnp.concatenate(y_diags, axis=1) + y_off + xg * dsk_ref[:, g * gw:(g + 1) * gw]
        zg = z_ref[:, g * gw:(g + 1) * gw].astype(F32)
        yg = y * (zg * jax.nn.sigmoid(zg))
        ms_ = jnp.mean(yg * yg, axis=-1, keepdims=True)
        y_ref[:, g * gw:(g + 1) * gw] = (yg * lax.rsqrt(ms_ + EPS) * nw_ref[:, g * gw:(g + 1) * gw]).astype(y_ref.dtype)


def ssd_main(zx, conv_w, conv_b, dt, acs, dtt, acst, dskip_e, norm_w, *, bsz, seq, d_ssd):
    t = zx.shape[0]
    nc = seq // SSD_CHUNK
    heads = d_ssd // SSD_HEAD_DIM
    hpg = heads // SSD_GROUPS
    bc_w = SSD_GROUPS * SSD_STATE
    conv_dim = conv_w.shape[1]
    assert d_ssd % bc_w == 0 and hpg % 2 == 0 and conv_dim == d_ssd + 2 * bc_w
    rowmap = lambda b, c: (b * nc + c, 0)
    hb = SSD_CHUNK // HALO_BF16

    def cur(width, col_block):
        return pl.BlockSpec((SSD_CHUNK, width), lambda b, c: (b * nc + c, col_block))

    def halo(width, col_block):
        return pl.BlockSpec((HALO_BF16, width), lambda b, c: (jnp.maximum((b * nc + c) * hb - 1, 0), col_block))

    xcol, bcol, ccol = 1, 2 * d_ssd // bc_w, 2 * d_ssd // bc_w + 1
    return pl.pallas_call(
        functools.partial(_ssd_main_kernel, heads_per_group=hpg),
        out_shape=jax.ShapeDtypeStruct((t, d_ssd), BF16),
        grid=(bsz, nc),
        in_specs=[
            cur(d_ssd, xcol), halo(d_ssd, xcol), cur(bc_w, bcol), halo(bc_w, bcol), cur(bc_w, ccol), halo(bc_w, ccol),
            pl.BlockSpec((SSD_CONV, conv_dim), lambda b, c: (0, 0)),
            pl.BlockSpec((1, conv_dim), lambda b, c: (0, 0)),
            pl.BlockSpec((SSD_CHUNK, d_ssd), rowmap),
            pl.BlockSpec((SSD_CHUNK, V7X_LANES), rowmap),
            pl.BlockSpec((SSD_CHUNK, V7X_LANES), rowmap),
            pl.BlockSpec((SSD_CHUNK, V7X_LANES), rowmap),
            pl.BlockSpec((SSD_CHUNK, V7X_LANES), rowmap),
            pl.BlockSpec((1, d_ssd), lambda b, c: (0, 0)),
            pl.BlockSpec((1, d_ssd), lambda b, c: (0, 0)),
        ],
        out_specs=pl.BlockSpec((SSD_CHUNK, d_ssd), rowmap),
        scratch_shapes=[pltpu.VMEM((SSD_GROUPS, SSD_STATE, hpg * SSD_HEAD_DIM), F32),
                        pltpu.VMEM((SSD_CHUNK + HALO_BF16, d_ssd), F32),
                        pltpu.VMEM((SSD_CHUNK + HALO_BF16, bc_w), F32),
                        pltpu.VMEM((SSD_CHUNK + HALO_BF16, bc_w), F32)],
        compiler_params=_cparams("parallel", "arbitrary"),
        name="ssd_main",
    )(zx, zx, zx, zx, zx, zx, conv_w, conv_b.reshape(1, conv_dim), zx, dt, acs, dtt, acst, dskip_e,
      norm_w.reshape(1, d_ssd))


CONF_HALO = 32


SUBLANES = 8


def _conf_kernel(a_ref, g_ref, ah_ref, gh_ref, w_ref, b_ref, lw_ref, lb_ref, o_ref, buf_ref, sh_ref, *, ts, rc, lt):
    i = pl.program_id(1)
    d = a_ref.shape[1]
    ah = ah_ref[...].astype(F32)
    gh = gh_ref[...].astype(F32)
    buf_ref[0:CONF_HALO, :] = jnp.where(i > 0, ah * jax.nn.sigmoid(gh), 0.0)
    a = a_ref[...].astype(F32)
    g = g_ref[...].astype(F32)
    buf_ref[CONF_HALO:CONF_HALO + ts, :] = a * jax.nn.sigmoid(g)
    span = ts + CONF_HALO - SUBLANES
    for j in range(1, SUBLANES):
        sh_ref[j - 1, 0:span, :] = buf_ref[j:j + span, :]
    base = CONF_HALO - (CONF_KERNEL - 1)
    outs = []
    for r in range(ts // rc):
        cols = []
        for c in range(d // lt):
            ls = slice(c * lt, (c + 1) * lt)
            acc = jnp.broadcast_to(b_ref[:, ls], (rc, lt))
            for k in range(CONF_KERNEL):
                j = (base + k) % SUBLANES
                off = r * rc + (base + k) - j
                src = buf_ref[off:off + rc, ls] if j == 0 else sh_ref[j - 1, off:off + rc, ls]
                acc = acc + w_ref[k:k + 1, ls] * src
            cols.append(acc)
        outs.append(jnp.concatenate(cols, axis=1))
    u = jnp.concatenate(outs, axis=0)
    mu = jnp.mean(u, axis=-1, keepdims=True)
    uc = u - mu
    var = jnp.mean(uc * uc, axis=-1, keepdims=True)
    y = uc * lax.rsqrt(var + EPS) * lw_ref[...] + lb_ref[...]
    o_ref[...] = (y * jax.nn.sigmoid(y)).astype(o_ref.dtype)


def conf_module(conf, dw_w, dw_b, ln_w, ln_b, *, bsz, seq, ts=256, rc=32, lt=512):
    t = conf.shape[0]
    d = dw_w.shape[1]
    nsb = seq // ts
    hb = ts // CONF_HALO
    cur = lambda col: pl.BlockSpec((ts, d), lambda b, i: (b * nsb + i, col))
    halo = lambda col: pl.BlockSpec((CONF_HALO, d), lambda b, i: (jnp.maximum((b * nsb + i) * hb - 1, 0), col))
    vec = pl.BlockSpec((1, d), lambda b, i: (0, 0))
    return pl.pallas_call(
        functools.partial(_conf_kernel, ts=ts, rc=rc, lt=lt),
        out_shape=jax.ShapeDtypeStruct((t, d), BF16),
        grid=(bsz, nsb),
        in_specs=[cur(0), cur(1), halo(0), halo(1), pl.BlockSpec((CONF_KERNEL, d), lambda b, i: (0, 0)),
                  vec, vec, vec],
        out_specs=pl.BlockSpec((ts, d), lambda b, i: (b * nsb + i, 0)),
        scratch_shapes=[pltpu.VMEM((ts + CONF_HALO, d), F32), pltpu.VMEM((SUBLANES - 1, ts + CONF_HALO, d), F32)],
        compiler_params=_cparams("parallel", "parallel"),
        name="conf_module",
    )(conf, conf, conf, conf, dw_w, dw_b.reshape(1, d), ln_w.reshape(1, d), ln_b.reshape(1, d))


FFN_SUB = 128


FFN_NCHUNK = 512


def pack_bf16_pair(a, b):
    ua = lax.bitcast_convert_type(a.astype(BF16).astype(F32), jnp.uint32)
    ub = lax.bitcast_convert_type(b.astype(BF16).astype(F32), jnp.uint32)
    return (ua >> 16) | (ub & jnp.uint32(0xFFFF0000))


def unpack_bf16_pair(u):
    a = lax.bitcast_convert_type(u << 16, F32)
    b = lax.bitcast_convert_type(u & jnp.uint32(0xFFFF0000), F32)
    return a, b


def _ffn_kernel(ge_ref, gb_ref, gr_ref, x_ref, w1_ref, w3_ref, w2_ref, o_ref, acc_ref, *xb_scr, nf, gsz, packed, ragged):
    g = pl.program_id(0)
    f = pl.program_id(1)
    rows = gr_ref[g]
    valid = rows > 0
    d = acc_ref.shape[1]

    @pl.when(valid & (f == 0))
    def _():
        acc_ref[...] = jnp.zeros_like(acc_ref)
        if packed:
            a, b = unpack_bf16_pair(x_ref[...])
            xb_scr[0][:, 0:d // 2] = a.astype(BF16)
            xb_scr[0][:, d // 2:d] = b.astype(BF16)

    xsrc = xb_scr[0] if packed else x_ref

    def slab(r0, nrows, w1, w3, w2):
        xs = xsrc[pl.ds(r0, nrows), :]
        h1 = jnp.dot(xs, w1, preferred_element_type=F32)
        h3 = jnp.dot(xs, w3, preferred_element_type=F32)
        hh = (h1 * jax.nn.sigmoid(h1) * h3).astype(BF16)
        for c in range(d // FFN_NCHUNK):
            cs = slice(c * FFN_NCHUNK, (c + 1) * FFN_NCHUNK)
            acc_ref[pl.ds(r0, nrows), cs] += jnp.dot(hh, w2[:, cs], preferred_element_type=F32)

    nslab = (rows + FFN_SUB - 1) // FFN_SUB
    for ns in (range(1, gsz // FFN_SUB + 1) if ragged else (gsz // FFN_SUB,)):
        @pl.when(nslab == ns)
        def _(ns=ns):
            slab(0, ns * FFN_SUB, w1_ref[...].astype(BF16), w3_ref[...].astype(BF16), w2_ref[...].astype(BF16))

    @pl.when(jnp.logical_not(valid) & (f == 0))
    def _():
        o_ref[...] = jnp.zeros_like(o_ref)

    @pl.when(valid & (f == nf - 1))
    def _():
        if packed:
            o_ref[...] = pack_bf16_pair(acc_ref[:, 0:d // 2], acc_ref[:, d // 2:d])
        else:
            o_ref[...] = acc_ref[...].astype(o_ref.dtype)


def ffn_groups(xs, w1, w3, w2, grp_expert, grp_block, grp_rows, *, gsz=1024, tf=256, packed=False, ragged=False,
               single_buffer_rows=False):
    e, d, ff = w1.shape
    r, dx = xs.shape
    nf = ff // tf
    ng = grp_expert.shape[0]

    def fmap(f, g, gr):
        return jnp.where(gr[g] > 0, f, nf - 1)

    scratch = [pltpu.VMEM((gsz, d), F32)]
    if packed:
        scratch.append(pltpu.VMEM((gsz, d), BF16))
    row_mode = dict(pipeline_mode=pl.Buffered(1)) if single_buffer_rows else {}
    grid_spec = pltpu.PrefetchScalarGridSpec(
        num_scalar_prefetch=3,
        grid=(ng, nf),
        in_specs=[
            pl.BlockSpec((gsz, dx), lambda g, f, ge, gb, gr: (gb[g], 0), **row_mode),
            pl.BlockSpec((None, d, tf), lambda g, f, ge, gb, gr: (ge[g], 0, fmap(f, g, gr))),
            pl.BlockSpec((None, d, tf), lambda g, f, ge, gb, gr: (ge[g], 0, fmap(f, g, gr))),
            pl.BlockSpec((None, tf, d), lambda g, f, ge, gb, gr: (ge[g], fmap(f, g, gr), 0)),
        ],
        out_specs=pl.BlockSpec((gsz, dx), lambda g, f, ge, gb, gr: (gb[g], 0), **row_mode),
        scratch_shapes=scratch,
    )
    return pl.pallas_call(
        functools.partial(_ffn_kernel, nf=nf, gsz=gsz, packed=packed, ragged=ragged),
        out_shape=jax.ShapeDtypeStruct((r, dx), xs.dtype),
        grid_spec=grid_spec,
        compiler_params=_cparams("arbitrary", "arbitrary"),
        name="ffn_groups",
    )(grp_expert, grp_block, grp_rows, xs, w1, w3, w2)


def _row_copy(src_ref, src_row, dst_ref, dst_row, sem):
    return pltpu.make_async_copy(src_ref.at[pl.ds(src_row, 1), :], dst_ref.at[pl.ds(dst_row, 1), :], sem)


DMA_UNROLL = 8


def _dispatch_kernel(pos0_ref, pos1_ref, h_ref, xs_in_ref, xs_ref, sem):
    del xs_in_ref
    tm = h_ref.shape[0]
    base = pl.program_id(0) * tm

    def copies(i):
        return [_row_copy(h_ref, i, xs_ref, p[base + i], sem) for p in (pos0_ref, pos1_ref)]

    def issue(i, c):
        for cp in copies(i):
            cp.start()
        return c

    lax.fori_loop(0, tm, issue, 0, unroll=DMA_UNROLL)

    def drain(i, c):
        for cp in copies(i):
            cp.wait()
        return c

    lax.fori_loop(0, tm, drain, 0, unroll=DMA_UNROLL)


def moe_dispatch(hp, pos0, pos1, n_rows, tm=512):
    t, w = hp.shape
    xs0 = jnp.zeros((n_rows, w), hp.dtype)
    grid_spec = pltpu.PrefetchScalarGridSpec(
        num_scalar_prefetch=2,
        grid=(t // tm,),
        in_specs=[pl.BlockSpec((tm, w), lambda i, p0, p1: (i, 0)), pl.BlockSpec(memory_space=pl.ANY)],
        out_specs=pl.BlockSpec(memory_space=pl.ANY),
        scratch_shapes=[pltpu.SemaphoreType.DMA],
    )
    return pl.pallas_call(
        _dispatch_kernel,
        out_shape=jax.ShapeDtypeStruct((n_rows, w), hp.dtype),
        grid_spec=grid_spec,
        input_output_aliases={3: 0},
        compiler_params=_cparams("arbitrary"),
        name="moe_dispatch",
    )(pos0, pos1, hp, xs0)


def _t5_bucket_np(dist):
    max_exact = REL_BUCKETS // 2
    d_f = np.maximum(dist, 1).astype(np.float32)
    large = max_exact + (np.log(d_f / np.float32(max_exact)) / np.float32(math.log(REL_MAX_DIST / max_exact))
                         * np.float32(REL_BUCKETS - max_exact)).astype(np.int32)
    large = np.minimum(large, REL_BUCKETS - 1)
    return np.where(dist < max_exact, dist, large)


RES = 4
TQ = ATTN_BLOCK
PIECE = TQ // RES


def _bucket_index_tables():
    assert DILATED_PATTERNS == ((128, 1), (512, 4), (2048, 16)) and RES == 4 and TQ == 128

    def fin(steps, dil, n_back=TQ):
        ok = (steps >= 0) & (steps <= n_back)
        return np.where(ok, _t5_bucket_np(np.clip(steps, 0, n_back) * dil), -1).astype(np.int32)

    i = np.arange(TQ)[:, None]
    t4 = fin(i + TQ - np.arange(2 * TQ)[None, :], 4)
    t16 = fin(i - np.arange(TQ)[None, :], 16)
    rq, mq = i // PIECE, i % PIECE
    jk = np.arange(2 * TQ)[None, :]
    rk, mk = jk // (2 * PIECE), jk % (2 * PIECE)
    t1 = fin(RES * (PIECE + mq - mk) + (rq - rk), 1)
    t1first = fin(RES * (mq - mk) + (rq - rk), 1)
    return t4, t1, t1first, t16


def _bias_tab_kernel(rel_ref, *refs):
    n = len(refs) // 2
    h = pl.program_id(0)
    for idx_ref, o_ref in zip(refs[:n], refs[n:]):
        idx = idx_ref[...]
        tab = jnp.full(idx.shape, NEG_INF, F32)
        for b in range(REL_BUCKETS):
            tab = jnp.where(idx == b, rel_ref[b, h], tab)
        o_ref[0] = tab


def attn_bias_tables(rel_bias):
    idx_tabs = [jnp.asarray(t) for t in _bucket_index_tables()]
    nh = rel_bias.shape[1]
    return pl.pallas_call(
        _bias_tab_kernel,
        out_shape=tuple(jax.ShapeDtypeStruct((nh,) + t.shape, F32) for t in idx_tabs),
        grid=(nh,),
        in_specs=[pl.BlockSpec(memory_space=pltpu.SMEM)] + [pl.BlockSpec(t.shape, lambda h: (0, 0)) for t in idx_tabs],
        out_specs=tuple(pl.BlockSpec((1,) + t.shape, lambda h: (h, 0, 0)) for t in idx_tabs),
        compiler_params=_cparams("parallel"),
        name="attn_bias_tables",
    )(rel_bias.astype(F32), *idx_tabs)


def _attn_kernel(q_ref, k_ref, v_ref, t4_ref, t1_ref, t1f_ref, t16_ref, o_ref, acc, m_s, l_s, qs, ks, vs, od, md, ld,
                 *, scale):
    nres, lsub, hd = q_ref.shape
    nt = (((1,), (1,)), ((), ()))

    def tile(q, k, v, bias):
        s = lax.dot_general(q, k, nt, preferred_element_type=F32) * scale + bias
        m = jnp.max(s, axis=-1, keepdims=True)
        p = jnp.exp(s - m)
        l = jnp.sum(p, axis=-1, keepdims=True)
        o = jnp.dot(p.astype(BF16), v, preferred_element_type=F32)
        return o, jnp.broadcast_to(m, o.shape), jnp.broadcast_to(l, o.shape)

    def merge(r, rows, o, m, l):
        m0 = m_s[r, rows, :]
        mn = jnp.maximum(m0, m)
        a = jnp.exp(m0 - mn)
        b = jnp.exp(m - mn)
        acc[r, rows, :] = acc[r, rows, :] * a + o * b
        l_s[r, rows, :] = l_s[r, rows, :] * a + l * b
        m_s[r, rows, :] = mn

    nsub = lsub // TQ
    for r in range(nres):
        qs[r] = q_ref[r].astype(F32)
        ks[r] = k_ref[r].astype(F32)
        vs[r] = v_ref[r].astype(F32)
    masked = jnp.full((TQ, TQ), NEG_INF, F32)
    bias16 = (jnp.concatenate([t16_ref[0], masked], axis=1), jnp.concatenate([masked, t16_ref[0]], axis=1))
    for r in range(nres):
        for pair in range(nsub // 2):
            sub = [pl.ds(2 * pair + e, TQ, stride=nsub) for e in range(2)]
            kk = jnp.concatenate([ks[r, s, :] for s in sub], axis=0).astype(BF16)
            vv = jnp.concatenate([vs[r, s, :] for s in sub], axis=0).astype(BF16)
            for e in range(2):
                o, m, l = tile(qs[r, sub[e], :].astype(BF16), kk, vv, bias16[e])
                od[r * nsub + 2 * pair + e] = o
                md[r * nsub + 2 * pair + e] = m
                ld[r * nsub + 2 * pair + e] = l

    for r in range(nres):
        for i in range(lsub // TQ):
            rows = pl.ds(i * TQ, TQ)
            if i == 0:
                krows = pl.ds(0, 2 * TQ)
                bias = jnp.concatenate([t4_ref[0, :, TQ:2 * TQ], masked], axis=1)
            else:
                krows = pl.ds((i - 1) * TQ, 2 * TQ)
                bias = t4_ref[0]
            o, m, l = tile(q_ref[r, rows, :], k_ref[r, krows, :], v_ref[r, krows, :], bias)
            acc[r, rows, :] = o
            m_s[r, rows, :] = m
            l_s[r, rows, :] = l

    for j in range(lsub // PIECE):
        q = jnp.concatenate([q_ref[r, pl.ds(j * PIECE, PIECE), :] for r in range(nres)], axis=0)
        if j == 0:
            krows, bias = pl.ds(0, 2 * PIECE), t1f_ref[0]
        else:
            krows, bias = pl.ds((j - 1) * PIECE, 2 * PIECE), t1_ref[0]
        k = jnp.concatenate([k_ref[r, krows, :] for r in range(nres)], axis=0)
        v = jnp.concatenate([v_ref[r, krows, :] for r in range(nres)], axis=0)
        o, m, l = tile(q, k, v, bias)
        for r in range(nres):
            part = slice(r * PIECE, (r + 1) * PIECE)
            merge(r, pl.ds(j * PIECE, PIECE), o[part], m[part], l[part])

    for r in range(nres):
        for c in range(nsub):
            merge(r, pl.ds(c, TQ, stride=nsub), od[r * nsub + c], md[r * nsub + c], ld[r * nsub + c])

    for r in range(nres):
        o_ref[r] = (acc[r] / l_s[r]).astype(o_ref.dtype)


def attention(qkv, tabs, *, bsz, seq, hd):
    dm = ATTN_HEADS * hd
    lsub = seq // RES
    assert hd == V7X_LANES and lsub // TQ == 16 // RES and seq // 16 == TQ
    view = qkv.reshape(bsz * RES, lsub, 3 * dm)
    blk = lambda which: pl.BlockSpec((RES, lsub, hd), lambda b, h: (b, 0, which * ATTN_HEADS + h))
    tab = lambda t: pl.BlockSpec((1,) + t.shape[1:], lambda b, h: (h, 0, 0))
    st = pltpu.VMEM((RES, lsub, hd), F32)
    sub_f = pltpu.VMEM((RES * lsub // TQ, TQ, hd), F32)
    o = pl.pallas_call(
        functools.partial(_attn_kernel, scale=hd ** -0.5),
        out_shape=jax.ShapeDtypeStruct((bsz * RES, lsub, dm), BF16),
        grid=(bsz, ATTN_HEADS),
        in_specs=[blk(0), blk(1), blk(2)] + [tab(t) for t in tabs],
        out_specs=pl.BlockSpec((RES, lsub, hd), lambda b, h: (b, 0, h)),
        scratch_shapes=[st, st, st, st, st, st, sub_f, sub_f, sub_f],
        compiler_params=_cparams("parallel", "arbitrary"),
        name="attention",
    )(view, view, view, *tabs)
    return o.reshape(bsz * seq, dm)


def _split_bf16(x):
    hi = x.astype(BF16)
    lo = (x - hi.astype(F32)).astype(BF16)
    return hi, lo


def _moe_route_kernel(x_ref, g_ref, wr_ref, h_ref, meta_ref, metat_ref, cnt_ref, carry_ref):
    step = pl.program_id(0)
    tm = x_ref.shape[0]

    @pl.when(step == 0)
    def _():
        carry_ref[...] = jnp.zeros_like(carry_ref)

    h = _rms(x_ref[...], g_ref[...])
    h_hi, h_lo = _split_bf16(h)
    dh = h.shape[1] // 2
    h_ref[...] = pack_bf16_pair(h[:, 0:dh], h[:, dh:2 * dh])
    w_hi, w_lo = _split_bf16(wr_ref[...])
    both = jnp.dot(h_hi, jnp.concatenate([w_hi, w_lo], axis=1), preferred_element_type=F32)
    logits = both[:, 0:V7X_LANES] + jnp.dot(h_lo, w_hi, preferred_element_type=F32) + both[:, V7X_LANES:2 * V7X_LANES]
    lane = lax.broadcasted_iota(jnp.int32, (tm, V7X_LANES), 1).astype(F32)
    lg = jnp.where(lane < N_EXPERTS, logits, NEG_INF)
    m1 = jnp.max(lg, axis=-1, keepdims=True)
    i1 = jnp.min(jnp.where(lg == m1, lane, float(V7X_LANES)), axis=-1, keepdims=True)
    lg2 = jnp.where(lane == i1, NEG_INF, lg)
    m2 = jnp.max(lg2, axis=-1, keepdims=True)
    i2 = jnp.min(jnp.where(lg2 == m2, lane, float(V7X_LANES)), axis=-1, keepdims=True)
    e2 = jnp.exp(m2 - m1)
    g1 = 1.0 / (1.0 + e2)
    g2 = e2 / (1.0 + e2)
    oh = ((lane == i1) | (lane == i2)).astype(BF16)
    r_i = lax.broadcasted_iota(jnp.int32, (tm, tm), 0)
    c_i = lax.broadcasted_iota(jnp.int32, (tm, tm), 1)
    tri = (c_i < r_i).astype(BF16)
    rank = jnp.dot(tri, oh, preferred_element_type=F32) + carry_ref[0:1, :]
    r1 = jnp.sum(jnp.where(lane == i1, rank, 0.0), axis=-1, keepdims=True)
    r2 = jnp.sum(jnp.where(lane == i2, rank, 0.0), axis=-1, keepdims=True)
    carry_ref[0:1, :] = carry_ref[0:1, :] + jnp.sum(oh.astype(F32), axis=0, keepdims=True)
    meta = jnp.where(lane == 0, i1.astype(F32), 0.0)
    meta = jnp.where(lane == 1, i2.astype(F32), meta)
    meta = jnp.where(lane == 2, g1, meta)
    meta = jnp.where(lane == 3, g2, meta)
    meta = jnp.where(lane == 4, r1, meta)
    meta = jnp.where(lane == 5, r2, meta)
    meta_ref[...] = meta
    for c in range(tm // V7X_LANES):
        blk = meta[c * V7X_LANES:(c + 1) * V7X_LANES, :].T
        metat_ref[:, c * V7X_LANES:(c + 1) * V7X_LANES] = blk[0:SUBLANES, :]
    cnt_ref[...] = jnp.broadcast_to(carry_ref[0:1, :], cnt_ref.shape)


def moe_route(x, g, w_router_p, tm=256):
    t, d = x.shape
    return pl.pallas_call(
        _moe_route_kernel,
        out_shape=(jax.ShapeDtypeStruct((t, d // 2), jnp.uint32), jax.ShapeDtypeStruct((t, V7X_LANES), F32),
                   jax.ShapeDtypeStruct((SUBLANES, t), F32), jax.ShapeDtypeStruct((8, V7X_LANES), F32)),
        grid=(t // tm,),
        in_specs=[pl.BlockSpec((tm, d), lambda i: (i, 0)), pl.BlockSpec((1, d), lambda i: (0, 0)),
                  pl.BlockSpec((d, V7X_LANES), lambda i: (0, 0))],
        out_specs=(pl.BlockSpec((tm, d // 2), lambda i: (i, 0)), pl.BlockSpec((tm, V7X_LANES), lambda i: (i, 0)),
                   pl.BlockSpec((SUBLANES, tm), lambda i: (0, i)), pl.BlockSpec((8, V7X_LANES), lambda i: (0, 0))),
        scratch_shapes=[pltpu.VMEM((8, V7X_LANES), F32)],
        compiler_params=_cparams("arbitrary"),
        name="moe_route",
    )(x, g.reshape(1, d), w_router_p)


def _moe_final_kernel(pos0_ref, pos1_ref, x_ref, meta_ref, g_ref, ys_ref, o_ref, ybuf, o_scr, sem, *, lsub):
    nres, tp, d = x_ref.shape
    dh = d // 2
    b = pl.program_id(0)
    j = pl.program_id(1)

    def copies(r, mm):
        tok = (b * nres + r) * lsub + j * tp + mm
        return [_row_copy(ys_ref, p[tok], ybuf.at[k], r * tp + mm, sem) for k, p in enumerate((pos0_ref, pos1_ref))]

    for r in range(nres):
        def issue(mm, c, r=r):
            for cp in copies(r, mm):
                cp.start()
            return c

        lax.fori_loop(0, tp, issue, 0, unroll=DMA_UNROLL)

    for r in range(nres):
        def drain(mm, c, r=r):
            for cp in copies(r, mm):
                cp.wait()
            return c

        lax.fori_loop(0, tp, drain, 0, unroll=DMA_UNROLL)
    for r in range(nres):
        rows = slice(r * tp, (r + 1) * tp)
        meta = meta_ref[r]
        g1 = meta[:, 2:3]
        g2 = meta[:, 3:4]
        a0, b0 = unpack_bf16_pair(ybuf[0, rows, :])
        a1, b1 = unpack_bf16_pair(ybuf[1, rows, :])
        x = x_ref[r]
        xa = x[:, 0:dh] + a0 * g1 + a1 * g2
        xb = x[:, dh:d] + b0 * g1 + b1 * g2
        ms = (jnp.sum(xa * xa, axis=-1, keepdims=True) + jnp.sum(xb * xb, axis=-1, keepdims=True)) * (1.0 / d)
        inv = lax.rsqrt(ms + EPS)
        ya = xa * inv * g_ref[:, 0:dh]
        yb = xb * inv * g_ref[:, dh:d]
        out_rows = pl.ds(r, tp, stride=nres)
        for s in range(dh // V7X_LANES):
            ls = slice(s * V7X_LANES, (s + 1) * V7X_LANES)
            o_scr[s, out_rows, :] = ya[:, ls]
            o_scr[dh // V7X_LANES + s, out_rows, :] = yb[:, ls]
    for s in range(d // V7X_LANES):
        o_ref[:, s * V7X_LANES:(s + 1) * V7X_LANES] = o_scr[s]


def moe_final(x, ys, pos0, pos1, meta, g, *, bsz, seq, tp=128):
    t, d = x.shape
    lsub = seq // RES
    tm = RES * tp
    xv = x.reshape(bsz * RES, lsub, d)
    mv = meta.reshape(bsz * RES, lsub, V7X_LANES)
    grid_spec = pltpu.PrefetchScalarGridSpec(
        num_scalar_prefetch=2,
        grid=(bsz, lsub // tp),
        in_specs=[pl.BlockSpec((RES, tp, d), lambda b, j, p0, p1: (b, j, 0)),
                  pl.BlockSpec((RES, tp, V7X_LANES), lambda b, j, p0, p1: (b, j, 0)),
                  pl.BlockSpec((1, d), lambda b, j, p0, p1: (0, 0)), pl.BlockSpec(memory_space=pl.ANY)],
        out_specs=pl.BlockSpec((tm, d), lambda b, j, p0, p1: (b * (lsub // tp) + j, 0)),
        scratch_shapes=[pltpu.VMEM((TOP_K, tm, d // 2), jnp.uint32), pltpu.VMEM((d // V7X_LANES, tm, V7X_LANES), F32),
                        pltpu.SemaphoreType.DMA],
    )
    return pl.pallas_call(
        functools.partial(_moe_final_kernel, lsub=lsub),
        out_shape=jax.ShapeDtypeStruct((t, d), F32),
        grid_spec=grid_spec,
        compiler_params=_cparams("arbitrary", "arbitrary"),
        name="moe_final",
    )(pos0, pos1, xv, mv, g.reshape(1, d), ys)


PROJ_TM = 2048


def _pad_lanes(v, fill=0.0):
    v = v.reshape(1, -1).astype(F32)
    return jnp.pad(v, ((0, 0), (0, V7X_LANES - v.shape[1])), constant_values=fill)


def even_layer(x, h, p, *, bsz, seq):
    t, d = x.shape
    d_ssd = d
    d_conf = d
    heads = d_ssd // SSD_HEAD_DIM
    conv_dim = d_ssd + 2 * SSD_GROUPS * SSD_STATE
    i1 = d_ssd + conv_dim
    i2 = i1 + heads
    w_in_t = jnp.swapaxes(p["w_in"], 0, 1)
    zx = matmul_nt(h, w_in_t, row_start=0, n_cols=i1, tn=1024, tm=PROJ_TM, name="in_proj_zx")
    dt_raw = matmul_nt(h, w_in_t, row_start=i1, n_cols=V7X_LANES, tn=V7X_LANES, out_dtype=F32, name="in_proj_dt")
    conf = matmul_nt(h, w_in_t, row_start=i2, n_cols=2 * d_conf, tn=1024, tm=PROJ_TM, name="in_proj_conf")
    u = conf_module(conf, p["conf_dw_w"], p["conf_dw_b"], p["conf_ln_w"], p["conf_ln_b"], bsz=bsz, seq=seq)
    dt, acs, dtt, acst = ssd_prep(dt_raw, _pad_lanes(p["dt_bias"]), _pad_lanes(p["a_log"]), bsz=bsz, seq=seq)
    dskip_e = jnp.repeat(p["d_skip"].astype(F32), SSD_HEAD_DIM).reshape(1, d_ssd)
    y_ssd = ssd_main(zx, p["conv_w"], p["conv_b"], dt, acs, dtt, acst, dskip_e, p["ssd_norm_w"],
                     bsz=bsz, seq=seq, d_ssd=d_ssd)
    return matmul([y_ssd, u], p["w_out"], n_cols=d, tn=512, res=x, out_dtype=F32, name="even_out_proj")


def dense_ffn(h, w1, w3, w2, gsz=1024):
    t = h.shape[0]
    ng = t // gsz
    ge = jnp.zeros((ng,), jnp.int32)
    gb = jnp.arange(ng, dtype=jnp.int32)
    gr = jnp.full((ng,), gsz, jnp.int32)
    return ffn_groups(h, w1[None], w3[None], w2[None], ge, gb, gr, gsz=gsz)


def odd_layer_attn(x, h, p, *, bsz, seq):
    t, d = x.shape
    hd = d // ATTN_HEADS
    qkv = matmul([h], p["w_qkv"], n_cols=3 * d, tn=1024, tm=PROJ_TM, name="qkv_proj")
    tabs = attn_bias_tables(p["rel_bias"])
    o = attention(qkv, tabs, bsz=bsz, seq=seq, hd=hd)
    return matmul([o], p["w_attn_out"], n_cols=d, tn=512, tm=2048, res=x, out_dtype=F32, name="attn_out_proj")


def moe_layer(x, norm_g, w_router, w1, w3, w2, final_g, *, bsz, seq, gsz=1280):
    t, d = x.shape
    ne = w1.shape[0]
    w_router_p = jnp.pad(w_router.astype(F32), ((0, 0), (0, V7X_LANES - ne)))
    h, meta, meta_t, cnt = moe_route(x, norm_g, w_router_p)
    counts = cnt[0, :ne].astype(jnp.int32)
    ngrp_e = (counts + gsz - 1) // gsz
    grp_end = jnp.cumsum(ngrp_e)
    grp_start = grp_end - ngrp_e
    ng = (t * TOP_K) // gsz + ne
    j = jnp.arange(ng, dtype=jnp.int32)
    total = grp_end[-1]
    last = jnp.maximum(total - 1, 0)
    jj = jnp.minimum(j, last)
    ge = jnp.searchsorted(grp_end, jj, side="right").astype(jnp.int32)
    ge = jnp.minimum(ge, ne - 1)
    rows = jnp.clip(counts[ge] - (jj - grp_start[ge]) * gsz, 0, gsz)
    gr = jnp.where(j < total, rows, 0).astype(jnp.int32)
    gb = j

    def dest_row(e, rank):
        off = jnp.zeros_like(rank)
        for k in range(ne):
            off = jnp.where(e == k, grp_start[k] * gsz, off)
        return off + rank

    pos0 = dest_row(meta_t[0].astype(jnp.int32), meta_t[4].astype(jnp.int32))
    pos1 = dest_row(meta_t[1].astype(jnp.int32), meta_t[5].astype(jnp.int32))
    xs = moe_dispatch(h, pos0, pos1, ng * gsz)
    ys = ffn_groups(xs, w1, w3, w2, ge, gb, gr, gsz=gsz, packed=True, ragged=True)
    return moe_final(x, ys, pos0, pos1, meta, final_g, bsz=bsz, seq=seq)


def kernel(x, norm_mix, norm_ffn, norm_final, even_w_in, ssd_conv_w, ssd_conv_b, ssd_dt_bias, ssd_a_log, ssd_d, ssd_norm_w, conf_dw_w, conf_dw_b, conf_ln_w, conf_ln_b, even_w_out, ffn_w1, ffn_w3, ffn_w2, attn_w_qkv, attn_w_out, rel_bias, moe_router, moe_w1, moe_w3, moe_w2):
    bsz, seq, d = x.shape
    assert norm_mix.shape[0] == 2, "two-layer trunk: one even (SSD+Conformer/FFN) and one odd (attention/MoE) layer"
    xf = x.reshape(bsz * seq, d)
    h = rmsnorm(xf, norm_mix[0])
    p_even = dict(w_in=even_w_in[0], conv_w=ssd_conv_w[0], conv_b=ssd_conv_b[0], dt_bias=ssd_dt_bias[0],
                  a_log=ssd_a_log[0], d_skip=ssd_d[0], ssd_norm_w=ssd_norm_w[0], conf_dw_w=conf_dw_w[0],
                  conf_dw_b=conf_dw_b[0], conf_ln_w=conf_ln_w[0], conf_ln_b=conf_ln_b[0], w_out=even_w_out[0])
    x1 = even_layer(xf, h, p_even, bsz=bsz, seq=seq)
    h1 = rmsnorm(x1, norm_ffn[0])
    y_ffn = dense_ffn(h1, ffn_w1[0], ffn_w3[0], ffn_w2[0])
    x2, h2 = add_norm_to_residue_major(x1, y_ffn, norm_mix[1], bsz=bsz, seq=seq, nres=RES)
    p_odd = dict(w_qkv=attn_w_qkv[0], w_attn_out=attn_w_out[0], rel_bias=rel_bias)
    x3 = odd_layer_attn(x2, h2, p_odd, bsz=bsz, seq=seq)
    out = moe_layer(x3, norm_ffn[1], moe_router[0], moe_w1[0], moe_w3[0], moe_w2[0], norm_final, bsz=bsz, seq=seq)
    return out.reshape(bsz, seq, d)
```

```python
import functools
import math

import numpy as np
import jax
import jax.numpy as jnp
from jax import lax
from jax.experimental import pallas as pl
from jax.experimental.pallas import tpu as pltpu

SSD_HEAD_DIM = 64
SSD_GROUPS = 4
SSD_STATE = 128
SSD_CONV = 4
SSD_CHUNK = 128
CONF_KERNEL = 31
ATTN_HEADS = 16
DILATED_PATTERNS = ((128, 1), (512, 4), (2048, 16))
ATTN_BLOCK = 128
REL_BUCKETS = 32
REL_MAX_DIST = 2048
N_EXPERTS = 8
TOP_K = 2
EPS = 1e-6

V7X_LANES = 128
V7X_VMEM_BYTES = 64 * 1024 * 1024
VMEM_LIMIT = 60 * 1024 * 1024

F32 = jnp.float32
BF16 = jnp.bfloat16
NEG_INF = float("-inf")


def _cparams(*sem):
    return pltpu.CompilerParams(dimension_semantics=tuple(sem), vmem_limit_bytes=VMEM_LIMIT)


def _rms(x, g):
    ms = jnp.mean(x * x, axis=-1, keepdims=True)
    return x * lax.rsqrt(ms + EPS) * g


def _rmsnorm_kernel(x_ref, g_ref, h_ref):
    h_ref[...] = _rms(x_ref[...], g_ref[...]).astype(h_ref.dtype)


def rmsnorm(x, g, out_dtype=BF16, tm=512):
    t, d = x.shape
    return pl.pallas_call(
        _rmsnorm_kernel,
        out_shape=jax.ShapeDtypeStruct((t, d), out_dtype),
        grid=(t // tm,),
        in_specs=[pl.BlockSpec((tm, d), lambda i: (i, 0)), pl.BlockSpec((1, d), lambda i: (0, 0))],
        out_specs=pl.BlockSpec((tm, d), lambda i: (i, 0)),
        compiler_params=_cparams("parallel"),
        name="rmsnorm",
    )(x, g.reshape(1, d))


def _add_norm_res_kernel(x_ref, y_ref, g_ref, xo_ref, ho_ref, xs_scr, hs_scr):
    nres, tp, d = xo_ref.shape
    xn = x_ref[...] + y_ref[...].astype(F32)
    h = _rms(xn, g_ref[...])
    for s in range(d // V7X_LANES):
        ls = slice(s * V7X_LANES, (s + 1) * V7X_LANES)
        xs_scr[s] = xn[:, ls]
        hs_scr[s] = h[:, ls]
    for r in range(nres):
        rows = pl.ds(r, tp, stride=nres)
        for s in range(d // V7X_LANES):
            ls = slice(s * V7X_LANES, (s + 1) * V7X_LANES)
            xo_ref[r, :, ls] = xs_scr[s, rows, :]
            ho_ref[r, :, ls] = hs_scr[s, rows, :].astype(ho_ref.dtype)


def add_norm_to_residue_major(x, y, g, *, bsz, seq, nres, tp=128):
    t, d = x.shape
    lsub = seq // nres
    tm = nres * tp
    nj = lsub // tp
    slab = pltpu.VMEM((d // V7X_LANES, tm, V7X_LANES), F32)
    xo, ho = pl.pallas_call(
        _add_norm_res_kernel,
        out_shape=(jax.ShapeDtypeStruct((bsz * nres, lsub, d), F32), jax.ShapeDtypeStruct((bsz * nres, lsub, d), BF16)),
        grid=(bsz, nj),
        in_specs=[pl.BlockSpec((tm, d), lambda b, j: (b * nj + j, 0)), pl.BlockSpec((tm, d), lambda b, j: (b * nj + j, 0)),
                  pl.BlockSpec((1, d), lambda b, j: (0, 0))],
        out_specs=(pl.BlockSpec((nres, tp, d), lambda b, j: (b, j, 0)), pl.BlockSpec((nres, tp, d), lambda b, j: (b, j, 0))),
        scratch_shapes=[slab, slab],
        compiler_params=_cparams("parallel", "parallel"),
        name="add_norm_reorder",
    )(x, y, g.reshape(1, d))
    return xo.reshape(t, d), ho.reshape(t, d)


def _mm_kernel(*refs, n_lhs, ks, has_res):
    xs = refs[:n_lhs]
    w_ref = refs[n_lhs]
    r_ref = refs[n_lhs + 1] if has_res else None
    o_ref = refs[-1]
    acc = None
    k0 = 0
    for x_ref, k in zip(xs, ks):
        part = jnp.dot(x_ref[...], w_ref[k0:k0 + k, :].astype(BF16), preferred_element_type=F32)
        acc = part if acc is None else acc + part
        k0 += k
    if has_res:
        acc = acc + r_ref[...]
    o_ref[...] = acc.astype(o_ref.dtype)


def matmul(xs, w, *, n_cols, col_block_off=0, tn, tm=1024, res=None, out_dtype=BF16, name="matmul"):
    t = xs[0].shape[0]
    ks = tuple(x.shape[1] for x in xs)
    ktot = sum(ks)
    assert w.shape[0] == ktot and n_cols % tn == 0 and t % tm == 0
    in_specs = [pl.BlockSpec((tm, k), lambda i, j: (i, 0)) for k in ks]
    in_specs.append(pl.BlockSpec((ktot, tn), lambda i, j: (0, j + col_block_off)))
    args = list(xs) + [w]
    if res is not None:
        in_specs.append(pl.BlockSpec((tm, tn), lambda i, j: (i, j)))
        args.append(res)
    return pl.pallas_call(
        functools.partial(_mm_kernel, n_lhs=len(xs), ks=ks, has_res=res is not None),
        out_shape=jax.ShapeDtypeStruct((t, n_cols), out_dtype),
        grid=(t // tm, n_cols // tn),
        in_specs=in_specs,
        out_specs=pl.BlockSpec((tm, tn), lambda i, j: (i, j)),
        compiler_params=_cparams("parallel", "arbitrary"),
        name=name,
    )(*args)


def _mm_nt_kernel(x_ref, wt_ref, o_ref):
    wt = wt_ref[...].astype(BF16)
    acc = lax.dot_general(x_ref[...], wt, (((1,), (1,)), ((), ())), preferred_element_type=F32)
    o_ref[...] = acc.astype(o_ref.dtype)


def matmul_nt(x, w_t, *, row_start, n_cols, tn, tm=1024, out_dtype=BF16, name="matmul_nt"):
    t, k = x.shape
    assert w_t.shape[1] == k and n_cols % tn == 0 and t % tm == 0
    if row_start % tn == 0:
        w_spec = pl.BlockSpec((tn, k), lambda i, j: (row_start // tn + j, 0))
    else:
        assert row_start % SUBLANES == 0 and tn % SUBLANES == 0
        w_spec = pl.BlockSpec((pl.Element(tn), pl.Element(k)),
                              lambda i, j: ((row_start // SUBLANES + j * (tn // SUBLANES)) * SUBLANES, 0))
    return pl.pallas_call(
        _mm_nt_kernel,
        out_shape=jax.ShapeDtypeStruct((t, n_cols), out_dtype),
        grid=(t // tm, n_cols // tn),
        in_specs=[pl.BlockSpec((tm, k), lambda i, j: (i, 0)), w_spec],
        out_specs=pl.BlockSpec((tm, tn), lambda i, j: (i, j)),
        compiler_params=_cparams("parallel", "arbitrary"),
        name=name,
    )(x, w_t)


HALO_BF16 = 16


def _ssd_conv_kernel(cur_ref, halo_ref, w_ref, b_ref, o_ref, buf_ref, *, ts):
    i = pl.program_id(1)
    halo = halo_ref[...].astype(F32)
    buf_ref[0:HALO_BF16, :] = jnp.where(i > 0, halo, 0.0)
    buf_ref[HALO_BF16:HALO_BF16 + ts, :] = cur_ref[...].astype(F32)
    acc = b_ref[...]
    for k in range(SSD_CONV):
        off = HALO_BF16 - (SSD_CONV - 1) + k
        acc = acc + w_ref[k:k + 1, :] * buf_ref[off:off + ts, :]
    o_ref[...] = (acc * jax.nn.sigmoid(acc)).astype(o_ref.dtype)


def ssd_conv(zx, conv_w, conv_b, *, bsz, seq, col_off, ts=1024, tc=512):
    c = conv_w.shape[1]
    t = zx.shape[0]
    nsb = seq // ts
    cb0 = col_off // tc
    hb = ts // HALO_BF16
    return pl.pallas_call(
        functools.partial(_ssd_conv_kernel, ts=ts),
        out_shape=jax.ShapeDtypeStruct((t, c), BF16),
        grid=(bsz, nsb, c // tc),
        in_specs=[
            pl.BlockSpec((ts, tc), lambda b, i, j: (b * nsb + i, cb0 + j)),
            pl.BlockSpec((HALO_BF16, tc), lambda b, i, j: (jnp.maximum((b * nsb + i) * hb - 1, 0), cb0 + j)),
            pl.BlockSpec((SSD_CONV, tc), lambda b, i, j: (0, j)),
            pl.BlockSpec((1, tc), lambda b, i, j: (0, j)),
        ],
        out_specs=pl.BlockSpec((ts, tc), lambda b, i, j: (b * nsb + i, j)),
        scratch_shapes=[pltpu.VMEM((ts + HALO_BF16, tc), F32)],
        compiler_params=_cparams("parallel", "parallel", "parallel"),
        name="ssd_conv",
    )(zx, zx, conv_w, conv_b.reshape(1, c))


def _ssd_prep_kernel(raw_ref, bias_ref, alog_ref, dt_ref, acs_ref, dtt_ref, acst_ref, *, seq):
    raw = raw_ref[...] + bias_ref[...]
    dt = jnp.maximum(raw, 0.0) + jnp.log1p(jnp.exp(-jnp.abs(raw)))
    a = -jnp.exp(alog_ref[...])
    x = dt * a
    row = lax.broadcasted_iota(jnp.int32, x.shape, 0) % SSD_CHUNK
    sh = 1
    while sh < SSD_CHUNK:
        x = x + jnp.where(row >= sh, pltpu.roll(x, sh, 0), 0.0)
        sh *= 2
    dt_ref[...] = dt
    acs_ref[...] = x
    for c in range(seq // SSD_CHUNK):
        sl = slice(c * SSD_CHUNK, (c + 1) * SSD_CHUNK)
        dtt_ref[sl, :] = dt[sl, :].T
        acst_ref[sl, :] = x[sl, :].T


def ssd_prep(dt_raw, dt_bias_p, a_log_p, *, bsz, seq):
    t = dt_raw.shape[0]
    shp = jax.ShapeDtypeStruct((t, V7X_LANES), F32)
    blk = pl.BlockSpec((seq, V7X_LANES), lambda b: (b, 0))
    vec = pl.BlockSpec((1, V7X_LANES), lambda b: (0, 0))
    return pl.pallas_call(
        functools.partial(_ssd_prep_kernel, seq=seq),
        out_shape=(shp, shp, shp, shp),
        grid=(bsz,),
        in_specs=[blk, vec, vec],
        out_specs=(blk, blk, blk, blk),
        compiler_params=_cparams("parallel"),
        name="ssd_prep",
    )(dt_raw, dt_bias_p, a_log_p)


def _conv_silu(cur_ref, halo_ref, w_ref, b_ref, buf_ref, col0, first):
    rows, c = cur_ref.shape
    cols = slice(col0, col0 + c)
    buf_ref[0:HALO_BF16, :] = jnp.where(first, 0.0, halo_ref[...].astype(F32))
    buf_ref[HALO_BF16:HALO_BF16 + rows, :] = cur_ref[...].astype(F32)
    acc = b_ref[:, cols]
    for k in range(SSD_CONV):
        off = HALO_BF16 - (SSD_CONV - 1) + k
        acc = acc + w_ref[k:k + 1, cols] * buf_ref[off:off + rows, :]
    return acc * jax.nn.sigmoid(acc)


def _ssd_main_kernel(xr_ref, xh_ref, br_ref, bh_ref, cr_ref, ch_ref, cw_ref, cbias_ref,
                     z_ref, dt_ref, acs_ref, dtt_ref, acst_ref, dsk_ref, nw_ref,
                     y_ref, st_ref, xbuf, bbuf, cbuf, *, heads_per_group):
    ci = pl.program_id(1)
    L = SSD_CHUNK
    gw = heads_per_group * SSD_HEAD_DIM
    npair = heads_per_group // 2
    d_ssd = xr_ref.shape[1]
    bc_w = br_ref.shape[1]

    @pl.when(ci == 0)
    def _():
        st_ref[...] = jnp.zeros_like(st_ref)

    first = ci == 0
    x_all = _conv_silu(xr_ref, xh_ref, cw_ref, cbias_ref, xbuf, 0, first)
    b_all = _conv_silu(br_ref, bh_ref, cw_ref, cbias_ref, bbuf, d_ssd, first)
    c_all = _conv_silu(cr_ref, ch_ref, cw_ref, cbias_ref, cbuf, d_ssd + bc_w, first)
    dt = dt_ref[...]
    acs = acs_ref[...]
    dtt = dtt_ref[...]
    acst = acst_ref[...]
    row = lax.broadcasted_iota(jnp.int32, (L, L), 0)
    col = lax.broadcasted_iota(jnp.int32, (L, L), 1)
    causal = col <= row
    lane_lo = lax.broadcasted_iota(jnp.int32, (L, 2 * SSD_HEAD_DIM), 1) < SSD_HEAD_DIM
    lane_lo1 = lax.broadcasted_iota(jnp.int32, (1, 2 * SSD_HEAD_DIM), 1) < SSD_HEAD_DIM
    zero_b = jnp.zeros((L, 2 * SSD_HEAD_DIM), BF16)

    for g in range(SSD_GROUPS):
        bmf = b_all[:, g * SSD_STATE:(g + 1) * SSD_STATE]
        bm = bmf.astype(BF16)
        cm = c_all[:, g * SSD_STATE:(g + 1) * SSD_STATE].astype(BF16)
        bmt = bmf.T.astype(BF16)
        cb = lax.dot_general(cm, bm, (((1,), (1,)), ((), ())), preferred_element_type=F32)
        xg = x_all[:, g * gw:(g + 1) * gw]
        y_diags, e_ls, w_ls, cds = [], [], [], []
        for q in range(npair):
            xp = xg[:, q * 2 * SSD_HEAD_DIM:(q + 1) * 2 * SSD_HEAD_DIM].astype(BF16)
            ms = []
            for hh in range(2):
                h = g * heads_per_group + 2 * q + hh
                seg = acs[:, h:h + 1] - acst[h:h + 1, :]
                dec = jnp.exp(jnp.where(causal, seg, NEG_INF))
                ms.append((cb * dec * dtt[h:h + 1, :]).astype(BF16))
            h0 = g * heads_per_group + 2 * q
            h1 = h0 + 1
            lhs = jnp.concatenate(ms, axis=1)
            rhs = jnp.concatenate([jnp.where(lane_lo, xp, zero_b), jnp.where(lane_lo, zero_b, xp)], axis=0)
            y_diags.append(jnp.dot(lhs, rhs, preferred_element_type=F32))
            last0 = acs[L - 1:L, h0:h0 + 1]
            last1 = acs[L - 1:L, h1:h1 + 1]
            e_ls.append(jnp.where(lane_lo, jnp.exp(acs[:, h0:h0 + 1]), jnp.exp(acs[:, h1:h1 + 1])))
            w_ls.append(jnp.where(lane_lo, dt[:, h0:h0 + 1] * jnp.exp(last0 - acs[:, h0:h0 + 1]),
                                  dt[:, h1:h1 + 1] * jnp.exp(last1 - acs[:, h1:h1 + 1])))
            cds.append(jnp.where(lane_lo1, jnp.exp(last0), jnp.exp(last1)))
        st = st_ref[g]
        y_off = jnp.dot(cm, st.astype(BF16), preferred_element_type=F32) * jnp.concatenate(e_ls, axis=1)
        xw = (xg * jnp.concatenate(w_ls, axis=1)).astype(BF16)
        st_ref[g] = st * jnp.concatenate(cds, axis=1) + jnp.dot(bmt, xw, preferred_element_type=F32)
        y = jnp.concatenate(y_diags, axis=1) + y_off + xg * dsk_ref[:, g * gw:(g + 1) * gw]
        zg = z_ref[:, g * gw:(g + 1) * gw].astype(F32)
        yg = y * (zg * jax.nn.sigmoid(zg))
        ms_ = jnp.mean(yg * yg, axis=-1, keepdims=True)
        y_ref[:, g * gw:(g + 1) * gw] = (yg * lax.rsqrt(ms_ + EPS) * nw_ref[:, g * gw:(g + 1) * gw]).astype(y_ref.dtype)


def ssd_main(zx, conv_w, conv_b, dt, acs, dtt, acst, dskip_e, norm_w, *, bsz, seq, d_ssd):
    t = zx.shape[0]
    nc = seq // SSD_CHUNK
    heads = d_ssd // SSD_HEAD_DIM
    hpg = heads // SSD_GROUPS
    bc_w = SSD_GROUPS * SSD_STATE
    conv_dim = conv_w.shape[1]
    assert d_ssd % bc_w == 0 and hpg % 2 == 0 and conv_dim == d_ssd + 2 * bc_w
    rowmap = lambda b, c: (b * nc + c, 0)
    hb = SSD_CHUNK // HALO_BF16

    def cur(width, col_block):
        return pl.BlockSpec((SSD_CHUNK, width), lambda b, c: (b * nc + c, col_block))

    def halo(width, col_block):
        return pl.BlockSpec((HALO_BF16, width), lambda b, c: (jnp.maximum((b * nc + c) * hb - 1, 0), col_block))

    xcol, bcol, ccol = 1, 2 * d_ssd // bc_w, 2 * d_ssd // bc_w + 1
    return pl.pallas_call(
        functools.partial(_ssd_main_kernel, heads_per_group=hpg),
        out_shape=jax.ShapeDtypeStruct((t, d_ssd), BF16),
        grid=(bsz, nc),
        in_specs=[
            cur(d_ssd, xcol), halo(d_ssd, xcol), cur(bc_w, bcol), halo(bc_w, bcol), cur(bc_w, ccol), halo(bc_w, ccol),
            pl.BlockSpec((SSD_CONV, conv_dim), lambda b, c: (0, 0)),
            pl.BlockSpec((1, conv_dim), lambda b, c: (0, 0)),
            pl.BlockSpec((SSD_CHUNK, d_ssd), rowmap),
            pl.BlockSpec((SSD_CHUNK, V7X_LANES), rowmap),
            pl.BlockSpec((SSD_CHUNK, V7X_LANES), rowmap),
            pl.BlockSpec((SSD_CHUNK, V7X_LANES), rowmap),
            pl.BlockSpec((SSD_CHUNK, V7X_LANES), rowmap),
            pl.BlockSpec((1, d_ssd), lambda b, c: (0, 0)),
            pl.BlockSpec((1, d_ssd), lambda b, c: (0, 0)),
        ],
        out_specs=pl.BlockSpec((SSD_CHUNK, d_ssd), rowmap),
        scratch_shapes=[pltpu.VMEM((SSD_GROUPS, SSD_STATE, hpg * SSD_HEAD_DIM), F32),
                        pltpu.VMEM((SSD_CHUNK + HALO_BF16, d_ssd), F32),
                        pltpu.VMEM((SSD_CHUNK + HALO_BF16, bc_w), F32),
                        pltpu.VMEM((SSD_CHUNK + HALO_BF16, bc_w), F32)],
        compiler_params=_cparams("parallel", "arbitrary"),
        name="ssd_main",
    )(zx, zx, zx, zx, zx, zx, conv_w, conv_b.reshape(1, conv_dim), zx, dt, acs, dtt, acst, dskip_e,
      norm_w.reshape(1, d_ssd))


CONF_HALO = 32


SUBLANES = 8


def _conf_kernel(a_ref, g_ref, ah_ref, gh_ref, w_ref, b_ref, lw_ref, lb_ref, o_ref, buf_ref, sh_ref, *, ts, rc, lt):
    i = pl.program_id(1)
    d = a_ref.shape[1]
    ah = ah_ref[...].astype(F32)
    gh = gh_ref[...].astype(F32)
    buf_ref[0:CONF_HALO, :] = jnp.where(i > 0, ah * jax.nn.sigmoid(gh), 0.0)
    a = a_ref[...].astype(F32)
    g = g_ref[...].astype(F32)
    buf_ref[CONF_HALO:CONF_HALO + ts, :] = a * jax.nn.sigmoid(g)
    span = ts + CONF_HALO - SUBLANES
    for j in range(1, SUBLANES):
        sh_ref[j - 1, 0:span, :] = buf_ref[j:j + span, :]
    base = CONF_HALO - (CONF_KERNEL - 1)
    outs = []
    for r in range(ts // rc):
        cols = []
        for c in range(d // lt):
            ls = slice(c * lt, (c + 1) * lt)
            acc = jnp.broadcast_to(b_ref[:, ls], (rc, lt))
            for k in range(CONF_KERNEL):
                j = (base + k) % SUBLANES
                off = r * rc + (base + k) - j
                src = buf_ref[off:off + rc, ls] if j == 0 else sh_ref[j - 1, off:off + rc, ls]
                acc = acc + w_ref[k:k + 1, ls] * src
            cols.append(acc)
        outs.append(jnp.concatenate(cols, axis=1))
    u = jnp.concatenate(outs, axis=0)
    mu = jnp.mean(u, axis=-1, keepdims=True)
    uc = u - mu
    var = jnp.mean(uc * uc, axis=-1, keepdims=True)
    y = uc * lax.rsqrt(var + EPS) * lw_ref[...] + lb_ref[...]
    o_ref[...] = (y * jax.nn.sigmoid(y)).astype(o_ref.dtype)


def conf_module(conf, dw_w, dw_b, ln_w, ln_b, *, bsz, seq, ts=256, rc=32, lt=512):
    t = conf.shape[0]
    d = dw_w.shape[1]
    nsb = seq // ts
    hb = ts // CONF_HALO
    cur = lambda col: pl.BlockSpec((ts, d), lambda b, i: (b * nsb + i, col))
    halo = lambda col: pl.BlockSpec((CONF_HALO, d), lambda b, i: (jnp.maximum((b * nsb + i) * hb - 1, 0), col))
    vec = pl.BlockSpec((1, d), lambda b, i: (0, 0))
    return pl.pallas_call(
        functools.partial(_conf_kernel, ts=ts, rc=rc, lt=lt),
        out_shape=jax.ShapeDtypeStruct((t, d), BF16),
        grid=(bsz, nsb),
        in_specs=[cur(0), cur(1), halo(0), halo(1), pl.BlockSpec((CONF_KERNEL, d), lambda b, i: (0, 0)),
                  vec, vec, vec],
        out_specs=pl.BlockSpec((ts, d), lambda b, i: (b * nsb + i, 0)),
        scratch_shapes=[pltpu.VMEM((ts + CONF_HALO, d), F32), pltpu.VMEM((SUBLANES - 1, ts + CONF_HALO, d), F32)],
        compiler_params=_cparams("parallel", "parallel"),
        name="conf_module",
    )(conf, conf, conf, conf, dw_w, dw_b.reshape(1, d), ln_w.reshape(1, d), ln_b.reshape(1, d))


FFN_SUB = 128


FFN_NCHUNK = 512


def pack_bf16_pair(a, b):
    ua = lax.bitcast_convert_type(a.astype(BF16).astype(F32), jnp.uint32)
    ub = lax.bitcast_convert_type(b.astype(BF16).astype(F32), jnp.uint32)
    return (ua >> 16) | (ub & jnp.uint32(0xFFFF0000))


def unpack_bf16_pair(u):
    a = lax.bitcast_convert_type(u << 16, F32)
    b = lax.bitcast_convert_type(u & jnp.uint32(0xFFFF0000), F32)
    return a, b


def _ffn_kernel(ge_ref, gb_ref, gr_ref, x_ref, w1_ref, w3_ref, w2_ref, o_ref, acc_ref, *xb_scr, nf, gsz, packed, ragged):
    g = pl.program_id(0)
    f = pl.program_id(1)
    rows = gr_ref[g]
    valid = rows > 0
    d = acc_ref.shape[1]

    @pl.when(valid & (f == 0))
    def _():
        acc_ref[...] = jnp.zeros_like(acc_ref)
        if packed:
            a, b = unpack_bf16_pair(x_ref[...])
            xb_scr[0][:, 0:d // 2] = a.astype(BF16)
            xb_scr[0][:, d // 2:d] = b.astype(BF16)

    xsrc = xb_scr[0] if packed else x_ref

    def slab(r0, nrows, w1, w3, w2):
        xs = xsrc[pl.ds(r0, nrows), :]
        h1 = jnp.dot(xs, w1, preferred_element_type=F32)
        h3 = jnp.dot(xs, w3, preferred_element_type=F32)
        hh = (h1 * jax.nn.sigmoid(h1) * h3).astype(BF16)
        for c in range(d // FFN_NCHUNK):
            cs = slice(c * FFN_NCHUNK, (c + 1) * FFN_NCHUNK)
            acc_ref[pl.ds(r0, nrows), cs] += jnp.dot(hh, w2[:, cs], preferred_element_type=F32)

    nslab = (rows + FFN_SUB - 1) // FFN_SUB
    for ns in (range(1, gsz // FFN_SUB + 1) if ragged else (gsz // FFN_SUB,)):
        @pl.when(nslab == ns)
        def _(ns=ns):
            slab(0, ns * FFN_SUB, w1_ref[...].astype(BF16), w3_ref[...].astype(BF16), w2_ref[...].astype(BF16))

    @pl.when(jnp.logical_not(valid) & (f == 0))
    def _():
        o_ref[...] = jnp.zeros_like(o_ref)

    @pl.when(valid & (f == nf - 1))
    def _():
        if packed:
            o_ref[...] = pack_bf16_pair(acc_ref[:, 0:d // 2], acc_ref[:, d // 2:d])
        else:
            o_ref[...] = acc_ref[...].astype(o_ref.dtype)


def ffn_groups(xs, w1, w3, w2, grp_expert, grp_block, grp_rows, *, gsz=1024, tf=256, packed=False, ragged=False,
               single_buffer_rows=False):
    e, d, ff = w1.shape
    r, dx = xs.shape
    nf = ff // tf
    ng = grp_expert.shape[0]

    def fmap(f, g, gr):
        return jnp.where(gr[g] > 0, f, nf - 1)

    scratch = [pltpu.VMEM((gsz, d), F32)]
    if packed:
        scratch.append(pltpu.VMEM((gsz, d), BF16))
    row_mode = dict(pipeline_mode=pl.Buffered(1)) if single_buffer_rows else {}
    grid_spec = pltpu.PrefetchScalarGridSpec(
        num_scalar_prefetch=3,
        grid=(ng, nf),
        in_specs=[
            pl.BlockSpec((gsz, dx), lambda g, f, ge, gb, gr: (gb[g], 0), **row_mode),
            pl.BlockSpec((None, d, tf), lambda g, f, ge, gb, gr: (ge[g], 0, fmap(f, g, gr))),
            pl.BlockSpec((None, d, tf), lambda g, f, ge, gb, gr: (ge[g], 0, fmap(f, g, gr))),
            pl.BlockSpec((None, tf, d), lambda g, f, ge, gb, gr: (ge[g], fmap(f, g, gr), 0)),
        ],
        out_specs=pl.BlockSpec((gsz, dx), lambda g, f, ge, gb, gr: (gb[g], 0), **row_mode),
        scratch_shapes=scratch,
    )
    return pl.pallas_call(
        functools.partial(_ffn_kernel, nf=nf, gsz=gsz, packed=packed, ragged=ragged),
        out_shape=jax.ShapeDtypeStruct((r, dx), xs.dtype),
        grid_spec=grid_spec,
        compiler_params=_cparams("arbitrary", "arbitrary"),
        name="ffn_groups",
    )(grp_expert, grp_block, grp_rows, xs, w1, w3, w2)


def _row_copy(src_ref, src_row, dst_ref, dst_row, sem):
    return pltpu.make_async_copy(src_ref.at[pl.ds(src_row, 1), :], dst_ref.at[pl.ds(dst_row, 1), :], sem)


DMA_UNROLL = 8


def _dispatch_kernel(pos0_ref, pos1_ref, h_ref, xs_in_ref, xs_ref, sem):
    del xs_in_ref
    tm = h_ref.shape[0]
    base = pl.program_id(0) * tm

    def copies(i):
        return [_row_copy(h_ref, i, xs_ref, p[base + i], sem) for p in (pos0_ref, pos1_ref)]

    def issue(i, c):
        for cp in copies(i):
            cp.start()
        return c

    lax.fori_loop(0, tm, issue, 0, unroll=DMA_UNROLL)

    def drain(i, c):
        for cp in copies(i):
            cp.wait()
        return c

    lax.fori_loop(0, tm, drain, 0, unroll=DMA_UNROLL)


def moe_dispatch(hp, pos0, pos1, n_rows, tm=512):
    t, w = hp.shape
    xs0 = jnp.zeros((n_rows, w), hp.dtype)
    grid_spec = pltpu.PrefetchScalarGridSpec(
        num_scalar_prefetch=2,
        grid=(t // tm,),
        in_specs=[pl.BlockSpec((tm, w), lambda i, p0, p1: (i, 0)), pl.BlockSpec(memory_space=pl.ANY)],
        out_specs=pl.BlockSpec(memory_space=pl.ANY),
        scratch_shapes=[pltpu.SemaphoreType.DMA],
    )
    return pl.pallas_call(
        _dispatch_kernel,
        out_shape=jax.ShapeDtypeStruct((n_rows, w), hp.dtype),
        grid_spec=grid_spec,
        input_output_aliases={3: 0},
        compiler_params=_cparams("arbitrary"),
        name="moe_dispatch",
    )(pos0, pos1, hp, xs0)


def _t5_bucket_np(dist):
    max_exact = REL_BUCKETS // 2
    d_f = np.maximum(dist, 1).astype(np.float32)
    large = max_exact + (np.log(d_f / np.float32(max_exact)) / np.float32(math.log(REL_MAX_DIST / max_exact))
                         * np.float32(REL_BUCKETS - max_exact)).astype(np.int32)
    large = np.minimum(large, REL_BUCKETS - 1)
    return np.where(dist < max_exact, dist, large)


RES = 4
TQ = ATTN_BLOCK
PIECE = TQ // RES


def _bucket_index_tables():
    assert DILATED_PATTERNS == ((128, 1), (512, 4), (2048, 16)) and RES == 4 and TQ == 128

    def fin(steps, dil, n_back=TQ):
        ok = (steps >= 0) & (steps <= n_back)
        return np.where(ok, _t5_bucket_np(np.clip(steps, 0, n_back) * dil), -1).astype(np.int32)

    i = np.arange(TQ)[:, None]
    t4 = fin(i + TQ - np.arange(2 * TQ)[None, :], 4)
    t16 = fin(i - np.arange(TQ)[None, :], 16)
    rq, mq = i // PIECE, i % PIECE
    jk = np.arange(2 * TQ)[None, :]
    rk, mk = jk // (2 * PIECE), jk % (2 * PIECE)
    t1 = fin(RES * (PIECE + mq - mk) + (rq - rk), 1)
    t1first = fin(RES * (mq - mk) + (rq - rk), 1)
    return t4, t1, t1first, t16


def _bias_tab_kernel(rel_ref, *refs):
    n = len(refs) // 2
    h = pl.program_id(0)
    for idx_ref, o_ref in zip(refs[:n], refs[n:]):
        idx = idx_ref[...]
        tab = jnp.full(idx.shape, NEG_INF, F32)
        for b in range(REL_BUCKETS):
            tab = jnp.where(idx == b, rel_ref[b, h], tab)
        o_ref[0] = tab


def attn_bias_tables(rel_bias):
    idx_tabs = [jnp.asarray(t) for t in _bucket_index_tables()]
    nh = rel_bias.shape[1]
    return pl.pallas_call(
        _bias_tab_kernel,
        out_shape=tuple(jax.ShapeDtypeStruct((nh,) + t.shape, F32) for t in idx_tabs),
        grid=(nh,),
        in_specs=[pl.BlockSpec(memory_space=pltpu.SMEM)] + [pl.BlockSpec(t.shape, lambda h: (0, 0)) for t in idx_tabs],
        out_specs=tuple(pl.BlockSpec((1,) + t.shape, lambda h: (h, 0, 0)) for t in idx_tabs),
        compiler_params=_cparams("parallel"),
        name="attn_bias_tables",
    )(rel_bias.astype(F32), *idx_tabs)


def _attn_kernel(q_ref, k_ref, v_ref, t4_ref, t1_ref, t1f_ref, t16_ref, o_ref, acc, m_s, l_s, qs, ks, vs, od, md, ld,
                 *, scale):
    nres, lsub, hd = q_ref.shape
    nt = (((1,), (1,)), ((), ()))

    def tile(q, k, v, bias):
        s = lax.dot_general(q, k, nt, preferred_element_type=F32) * scale + bias
        m = jnp.max(s, axis=-1, keepdims=True)
        p = jnp.exp(s - m)
        l = jnp.sum(p, axis=-1, keepdims=True)
        o = jnp.dot(p.astype(BF16), v, preferred_element_type=F32)
        return o, jnp.broadcast_to(m, o.shape), jnp.broadcast_to(l, o.shape)

    def merge(r, rows, o, m, l):
        m0 = m_s[r, rows, :]
        mn = jnp.maximum(m0, m)
        a = jnp.exp(m0 - mn)
        b = jnp.exp(m - mn)
        acc[r, rows, :] = acc[r, rows, :] * a + o * b
        l_s[r, rows, :] = l_s[r, rows, :] * a + l * b
        m_s[r, rows, :] = mn

    nsub = lsub // TQ
    for r in range(nres):
        qs[r] = q_ref[r].astype(F32)
        ks[r] = k_ref[r].astype(F32)
        vs[r] = v_ref[r].astype(F32)
    masked = jnp.full((TQ, TQ), NEG_INF, F32)
    bias16 = (jnp.concatenate([t16_ref[0], masked], axis=1), jnp.concatenate([masked, t16_ref[0]], axis=1))
    for r in range(nres):
        for pair in range(nsub // 2):
            sub = [pl.ds(2 * pair + e, TQ, stride=nsub) for e in range(2)]
            kk = jnp.concatenate([ks[r, s, :] for s in sub], axis=0).astype(BF16)
            vv = jnp.concatenate([vs[r, s, :] for s in sub], axis=0).astype(BF16)
            for e in range(2):
                o, m, l = tile(qs[r, sub[e], :].astype(BF16), kk, vv, bias16[e])
                od[r * nsub + 2 * pair + e] = o
                md[r * nsub + 2 * pair + e] = m
                ld[r * nsub + 2 * pair + e] = l

    for r in range(nres):
        for i in range(lsub // TQ):
            rows = pl.ds(i * TQ, TQ)
            if i == 0:
                krows = pl.ds(0, 2 * TQ)
                bias = jnp.concatenate([t4_ref[0, :, TQ:2 * TQ], masked], axis=1)
            else:
                krows = pl.ds((i - 1) * TQ, 2 * TQ)
                bias = t4_ref[0]
            o, m, l = tile(q_ref[r, rows, :], k_ref[r, krows, :], v_ref[r, krows, :], bias)
            acc[r, rows, :] = o
            m_s[r, rows, :] = m
            l_s[r, rows, :] = l

    for j in range(lsub // PIECE):
        q = jnp.concatenate([q_ref[r, pl.ds(j * PIECE, PIECE), :] for r in range(nres)], axis=0)
        if j == 0:
            krows, bias = pl.ds(0, 2 * PIECE), t1f_ref[0]
        else:
            krows, bias = pl.ds((j - 1) * PIECE, 2 * PIECE), t1_ref[0]
        k = jnp.concatenate([k_ref[r, krows, :] for r in range(nres)], axis=0)
        v = jnp.concatenate([v_ref[r, krows, :] for r in range(nres)], axis=0)
        o, m, l = tile(q, k, v, bias)
        for r in range(nres):
            part = slice(r * PIECE, (r + 1) * PIECE)
            merge(r, pl.ds(j * PIECE, PIECE), o[part], m[part], l[part])

    for r in range(nres):
        for c in range(nsub):
            merge(r, pl.ds(c, TQ, stride=nsub), od[r * nsub + c], md[r * nsub + c], ld[r * nsub + c])

    for r in range(nres):
        o_ref[r] = (acc[r] / l_s[r]).astype(o_ref.dtype)


def attention(qkv, tabs, *, bsz, seq, hd):
    dm = ATTN_HEADS * hd
    lsub = seq // RES
    assert hd == V7X_LANES and lsub // TQ == 16 // RES and seq // 16 == TQ
    view = qkv.reshape(bsz * RES, lsub, 3 * dm)
    blk = lambda which: pl.BlockSpec((RES, lsub, hd), lambda b, h: (b, 0, which * ATTN_HEADS + h))
    tab = lambda t: pl.BlockSpec((1,) + t.shape[1:], lambda b, h: (h, 0, 0))
    st = pltpu.VMEM((RES, lsub, hd), F32)
    sub_f = pltpu.VMEM((RES * lsub // TQ, TQ, hd), F32)
    o = pl.pallas_call(
        functools.partial(_attn_kernel, scale=hd ** -0.5),
        out_shape=jax.ShapeDtypeStruct((bsz * RES, lsub, dm), BF16),
        grid=(bsz, ATTN_HEADS),
        in_specs=[blk(0), blk(1), blk(2)] + [tab(t) for t in tabs],
        out_specs=pl.BlockSpec((RES, lsub, hd), lambda b, h: (b, 0, h)),
        scratch_shapes=[st, st, st, st, st, st, sub_f, sub_f, sub_f],
        compiler_params=_cparams("parallel", "arbitrary"),
        name="attention",
    )(view, view, view, *tabs)
    return o.reshape(bsz * seq, dm)


def _split_bf16(x):
    hi = x.astype(BF16)
    lo = (x - hi.astype(F32)).astype(BF16)
    return hi, lo


def _moe_route_kernel(x_ref, g_ref, wr_ref, h_ref, meta_ref, metat_ref, cnt_ref, carry_ref):
    step = pl.program_id(0)
    tm = x_ref.shape[0]

    @pl.when(step == 0)
    def _():
        carry_ref[...] = jnp.zeros_like(carry_ref)

    h = _rms(x_ref[...], g_ref[...])
    h_hi, h_lo = _split_bf16(h)
    dh = h.shape[1] // 2
    h_ref[...] = pack_bf16_pair(h[:, 0:dh], h[:, dh:2 * dh])
    w_hi, w_lo = _split_bf16(wr_ref[...])
    both = jnp.dot(h_hi, jnp.concatenate([w_hi, w_lo], axis=1), preferred_element_type=F32)
    logits = both[:, 0:V7X_LANES] + jnp.dot(h_lo, w_hi, preferred_element_type=F32) + both[:, V7X_LANES:2 * V7X_LANES]
    lane = lax.broadcasted_iota(jnp.int32, (tm, V7X_LANES), 1).astype(F32)
    lg = jnp.where(lane < N_EXPERTS, logits, NEG_INF)
    m1 = jnp.max(lg, axis=-1, keepdims=True)
    i1 = jnp.min(jnp.where(lg == m1, lane, float(V7X_LANES)), axis=-1, keepdims=True)
    lg2 = jnp.where(lane == i1, NEG_INF, lg)
    m2 = jnp.max(lg2, axis=-1, keepdims=True)
    i2 = jnp.min(jnp.where(lg2 == m2, lane, float(V7X_LANES)), axis=-1, keepdims=True)
    e2 = jnp.exp(m2 - m1)
    g1 = 1.0 / (1.0 + e2)
    g2 = e2 / (1.0 + e2)
    oh = ((lane == i1) | (lane == i2)).astype(BF16)
    r_i = lax.broadcasted_iota(jnp.int32, (tm, tm), 0)
    c_i = lax.broadcasted_iota(jnp.int32, (tm, tm), 1)
    tri = (c_i < r_i).astype(BF16)
    rank = jnp.dot(tri, oh, preferred_element_type=F32) + carry_ref[0:1, :]
    r1 = jnp.sum(jnp.where(lane == i1, rank, 0.0), axis=-1, keepdims=True)
    r2 = jnp.sum(jnp.where(lane == i2, rank, 0.0), axis=-1, keepdims=True)
    carry_ref[0:1, :] = carry_ref[0:1, :] + jnp.sum(oh.astype(F32), axis=0, keepdims=True)
    meta = jnp.where(lane == 0, i1.astype(F32), 0.0)
    meta = jnp.where(lane == 1, i2.astype(F32), meta)
    meta = jnp.where(lane == 2, g1, meta)
    meta = jnp.where(lane == 3, g2, meta)
    meta = jnp.where(lane == 4, r1, meta)
    meta = jnp.where(lane == 5, r2, meta)
    meta_ref[...] = meta
    for c in range(tm // V7X_LANES):
        blk = meta[c * V7X_LANES:(c + 1) * V7X_LANES, :].T
        metat_ref[:, c * V7X_LANES:(c + 1) * V7X_LANES] = blk[0:SUBLANES, :]
    cnt_ref[...] = jnp.broadcast_to(carry_ref[0:1, :], cnt_ref.shape)


def moe_route(x, g, w_router_p, tm=256):
    t, d = x.shape
    return pl.pallas_call(
        _moe_route_kernel,
        out_shape=(jax.ShapeDtypeStruct((t, d // 2), jnp.uint32), jax.ShapeDtypeStruct((t, V7X_LANES), F32),
                   jax.ShapeDtypeStruct((SUBLANES, t), F32), jax.ShapeDtypeStruct((8, V7X_LANES), F32)),
        grid=(t // tm,),
        in_specs=[pl.BlockSpec((tm, d), lambda i: (i, 0)), pl.BlockSpec((1, d), lambda i: (0, 0)),
                  pl.BlockSpec((d, V7X_LANES), lambda i: (0, 0))],
        out_specs=(pl.BlockSpec((tm, d // 2), lambda i: (i, 0)), pl.BlockSpec((tm, V7X_LANES), lambda i: (i, 0)),
                   pl.BlockSpec((SUBLANES, tm), lambda i: (0, i)), pl.BlockSpec((8, V7X_LANES), lambda i: (0, 0))),
        scratch_shapes=[pltpu.VMEM((8, V7X_LANES), F32)],
        compiler_params=_cparams("arbitrary"),
        name="moe_route",
    )(x, g.reshape(1, d), w_router_p)


def _moe_final_kernel(pos0_ref, pos1_ref, x_ref, meta_ref, g_ref, ys_ref, o_ref, ybuf, o_scr, sem, *, lsub):
    nres, tp, d = x_ref.shape
    dh = d // 2
    b = pl.program_id(0)
    j = pl.program_id(1)

    def copies(r, mm):
        tok = (b * nres + r) * lsub + j * tp + mm
        return [_row_copy(ys_ref, p[tok], ybuf.at[k], r * tp + mm, sem) for k, p in enumerate((pos0_ref, pos1_ref))]

    for r in range(nres):
        def issue(mm, c, r=r):
            for cp in copies(r, mm):
                cp.start()
            return c

        lax.fori_loop(0, tp, issue, 0, unroll=DMA_UNROLL)

    for r in range(nres):
        def drain(mm, c, r=r):
            for cp in copies(r, mm):
                cp.wait()
            return c

        lax.fori_loop(0, tp, drain, 0, unroll=DMA_UNROLL)
    for r in range(nres):
        rows = slice(r * tp, (r + 1) * tp)
        meta = meta_ref[r]
        g1 = meta[:, 2:3]
        g2 = meta[:, 3:4]
        a0, b0 = unpack_bf16_pair(ybuf[0, rows, :])
        a1, b1 = unpack_bf16_pair(ybuf[1, rows, :])
        x = x_ref[r]
        xa = x[:, 0:dh] + a0 * g1 + a1 * g2
        xb = x[:, dh:d] + b0 * g1 + b1 * g2
        ms = (jnp.sum(xa * xa, axis=-1, keepdims=True) + jnp.sum(xb * xb, axis=-1, keepdims=True)) * (1.0 / d)
        inv = lax.rsqrt(ms + EPS)
        ya = xa * inv * g_ref[:, 0:dh]
        yb = xb * inv * g_ref[:, dh:d]
        out_rows = pl.ds(r, tp, stride=nres)
        for s in range(dh // V7X_LANES):
            ls = slice(s * V7X_LANES, (s + 1) * V7X_LANES)
            o_scr[s, out_rows, :] = ya[:, ls]
            o_scr[dh // V7X_LANES + s, out_rows, :] = yb[:, ls]
    for s in range(d // V7X_LANES):
        o_ref[:, s * V7X_LANES:(s + 1) * V7X_LANES] = o_scr[s]


def moe_final(x, ys, pos0, pos1, meta, g, *, bsz, seq, tp=128):
    t, d = x.shape
    lsub = seq // RES
    tm = RES * tp
    xv = x.reshape(bsz * RES, lsub, d)
    mv = meta.reshape(bsz * RES, lsub, V7X_LANES)
    grid_spec = pltpu.PrefetchScalarGridSpec(
        num_scalar_prefetch=2,
        grid=(bsz, lsub // tp),
        in_specs=[pl.BlockSpec((RES, tp, d), lambda b, j, p0, p1: (b, j, 0)),
                  pl.BlockSpec((RES, tp, V7X_LANES), lambda b, j, p0, p1: (b, j, 0)),
                  pl.BlockSpec((1, d), lambda b, j, p0, p1: (0, 0)), pl.BlockSpec(memory_space=pl.ANY)],
        out_specs=pl.BlockSpec((tm, d), lambda b, j, p0, p1: (b * (lsub // tp) + j, 0)),
        scratch_shapes=[pltpu.VMEM((TOP_K, tm, d // 2), jnp.uint32), pltpu.VMEM((d // V7X_LANES, tm, V7X_LANES), F32),
                        pltpu.SemaphoreType.DMA],
    )
    return pl.pallas_call(
        functools.partial(_moe_final_kernel, lsub=lsub),
        out_shape=jax.ShapeDtypeStruct((t, d), F32),
        grid_spec=grid_spec,
        compiler_params=_cparams("arbitrary", "arbitrary"),
        name="moe_final",
    )(pos0, pos1, xv, mv, g.reshape(1, d), ys)


PROJ_TM = 2048


def _pad_lanes(v, fill=0.0):
    v = v.reshape(1, -1).astype(F32)
    return jnp.pad(v, ((0, 0), (0, V7X_LANES - v.shape[1])), constant_values=fill)


def even_layer(x, h, p, *, bsz, seq):
    t, d = x.shape
    d_ssd = d
    d_conf = d
    heads = d_ssd // SSD_HEAD_DIM
    conv_dim = d_ssd + 2 * SSD_GROUPS * SSD_STATE
    i1 = d_ssd + conv_dim
    i2 = i1 + heads
    w_in_t = jnp.swapaxes(p["w_in"], 0, 1)
    zx = matmul_nt(h, w_in_t, row_start=0, n_cols=i1, tn=1024, tm=PROJ_TM, name="in_proj_zx")
    dt_raw = matmul_nt(h, w_in_t, row_start=i1, n_cols=V7X_LANES, tn=V7X_LANES, out_dtype=F32, name="in_proj_dt")
    conf = matmul_nt(h, w_in_t, row_start=i2, n_cols=2 * d_conf, tn=1024, tm=PROJ_TM, name="in_proj_conf")
    u = conf_module(conf, p["conf_dw_w"], p["conf_dw_b"], p["conf_ln_w"], p["conf_ln_b"], bsz=bsz, seq=seq)
    dt, acs, dtt, acst = ssd_prep(dt_raw, _pad_lanes(p["dt_bias"]), _pad_lanes(p["a_log"]), bsz=bsz, seq=seq)
    dskip_e = jnp.repeat(p["d_skip"].astype(F32), SSD_HEAD_DIM).reshape(1, d_ssd)
    y_ssd = ssd_main(zx, p["conv_w"], p["conv_b"], dt, acs, dtt, acst, dskip_e, p["ssd_norm_w"],
                     bsz=bsz, seq=seq, d_ssd=d_ssd)
    return matmul([y_ssd, u], p["w_out"], n_cols=d, tn=256, tm=PROJ_TM, res=x, out_dtype=F32, name="even_out_proj")


def dense_ffn(h, w1, w3, w2, gsz=1024):
    t = h.shape[0]
    ng = t // gsz
    ge = jnp.zeros((ng,), jnp.int32)
    gb = jnp.arange(ng, dtype=jnp.int32)
    gr = jnp.full((ng,), gsz, jnp.int32)
    return ffn_groups(h, w1[None], w3[None], w2[None], ge, gb, gr, gsz=gsz)


def odd_layer_attn(x, h, p, *, bsz, seq):
    t, d = x.shape
    hd = d // ATTN_HEADS
    qkv = matmul([h], p["w_qkv"], n_cols=3 * d, tn=1024, tm=PROJ_TM, name="qkv_proj")
    tabs = attn_bias_tables(p["rel_bias"])
    o = attention(qkv, tabs, bsz=bsz, seq=seq, hd=hd)
    return matmul([o], p["w_attn_out"], n_cols=d, tn=512, tm=2048, res=x, out_dtype=F32, name="attn_out_proj")


def moe_layer(x, norm_g, w_router, w1, w3, w2, final_g, *, bsz, seq, gsz=1280):
    t, d = x.shape
    ne = w1.shape[0]
    w_router_p = jnp.pad(w_router.astype(F32), ((0, 0), (0, V7X_LANES - ne)))
    h, meta, meta_t, cnt = moe_route(x, norm_g, w_router_p)
    counts = cnt[0, :ne].astype(jnp.int32)
    ngrp_e = (counts + gsz - 1) // gsz
    grp_end = jnp.cumsum(ngrp_e)
    grp_start = grp_end - ngrp_e
    ng = (t * TOP_K) // gsz + ne
    j = jnp.arange(ng, dtype=jnp.int32)
    total = grp_end[-1]
    last = jnp.maximum(total - 1, 0)
    jj = jnp.minimum(j, last)
    ge = jnp.searchsorted(grp_end, jj, side="right").astype(jnp.int32)
    ge = jnp.minimum(ge, ne - 1)
    rows = jnp.clip(counts[ge] - (jj - grp_start[ge]) * gsz, 0, gsz)
    gr = jnp.where(j < total, rows, 0).astype(jnp.int32)
    gb = j

    def dest_row(e, rank):
        off = jnp.zeros_like(rank)
        for k in range(ne):
            off = jnp.where(e == k, grp_start[k] * gsz, off)
        return off + rank

    pos0 = dest_row(meta_t[0].astype(jnp.int32), meta_t[4].astype(jnp.int32))
    pos1 = dest_row(meta_t[1].astype(jnp.int32), meta_t[5].astype(jnp.int32))
    xs = moe_dispatch(h, pos0, pos1, ng * gsz)
    ys = ffn_groups(xs, w1, w3, w2, ge, gb, gr, gsz=gsz, packed=True, ragged=True)
    return moe_final(x, ys, pos0, pos1, meta, final_g, bsz=bsz, seq=seq)


def kernel(x, norm_mix, norm_ffn, norm_final, even_w_in, ssd_conv_w, ssd_conv_b, ssd_dt_bias, ssd_a_log, ssd_d, ssd_norm_w, conf_dw_w, conf_dw_b, conf_ln_w, conf_ln_b, even_w_out, ffn_w1, ffn_w3, ffn_w2, attn_w_qkv, attn_w_out, rel_bias, moe_router, moe_w1, moe_w3, moe_w2):
    bsz, seq, d = x.shape
    assert norm_mix.shape[0] == 2, "two-layer trunk: one even (SSD+Conformer/FFN) and one odd (attention/MoE) layer"
    xf = x.reshape(bsz * seq, d)
    h = rmsnorm(xf, norm_mix[0])
    p_even = dict(w_in=even_w_in[0], conv_w=ssd_conv_w[0], conv_b=ssd_conv_b[0], dt_bias=ssd_dt_bias[0],
                  a_log=ssd_a_log[0], d_skip=ssd_d[0], ssd_norm_w=ssd_norm_w[0], conf_dw_w=conf_dw_w[0],
                  conf_dw_b=conf_dw_b[0], conf_ln_w=conf_ln_w[0], conf_ln_b=conf_ln_b[0], w_out=even_w_out[0])
    x1 = even_layer(xf, h, p_even, bsz=bsz, seq=seq)
    h1 = rmsnorm(x1, norm_ffn[0])
    y_ffn = dense_ffn(h1, ffn_w1[0], ffn_w3[0], ffn_w2[0])
    x2, h2 = add_norm_to_residue_major(x1, y_ffn, norm_mix[1], bsz=bsz, seq=seq, nres=RES)
    p_odd = dict(w_qkv=attn_w_qkv[0], w_attn_out=attn_w_out[0], rel_bias=rel_bias)
    x3 = odd_layer_attn(x2, h2, p_odd, bsz=bsz, seq=seq)
    out = moe_layer(x3, norm_ffn[1], moe_router[0], moe_w1[0], moe_w3[0], moe_w2[0], norm_final, bsz=bsz, seq=seq)
    return out.reshape(bsz, seq, d)
```

```python
import functools
import math

import numpy as np
import jax
import jax.numpy as jnp
from jax import lax
from jax.experimental import pallas as pl
from jax.experimental.pallas import tpu as pltpu

SSD_HEAD_DIM = 64
SSD_GROUPS = 4
SSD_STATE = 128
SSD_CONV = 4
SSD_CHUNK = 128
CONF_KERNEL = 31
ATTN_HEADS = 16
DILATED_PATTERNS = ((128, 1), (512, 4), (2048, 16))
ATTN_BLOCK = 128
REL_BUCKETS = 32
REL_MAX_DIST = 2048
N_EXPERTS = 8
TOP_K = 2
EPS = 1e-6

V7X_LANES = 128
V7X_VMEM_BYTES = 64 * 1024 * 1024
VMEM_LIMIT = 60 * 1024 * 1024

F32 = jnp.float32
BF16 = jnp.bfloat16
NEG_INF = float("-inf")


def _cparams(*sem):
    return pltpu.CompilerParams(dimension_semantics=tuple(sem), vmem_limit_bytes=VMEM_LIMIT)


def _rms(x, g):
    ms = jnp.mean(x * x, axis=-1, keepdims=True)
    return x * lax.rsqrt(ms + EPS) * g


def _rmsnorm_kernel(x_ref, g_ref, h_ref):
    h_ref[...] = _rms(x_ref[...], g_ref[...]).astype(h_ref.dtype)


def rmsnorm(x, g, out_dtype=BF16, tm=512):
    t, d = x.shape
    return pl.pallas_call(
        _rmsnorm_kernel,
        out_shape=jax.ShapeDtypeStruct((t, d), out_dtype),
        grid=(t // tm,),
        in_specs=[pl.BlockSpec((tm, d), lambda i: (i, 0)), pl.BlockSpec((1, d), lambda i: (0, 0))],
        out_specs=pl.BlockSpec((tm, d), lambda i: (i, 0)),
        compiler_params=_cparams("parallel"),
        name="rmsnorm",
    )(x, g.reshape(1, d))


def _add_norm_res_kernel(x_ref, y_ref, g_ref, xo_ref, ho_ref, xs_scr, hs_scr):
    nres, tp, d = xo_ref.shape
    xn = x_ref[...] + y_ref[...].astype(F32)
    h = _rms(xn, g_ref[...])
    for s in range(d // V7X_LANES):
        ls = slice(s * V7X_LANES, (s + 1) * V7X_LANES)
        xs_scr[s] = xn[:, ls]
        hs_scr[s] = h[:, ls]
    for r in range(nres):
        rows = pl.ds(r, tp, stride=nres)
        for s in range(d // V7X_LANES):
            ls = slice(s * V7X_LANES, (s + 1) * V7X_LANES)
            xo_ref[r, :, ls] = xs_scr[s, rows, :]
            ho_ref[r, :, ls] = hs_scr[s, rows, :].astype(ho_ref.dtype)


def add_norm_to_residue_major(x, y, g, *, bsz, seq, nres, tp=128):
    t, d = x.shape
    lsub = seq // nres
    tm = nres * tp
    nj = lsub // tp
    slab = pltpu.VMEM((d // V7X_LANES, tm, V7X_LANES), F32)
    xo, ho = pl.pallas_call(
        _add_norm_res_kernel,
        out_shape=(jax.ShapeDtypeStruct((bsz * nres, lsub, d), F32), jax.ShapeDtypeStruct((bsz * nres, lsub, d), BF16)),
        grid=(bsz, nj),
        in_specs=[pl.BlockSpec((tm, d), lambda b, j: (b * nj + j, 0)), pl.BlockSpec((tm, d), lambda b, j: (b * nj + j, 0)),
                  pl.BlockSpec((1, d), lambda b, j: (0, 0))],
        out_specs=(pl.BlockSpec((nres, tp, d), lambda b, j: (b, j, 0)), pl.BlockSpec((nres, tp, d), lambda b, j: (b, j, 0))),
        scratch_shapes=[slab, slab],
        compiler_params=_cparams("parallel", "parallel"),
        name="add_norm_reorder",
    )(x, y, g.reshape(1, d))
    return xo.reshape(t, d), ho.reshape(t, d)


def _mm_kernel(*refs, n_lhs, ks, has_res):
    xs = refs[:n_lhs]
    w_ref = refs[n_lhs]
    r_ref = refs[n_lhs + 1] if has_res else None
    o_ref = refs[-1]
    acc = None
    k0 = 0
    for x_ref, k in zip(xs, ks):
        part = jnp.dot(x_ref[...], w_ref[k0:k0 + k, :].astype(BF16), preferred_element_type=F32)
        acc = part if acc is None else acc + part
        k0 += k
    if has_res:
        acc = acc + r_ref[...]
    o_ref[...] = acc.astype(o_ref.dtype)


def matmul(xs, w, *, n_cols, col_block_off=0, tn, tm=1024, res=None, out_dtype=BF16, name="matmul"):
    t = xs[0].shape[0]
    ks = tuple(x.shape[1] for x in xs)
    ktot = sum(ks)
    assert w.shape[0] == ktot and n_cols % tn == 0 and t % tm == 0
    in_specs = [pl.BlockSpec((tm, k), lambda i, j: (i, 0)) for k in ks]
    in_specs.append(pl.BlockSpec((ktot, tn), lambda i, j: (0, j + col_block_off)))
    args = list(xs) + [w]
    if res is not None:
        in_specs.append(pl.BlockSpec((tm, tn), lambda i, j: (i, j)))
        args.append(res)
    return pl.pallas_call(
        functools.partial(_mm_kernel, n_lhs=len(xs), ks=ks, has_res=res is not None),
        out_shape=jax.ShapeDtypeStruct((t, n_cols), out_dtype),
        grid=(t // tm, n_cols // tn),
        in_specs=in_specs,
        out_specs=pl.BlockSpec((tm, tn), lambda i, j: (i, j)),
        compiler_params=_cparams("parallel", "arbitrary"),
        name=name,
    )(*args)


def _mm_nt_kernel(x_ref, wt_ref, o_ref):
    wt = wt_ref[...].astype(BF16)
    acc = lax.dot_general(x_ref[...], wt, (((1,), (1,)), ((), ())), preferred_element_type=F32)
    o_ref[...] = acc.astype(o_ref.dtype)


def matmul_nt(x, w_t, *, row_start, n_cols, tn, tm=1024, out_dtype=BF16, name="matmul_nt"):
    t, k = x.shape
    assert w_t.shape[1] == k and n_cols % tn == 0 and t % tm == 0
    if row_start % tn == 0:
        w_spec = pl.BlockSpec((tn, k), lambda i, j: (row_start // tn + j, 0))
    else:
        assert row_start % SUBLANES == 0 and tn % SUBLANES == 0
        w_spec = pl.BlockSpec((pl.Element(tn), pl.Element(k)),
                              lambda i, j: ((row_start // SUBLANES + j * (tn // SUBLANES)) * SUBLANES, 0))
    return pl.pallas_call(
        _mm_nt_kernel,
        out_shape=jax.ShapeDtypeStruct((t, n_cols), out_dtype),
        grid=(t // tm, n_cols // tn),
        in_specs=[pl.BlockSpec((tm, k), lambda i, j: (i, 0)), w_spec],
        out_specs=pl.BlockSpec((tm, tn), lambda i, j: (i, j)),
        compiler_params=_cparams("parallel", "arbitrary"),
        name=name,
    )(x, w_t)


HALO_BF16 = 16


def _ssd_prep_kernel(raw_ref, bias_ref, alog_ref, dt_ref, acs_ref, dtt_ref, acst_ref, *, seq):
    raw = raw_ref[...] + bias_ref[...]
    dt = jnp.maximum(raw, 0.0) + jnp.log1p(jnp.exp(-jnp.abs(raw)))
    a = -jnp.exp(alog_ref[...])
    x = dt * a
    row = lax.broadcasted_iota(jnp.int32, x.shape, 0) % SSD_CHUNK
    sh = 1
    while sh < SSD_CHUNK:
        x = x + jnp.where(row >= sh, pltpu.roll(x, sh, 0), 0.0)
        sh *= 2
    dt_ref[...] = dt
    acs_ref[...] = x
    for c in range(seq // SSD_CHUNK):
        sl = slice(c * SSD_CHUNK, (c + 1) * SSD_CHUNK)
        dtt_ref[sl, :] = dt[sl, :].T
        acst_ref[sl, :] = x[sl, :].T


def ssd_prep(dt_raw, dt_bias_p, a_log_p, *, bsz, seq):
    t = dt_raw.shape[0]
    shp = jax.ShapeDtypeStruct((t, V7X_LANES), F32)
    blk = pl.BlockSpec((seq, V7X_LANES), lambda b: (b, 0))
    vec = pl.BlockSpec((1, V7X_LANES), lambda b: (0, 0))
    return pl.pallas_call(
        functools.partial(_ssd_prep_kernel, seq=seq),
        out_shape=(shp, shp, shp, shp),
        grid=(bsz,),
        in_specs=[blk, vec, vec],
        out_specs=(blk, blk, blk, blk),
        compiler_params=_cparams("parallel"),
        name="ssd_prep",
    )(dt_raw, dt_bias_p, a_log_p)


def _conv_silu(cur_ref, halo_ref, w_ref, b_ref, buf_ref, col0, first):
    rows, c = cur_ref.shape
    cols = slice(col0, col0 + c)
    buf_ref[0:HALO_BF16, :] = jnp.where(first, 0.0, halo_ref[...].astype(F32))
    buf_ref[HALO_BF16:HALO_BF16 + rows, :] = cur_ref[...].astype(F32)
    acc = b_ref[:, cols]
    for k in range(SSD_CONV):
        off = HALO_BF16 - (SSD_CONV - 1) + k
        acc = acc + w_ref[k:k + 1, cols] * buf_ref[off:off + rows, :]
    return acc * jax.nn.sigmoid(acc)


def _ssd_main_kernel(xr_ref, xh_ref, br_ref, bh_ref, cr_ref, ch_ref, cw_ref, cbias_ref,
                     z_ref, dt_ref, acs_ref, dtt_ref, acst_ref, dsk_ref, nw_ref,
                     y_ref, st_ref, xbuf, bbuf, cbuf, *, heads_per_group):
    ci = pl.program_id(1)
    L = SSD_CHUNK
    gw = heads_per_group * SSD_HEAD_DIM
    npair = heads_per_group // 2
    d_ssd = xr_ref.shape[1]
    bc_w = br_ref.shape[1]

    @pl.when(ci == 0)
    def _():
        st_ref[...] = jnp.zeros_like(st_ref)

    first = ci == 0
    x_all = _conv_silu(xr_ref, xh_ref, cw_ref, cbias_ref, xbuf, 0, first)
    b_all = _conv_silu(br_ref, bh_ref, cw_ref, cbias_ref, bbuf, d_ssd, first)
    c_all = _conv_silu(cr_ref, ch_ref, cw_ref, cbias_ref, cbuf, d_ssd + bc_w, first)
    dt = dt_ref[...]
    acs = acs_ref[...]
    dtt = dtt_ref[...]
    acst = acst_ref[...]
    row = lax.broadcasted_iota(jnp.int32, (L, L), 0)
    col = lax.broadcasted_iota(jnp.int32, (L, L), 1)
    causal = col <= row
    lane_lo = lax.broadcasted_iota(jnp.int32, (L, 2 * SSD_HEAD_DIM), 1) < SSD_HEAD_DIM
    lane_lo1 = lax.broadcasted_iota(jnp.int32, (1, 2 * SSD_HEAD_DIM), 1) < SSD_HEAD_DIM
    zero_b = jnp.zeros((L, 2 * SSD_HEAD_DIM), BF16)

    for g in range(SSD_GROUPS):
        bmf = b_all[:, g * SSD_STATE:(g + 1) * SSD_STATE]
        bm = bmf.astype(BF16)
        cm = c_all[:, g * SSD_STATE:(g + 1) * SSD_STATE].astype(BF16)
        bmt = bmf.T.astype(BF16)
        cb = lax.dot_general(cm, bm, (((1,), (1,)), ((), ())), preferred_element_type=F32)
        xg = x_all[:, g * gw:(g + 1) * gw]
        y_diags, e_ls, w_ls, cds = [], [], [], []
        for q in range(npair):
            xp = xg[:, q * 2 * SSD_HEAD_DIM:(q + 1) * 2 * SSD_HEAD_DIM].astype(BF16)
            ms = []
            for hh in range(2):
                h = g * heads_per_group + 2 * q + hh
                seg = acs[:, h:h + 1] - acst[h:h + 1, :]
                dec = jnp.exp(jnp.where(causal, seg, NEG_INF))
                ms.append((cb * dec * dtt[h:h + 1, :]).astype(BF16))
            h0 = g * heads_per_group + 2 * q
            h1 = h0 + 1
            lhs = jnp.concatenate(ms, axis=1)
            rhs = jnp.concatenate([jnp.where(lane_lo, xp, zero_b), jnp.where(lane_lo, zero_b, xp)], axis=0)
            y_diags.append(jnp.dot(lhs, rhs, preferred_element_type=F32))
            last0 = acs[L - 1:L, h0:h0 + 1]
            last1 = acs[L - 1:L, h1:h1 + 1]
            e_ls.append(jnp.where(lane_lo, jnp.exp(acs[:, h0:h0 + 1]), jnp.exp(acs[:, h1:h1 + 1])))
            w_ls.append(jnp.where(lane_lo, dt[:, h0:h0 + 1] * jnp.exp(last0 - acs[:, h0:h0 + 1]),
                                  dt[:, h1:h1 + 1] * jnp.exp(last1 - acs[:, h1:h1 + 1])))
            cds.append(jnp.where(lane_lo1, jnp.exp(last0), jnp.exp(last1)))
        st = st_ref[g]
        y_off = jnp.dot(cm, st.astype(BF16), preferred_element_type=F32) * jnp.concatenate(e_ls, axis=1)
        xw = (xg * jnp.concatenate(w_ls, axis=1)).astype(BF16)
        st_ref[g] = st * jnp.concatenate(cds, axis=1) + jnp.dot(bmt, xw, preferred_element_type=F32)
        y = jnp.concatenate(y_diags, axis=1) + y_off + xg * dsk_ref[:, g * gw:(g + 1) * gw]
        zg = z_ref[:, g * gw:(g + 1) * gw].astype(F32)
        yg = y * (zg * jax.nn.sigmoid(zg))
        ms_ = jnp.mean(yg * yg, axis=-1, keepdims=True)
        y_ref[:, g * gw:(g + 1) * gw] = (yg * lax.rsqrt(ms_ + EPS) * nw_ref[:, g * gw:(g + 1) * gw]).astype(y_ref.dtype)


def ssd_main(zx, conv_w, conv_b, dt, acs, dtt, acst, dskip_e, norm_w, *, bsz, seq, d_ssd):
    t = zx.shape[0]
    nc = seq // SSD_CHUNK
    heads = d_ssd // SSD_HEAD_DIM
    hpg = heads // SSD_GROUPS
    bc_w = SSD_GROUPS * SSD_STATE
    conv_dim = conv_w.shape[1]
    assert d_ssd % bc_w == 0 and hpg % 2 == 0 and conv_dim == d_ssd + 2 * bc_w
    rowmap = lambda b, c: (b * nc + c, 0)
    hb = SSD_CHUNK // HALO_BF16

    def cur(width, col_block):
        return pl.BlockSpec((SSD_CHUNK, width), lambda b, c: (b * nc + c, col_block))

    def halo(width, col_block):
        return pl.BlockSpec((HALO_BF16, width), lambda b, c: (jnp.maximum((b * nc + c) * hb - 1, 0), col_block))

    xcol, bcol, ccol = 1, 2 * d_ssd // bc_w, 2 * d_ssd // bc_w + 1
    return pl.pallas_call(
        functools.partial(_ssd_main_kernel, heads_per_group=hpg),
        out_shape=jax.ShapeDtypeStruct((t, d_ssd), BF16),
        grid=(bsz, nc),
        in_specs=[
            cur(d_ssd, xcol), halo(d_ssd, xcol), cur(bc_w, bcol), halo(bc_w, bcol), cur(bc_w, ccol), halo(bc_w, ccol),
            pl.BlockSpec((SSD_CONV, conv_dim), lambda b, c: (0, 0)),
            pl.BlockSpec((1, conv_dim), lambda b, c: (0, 0)),
            pl.BlockSpec((SSD_CHUNK, d_ssd), rowmap),
            pl.BlockSpec((SSD_CHUNK, V7X_LANES), rowmap),
            pl.BlockSpec((SSD_CHUNK, V7X_LANES), rowmap),
            pl.BlockSpec((SSD_CHUNK, V7X_LANES), rowmap),
            pl.BlockSpec((SSD_CHUNK, V7X_LANES), rowmap),
            pl.BlockSpec((1, d_ssd), lambda b, c: (0, 0)),
            pl.BlockSpec((1, d_ssd), lambda b, c: (0, 0)),
        ],
        out_specs=pl.BlockSpec((SSD_CHUNK, d_ssd), rowmap),
        scratch_shapes=[pltpu.VMEM((SSD_GROUPS, SSD_STATE, hpg * SSD_HEAD_DIM), F32),
                        pltpu.VMEM((SSD_CHUNK + HALO_BF16, d_ssd), F32),
                        pltpu.VMEM((SSD_CHUNK + HALO_BF16, bc_w), F32),
                        pltpu.VMEM((SSD_CHUNK + HALO_BF16, bc_w), F32)],
        compiler_params=_cparams("parallel", "arbitrary"),
        name="ssd_main",
    )(zx, zx, zx, zx, zx, zx, conv_w, conv_b.reshape(1, conv_dim), zx, dt, acs, dtt, acst, dskip_e,
      norm_w.reshape(1, d_ssd))


CONF_HALO = 32


SUBLANES = 8


def _conf_kernel(a_ref, g_ref, ah_ref, gh_ref, w_ref, b_ref, lw_ref, lb_ref, o_ref, buf_ref, sh_ref, *, ts, rc, lt):
    i = pl.program_id(1)
    d = a_ref.shape[1]
    ah = ah_ref[...].astype(F32)
    gh = gh_ref[...].astype(F32)
    buf_ref[0:CONF_HALO, :] = jnp.where(i > 0, ah * jax.nn.sigmoid(gh), 0.0)
    a = a_ref[...].astype(F32)
    g = g_ref[...].astype(F32)
    buf_ref[CONF_HALO:CONF_HALO + ts, :] = a * jax.nn.sigmoid(g)
    span = ts + CONF_HALO - SUBLANES
    for j in range(1, SUBLANES):
        sh_ref[j - 1, 0:span, :] = buf_ref[j:j + span, :]
    base = CONF_HALO - (CONF_KERNEL - 1)
    outs = []
    for r in range(ts // rc):
        cols = []
        for c in range(d // lt):
            ls = slice(c * lt, (c + 1) * lt)
            acc = jnp.broadcast_to(b_ref[:, ls], (rc, lt))
            for k in range(CONF_KERNEL):
                j = (base + k) % SUBLANES
                off = r * rc + (base + k) - j
                src = buf_ref[off:off + rc, ls] if j == 0 else sh_ref[j - 1, off:off + rc, ls]
                acc = acc + w_ref[k:k + 1, ls] * src
            cols.append(acc)
        outs.append(jnp.concatenate(cols, axis=1))
    u = jnp.concatenate(outs, axis=0)
    mu = jnp.mean(u, axis=-1, keepdims=True)
    uc = u - mu
    var = jnp.mean(uc * uc, axis=-1, keepdims=True)
    y = uc * lax.rsqrt(var + EPS) * lw_ref[...] + lb_ref[...]
    o_ref[...] = (y * jax.nn.sigmoid(y)).astype(o_ref.dtype)


def conf_module(conf, dw_w, dw_b, ln_w, ln_b, *, bsz, seq, ts=256, rc=32, lt=512):
    t = conf.shape[0]
    d = dw_w.shape[1]
    nsb = seq // ts
    hb = ts // CONF_HALO
    cur = lambda col: pl.BlockSpec((ts, d), lambda b, i: (b * nsb + i, col))
    halo = lambda col: pl.BlockSpec((CONF_HALO, d), lambda b, i: (jnp.maximum((b * nsb + i) * hb - 1, 0), col))
    vec = pl.BlockSpec((1, d), lambda b, i: (0, 0))
    return pl.pallas_call(
        functools.partial(_conf_kernel, ts=ts, rc=rc, lt=lt),
        out_shape=jax.ShapeDtypeStruct((t, d), BF16),
        grid=(bsz, nsb),
        in_specs=[cur(0), cur(1), halo(0), halo(1), pl.BlockSpec((CONF_KERNEL, d), lambda b, i: (0, 0)),
                  vec, vec, vec],
        out_specs=pl.BlockSpec((ts, d), lambda b, i: (b * nsb + i, 0)),
        scratch_shapes=[pltpu.VMEM((ts + CONF_HALO, d), F32), pltpu.VMEM((SUBLANES - 1, ts + CONF_HALO, d), F32)],
        compiler_params=_cparams("parallel", "parallel"),
        name="conf_module",
    )(conf, conf, conf, conf, dw_w, dw_b.reshape(1, d), ln_w.reshape(1, d), ln_b.reshape(1, d))


FFN_SUB = 128


FFN_NCHUNK = 512


def pack_bf16_pair(a, b):
    ua = lax.bitcast_convert_type(a.astype(BF16).astype(F32), jnp.uint32)
    ub = lax.bitcast_convert_type(b.astype(BF16).astype(F32), jnp.uint32)
    return (ua >> 16) | (ub & jnp.uint32(0xFFFF0000))


def unpack_bf16_pair(u):
    a = lax.bitcast_convert_type(u << 16, F32)
    b = lax.bitcast_convert_type(u & jnp.uint32(0xFFFF0000), F32)
    return a, b


def _ffn_kernel(ge_ref, gb_ref, gr_ref, x_ref, w1_ref, w3_ref, w2_ref, o_ref, acc_ref, *xb_scr, nf, gsz, packed, ragged):
    g = pl.program_id(0)
    f = pl.program_id(1)
    rows = gr_ref[g]
    valid = rows > 0
    d = acc_ref.shape[1]

    @pl.when(valid & (f == 0))
    def _():
        acc_ref[...] = jnp.zeros_like(acc_ref)
        if packed:
            a, b = unpack_bf16_pair(x_ref[...])
            xb_scr[0][:, 0:d // 2] = a.astype(BF16)
            xb_scr[0][:, d // 2:d] = b.astype(BF16)

    xsrc = xb_scr[0] if packed else x_ref

    def slab(r0, nrows, w1, w3, w2):
        xs = xsrc[pl.ds(r0, nrows), :]
        h1 = jnp.dot(xs, w1, preferred_element_type=F32)
        h3 = jnp.dot(xs, w3, preferred_element_type=F32)
        hh = (h1 * jax.nn.sigmoid(h1) * h3).astype(BF16)
        for c in range(d // FFN_NCHUNK):
            cs = slice(c * FFN_NCHUNK, (c + 1) * FFN_NCHUNK)
            acc_ref[pl.ds(r0, nrows), cs] += jnp.dot(hh, w2[:, cs], preferred_element_type=F32)

    nslab = (rows + FFN_SUB - 1) // FFN_SUB
    for ns in (range(1, gsz // FFN_SUB + 1) if ragged else (gsz // FFN_SUB,)):
        @pl.when(nslab == ns)
        def _(ns=ns):
            slab(0, ns * FFN_SUB, w1_ref[...].astype(BF16), w3_ref[...].astype(BF16), w2_ref[...].astype(BF16))

    @pl.when(jnp.logical_not(valid) & (f == 0))
    def _():
        o_ref[...] = jnp.zeros_like(o_ref)

    @pl.when(valid & (f == nf - 1))
    def _():
        if packed:
            o_ref[...] = pack_bf16_pair(acc_ref[:, 0:d // 2], acc_ref[:, d // 2:d])
        else:
            o_ref[...] = acc_ref[...].astype(o_ref.dtype)


def ffn_groups(xs, w1, w3, w2, grp_expert, grp_block, grp_rows, *, gsz=1024, tf=256, packed=False, ragged=False,
               single_buffer_rows=False):
    e, d, ff = w1.shape
    r, dx = xs.shape
    nf = ff // tf
    ng = grp_expert.shape[0]

    def fmap(f, g, gr):
        return jnp.where(gr[g] > 0, f, nf - 1)

    scratch = [pltpu.VMEM((gsz, d), F32)]
    if packed:
        scratch.append(pltpu.VMEM((gsz, d), BF16))
    row_mode = dict(pipeline_mode=pl.Buffered(1)) if single_buffer_rows else {}
    grid_spec = pltpu.PrefetchScalarGridSpec(
        num_scalar_prefetch=3,
        grid=(ng, nf),
        in_specs=[
            pl.BlockSpec((gsz, dx), lambda g, f, ge, gb, gr: (gb[g], 0), **row_mode),
            pl.BlockSpec((None, d, tf), lambda g, f, ge, gb, gr: (ge[g], 0, fmap(f, g, gr))),
            pl.BlockSpec((None, d, tf), lambda g, f, ge, gb, gr: (ge[g], 0, fmap(f, g, gr))),
            pl.BlockSpec((None, tf, d), lambda g, f, ge, gb, gr: (ge[g], fmap(f, g, gr), 0)),
        ],
        out_specs=pl.BlockSpec((gsz, dx), lambda g, f, ge, gb, gr: (gb[g], 0), **row_mode),
        scratch_shapes=scratch,
    )
    return pl.pallas_call(
        functools.partial(_ffn_kernel, nf=nf, gsz=gsz, packed=packed, ragged=ragged),
        out_shape=jax.ShapeDtypeStruct((r, dx), xs.dtype),
        grid_spec=grid_spec,
        compiler_params=_cparams("arbitrary", "arbitrary"),
        name="ffn_groups",
    )(grp_expert, grp_block, grp_rows, xs, w1, w3, w2)


def _row_copy(src_ref, src_row, dst_ref, dst_row, sem):
    return pltpu.make_async_copy(src_ref.at[pl.ds(src_row, 1), :], dst_ref.at[pl.ds(dst_row, 1), :], sem)


DMA_UNROLL = 8


def _dispatch_kernel(pos0_ref, pos1_ref, gr_ref, h_ref, xs_ref, zbuf, sem, zsem, *, gsz, ng):
    tm = h_ref.shape[0]
    step = pl.program_id(0)
    base = step * tm

    @pl.when(step == 0)
    def _():
        zbuf[...] = jnp.zeros_like(zbuf)
        for start in (True, False):
            for g in range(ng):
                rows = gr_ref[g]
                for c in range(gsz // FFN_SUB):
                    @pl.when(rows < (c + 1) * FFN_SUB)
                    def _(g=g, c=c, start=start):
                        cp = pltpu.make_async_copy(zbuf, xs_ref.at[pl.ds(g * gsz + c * FFN_SUB, FFN_SUB), :], zsem)
                        if start:
                            cp.start()
                        else:
                            cp.wait()

    def copies(i):
        return [_row_copy(h_ref, i, xs_ref, p[base + i], sem) for p in (pos0_ref, pos1_ref)]

    def issue(i, c):
        for cp in copies(i):
            cp.start()
        return c

    lax.fori_loop(0, tm, issue, 0, unroll=DMA_UNROLL)

    def drain(i, c):
        for cp in copies(i):
            cp.wait()
        return c

    lax.fori_loop(0, tm, drain, 0, unroll=DMA_UNROLL)


def moe_dispatch(hp, pos0, pos1, grp_rows, gsz, tm=512):
    t, w = hp.shape
    ng = grp_rows.shape[0]
    grid_spec = pltpu.PrefetchScalarGridSpec(
        num_scalar_prefetch=3,
        grid=(t // tm,),
        in_specs=[pl.BlockSpec((tm, w), lambda i, p0, p1, gr: (i, 0))],
        out_specs=pl.BlockSpec(memory_space=pl.ANY),
        scratch_shapes=[pltpu.VMEM((FFN_SUB, w), hp.dtype), pltpu.SemaphoreType.DMA, pltpu.SemaphoreType.DMA],
    )
    return pl.pallas_call(
        functools.partial(_dispatch_kernel, gsz=gsz, ng=ng),
        out_shape=jax.ShapeDtypeStruct((ng * gsz, w), hp.dtype),
        grid_spec=grid_spec,
        compiler_params=_cparams("arbitrary"),
        name="moe_dispatch",
    )(pos0, pos1, grp_rows, hp)


def _t5_bucket_np(dist):
    max_exact = REL_BUCKETS // 2
    d_f = np.maximum(dist, 1).astype(np.float32)
    large = max_exact + (np.log(d_f / np.float32(max_exact)) / np.float32(math.log(REL_MAX_DIST / max_exact))
                         * np.float32(REL_BUCKETS - max_exact)).astype(np.int32)
    large = np.minimum(large, REL_BUCKETS - 1)
    return np.where(dist < max_exact, dist, large)


RES = 4
TQ = ATTN_BLOCK
PIECE = TQ // RES


def _bucket_index_tables():
    assert DILATED_PATTERNS == ((128, 1), (512, 4), (2048, 16)) and RES == 4 and TQ == 128

    def fin(steps, dil, n_back=TQ):
        ok = (steps >= 0) & (steps <= n_back)
        return np.where(ok, _t5_bucket_np(np.clip(steps, 0, n_back) * dil), -1).astype(np.int32)

    i = np.arange(TQ)[:, None]
    t4 = fin(i + TQ - np.arange(2 * TQ)[None, :], 4)
    t16 = fin(i - np.arange(TQ)[None, :], 16)
    rq, mq = i // PIECE, i % PIECE
    jk = np.arange(2 * TQ)[None, :]
    rk, mk = jk // (2 * PIECE), jk % (2 * PIECE)
    t1 = fin(RES * (PIECE + mq - mk) + (rq - rk), 1)
    t1first = fin(RES * (mq - mk) + (rq - rk), 1)
    return t4, t1, t1first, t16


def _bias_tab_kernel(rel_ref, *refs):
    n = len(refs) // 2
    h = pl.program_id(0)
    for idx_ref, o_ref in zip(refs[:n], refs[n:]):
        idx = idx_ref[...]
        tab = jnp.full(idx.shape, NEG_INF, F32)
        for b in range(REL_BUCKETS):
            tab = jnp.where(idx == b, rel_ref[b, h], tab)
        o_ref[0] = tab


def attn_bias_tables(rel_bias):
    idx_tabs = [jnp.asarray(t) for t in _bucket_index_tables()]
    nh = rel_bias.shape[1]
    return pl.pallas_call(
        _bias_tab_kernel,
        out_shape=tuple(jax.ShapeDtypeStruct((nh,) + t.shape, F32) for t in idx_tabs),
        grid=(nh,),
        in_specs=[pl.BlockSpec(memory_space=pltpu.SMEM)] + [pl.BlockSpec(t.shape, lambda h: (0, 0)) for t in idx_tabs],
        out_specs=tuple(pl.BlockSpec((1,) + t.shape, lambda h: (h, 0, 0)) for t in idx_tabs),
        compiler_params=_cparams("parallel"),
        name="attn_bias_tables",
    )(rel_bias.astype(F32), *idx_tabs)


def _attn_kernel(q_ref, k_ref, v_ref, t4_ref, t1_ref, t1f_ref, t16_ref, o_ref, acc, m_s, l_s, qs, ks, vs, od, md, ld,
                 *, scale):
    nres, lsub, hd = q_ref.shape
    nt = (((1,), (1,)), ((), ()))

    def tile(q, k, v, bias):
        s = lax.dot_general(q, k, nt, preferred_element_type=F32) * scale + bias
        m = jnp.max(s, axis=-1, keepdims=True)
        p = jnp.exp(s - m)
        l = jnp.sum(p, axis=-1, keepdims=True)
        o = jnp.dot(p.astype(BF16), v, preferred_element_type=F32)
        return o, jnp.broadcast_to(m, o.shape), jnp.broadcast_to(l, o.shape)

    def merge(r, rows, o, m, l):
        m0 = m_s[r, rows, :]
        mn = jnp.maximum(m0, m)
        a = jnp.exp(m0 - mn)
        b = jnp.exp(m - mn)
        acc[r, rows, :] = acc[r, rows, :] * a + o * b
        l_s[r, rows, :] = l_s[r, rows, :] * a + l * b
        m_s[r, rows, :] = mn

    nsub = lsub // TQ
    for r in range(nres):
        qs[r] = q_ref[r].astype(F32)
        ks[r] = k_ref[r].astype(F32)
        vs[r] = v_ref[r].astype(F32)
    masked = jnp.full((TQ, TQ), NEG_INF, F32)
    bias16 = (jnp.concatenate([t16_ref[0], masked], axis=1), jnp.concatenate([masked, t16_ref[0]], axis=1))
    for r in range(nres):
        for pair in range(nsub // 2):
            sub = [pl.ds(2 * pair + e, TQ, stride=nsub) for e in range(2)]
            kk = jnp.concatenate([ks[r, s, :] for s in sub], axis=0).astype(BF16)
            vv = jnp.concatenate([vs[r, s, :] for s in sub], axis=0).astype(BF16)
            for e in range(2):
                o, m, l = tile(qs[r, sub[e], :].astype(BF16), kk, vv, bias16[e])
                od[r * nsub + 2 * pair + e] = o
                md[r * nsub + 2 * pair + e] = m
                ld[r * nsub + 2 * pair + e] = l

    for r in range(nres):
        for i in range(lsub // TQ):
            rows = pl.ds(i * TQ, TQ)
            if i == 0:
                krows = pl.ds(0, 2 * TQ)
                bias = jnp.concatenate([t4_ref[0, :, TQ:2 * TQ], masked], axis=1)
            else:
                krows = pl.ds((i - 1) * TQ, 2 * TQ)
                bias = t4_ref[0]
            o, m, l = tile(q_ref[r, rows, :], k_ref[r, krows, :], v_ref[r, krows, :], bias)
            acc[r, rows, :] = o
            m_s[r, rows, :] = m
            l_s[r, rows, :] = l

    for j in range(lsub // PIECE):
        q = jnp.concatenate([q_ref[r, pl.ds(j * PIECE, PIECE), :] for r in range(nres)], axis=0)
        if j == 0:
            krows, bias = pl.ds(0, 2 * PIECE), t1f_ref[0]
        else:
            krows, bias = pl.ds((j - 1) * PIECE, 2 * PIECE), t1_ref[0]
        k = jnp.concatenate([k_ref[r, krows, :] for r in range(nres)], axis=0)
        v = jnp.concatenate([v_ref[r, krows, :] for r in range(nres)], axis=0)
        o, m, l = tile(q, k, v, bias)
        for r in range(nres):
            part = slice(r * PIECE, (r + 1) * PIECE)
            merge(r, pl.ds(j * PIECE, PIECE), o[part], m[part], l[part])

    for r in range(nres):
        for c in range(nsub):
            merge(r, pl.ds(c, TQ, stride=nsub), od[r * nsub + c], md[r * nsub + c], ld[r * nsub + c])

    for r in range(nres):
        o_ref[r] = (acc[r] / l_s[r]).astype(o_ref.dtype)


def attention(qkv, tabs, *, bsz, seq, hd):
    dm = ATTN_HEADS * hd
    lsub = seq // RES
    assert hd == V7X_LANES and lsub // TQ == 16 // RES and seq // 16 == TQ
    view = qkv.reshape(bsz * RES, lsub, 3 * dm)
    blk = lambda which: pl.BlockSpec((RES, lsub, hd), lambda b, h: (b, 0, which * ATTN_HEADS + h))
    tab = lambda t: pl.BlockSpec((1,) + t.shape[1:], lambda b, h: (h, 0, 0))
    st = pltpu.VMEM((RES, lsub, hd), F32)
    sub_f = pltpu.VMEM((RES * lsub // TQ, TQ, hd), F32)
    o = pl.pallas_call(
        functools.partial(_attn_kernel, scale=hd ** -0.5),
        out_shape=jax.ShapeDtypeStruct((bsz * RES, lsub, dm), BF16),
        grid=(bsz, ATTN_HEADS),
        in_specs=[blk(0), blk(1), blk(2)] + [tab(t) for t in tabs],
        out_specs=pl.BlockSpec((RES, lsub, hd), lambda b, h: (b, 0, h)),
        scratch_shapes=[st, st, st, st, st, st, sub_f, sub_f, sub_f],
        compiler_params=_cparams("parallel", "arbitrary"),
        name="attention",
    )(view, view, view, *tabs)
    return o.reshape(bsz * seq, dm)


def _split_bf16(x):
    hi = x.astype(BF16)
    lo = (x - hi.astype(F32)).astype(BF16)
    return hi, lo


def _moe_route_kernel(x_ref, g_ref, wr_ref, h_ref, meta_ref, metat_ref, cnt_ref, carry_ref):
    step = pl.program_id(0)
    tm = x_ref.shape[0]

    @pl.when(step == 0)
    def _():
        carry_ref[...] = jnp.zeros_like(carry_ref)

    h = _rms(x_ref[...], g_ref[...])
    h_hi, h_lo = _split_bf16(h)
    dh = h.shape[1] // 2
    h_ref[...] = pack_bf16_pair(h[:, 0:dh], h[:, dh:2 * dh])
    w_hi, w_lo = _split_bf16(wr_ref[...])
    both = jnp.dot(h_hi, jnp.concatenate([w_hi, w_lo], axis=1), preferred_element_type=F32)
    logits = both[:, 0:V7X_LANES] + jnp.dot(h_lo, w_hi, preferred_element_type=F32) + both[:, V7X_LANES:2 * V7X_LANES]
    lane = lax.broadcasted_iota(jnp.int32, (tm, V7X_LANES), 1).astype(F32)
    lg = jnp.where(lane < N_EXPERTS, logits, NEG_INF)
    m1 = jnp.max(lg, axis=-1, keepdims=True)
    i1 = jnp.min(jnp.where(lg == m1, lane, float(V7X_LANES)), axis=-1, keepdims=True)
    lg2 = jnp.where(lane == i1, NEG_INF, lg)
    m2 = jnp.max(lg2, axis=-1, keepdims=True)
    i2 = jnp.min(jnp.where(lg2 == m2, lane, float(V7X_LANES)), axis=-1, keepdims=True)
    e2 = jnp.exp(m2 - m1)
    g1 = 1.0 / (1.0 + e2)
    g2 = e2 / (1.0 + e2)
    oh = ((lane == i1) | (lane == i2)).astype(BF16)
    r_i = lax.broadcasted_iota(jnp.int32, (tm, tm), 0)
    c_i = lax.broadcasted_iota(jnp.int32, (tm, tm), 1)
    tri = (c_i < r_i).astype(BF16)
    rank = jnp.dot(tri, oh, preferred_element_type=F32) + carry_ref[0:1, :]
    r1 = jnp.sum(jnp.where(lane == i1, rank, 0.0), axis=-1, keepdims=True)
    r2 = jnp.sum(jnp.where(lane == i2, rank, 0.0), axis=-1, keepdims=True)
    carry_ref[0:1, :] = carry_ref[0:1, :] + jnp.sum(oh.astype(F32), axis=0, keepdims=True)
    meta = jnp.where(lane == 0, i1.astype(F32), 0.0)
    meta = jnp.where(lane == 1, i2.astype(F32), meta)
    meta = jnp.where(lane == 2, g1, meta)
    meta = jnp.where(lane == 3, g2, meta)
    meta = jnp.where(lane == 4, r1, meta)
    meta = jnp.where(lane == 5, r2, meta)
    meta_ref[...] = meta
    for c in range(tm // V7X_LANES):
        blk = meta[c * V7X_LANES:(c + 1) * V7X_LANES, :].T
        metat_ref[:, c * V7X_LANES:(c + 1) * V7X_LANES] = blk[0:SUBLANES, :]
    cnt_ref[...] = jnp.broadcast_to(carry_ref[0:1, :], cnt_ref.shape)


def moe_route(x, g, w_router_p, tm=256):
    t, d = x.shape
    return pl.pallas_call(
        _moe_route_kernel,
        out_shape=(jax.ShapeDtypeStruct((t, d // 2), jnp.uint32), jax.ShapeDtypeStruct((t, V7X_LANES), F32),
                   jax.ShapeDtypeStruct((SUBLANES, t), F32), jax.ShapeDtypeStruct((8, V7X_LANES), F32)),
        grid=(t // tm,),
        in_specs=[pl.BlockSpec((tm, d), lambda i: (i, 0)), pl.BlockSpec((1, d), lambda i: (0, 0)),
                  pl.BlockSpec((d, V7X_LANES), lambda i: (0, 0))],
        out_specs=(pl.BlockSpec((tm, d // 2), lambda i: (i, 0)), pl.BlockSpec((tm, V7X_LANES), lambda i: (i, 0)),
                   pl.BlockSpec((SUBLANES, tm), lambda i: (0, i)), pl.BlockSpec((8, V7X_LANES), lambda i: (0, 0))),
        scratch_shapes=[pltpu.VMEM((8, V7X_LANES), F32)],
        compiler_params=_cparams("arbitrary"),
        name="moe_route",
    )(x, g.reshape(1, d), w_router_p)


def _moe_final_kernel(pos0_ref, pos1_ref, x_ref, meta_ref, g_ref, ys_ref, o_ref, ybuf, o_scr, sems, *, lsub,
                      nsteps):
    nres, tp, d = x_ref.shape
    dh = d // 2
    nj = lsub // tp
    b = pl.program_id(0)
    j = pl.program_id(1)
    step = b * nj + j
    slot = step % 2

    def copies(bb, jj, sl, r, mm):
        tok = (bb * nres + r) * lsub + jj * tp + mm
        return [_row_copy(ys_ref, p[tok], ybuf.at[sl, k], r * tp + mm, sems.at[sl])
                for k, p in enumerate((pos0_ref, pos1_ref))]

    def for_all_rows(bb, jj, sl, start):
        for r in range(nres):
            def body(mm, c, r=r):
                for cp in copies(bb, jj, sl, r, mm):
                    if start:
                        cp.start()
                    else:
                        cp.wait()
                return c

            lax.fori_loop(0, tp, body, 0, unroll=DMA_UNROLL)

    @pl.when(step == 0)
    def _():
        for_all_rows(b, j, slot, True)

    @pl.when(step + 1 < nsteps)
    def _():
        nxt = j + 1 == nj
        for_all_rows(jnp.where(nxt, b + 1, b), jnp.where(nxt, 0, j + 1), 1 - slot, True)

    for_all_rows(b, j, slot, False)
    for r in range(nres):
        rows = slice(r * tp, (r + 1) * tp)
        meta = meta_ref[r]
        g1 = meta[:, 2:3]
        g2 = meta[:, 3:4]
        a0, b0 = unpack_bf16_pair(ybuf[slot, 0, rows, :])
        a1, b1 = unpack_bf16_pair(ybuf[slot, 1, rows, :])
        x = x_ref[r]
        xa = x[:, 0:dh] + a0 * g1 + a1 * g2
        xb = x[:, dh:d] + b0 * g1 + b1 * g2
        ms = (jnp.sum(xa * xa, axis=-1, keepdims=True) + jnp.sum(xb * xb, axis=-1, keepdims=True)) * (1.0 / d)
        inv = lax.rsqrt(ms + EPS)
        ya = xa * inv * g_ref[:, 0:dh]
        yb = xb * inv * g_ref[:, dh:d]
        out_rows = pl.ds(r, tp, stride=nres)
        for s in range(dh // V7X_LANES):
            ls = slice(s * V7X_LANES, (s + 1) * V7X_LANES)
            o_scr[s, out_rows, :] = ya[:, ls]
            o_scr[dh // V7X_LANES + s, out_rows, :] = yb[:, ls]
    for s in range(d // V7X_LANES):
        o_ref[:, s * V7X_LANES:(s + 1) * V7X_LANES] = o_scr[s]


def moe_final(x, ys, pos0, pos1, meta, g, *, bsz, seq, tp=128):
    t, d = x.shape
    lsub = seq // RES
    tm = RES * tp
    xv = x.reshape(bsz * RES, lsub, d)
    mv = meta.reshape(bsz * RES, lsub, V7X_LANES)
    grid_spec = pltpu.PrefetchScalarGridSpec(
        num_scalar_prefetch=2,
        grid=(bsz, lsub // tp),
        in_specs=[pl.BlockSpec((RES, tp, d), lambda b, j, p0, p1: (b, j, 0)),
                  pl.BlockSpec((RES, tp, V7X_LANES), lambda b, j, p0, p1: (b, j, 0)),
                  pl.BlockSpec((1, d), lambda b, j, p0, p1: (0, 0)), pl.BlockSpec(memory_space=pl.ANY)],
        out_specs=pl.BlockSpec((tm, d), lambda b, j, p0, p1: (b * (lsub // tp) + j, 0)),
        scratch_shapes=[pltpu.VMEM((2, TOP_K, tm, d // 2), jnp.uint32), pltpu.VMEM((d // V7X_LANES, tm, V7X_LANES), F32),
                        pltpu.SemaphoreType.DMA((2,))],
    )
    return pl.pallas_call(
        functools.partial(_moe_final_kernel, lsub=lsub, nsteps=bsz * (lsub // tp)),
        out_shape=jax.ShapeDtypeStruct((t, d), F32),
        grid_spec=grid_spec,
        compiler_params=_cparams("arbitrary", "arbitrary"),
        name="moe_final",
    )(pos0, pos1, xv, mv, g.reshape(1, d), ys)


PROJ_TM = 2048


def _pad_lanes(v, fill=0.0):
    v = v.reshape(1, -1).astype(F32)
    return jnp.pad(v, ((0, 0), (0, V7X_LANES - v.shape[1])), constant_values=fill)


def even_layer(x, h, p, *, bsz, seq):
    t, d = x.shape
    d_ssd = d
    d_conf = d
    heads = d_ssd // SSD_HEAD_DIM
    conv_dim = d_ssd + 2 * SSD_GROUPS * SSD_STATE
    i1 = d_ssd + conv_dim
    i2 = i1 + heads
    w_in_t = jnp.swapaxes(p["w_in"], 0, 1)
    zx = matmul_nt(h, w_in_t, row_start=0, n_cols=i1, tn=1024, tm=PROJ_TM, name="in_proj_zx")
    dt_raw = matmul_nt(h, w_in_t, row_start=i1, n_cols=V7X_LANES, tn=V7X_LANES, out_dtype=F32, name="in_proj_dt")
    conf = matmul_nt(h, w_in_t, row_start=i2, n_cols=2 * d_conf, tn=1024, tm=PROJ_TM, name="in_proj_conf")
    u = conf_module(conf, p["conf_dw_w"], p["conf_dw_b"], p["conf_ln_w"], p["conf_ln_b"], bsz=bsz, seq=seq)
    dt, acs, dtt, acst = ssd_prep(dt_raw, _pad_lanes(p["dt_bias"]), _pad_lanes(p["a_log"]), bsz=bsz, seq=seq)
    dskip_e = jnp.repeat(p["d_skip"].astype(F32), SSD_HEAD_DIM).reshape(1, d_ssd)
    y_ssd = ssd_main(zx, p["conv_w"], p["conv_b"], dt, acs, dtt, acst, dskip_e, p["ssd_norm_w"],
                     bsz=bsz, seq=seq, d_ssd=d_ssd)
    return matmul([y_ssd, u], p["w_out"], n_cols=d, tn=256, tm=PROJ_TM, res=x, out_dtype=F32, name="even_out_proj")


def dense_ffn(h, w1, w3, w2, gsz=1024):
    t = h.shape[0]
    ng = t // gsz
    ge = jnp.zeros((ng,), jnp.int32)
    gb = jnp.arange(ng, dtype=jnp.int32)
    gr = jnp.full((ng,), gsz, jnp.int32)
    return ffn_groups(h, w1[None], w3[None], w2[None], ge, gb, gr, gsz=gsz)


def odd_layer_attn(x, h, p, *, bsz, seq):
    t, d = x.shape
    hd = d // ATTN_HEADS
    qkv = matmul([h], p["w_qkv"], n_cols=3 * d, tn=1024, tm=PROJ_TM, name="qkv_proj")
    tabs = attn_bias_tables(p["rel_bias"])
    o = attention(qkv, tabs, bsz=bsz, seq=seq, hd=hd)
    return matmul([o], p["w_attn_out"], n_cols=d, tn=512, tm=2048, res=x, out_dtype=F32, name="attn_out_proj")


def moe_layer(x, norm_g, w_router, w1, w3, w2, final_g, *, bsz, seq, gsz=1280):
    t, d = x.shape
    ne = w1.shape[0]
    w_router_p = jnp.pad(w_router.astype(F32), ((0, 0), (0, V7X_LANES - ne)))
    h, meta, meta_t, cnt = moe_route(x, norm_g, w_router_p)
    counts = cnt[0, :ne].astype(jnp.int32)
    ngrp_e = (counts + gsz - 1) // gsz
    grp_end = jnp.cumsum(ngrp_e)
    grp_start = grp_end - ngrp_e
    ng = (t * TOP_K) // gsz + ne
    j = jnp.arange(ng, dtype=jnp.int32)
    total = grp_end[-1]
    last = jnp.maximum(total - 1, 0)
    jj = jnp.minimum(j, last)
    ge = jnp.searchsorted(grp_end, jj, side="right").astype(jnp.int32)
    ge = jnp.minimum(ge, ne - 1)
    rows = jnp.clip(counts[ge] - (jj - grp_start[ge]) * gsz, 0, gsz)
    gr = jnp.where(j < total, rows, 0).astype(jnp.int32)
    gb = j

    def dest_row(e, rank):
        off = jnp.zeros_like(rank)
        for k in range(ne):
            off = jnp.where(e == k, grp_start[k] * gsz, off)
        return off + rank

    pos0 = dest_row(meta_t[0].astype(jnp.int32), meta_t[4].astype(jnp.int32))
    pos1 = dest_row(meta_t[1].astype(jnp.int32), meta_t[5].astype(jnp.int32))
    xs = moe_dispatch(h, pos0, pos1, gr, gsz)
    ys = ffn_groups(xs, w1, w3, w2, ge, gb, gr, gsz=gsz, packed=True, ragged=True)
    return moe_final(x, ys, pos0, pos1, meta, final_g, bsz=bsz, seq=seq)


def kernel(x, norm_mix, norm_ffn, norm_final, even_w_in, ssd_conv_w, ssd_conv_b, ssd_dt_bias, ssd_a_log, ssd_d, ssd_norm_w, conf_dw_w, conf_dw_b, conf_ln_w, conf_ln_b, even_w_out, ffn_w1, ffn_w3, ffn_w2, attn_w_qkv, attn_w_out, rel_bias, moe_router, moe_w1, moe_w3, moe_w2):
    bsz, seq, d = x.shape
    assert norm_mix.shape[0] == 2, "two-layer trunk: one even (SSD+Conformer/FFN) and one odd (attention/MoE) layer"
    xf = x.reshape(bsz * seq, d)
    h = rmsnorm(xf, norm_mix[0])
    p_even = dict(w_in=even_w_in[0], conv_w=ssd_conv_w[0], conv_b=ssd_conv_b[0], dt_bias=ssd_dt_bias[0],
                  a_log=ssd_a_log[0], d_skip=ssd_d[0], ssd_norm_w=ssd_norm_w[0], conf_dw_w=conf_dw_w[0],
                  conf_dw_b=conf_dw_b[0], conf_ln_w=conf_ln_w[0], conf_ln_b=conf_ln_b[0], w_out=even_w_out[0])
    x1 = even_layer(xf, h, p_even, bsz=bsz, seq=seq)
    h1 = rmsnorm(x1, norm_ffn[0])
    y_ffn = dense_ffn(h1, ffn_w1[0], ffn_w3[0], ffn_w2[0])
    x2, h2 = add_norm_to_residue_major(x1, y_ffn, norm_mix[1], bsz=bsz, seq=seq, nres=RES)
    p_odd = dict(w_qkv=attn_w_qkv[0], w_attn_out=attn_w_out[0], rel_bias=rel_bias)
    x3 = odd_layer_attn(x2, h2, p_odd, bsz=bsz, seq=seq)
    out = moe_layer(x3, norm_ffn[1], moe_router[0], moe_w1[0], moe_w3[0], moe_w2[0], norm_final, bsz=bsz, seq=seq)
    return out.reshape(bsz, seq, d)
```

```python
import functools
import math

import numpy as np
import jax
import jax.numpy as jnp
from jax import lax
from jax.experimental import pallas as pl
from jax.experimental.pallas import tpu as pltpu

SSD_HEAD_DIM = 64
SSD_GROUPS = 4
SSD_STATE = 128
SSD_CONV = 4
SSD_CHUNK = 128
CONF_KERNEL = 31
ATTN_HEADS = 16
DILATED_PATTERNS = ((128, 1), (512, 4), (2048, 16))
ATTN_BLOCK = 128
REL_BUCKETS = 32
REL_MAX_DIST = 2048
N_EXPERTS = 8
TOP_K = 2
EPS = 1e-6
LOG2E = math.log2(math.e)

V7X_LANES = 128
V7X_VMEM_BYTES = 64 * 1024 * 1024
VMEM_LIMIT = 60 * 1024 * 1024

F32 = jnp.float32
BF16 = jnp.bfloat16
NEG_INF = float("-inf")


def _cparams(*sem):
    return pltpu.CompilerParams(dimension_semantics=tuple(sem), vmem_limit_bytes=VMEM_LIMIT)


def _rms(x, g):
    ms = jnp.mean(x * x, axis=-1, keepdims=True)
    return x * lax.rsqrt(ms + EPS) * g


def _rmsnorm_kernel(x_ref, g_ref, h_ref):
    h_ref[...] = _rms(x_ref[...], g_ref[...]).astype(h_ref.dtype)


def rmsnorm(x, g, out_dtype=BF16, tm=512):
    t, d = x.shape
    return pl.pallas_call(
        _rmsnorm_kernel,
        out_shape=jax.ShapeDtypeStruct((t, d), out_dtype),
        grid=(t // tm,),
        in_specs=[pl.BlockSpec((tm, d), lambda i: (i, 0)), pl.BlockSpec((1, d), lambda i: (0, 0))],
        out_specs=pl.BlockSpec((tm, d), lambda i: (i, 0)),
        compiler_params=_cparams("parallel"),
        name="rmsnorm",
    )(x, g.reshape(1, d))


def _add_norm_res_kernel(x_ref, y_ref, g_ref, xo_ref, ho_ref, xs_scr, hs_scr):
    nres, tp, d = xo_ref.shape
    xn = x_ref[...] + y_ref[...].astype(F32)
    h = _rms(xn, g_ref[...])
    for s in range(d // V7X_LANES):
        ls = slice(s * V7X_LANES, (s + 1) * V7X_LANES)
        xs_scr[s] = xn[:, ls]
        hs_scr[s] = h[:, ls]
    for r in range(nres):
        rows = pl.ds(r, tp, stride=nres)
        for s in range(d // V7X_LANES):
            ls = slice(s * V7X_LANES, (s + 1) * V7X_LANES)
            xo_ref[r, :, ls] = xs_scr[s, rows, :]
            ho_ref[r, :, ls] = hs_scr[s, rows, :].astype(ho_ref.dtype)


def add_norm_to_residue_major(x, y, g, *, bsz, seq, nres, tp=128):
    t, d = x.shape
    lsub = seq // nres
    tm = nres * tp
    nj = lsub // tp
    slab = pltpu.VMEM((d // V7X_LANES, tm, V7X_LANES), F32)
    xo, ho = pl.pallas_call(
        _add_norm_res_kernel,
        out_shape=(jax.ShapeDtypeStruct((bsz * nres, lsub, d), F32), jax.ShapeDtypeStruct((bsz * nres, lsub, d), BF16)),
        grid=(bsz, nj),
        in_specs=[pl.BlockSpec((tm, d), lambda b, j: (b * nj + j, 0)), pl.BlockSpec((tm, d), lambda b, j: (b * nj + j, 0)),
                  pl.BlockSpec((1, d), lambda b, j: (0, 0))],
        out_specs=(pl.BlockSpec((nres, tp, d), lambda b, j: (b, j, 0)), pl.BlockSpec((nres, tp, d), lambda b, j: (b, j, 0))),
        scratch_shapes=[slab, slab],
        compiler_params=_cparams("parallel", "parallel"),
        name="add_norm_reorder",
    )(x, y, g.reshape(1, d))
    return xo.reshape(t, d), ho.reshape(t, d)


def _mm_kernel(*refs, n_lhs, ks, has_res):
    xs = refs[:n_lhs]
    w_ref = refs[n_lhs]
    r_ref = refs[n_lhs + 1] if has_res else None
    o_ref = refs[-1]
    acc = None
    k0 = 0
    for x_ref, k in zip(xs, ks):
        part = jnp.dot(x_ref[...], w_ref[k0:k0 + k, :].astype(BF16), preferred_element_type=F32)
        acc = part if acc is None else acc + part
        k0 += k
    if has_res:
        acc = acc + r_ref[...]
    o_ref[...] = acc.astype(o_ref.dtype)


def matmul(xs, w, *, n_cols, tn, tm=1024, res=None, out_dtype=BF16, name="matmul"):
    t = xs[0].shape[0]
    ks = tuple(x.shape[1] for x in xs)
    ktot = sum(ks)
    assert w.shape[0] == ktot and n_cols % tn == 0 and t % tm == 0
    in_specs = [pl.BlockSpec((tm, k), lambda i, j: (i, 0)) for k in ks]
    in_specs.append(pl.BlockSpec((ktot, tn), lambda i, j: (0, j)))
    args = list(xs) + [w]
    if res is not None:
        in_specs.append(pl.BlockSpec((tm, tn), lambda i, j: (i, j)))
        args.append(res)
    return pl.pallas_call(
        functools.partial(_mm_kernel, n_lhs=len(xs), ks=ks, has_res=res is not None),
        out_shape=jax.ShapeDtypeStruct((t, n_cols), out_dtype),
        grid=(t // tm, n_cols // tn),
        in_specs=in_specs,
        out_specs=pl.BlockSpec((tm, tn), lambda i, j: (i, j)),
        compiler_params=_cparams("parallel", "arbitrary"),
        name=name,
    )(*args)


def _mm_nt_kernel(x_ref, wt_ref, o_ref):
    wt = wt_ref[...].astype(BF16)
    acc = lax.dot_general(x_ref[...], wt, (((1,), (1,)), ((), ())), preferred_element_type=F32)
    o_ref[...] = acc.astype(o_ref.dtype)


def matmul_nt(x, w_t, *, row_start, n_cols, tn, tm=1024, out_dtype=BF16, name="matmul_nt"):
    t, k = x.shape
    assert w_t.shape[1] == k and n_cols % tn == 0 and t % tm == 0
    if row_start % tn == 0:
        w_spec = pl.BlockSpec((tn, k), lambda i, j: (row_start // tn + j, 0))
    else:
        assert row_start % SUBLANES == 0 and tn % SUBLANES == 0
        w_spec = pl.BlockSpec((pl.Element(tn), pl.Element(k)),
                              lambda i, j: ((row_start // SUBLANES + j * (tn // SUBLANES)) * SUBLANES, 0))
    return pl.pallas_call(
        _mm_nt_kernel,
        out_shape=jax.ShapeDtypeStruct((t, n_cols), out_dtype),
        grid=(t // tm, n_cols // tn),
        in_specs=[pl.BlockSpec((tm, k), lambda i, j: (i, 0)), w_spec],
        out_specs=pl.BlockSpec((tm, tn), lambda i, j: (i, j)),
        compiler_params=_cparams("parallel", "arbitrary"),
        name=name,
    )(x, w_t)


HALO_BF16 = 16


def _ssd_prep_kernel(raw_ref, bias_ref, alog_ref, dt_ref, acs_ref, dtt_ref, acst_ref, *, seq):
    raw = raw_ref[...] + bias_ref[...]
    dt = jnp.maximum(raw, 0.0) + jnp.log1p(jnp.exp(-jnp.abs(raw)))
    a = -jnp.exp(alog_ref[...])
    x = dt * a
    row = lax.broadcasted_iota(jnp.int32, x.shape, 0) % SSD_CHUNK
    sh = 1
    while sh < SSD_CHUNK:
        x = x + jnp.where(row >= sh, pltpu.roll(x, sh, 0), 0.0)
        sh *= 2
    dt_ref[...] = dt
    acs_ref[...] = x
    for c in range(seq // SSD_CHUNK):
        sl = slice(c * SSD_CHUNK, (c + 1) * SSD_CHUNK)
        dtt_ref[sl, :] = dt[sl, :].T
        acst_ref[sl, :] = x[sl, :].T


def ssd_prep(dt_raw, dt_bias_p, a_log_p, *, bsz, seq):
    t = dt_raw.shape[0]
    shp = jax.ShapeDtypeStruct((t, V7X_LANES), F32)
    blk = pl.BlockSpec((seq, V7X_LANES), lambda b: (b, 0))
    vec = pl.BlockSpec((1, V7X_LANES), lambda b: (0, 0))
    return pl.pallas_call(
        functools.partial(_ssd_prep_kernel, seq=seq),
        out_shape=(shp, shp, shp, shp),
        grid=(bsz,),
        in_specs=[blk, vec, vec],
        out_specs=(blk, blk, blk, blk),
        compiler_params=_cparams("parallel"),
        name="ssd_prep",
    )(dt_raw, dt_bias_p, a_log_p)


def _conv_silu(cur_ref, halo_ref, w_ref, b_ref, buf_ref, col0, first):
    rows, c = cur_ref.shape
    cols = slice(col0, col0 + c)
    buf_ref[0:HALO_BF16, :] = jnp.where(first, 0.0, halo_ref[...].astype(F32))
    buf_ref[HALO_BF16:HALO_BF16 + rows, :] = cur_ref[...].astype(F32)
    acc = b_ref[:, cols]
    for k in range(SSD_CONV):
        off = HALO_BF16 - (SSD_CONV - 1) + k
        acc = acc + w_ref[k:k + 1, cols] * buf_ref[off:off + rows, :]
    return acc * jax.nn.sigmoid(acc)


def _ssd_main_kernel(xr_ref, xh_ref, br_ref, bh_ref, cr_ref, ch_ref, cw_ref, cbias_ref,
                     z_ref, dt_ref, acs_ref, dtt_ref, acst_ref, dsk_ref, nw_ref,
                     y_ref, st_ref, xbuf, bbuf, cbuf, *, heads_per_group):
    ci = pl.program_id(1)
    L = SSD_CHUNK
    gw = heads_per_group * SSD_HEAD_DIM
    npair = heads_per_group // 2
    d_ssd = xr_ref.shape[1]
    bc_w = br_ref.shape[1]

    @pl.when(ci == 0)
    def _():
        st_ref[...] = jnp.zeros_like(st_ref)

    first = ci == 0
    x_all = _conv_silu(xr_ref, xh_ref, cw_ref, cbias_ref, xbuf, 0, first)
    b_all = _conv_silu(br_ref, bh_ref, cw_ref, cbias_ref, bbuf, d_ssd, first)
    c_all = _conv_silu(cr_ref, ch_ref, cw_ref, cbias_ref, cbuf, d_ssd + bc_w, first)
    dt = dt_ref[...]
    acs = acs_ref[...]
    dtt = dtt_ref[...]
    acst = acst_ref[...]
    row = lax.broadcasted_iota(jnp.int32, (L, L), 0)
    col = lax.broadcasted_iota(jnp.int32, (L, L), 1)
    causal = col <= row
    lane_lo = lax.broadcasted_iota(jnp.int32, (L, 2 * SSD_HEAD_DIM), 1) < SSD_HEAD_DIM
    lane_lo1 = lax.broadcasted_iota(jnp.int32, (1, 2 * SSD_HEAD_DIM), 1) < SSD_HEAD_DIM
    zero_b = jnp.zeros((L, 2 * SSD_HEAD_DIM), BF16)

    for g in range(SSD_GROUPS):
        bmf = b_all[:, g * SSD_STATE:(g + 1) * SSD_STATE]
        bm = bmf.astype(BF16)
        cm = c_all[:, g * SSD_STATE:(g + 1) * SSD_STATE].astype(BF16)
        bmt = bmf.T.astype(BF16)
        cb = lax.dot_general(cm, bm, (((1,), (1,)), ((), ())), preferred_element_type=F32)
        xg = x_all[:, g * gw:(g + 1) * gw]
        y_diags, e_ls, w_ls, cds = [], [], [], []
        for q in range(npair):
            xp = xg[:, q * 2 * SSD_HEAD_DIM:(q + 1) * 2 * SSD_HEAD_DIM].astype(BF16)
            ms = []
            for hh in range(2):
                h = g * heads_per_group + 2 * q + hh
                seg = acs[:, h:h + 1] - acst[h:h + 1, :]
                dec = jnp.exp(jnp.where(causal, seg, NEG_INF))
                ms.append((cb * dec * dtt[h:h + 1, :]).astype(BF16))
            h0 = g * heads_per_group + 2 * q
            h1 = h0 + 1
            lhs = jnp.concatenate(ms, axis=1)
            rhs = jnp.concatenate([jnp.where(lane_lo, xp, zero_b), jnp.where(lane_lo, zero_b, xp)], axis=0)
            y_diags.append(jnp.dot(lhs, rhs, preferred_element_type=F32))
            last0 = acs[L - 1:L, h0:h0 + 1]
            last1 = acs[L - 1:L, h1:h1 + 1]
            e_ls.append(jnp.where(lane_lo, jnp.exp(acs[:, h0:h0 + 1]), jnp.exp(acs[:, h1:h1 + 1])))
            w_ls.append(jnp.where(lane_lo, dt[:, h0:h0 + 1] * jnp.exp(last0 - acs[:, h0:h0 + 1]),
                                  dt[:, h1:h1 + 1] * jnp.exp(last1 - acs[:, h1:h1 + 1])))
            cds.append(jnp.where(lane_lo1, jnp.exp(last0), jnp.exp(last1)))
        st = st_ref[g]
        y_off = jnp.dot(cm, st.astype(BF16), preferred_element_type=F32) * jnp.concatenate(e_ls, axis=1)
        xw = (xg * jnp.concatenate(w_ls, axis=1)).astype(BF16)
        st_ref[g] = st * jnp.concatenate(cds, axis=1) + jnp.dot(bmt, xw, preferred_element_type=F32)
        y = jnp.concatenate(y_diags, axis=1) + y_off + xg * dsk_ref[:, g * gw:(g + 1) * gw]
        zg = z_ref[:, g * gw:(g + 1) * gw].astype(F32)
        yg = y * (zg * jax.nn.sigmoid(zg))
        ms_ = jnp.mean(yg * yg, axis=-1, keepdims=True)
        y_ref[:, g * gw:(g + 1) * gw] = (yg * lax.rsqrt(ms_ + EPS) * nw_ref[:, g * gw:(g + 1) * gw]).astype(y_ref.dtype)


def ssd_main(zx, conv_w, conv_b, dt, acs, dtt, acst, dskip_e, norm_w, *, bsz, seq, d_ssd):
    t = zx.shape[0]
    nc = seq // SSD_CHUNK
    heads = d_ssd // SSD_HEAD_DIM
    hpg = heads // SSD_GROUPS
    bc_w = SSD_GROUPS * SSD_STATE
    conv_dim = conv_w.shape[1]
    assert d_ssd % bc_w == 0 and hpg % 2 == 0 and conv_dim == d_ssd + 2 * bc_w
    rowmap = lambda b, c: (b * nc + c, 0)
    hb = SSD_CHUNK // HALO_BF16

    def cur(width, col_block):
        return pl.BlockSpec((SSD_CHUNK, width), lambda b, c: (b * nc + c, col_block))

    def halo(width, col_block):
        return pl.BlockSpec((HALO_BF16, width), lambda b, c: (jnp.maximum((b * nc + c) * hb - 1, 0), col_block))

    xcol, bcol, ccol = 1, 2 * d_ssd // bc_w, 2 * d_ssd // bc_w + 1
    return pl.pallas_call(
        functools.partial(_ssd_main_kernel, heads_per_group=hpg),
        out_shape=jax.ShapeDtypeStruct((t, d_ssd), BF16),
        grid=(bsz, nc),
        in_specs=[
            cur(d_ssd, xcol), halo(d_ssd, xcol), cur(bc_w, bcol), halo(bc_w, bcol), cur(bc_w, ccol), halo(bc_w, ccol),
            pl.BlockSpec((SSD_CONV, conv_dim), lambda b, c: (0, 0)),
            pl.BlockSpec((1, conv_dim), lambda b, c: (0, 0)),
            pl.BlockSpec((SSD_CHUNK, d_ssd), rowmap),
            pl.BlockSpec((SSD_CHUNK, V7X_LANES), rowmap),
            pl.BlockSpec((SSD_CHUNK, V7X_LANES), rowmap),
            pl.BlockSpec((SSD_CHUNK, V7X_LANES), rowmap),
            pl.BlockSpec((SSD_CHUNK, V7X_LANES), rowmap),
            pl.BlockSpec((1, d_ssd), lambda b, c: (0, 0)),
            pl.BlockSpec((1, d_ssd), lambda b, c: (0, 0)),
        ],
        out_specs=pl.BlockSpec((SSD_CHUNK, d_ssd), rowmap),
        scratch_shapes=[pltpu.VMEM((SSD_GROUPS, SSD_STATE, hpg * SSD_HEAD_DIM), F32),
                        pltpu.VMEM((SSD_CHUNK + HALO_BF16, d_ssd), F32),
                        pltpu.VMEM((SSD_CHUNK + HALO_BF16, bc_w), F32),
                        pltpu.VMEM((SSD_CHUNK + HALO_BF16, bc_w), F32)],
        compiler_params=_cparams("parallel", "arbitrary"),
        name="ssd_main",
    )(zx, zx, zx, zx, zx, zx, conv_w, conv_b.reshape(1, conv_dim), zx, dt, acs, dtt, acst, dskip_e,
      norm_w.reshape(1, d_ssd))


CONF_HALO = 32


SUBLANES = 8


def _conf_kernel(a_ref, g_ref, ah_ref, gh_ref, w_ref, b_ref, lw_ref, lb_ref, o_ref, buf_ref, sh_ref, *, ts, rc, lt):
    i = pl.program_id(1)
    d = a_ref.shape[1]
    ah = ah_ref[...].astype(F32)
    gh = gh_ref[...].astype(F32)
    buf_ref[0:CONF_HALO, :] = jnp.where(i > 0, ah * jax.nn.sigmoid(gh), 0.0)
    a = a_ref[...].astype(F32)
    g = g_ref[...].astype(F32)
    buf_ref[CONF_HALO:CONF_HALO + ts, :] = a * jax.nn.sigmoid(g)
    span = ts + CONF_HALO - SUBLANES
    for j in range(1, SUBLANES):
        sh_ref[j - 1, 0:span, :] = buf_ref[j:j + span, :]
    base = CONF_HALO - (CONF_KERNEL - 1)
    outs = []
    for r in range(ts // rc):
        cols = []
        for c in range(d // lt):
            ls = slice(c * lt, (c + 1) * lt)
            acc = jnp.broadcast_to(b_ref[:, ls], (rc, lt))
            for k in range(CONF_KERNEL):
                j = (base + k) % SUBLANES
                off = r * rc + (base + k) - j
                src = buf_ref[off:off + rc, ls] if j == 0 else sh_ref[j - 1, off:off + rc, ls]
                acc = acc + w_ref[k:k + 1, ls] * src
            cols.append(acc)
        outs.append(jnp.concatenate(cols, axis=1))
    u = jnp.concatenate(outs, axis=0)
    mu = jnp.mean(u, axis=-1, keepdims=True)
    uc = u - mu
    var = jnp.mean(uc * uc, axis=-1, keepdims=True)
    y = uc * lax.rsqrt(var + EPS) * lw_ref[...] + lb_ref[...]
    o_ref[...] = (y * jax.nn.sigmoid(y)).astype(o_ref.dtype)


def conf_module(conf, dw_w, dw_b, ln_w, ln_b, *, bsz, seq, ts=256, rc=32, lt=512):
    t = conf.shape[0]
    d = dw_w.shape[1]
    nsb = seq // ts
    hb = ts // CONF_HALO
    cur = lambda col: pl.BlockSpec((ts, d), lambda b, i: (b * nsb + i, col))
    halo = lambda col: pl.BlockSpec((CONF_HALO, d), lambda b, i: (jnp.maximum((b * nsb + i) * hb - 1, 0), col))
    vec = pl.BlockSpec((1, d), lambda b, i: (0, 0))
    return pl.pallas_call(
        functools.partial(_conf_kernel, ts=ts, rc=rc, lt=lt),
        out_shape=jax.ShapeDtypeStruct((t, d), BF16),
        grid=(bsz, nsb),
        in_specs=[cur(0), cur(1), halo(0), halo(1), pl.BlockSpec((CONF_KERNEL, d), lambda b, i: (0, 0)),
                  vec, vec, vec],
        out_specs=pl.BlockSpec((ts, d), lambda b, i: (b * nsb + i, 0)),
        scratch_shapes=[pltpu.VMEM((ts + CONF_HALO, d), F32), pltpu.VMEM((SUBLANES - 1, ts + CONF_HALO, d), F32)],
        compiler_params=_cparams("parallel", "parallel"),
        name="conf_module",
    )(conf, conf, conf, conf, dw_w, dw_b.reshape(1, d), ln_w.reshape(1, d), ln_b.reshape(1, d))


FFN_SUB = 128


FFN_NCHUNK = 512


def pack_bf16_pair(a, b):
    ua = lax.bitcast_convert_type(a.astype(BF16).astype(F32), jnp.uint32)
    ub = lax.bitcast_convert_type(b.astype(BF16).astype(F32), jnp.uint32)
    return (ua >> 16) | (ub & jnp.uint32(0xFFFF0000))


def unpack_bf16_pair(u):
    a = lax.bitcast_convert_type(u << 16, F32)
    b = lax.bitcast_convert_type(u & jnp.uint32(0xFFFF0000), F32)
    return a, b


def _ffn_kernel(ge_ref, gb_ref, gr_ref, x_ref, w1_ref, w3_ref, w2_ref, o_ref, acc_ref, *xb_scr, nf, gsz, packed):
    g = pl.program_id(0)
    f = pl.program_id(1)
    rows = gr_ref[g]
    valid = rows > 0
    d = acc_ref.shape[1]

    @pl.when(valid & (f == 0))
    def _():
        acc_ref[...] = jnp.zeros_like(acc_ref)
        if packed:
            a, b = unpack_bf16_pair(x_ref[...])
            xb_scr[0][:, 0:d // 2] = a.astype(BF16)
            xb_scr[0][:, d // 2:d] = b.astype(BF16)

    xsrc = xb_scr[0] if packed else x_ref

    def slab(r0, nrows, w1, w3, w2):
        xs = xsrc[pl.ds(r0, nrows), :]
        h1 = jnp.dot(xs, w1, preferred_element_type=F32)
        h3 = jnp.dot(xs, w3, preferred_element_type=F32)
        hh = (h1 * jax.nn.sigmoid(h1) * h3).astype(BF16)
        for c in range(d // FFN_NCHUNK):
            cs = slice(c * FFN_NCHUNK, (c + 1) * FFN_NCHUNK)
            acc_ref[pl.ds(r0, nrows), cs] += jnp.dot(hh, w2[:, cs], preferred_element_type=F32)

    nslab = (rows + FFN_SUB - 1) // FFN_SUB
    for ns in (range(1, gsz // FFN_SUB + 1) if packed else (gsz // FFN_SUB,)):
        @pl.when(nslab == ns)
        def _(ns=ns):
            slab(0, ns * FFN_SUB, w1_ref[...].astype(BF16), w3_ref[...].astype(BF16), w2_ref[...].astype(BF16))

    @pl.when(jnp.logical_not(valid) & (f == 0))
    def _():
        o_ref[...] = jnp.zeros_like(o_ref)

    @pl.when(valid & (f == nf - 1))
    def _():
        if packed:
            o_ref[...] = pack_bf16_pair(acc_ref[:, 0:d // 2], acc_ref[:, d // 2:d])
        else:
            o_ref[...] = acc_ref[...].astype(o_ref.dtype)


def ffn_groups(xs, w1, w3, w2, grp_expert, grp_block, grp_rows, *, gsz=1024, tf=256, packed=False, out_buffers=2):
    e, d, ff = w1.shape
    r, dx = xs.shape
    nf = ff // tf
    ng = grp_expert.shape[0]

    def fmap(f, g, gr):
        return jnp.where(gr[g] > 0, f, nf - 1)

    scratch = [pltpu.VMEM((gsz, d), F32)]
    if packed:
        scratch.append(pltpu.VMEM((gsz, d), BF16))
    grid_spec = pltpu.PrefetchScalarGridSpec(
        num_scalar_prefetch=3,
        grid=(ng, nf),
        in_specs=[
            pl.BlockSpec((gsz, dx), lambda g, f, ge, gb, gr: (gb[g], 0)),
            pl.BlockSpec((None, d, tf), lambda g, f, ge, gb, gr: (ge[g], 0, fmap(f, g, gr))),
            pl.BlockSpec((None, d, tf), lambda g, f, ge, gb, gr: (ge[g], 0, fmap(f, g, gr))),
            pl.BlockSpec((None, tf, d), lambda g, f, ge, gb, gr: (ge[g], fmap(f, g, gr), 0)),
        ],
        out_specs=pl.BlockSpec((gsz, dx), lambda g, f, ge, gb, gr: (gb[g], 0),
                               **({} if out_buffers == 2 else dict(pipeline_mode=pl.Buffered(out_buffers)))),
        scratch_shapes=scratch,
    )
    return pl.pallas_call(
        functools.partial(_ffn_kernel, nf=nf, gsz=gsz, packed=packed),
        out_shape=jax.ShapeDtypeStruct((r, dx), xs.dtype),
        grid_spec=grid_spec,
        compiler_params=_cparams("arbitrary", "arbitrary"),
        name="ffn_groups",
    )(grp_expert, grp_block, grp_rows, xs, w1, w3, w2)


def _row_copy(src_ref, src_row, dst_ref, dst_row, sem):
    return pltpu.make_async_copy(src_ref.at[pl.ds(src_row, 1), :], dst_ref.at[pl.ds(dst_row, 1), :], sem)


DMA_UNROLL = 8


def _dispatch_kernel(pos0_ref, pos1_ref, gr_ref, h_ref, xs_ref, zbuf, sem, zsem, *, gsz, ng):
    tm = h_ref.shape[0]
    step = pl.program_id(0)
    base = step * tm

    @pl.when(step == 0)
    def _():
        zbuf[...] = jnp.zeros_like(zbuf)
        for start in (True, False):
            for g in range(ng):
                rows = gr_ref[g]
                for c in range(gsz // FFN_SUB):
                    @pl.when(rows < (c + 1) * FFN_SUB)
                    def _(g=g, c=c, start=start):
                        cp = pltpu.make_async_copy(zbuf, xs_ref.at[pl.ds(g * gsz + c * FFN_SUB, FFN_SUB), :], zsem)
                        if start:
                            cp.start()
                        else:
                            cp.wait()

    def copies(i):
        return [_row_copy(h_ref, i, xs_ref, p[base + i], sem) for p in (pos0_ref, pos1_ref)]

    def issue(i, c):
        for cp in copies(i):
            cp.start()
        return c

    lax.fori_loop(0, tm, issue, 0, unroll=DMA_UNROLL)

    def drain(i, c):
        for cp in copies(i):
            cp.wait()
        return c

    lax.fori_loop(0, tm, drain, 0, unroll=DMA_UNROLL)


def moe_dispatch(hp, pos0, pos1, grp_rows, gsz, tm=512):
    t, w = hp.shape
    ng = grp_rows.shape[0]
    grid_spec = pltpu.PrefetchScalarGridSpec(
        num_scalar_prefetch=3,
        grid=(t // tm,),
        in_specs=[pl.BlockSpec((tm, w), lambda i, p0, p1, gr: (i, 0))],
        out_specs=pl.BlockSpec(memory_space=pl.ANY),
        scratch_shapes=[pltpu.VMEM((FFN_SUB, w), hp.dtype), pltpu.SemaphoreType.DMA, pltpu.SemaphoreType.DMA],
    )
    return pl.pallas_call(
        functools.partial(_dispatch_kernel, gsz=gsz, ng=ng),
        out_shape=jax.ShapeDtypeStruct((ng * gsz, w), hp.dtype),
        grid_spec=grid_spec,
        compiler_params=_cparams("arbitrary"),
        name="moe_dispatch",
    )(pos0, pos1, grp_rows, hp)


def _t5_bucket_np(dist):
    max_exact = REL_BUCKETS // 2
    d_f = np.maximum(dist, 1).astype(np.float32)
    large = max_exact + (np.log(d_f / np.float32(max_exact)) / np.float32(math.log(REL_MAX_DIST / max_exact))
                         * np.float32(REL_BUCKETS - max_exact)).astype(np.int32)
    large = np.minimum(large, REL_BUCKETS - 1)
    return np.where(dist < max_exact, dist, large)


RES = 4
TQ = ATTN_BLOCK
PIECE = TQ // RES


def _bucket_index_tables():
    assert DILATED_PATTERNS == ((128, 1), (512, 4), (2048, 16)) and RES == 4 and TQ == 128

    def fin(steps, dil, n_back=TQ):
        ok = (steps >= 0) & (steps <= n_back)
        return np.where(ok, _t5_bucket_np(np.clip(steps, 0, n_back) * dil), -1).astype(np.int32)

    i = np.arange(TQ)[:, None]
    t4 = fin(i + TQ - np.arange(2 * TQ)[None, :], 4)
    t16 = fin(i - np.arange(TQ)[None, :], 16)
    rq, mq = i // PIECE, i % PIECE
    jk = np.arange(2 * TQ)[None, :]
    rk, mk = jk // (2 * PIECE), jk % (2 * PIECE)
    t1 = fin(RES * (PIECE + mq - mk) + (rq - rk), 1)
    t1first = fin(RES * (mq - mk) + (rq - rk), 1)
    return t4, t1, t1first, t16


def _bias_tab_kernel(rel_ref, *refs):
    n = len(refs) // 2
    h = pl.program_id(0)
    for idx_ref, o_ref in zip(refs[:n], refs[n:]):
        idx = idx_ref[...]
        tab = jnp.full(idx.shape, NEG_INF, F32)
        for b in range(REL_BUCKETS):
            tab = jnp.where(idx == b, rel_ref[b, h] * LOG2E, tab)
        o_ref[0] = tab


def attn_bias_tables(rel_bias):
    idx_tabs = [jnp.asarray(t) for t in _bucket_index_tables()]
    nh = rel_bias.shape[1]
    return pl.pallas_call(
        _bias_tab_kernel,
        out_shape=tuple(jax.ShapeDtypeStruct((nh,) + t.shape, F32) for t in idx_tabs),
        grid=(nh,),
        in_specs=[pl.BlockSpec(memory_space=pltpu.SMEM)] + [pl.BlockSpec(t.shape, lambda h: (0, 0)) for t in idx_tabs],
        out_specs=tuple(pl.BlockSpec((1,) + t.shape, lambda h: (h, 0, 0)) for t in idx_tabs),
        compiler_params=_cparams("parallel"),
        name="attn_bias_tables",
    )(rel_bias.astype(F32), *idx_tabs)


def _attn_kernel(q_ref, k_ref, v_ref, t4_ref, t1_ref, t1f_ref, t16_ref, o_ref, acc, m_s, l_s, qs, ks, vs, od, md, ld,
                 kb, *, scale):
    nres, lsub, hd = q_ref.shape
    nt = (((1,), (1,)), ((), ()))

    def tile(q, k, v, bias):
        s = lax.dot_general(q, k, nt, preferred_element_type=F32) + bias
        m = jnp.max(s, axis=-1, keepdims=True)
        p = jnp.exp2(s - m)
        l = jnp.sum(p, axis=-1, keepdims=True)
        o = jnp.dot(p.astype(BF16), v, preferred_element_type=F32)
        return o, jnp.broadcast_to(m, o.shape), jnp.broadcast_to(l, o.shape)

    def merge(r, rows, o, m, l):
        m0 = m_s[r, rows, :]
        mn = jnp.maximum(m0, m)
        a = jnp.exp2(m0 - mn)
        b = jnp.exp2(m - mn)
        acc[r, rows, :] = acc[r, rows, :] * a + o * b
        l_s[r, rows, :] = l_s[r, rows, :] * a + l * b
        m_s[r, rows, :] = mn

    nsub = lsub // TQ
    for r in range(nres):
        qs[r] = q_ref[r].astype(F32)
        kf = k_ref[r].astype(F32) * (scale * LOG2E)
        ks[r] = kf
        kb[r] = kf.astype(BF16)
        vs[r] = v_ref[r].astype(F32)
    masked = jnp.full((TQ, TQ), NEG_INF, F32)
    bias16 = (jnp.concatenate([t16_ref[0], masked], axis=1), jnp.concatenate([masked, t16_ref[0]], axis=1))
    for r in range(nres):
        for pair in range(nsub // 2):
            sub = [pl.ds(2 * pair + e, TQ, stride=nsub) for e in range(2)]
            kk = jnp.concatenate([ks[r, s, :] for s in sub], axis=0).astype(BF16)
            vv = jnp.concatenate([vs[r, s, :] for s in sub], axis=0).astype(BF16)
            for e in range(2):
                o, m, l = tile(qs[r, sub[e], :].astype(BF16), kk, vv, bias16[e])
                od[r * nsub + 2 * pair + e] = o
                md[r * nsub + 2 * pair + e] = m
                ld[r * nsub + 2 * pair + e] = l

    for r in range(nres):
        for i in range(lsub // TQ):
            rows = pl.ds(i * TQ, TQ)
            if i == 0:
                krows = pl.ds(0, 2 * TQ)
                bias = jnp.concatenate([t4_ref[0, :, TQ:2 * TQ], masked], axis=1)
            else:
                krows = pl.ds((i - 1) * TQ, 2 * TQ)
                bias = t4_ref[0]
            o, m, l = tile(q_ref[r, rows, :], kb[r, krows, :], v_ref[r, krows, :], bias)
            acc[r, rows, :] = o
            m_s[r, rows, :] = m
            l_s[r, rows, :] = l

    for j in range(lsub // PIECE):
        q = jnp.concatenate([q_ref[r, pl.ds(j * PIECE, PIECE), :] for r in range(nres)], axis=0)
        if j == 0:
            krows, bias = pl.ds(0, 2 * PIECE), t1f_ref[0]
        else:
            krows, bias = pl.ds((j - 1) * PIECE, 2 * PIECE), t1_ref[0]
        k = jnp.concatenate([kb[r, krows, :] for r in range(nres)], axis=0)
        v = jnp.concatenate([v_ref[r, krows, :] for r in range(nres)], axis=0)
        o, m, l = tile(q, k, v, bias)
        for r in range(nres):
            part = slice(r * PIECE, (r + 1) * PIECE)
            merge(r, pl.ds(j * PIECE, PIECE), o[part], m[part], l[part])

    for r in range(nres):
        for c in range(nsub):
            merge(r, pl.ds(c, TQ, stride=nsub), od[r * nsub + c], md[r * nsub + c], ld[r * nsub + c])

    for r in range(nres):
        o_ref[r] = (acc[r] / l_s[r]).astype(o_ref.dtype)


def _attn_heads_kernel(q_ref, k_ref, v_ref, t4_ref, t1_ref, t1f_ref, t16_ref, o_ref, *scratch, scale, hd):
    for hh in range(q_ref.shape[2] // hd):
        lanes = pl.ds(hh * hd, hd)
        one = pl.ds(hh, 1)
        _attn_kernel(q_ref.at[:, :, lanes], k_ref.at[:, :, lanes], v_ref.at[:, :, lanes],
                     t4_ref.at[one], t1_ref.at[one], t1f_ref.at[one], t16_ref.at[one],
                     o_ref.at[:, :, lanes], *scratch, scale=scale)


ATTN_HEADS_PER_STEP = 2


def attention(qkv, tabs, *, bsz, seq, hd):
    dm = ATTN_HEADS * hd
    lsub = seq // RES
    hg = ATTN_HEADS_PER_STEP
    nhg = ATTN_HEADS // hg
    assert hd == V7X_LANES and lsub // TQ == 16 // RES and seq // 16 == TQ and ATTN_HEADS % hg == 0
    view = qkv.reshape(bsz * RES, lsub, 3 * dm)
    blk = lambda which: pl.BlockSpec((RES, lsub, hg * hd), lambda b, h: (b, 0, which * nhg + h))
    tab = lambda t: pl.BlockSpec((hg,) + t.shape[1:], lambda b, h: (h, 0, 0))
    st = pltpu.VMEM((RES, lsub, hd), F32)
    sub_f = pltpu.VMEM((RES * lsub // TQ, TQ, hd), F32)
    o = pl.pallas_call(
        functools.partial(_attn_heads_kernel, scale=hd ** -0.5, hd=hd),
        out_shape=jax.ShapeDtypeStruct((bsz * RES, lsub, dm), BF16),
        grid=(bsz, nhg),
        in_specs=[blk(0), blk(1), blk(2)] + [tab(t) for t in tabs],
        out_specs=pl.BlockSpec((RES, lsub, hg * hd), lambda b, h: (b, 0, h)),
        scratch_shapes=[st, st, st, st, st, st, sub_f, sub_f, sub_f, pltpu.VMEM((RES, lsub, hd), BF16)],
        compiler_params=_cparams("parallel", "arbitrary"),
        name="attention",
    )(view, view, view, *tabs)
    return o.reshape(bsz * seq, dm)


def _split_bf16(x):
    hi = x.astype(BF16)
    lo = (x - hi.astype(F32)).astype(BF16)
    return hi, lo


def _moe_route_kernel(x_ref, g_ref, wr_ref, h_ref, meta_ref, metat_ref, cnt_ref, carry_ref):
    step = pl.program_id(0)
    tm = x_ref.shape[0]

    @pl.when(step == 0)
    def _():
        carry_ref[...] = jnp.zeros_like(carry_ref)

    h = _rms(x_ref[...], g_ref[...])
    h_hi, h_lo = _split_bf16(h)
    dh = h.shape[1] // 2
    h_ref[...] = pack_bf16_pair(h[:, 0:dh], h[:, dh:2 * dh])
    w_hi, w_lo = _split_bf16(wr_ref[...])
    both = jnp.dot(h_hi, jnp.concatenate([w_hi, w_lo], axis=1), preferred_element_type=F32)
    logits = both[:, 0:V7X_LANES] + jnp.dot(h_lo, w_hi, preferred_element_type=F32) + both[:, V7X_LANES:2 * V7X_LANES]
    lane = lax.broadcasted_iota(jnp.int32, (tm, V7X_LANES), 1).astype(F32)
    lg = jnp.where(lane < N_EXPERTS, logits, NEG_INF)
    m1 = jnp.max(lg, axis=-1, keepdims=True)
    i1 = jnp.min(jnp.where(lg == m1, lane, float(V7X_LANES)), axis=-1, keepdims=True)
    lg2 = jnp.where(lane == i1, NEG_INF, lg)
    m2 = jnp.max(lg2, axis=-1, keepdims=True)
    i2 = jnp.min(jnp.where(lg2 == m2, lane, float(V7X_LANES)), axis=-1, keepdims=True)
    e2 = jnp.exp(m2 - m1)
    g1 = 1.0 / (1.0 + e2)
    g2 = e2 / (1.0 + e2)
    oh = ((lane == i1) | (lane == i2)).astype(BF16)
    r_i = lax.broadcasted_iota(jnp.int32, (tm, tm), 0)
    c_i = lax.broadcasted_iota(jnp.int32, (tm, tm), 1)
    tri = (c_i < r_i).astype(BF16)
    rank = jnp.dot(tri, oh, preferred_element_type=F32) + carry_ref[0:1, :]
    r1 = jnp.sum(jnp.where(lane == i1, rank, 0.0), axis=-1, keepdims=True)
    r2 = jnp.sum(jnp.where(lane == i2, rank, 0.0), axis=-1, keepdims=True)
    carry_ref[0:1, :] = carry_ref[0:1, :] + jnp.sum(oh.astype(F32), axis=0, keepdims=True)
    meta = jnp.where(lane == 0, i1.astype(F32), 0.0)
    meta = jnp.where(lane == 1, i2.astype(F32), meta)
    meta = jnp.where(lane == 2, g1, meta)
    meta = jnp.where(lane == 3, g2, meta)
    meta = jnp.where(lane == 4, r1, meta)
    meta = jnp.where(lane == 5, r2, meta)
    meta_ref[...] = meta
    for c in range(tm // V7X_LANES):
        blk = meta[c * V7X_LANES:(c + 1) * V7X_LANES, :].T
        metat_ref[:, c * V7X_LANES:(c + 1) * V7X_LANES] = blk[0:SUBLANES, :]
    cnt_ref[...] = jnp.broadcast_to(carry_ref[0:1, :], cnt_ref.shape)


def moe_route(x, g, w_router_p, tm=256):
    t, d = x.shape
    return pl.pallas_call(
        _moe_route_kernel,
        out_shape=(jax.ShapeDtypeStruct((t, d // 2), jnp.uint32), jax.ShapeDtypeStruct((t, V7X_LANES), F32),
                   jax.ShapeDtypeStruct((SUBLANES, t), F32), jax.ShapeDtypeStruct((8, V7X_LANES), F32)),
        grid=(t // tm,),
        in_specs=[pl.BlockSpec((tm, d), lambda i: (i, 0)), pl.BlockSpec((1, d), lambda i: (0, 0)),
                  pl.BlockSpec((d, V7X_LANES), lambda i: (0, 0))],
        out_specs=(pl.BlockSpec((tm, d // 2), lambda i: (i, 0)), pl.BlockSpec((tm, V7X_LANES), lambda i: (i, 0)),
                   pl.BlockSpec((SUBLANES, tm), lambda i: (0, i)), pl.BlockSpec((8, V7X_LANES), lambda i: (0, 0))),
        scratch_shapes=[pltpu.VMEM((8, V7X_LANES), F32)],
        compiler_params=_cparams("arbitrary"),
        name="moe_route",
    )(x, g.reshape(1, d), w_router_p)


def _moe_final_kernel(pos0_ref, pos1_ref, x_ref, meta_ref, g_ref, ys_ref, o_ref, ybuf, o_scr, sems, *, lsub,
                      nsteps):
    nres, tp, d = x_ref.shape
    dh = d // 2
    nj = lsub // tp
    b = pl.program_id(0)
    j = pl.program_id(1)
    step = b * nj + j
    slot = step % 2

    def copies(bb, jj, sl, r, mm):
        tok = (bb * nres + r) * lsub + jj * tp + mm
        return [_row_copy(ys_ref, p[tok], ybuf.at[sl, k], r * tp + mm, sems.at[sl])
                for k, p in enumerate((pos0_ref, pos1_ref))]

    def for_all_rows(bb, jj, sl, start):
        for r in range(nres):
            def body(mm, c, r=r):
                for cp in copies(bb, jj, sl, r, mm):
                    if start:
                        cp.start()
                    else:
                        cp.wait()
                return c

            lax.fori_loop(0, tp, body, 0, unroll=DMA_UNROLL)

    @pl.when(step == 0)
    def _():
        for_all_rows(b, j, slot, True)

    @pl.when(step + 1 < nsteps)
    def _():
        nxt = j + 1 == nj
        for_all_rows(jnp.where(nxt, b + 1, b), jnp.where(nxt, 0, j + 1), 1 - slot, True)

    for_all_rows(b, j, slot, False)
    for r in range(nres):
        rows = slice(r * tp, (r + 1) * tp)
        meta = meta_ref[r]
        g1 = meta[:, 2:3]
        g2 = meta[:, 3:4]
        a0, b0 = unpack_bf16_pair(ybuf[slot, 0, rows, :])
        a1, b1 = unpack_bf16_pair(ybuf[slot, 1, rows, :])
        x = x_ref[r]
        xa = x[:, 0:dh] + a0 * g1 + a1 * g2
        xb = x[:, dh:d] + b0 * g1 + b1 * g2
        ms = (jnp.sum(xa * xa, axis=-1, keepdims=True) + jnp.sum(xb * xb, axis=-1, keepdims=True)) * (1.0 / d)
        inv = lax.rsqrt(ms + EPS)
        ya = xa * inv * g_ref[:, 0:dh]
        yb = xb * inv * g_ref[:, dh:d]
        out_rows = pl.ds(r, tp, stride=nres)
        for s in range(dh // V7X_LANES):
            ls = slice(s * V7X_LANES, (s + 1) * V7X_LANES)
            o_scr[s, out_rows, :] = ya[:, ls]
            o_scr[dh // V7X_LANES + s, out_rows, :] = yb[:, ls]
    for s in range(d // V7X_LANES):
        o_ref[:, s * V7X_LANES:(s + 1) * V7X_LANES] = o_scr[s]


def moe_final(x, ys, pos0, pos1, meta, g, *, bsz, seq, tp=128):
    t, d = x.shape
    lsub = seq // RES
    tm = RES * tp
    xv = x.reshape(bsz * RES, lsub, d)
    mv = meta.reshape(bsz * RES, lsub, V7X_LANES)
    grid_spec = pltpu.PrefetchScalarGridSpec(
        num_scalar_prefetch=2,
        grid=(bsz, lsub // tp),
        in_specs=[pl.BlockSpec((RES, tp, d), lambda b, j, p0, p1: (b, j, 0)),
                  pl.BlockSpec((RES, tp, V7X_LANES), lambda b, j, p0, p1: (b, j, 0)),
                  pl.BlockSpec((1, d), lambda b, j, p0, p1: (0, 0)), pl.BlockSpec(memory_space=pl.ANY)],
        out_specs=pl.BlockSpec((tm, d), lambda b, j, p0, p1: (b * (lsub // tp) + j, 0)),
        scratch_shapes=[pltpu.VMEM((2, TOP_K, tm, d // 2), jnp.uint32), pltpu.VMEM((d // V7X_LANES, tm, V7X_LANES), F32),
                        pltpu.SemaphoreType.DMA((2,))],
    )
    return pl.pallas_call(
        functools.partial(_moe_final_kernel, lsub=lsub, nsteps=bsz * (lsub // tp)),
        out_shape=jax.ShapeDtypeStruct((t, d), F32),
        grid_spec=grid_spec,
        compiler_params=_cparams("arbitrary", "arbitrary"),
        name="moe_final",
    )(pos0, pos1, xv, mv, g.reshape(1, d), ys)


PROJ_TM = 2048


def _pad_lanes(v, fill=0.0):
    v = v.reshape(1, -1).astype(F32)
    return jnp.pad(v, ((0, 0), (0, V7X_LANES - v.shape[1])), constant_values=fill)


def even_layer(x, h, p, *, bsz, seq):
    t, d = x.shape
    d_ssd = d
    d_conf = d
    heads = d_ssd // SSD_HEAD_DIM
    conv_dim = d_ssd + 2 * SSD_GROUPS * SSD_STATE
    i1 = d_ssd + conv_dim
    i2 = i1 + heads
    w_in_t = jnp.swapaxes(p["w_in"], 0, 1)
    zx = matmul_nt(h, w_in_t, row_start=0, n_cols=i1, tn=1024, tm=PROJ_TM, name="in_proj_zx")
    dt_raw = matmul_nt(h, w_in_t, row_start=i1, n_cols=V7X_LANES, tn=V7X_LANES, out_dtype=F32, name="in_proj_dt")
    conf = matmul_nt(h, w_in_t, row_start=i2, n_cols=2 * d_conf, tn=1024, tm=PROJ_TM, name="in_proj_conf")
    u = conf_module(conf, p["conf_dw_w"], p["conf_dw_b"], p["conf_ln_w"], p["conf_ln_b"], bsz=bsz, seq=seq)
    dt, acs, dtt, acst = ssd_prep(dt_raw, _pad_lanes(p["dt_bias"]), _pad_lanes(p["a_log"]), bsz=bsz, seq=seq)
    dskip_e = jnp.repeat(p["d_skip"].astype(F32), SSD_HEAD_DIM).reshape(1, d_ssd)
    y_ssd = ssd_main(zx, p["conv_w"], p["conv_b"], dt, acs, dtt, acst, dskip_e, p["ssd_norm_w"],
                     bsz=bsz, seq=seq, d_ssd=d_ssd)
    return matmul([y_ssd, u], p["w_out"], n_cols=d, tn=256, tm=PROJ_TM, res=x, out_dtype=F32, name="even_out_proj")


def dense_ffn(h, w1, w3, w2, gsz=1024):
    t = h.shape[0]
    ng = t // gsz
    ge = jnp.zeros((ng,), jnp.int32)
    gb = jnp.arange(ng, dtype=jnp.int32)
    gr = jnp.full((ng,), gsz, jnp.int32)
    return ffn_groups(h, w1[None], w3[None], w2[None], ge, gb, gr, gsz=gsz, tf=512, out_buffers=1)


def odd_layer_attn(x, h, p, *, bsz, seq):
    t, d = x.shape
    hd = d // ATTN_HEADS
    qkv = matmul([h], p["w_qkv"], n_cols=3 * d, tn=1024, tm=PROJ_TM, name="qkv_proj")
    tabs = attn_bias_tables(p["rel_bias"])
    o = attention(qkv, tabs, bsz=bsz, seq=seq, hd=hd)
    return matmul([o], p["w_attn_out"], n_cols=d, tn=512, tm=2048, res=x, out_dtype=F32, name="attn_out_proj")


def moe_layer(x, norm_g, w_router, w1, w3, w2, final_g, *, bsz, seq, gsz=1280):
    t, d = x.shape
    ne = w1.shape[0]
    w_router_p = jnp.pad(w_router.astype(F32), ((0, 0), (0, V7X_LANES - ne)))
    h, meta, meta_t, cnt = moe_route(x, norm_g, w_router_p)
    counts = cnt[0, :ne].astype(jnp.int32)
    ngrp_e = (counts + gsz - 1) // gsz
    grp_end = jnp.cumsum(ngrp_e)
    grp_start = grp_end - ngrp_e
    ng = (t * TOP_K) // gsz + ne
    j = jnp.arange(ng, dtype=jnp.int32)
    total = grp_end[-1]
    last = jnp.maximum(total - 1, 0)
    jj = jnp.minimum(j, last)
    ge = jnp.searchsorted(grp_end, jj, side="right").astype(jnp.int32)
    ge = jnp.minimum(ge, ne - 1)
    rows = jnp.clip(counts[ge] - (jj - grp_start[ge]) * gsz, 0, gsz)
    gr = jnp.where(j < total, rows, 0).astype(jnp.int32)
    gb = j

    def dest_row(e, rank):
        off = jnp.zeros_like(rank)
        for k in range(ne):
            off = jnp.where(e == k, grp_start[k] * gsz, off)
        return off + rank

    pos0 = dest_row(meta_t[0].astype(jnp.int32), meta_t[4].astype(jnp.int32))
    pos1 = dest_row(meta_t[1].astype(jnp.int32), meta_t[5].astype(jnp.int32))
    xs = moe_dispatch(h, pos0, pos1, gr, gsz)
    ys = ffn_groups(xs, w1, w3, w2, ge, gb, gr, gsz=gsz, packed=True)
    return moe_final(x, ys, pos0, pos1, meta, final_g, bsz=bsz, seq=seq)


def kernel(x, norm_mix, norm_ffn, norm_final, even_w_in, ssd_conv_w, ssd_conv_b, ssd_dt_bias, ssd_a_log, ssd_d, ssd_norm_w, conf_dw_w, conf_dw_b, conf_ln_w, conf_ln_b, even_w_out, ffn_w1, ffn_w3, ffn_w2, attn_w_qkv, attn_w_out, rel_bias, moe_router, moe_w1, moe_w3, moe_w2):
    bsz, seq, d = x.shape
    assert norm_mix.shape[0] == 2, "two-layer trunk: one even (SSD+Conformer/FFN) and one odd (attention/MoE) layer"
    xf = x.reshape(bsz * seq, d)
    h = rmsnorm(xf, norm_mix[0])
    p_even = dict(w_in=even_w_in[0], conv_w=ssd_conv_w[0], conv_b=ssd_conv_b[0], dt_bias=ssd_dt_bias[0],
                  a_log=ssd_a_log[0], d_skip=ssd_d[0], ssd_norm_w=ssd_norm_w[0], conf_dw_w=conf_dw_w[0],
                  conf_dw_b=conf_dw_b[0], conf_ln_w=conf_ln_w[0], conf_ln_b=conf_ln_b[0], w_out=even_w_out[0])
    x1 = even_layer(xf, h, p_even, bsz=bsz, seq=seq)
    h1 = rmsnorm(x1, norm_ffn[0])
    y_ffn = dense_ffn(h1, ffn_w1[0], ffn_w3[0], ffn_w2[0])
    x2, h2 = add_norm_to_residue_major(x1, y_ffn, norm_mix[1], bsz=bsz, seq=seq, nres=RES)
    p_odd = dict(w_qkv=attn_w_qkv[0], w_attn_out=attn_w_out[0], rel_bias=rel_bias)
    x3 = odd_layer_attn(x2, h2, p_odd, bsz=bsz, seq=seq)
    out = moe_layer(x3, norm_ffn[1], moe_router[0], moe_w1[0], moe_w3[0], moe_w2[0], norm_final, bsz=bsz, seq=seq)
    return out.reshape(bsz, seq, d)
```

```python
import functools
import math

import numpy as np
import jax
import jax.numpy as jnp
from jax import lax
from jax.experimental import pallas as pl
from jax.experimental.pallas import tpu as pltpu

SSD_HEAD_DIM = 64
SSD_GROUPS = 4
SSD_STATE = 128
SSD_CONV = 4
SSD_CHUNK = 128
CONF_KERNEL = 31
ATTN_HEADS = 16
DILATED_PATTERNS = ((128, 1), (512, 4), (2048, 16))
ATTN_BLOCK = 128
REL_BUCKETS = 32
REL_MAX_DIST = 2048
N_EXPERTS = 8
TOP_K = 2
EPS = 1e-6
LOG2E = math.log2(math.e)

V7X_LANES = 128
V7X_VMEM_BYTES = 64 * 1024 * 1024
VMEM_LIMIT = 60 * 1024 * 1024

F32 = jnp.float32
BF16 = jnp.bfloat16
NEG_INF = float("-inf")


def _cparams(*sem):
    return pltpu.CompilerParams(dimension_semantics=tuple(sem), vmem_limit_bytes=VMEM_LIMIT)


def _rms(x, g):
    ms = jnp.mean(x * x, axis=-1, keepdims=True)
    return x * lax.rsqrt(ms + EPS) * g


def _rmsnorm_kernel(x_ref, g_ref, h_ref):
    h_ref[...] = _rms(x_ref[...], g_ref[...]).astype(h_ref.dtype)


def rmsnorm(x, g, out_dtype=BF16, tm=512):
    t, d = x.shape
    return pl.pallas_call(
        _rmsnorm_kernel,
        out_shape=jax.ShapeDtypeStruct((t, d), out_dtype),
        grid=(t // tm,),
        in_specs=[pl.BlockSpec((tm, d), lambda i: (i, 0)), pl.BlockSpec((1, d), lambda i: (0, 0))],
        out_specs=pl.BlockSpec((tm, d), lambda i: (i, 0)),
        compiler_params=_cparams("parallel"),
        name="rmsnorm",
    )(x, g.reshape(1, d))


def _add_norm_res_kernel(x_ref, y_ref, g_ref, xo_ref, ho_ref, xs_scr, hs_scr):
    nres, tp, d = xo_ref.shape
    xn = x_ref[...] + y_ref[...].astype(F32)
    h = _rms(xn, g_ref[...])
    for s in range(d // V7X_LANES):
        ls = slice(s * V7X_LANES, (s + 1) * V7X_LANES)
        xs_scr[s] = xn[:, ls]
        hs_scr[s] = h[:, ls]
    for r in range(nres):
        rows = pl.ds(r, tp, stride=nres)
        for s in range(d // V7X_LANES):
            ls = slice(s * V7X_LANES, (s + 1) * V7X_LANES)
            xo_ref[r, :, ls] = xs_scr[s, rows, :]
            ho_ref[r, :, ls] = hs_scr[s, rows, :].astype(ho_ref.dtype)


def add_norm_to_residue_major(x, y, g, *, bsz, seq, nres, tp=128):
    t, d = x.shape
    lsub = seq // nres
    tm = nres * tp
    nj = lsub // tp
    slab = pltpu.VMEM((d // V7X_LANES, tm, V7X_LANES), F32)
    xo, ho = pl.pallas_call(
        _add_norm_res_kernel,
        out_shape=(jax.ShapeDtypeStruct((bsz * nres, lsub, d), F32), jax.ShapeDtypeStruct((bsz * nres, lsub, d), BF16)),
        grid=(bsz, nj),
        in_specs=[pl.BlockSpec((tm, d), lambda b, j: (b * nj + j, 0)), pl.BlockSpec((tm, d), lambda b, j: (b * nj + j, 0)),
                  pl.BlockSpec((1, d), lambda b, j: (0, 0))],
        out_specs=(pl.BlockSpec((nres, tp, d), lambda b, j: (b, j, 0)), pl.BlockSpec((nres, tp, d), lambda b, j: (b, j, 0))),
        scratch_shapes=[slab, slab],
        compiler_params=_cparams("parallel", "parallel"),
        name="add_norm_reorder",
    )(x, y, g.reshape(1, d))
    return xo.reshape(t, d), ho.reshape(t, d)


def _mm_kernel(*refs, n_lhs, ks, has_res):
    xs = refs[:n_lhs]
    w_ref = refs[n_lhs]
    r_ref = refs[n_lhs + 1] if has_res else None
    o_ref = refs[-1]
    acc = None
    k0 = 0
    for x_ref, k in zip(xs, ks):
        part = jnp.dot(x_ref[...], w_ref[k0:k0 + k, :].astype(BF16), preferred_element_type=F32)
        acc = part if acc is None else acc + part
        k0 += k
    if has_res:
        acc = acc + r_ref[...]
    o_ref[...] = acc.astype(o_ref.dtype)


def matmul(xs, w, *, n_cols, tn, tm=1024, res=None, out_dtype=BF16, name="matmul"):
    t = xs[0].shape[0]
    ks = tuple(x.shape[1] for x in xs)
    ktot = sum(ks)
    assert w.shape[0] == ktot and n_cols % tn == 0 and t % tm == 0
    in_specs = [pl.BlockSpec((tm, k), lambda i, j: (i, 0)) for k in ks]
    in_specs.append(pl.BlockSpec((ktot, tn), lambda i, j: (0, j)))
    args = list(xs) + [w]
    if res is not None:
        in_specs.append(pl.BlockSpec((tm, tn), lambda i, j: (i, j)))
        args.append(res)
    return pl.pallas_call(
        functools.partial(_mm_kernel, n_lhs=len(xs), ks=ks, has_res=res is not None),
        out_shape=jax.ShapeDtypeStruct((t, n_cols), out_dtype),
        grid=(t // tm, n_cols // tn),
        in_specs=in_specs,
        out_specs=pl.BlockSpec((tm, tn), lambda i, j: (i, j)),
        compiler_params=_cparams("parallel", "arbitrary"),
        name=name,
    )(*args)


def _mm_nt_kernel(x_ref, wt_ref, o_ref):
    wt = wt_ref[...].astype(BF16)
    acc = lax.dot_general(x_ref[...], wt, (((1,), (1,)), ((), ())), preferred_element_type=F32)
    o_ref[...] = acc.astype(o_ref.dtype)


def matmul_nt(x, w_t, *, row_start, n_cols, tn, tm=1024, out_dtype=BF16, name="matmul_nt"):
    t, k = x.shape
    assert w_t.shape[1] == k and n_cols % tn == 0 and t % tm == 0
    if row_start % tn == 0:
        w_spec = pl.BlockSpec((tn, k), lambda i, j: (row_start // tn + j, 0))
    else:
        assert row_start % SUBLANES == 0 and tn % SUBLANES == 0
        w_spec = pl.BlockSpec((pl.Element(tn), pl.Element(k)),
                              lambda i, j: ((row_start // SUBLANES + j * (tn // SUBLANES)) * SUBLANES, 0))
    return pl.pallas_call(
        _mm_nt_kernel,
        out_shape=jax.ShapeDtypeStruct((t, n_cols), out_dtype),
        grid=(t // tm, n_cols // tn),
        in_specs=[pl.BlockSpec((tm, k), lambda i, j: (i, 0)), w_spec],
        out_specs=pl.BlockSpec((tm, tn), lambda i, j: (i, j)),
        compiler_params=_cparams("parallel", "arbitrary"),
        name=name,
    )(x, w_t)


HALO_BF16 = 16


def _ssd_prep_kernel(raw_ref, bias_ref, alog_ref, dt_ref, acs_ref, rowt_ref, *, seq):
    raw = raw_ref[...] + bias_ref[...]
    dt = jnp.maximum(raw, 0.0) + jnp.log1p(jnp.exp(-jnp.abs(raw)))
    a = -jnp.exp(alog_ref[...])
    x = dt * a
    row = lax.broadcasted_iota(jnp.int32, x.shape, 0) % SSD_CHUNK
    sh = 1
    while sh < SSD_CHUNK:
        x = x + jnp.where(row >= sh, pltpu.roll(x, sh, 0), 0.0)
        sh *= 2
    x2 = x * LOG2E
    dt_ref[...] = dt
    acs_ref[...] = x2
    rowterm = x2 - jnp.log2(dt)
    for c in range(seq // SSD_CHUNK):
        sl = slice(c * SSD_CHUNK, (c + 1) * SSD_CHUNK)
        rowt_ref[sl, :] = rowterm[sl, :].T


def ssd_prep(dt_raw, dt_bias_p, a_log_p, *, bsz, seq):
    t = dt_raw.shape[0]
    shp = jax.ShapeDtypeStruct((t, V7X_LANES), F32)
    blk = pl.BlockSpec((seq, V7X_LANES), lambda b: (b, 0))
    vec = pl.BlockSpec((1, V7X_LANES), lambda b: (0, 0))
    return pl.pallas_call(
        functools.partial(_ssd_prep_kernel, seq=seq),
        out_shape=(shp, shp, shp),
        grid=(bsz,),
        in_specs=[blk, vec, vec],
        out_specs=(blk, blk, blk),
        compiler_params=_cparams("parallel"),
        name="ssd_prep",
    )(dt_raw, dt_bias_p, a_log_p)


def _conv_silu(cur_ref, halo_ref, w_ref, b_ref, buf_ref, col0, first):
    rows, c = cur_ref.shape
    cols = slice(col0, col0 + c)
    buf_ref[0:HALO_BF16, :] = jnp.where(first, 0.0, halo_ref[...].astype(F32))
    buf_ref[HALO_BF16:HALO_BF16 + rows, :] = cur_ref[...].astype(F32)
    acc = b_ref[:, cols]
    for k in range(SSD_CONV):
        off = HALO_BF16 - (SSD_CONV - 1) + k
        acc = acc + w_ref[k:k + 1, cols] * buf_ref[off:off + rows, :]
    return acc * jax.nn.sigmoid(acc)


def _ssd_main_kernel(xr_ref, xh_ref, br_ref, bh_ref, cr_ref, ch_ref, cw_ref, cbias_ref,
                     z_ref, dt_ref, acs_ref, rowt_ref, dsk_ref, nw_ref,
                     y_ref, st_ref, xbuf, bbuf, cbuf, *, heads_per_group):
    ci = pl.program_id(1)
    L = SSD_CHUNK
    gw = heads_per_group * SSD_HEAD_DIM
    npair = heads_per_group // 2
    d_ssd = xr_ref.shape[1]
    bc_w = br_ref.shape[1]

    @pl.when(ci == 0)
    def _():
        st_ref[...] = jnp.zeros_like(st_ref)

    first = ci == 0
    x_all = _conv_silu(xr_ref, xh_ref, cw_ref, cbias_ref, xbuf, 0, first)
    b_all = _conv_silu(br_ref, bh_ref, cw_ref, cbias_ref, bbuf, d_ssd, first)
    c_all = _conv_silu(cr_ref, ch_ref, cw_ref, cbias_ref, cbuf, d_ssd + bc_w, first)
    dt = dt_ref[...]
    acs = acs_ref[...]
    rowt = rowt_ref[...]
    row = lax.broadcasted_iota(jnp.int32, (L, L), 0)
    col = lax.broadcasted_iota(jnp.int32, (L, L), 1)
    causal = col <= row
    lane_lo = lax.broadcasted_iota(jnp.int32, (L, 2 * SSD_HEAD_DIM), 1) < SSD_HEAD_DIM
    lane_lo1 = lax.broadcasted_iota(jnp.int32, (1, 2 * SSD_HEAD_DIM), 1) < SSD_HEAD_DIM
    zero_b = jnp.zeros((L, 2 * SSD_HEAD_DIM), BF16)

    for g in range(SSD_GROUPS):
        bmf = b_all[:, g * SSD_STATE:(g + 1) * SSD_STATE]
        bm = bmf.astype(BF16)
        cm = c_all[:, g * SSD_STATE:(g + 1) * SSD_STATE].astype(BF16)
        bmt = bmf.T.astype(BF16)
        cb = lax.dot_general(cm, bm, (((1,), (1,)), ((), ())), preferred_element_type=F32)
        xg = x_all[:, g * gw:(g + 1) * gw]
        y_diags, e_ls, w_ls, cds = [], [], [], []
        for q in range(npair):
            xp = xg[:, q * 2 * SSD_HEAD_DIM:(q + 1) * 2 * SSD_HEAD_DIM].astype(BF16)
            ms = []
            for hh in range(2):
                h = g * heads_per_group + 2 * q + hh
                seg = acs[:, h:h + 1] - rowt[h:h + 1, :]
                ms.append((cb * jnp.exp2(jnp.where(causal, seg, NEG_INF))).astype(BF16))
            h0 = g * heads_per_group + 2 * q
            h1 = h0 + 1
            lhs = jnp.concatenate(ms, axis=1)
            rhs = jnp.concatenate([jnp.where(lane_lo, xp, zero_b), jnp.where(lane_lo, zero_b, xp)], axis=0)
            y_diags.append(jnp.dot(lhs, rhs, preferred_element_type=F32))
            last0 = acs[L - 1:L, h0:h0 + 1]
            last1 = acs[L - 1:L, h1:h1 + 1]
            e_ls.append(jnp.where(lane_lo, jnp.exp2(acs[:, h0:h0 + 1]), jnp.exp2(acs[:, h1:h1 + 1])))
            w_ls.append(jnp.where(lane_lo, dt[:, h0:h0 + 1] * jnp.exp2(last0 - acs[:, h0:h0 + 1]),
                                  dt[:, h1:h1 + 1] * jnp.exp2(last1 - acs[:, h1:h1 + 1])))
            cds.append(jnp.where(lane_lo1, jnp.exp2(last0), jnp.exp2(last1)))
        st = st_ref[g]
        y_off = jnp.dot(cm, st.astype(BF16), preferred_element_type=F32) * jnp.concatenate(e_ls, axis=1)
        xw = (xg * jnp.concatenate(w_ls, axis=1)).astype(BF16)
        st_ref[g] = st * jnp.concatenate(cds, axis=1) + jnp.dot(bmt, xw, preferred_element_type=F32)
        y = jnp.concatenate(y_diags, axis=1) + y_off + xg * dsk_ref[:, g * gw:(g + 1) * gw]
        zg = z_ref[:, g * gw:(g + 1) * gw].astype(F32)
        yg = y * (zg * jax.nn.sigmoid(zg))
        ms_ = jnp.mean(yg * yg, axis=-1, keepdims=True)
        y_ref[:, g * gw:(g + 1) * gw] = (yg * lax.rsqrt(ms_ + EPS) * nw_ref[:, g * gw:(g + 1) * gw]).astype(y_ref.dtype)


def ssd_main(zx, conv_w, conv_b, dt, acs2, rowt, dskip_e, norm_w, *, bsz, seq, d_ssd):
    t = zx.shape[0]
    nc = seq // SSD_CHUNK
    heads = d_ssd // SSD_HEAD_DIM
    hpg = heads // SSD_GROUPS
    bc_w = SSD_GROUPS * SSD_STATE
    conv_dim = conv_w.shape[1]
    assert d_ssd % bc_w == 0 and hpg % 2 == 0 and conv_dim == d_ssd + 2 * bc_w
    rowmap = lambda b, c: (b * nc + c, 0)
    hb = SSD_CHUNK // HALO_BF16

    def cur(width, col_block):
        return pl.BlockSpec((SSD_CHUNK, width), lambda b, c: (b * nc + c, col_block))

    def halo(width, col_block):
        return pl.BlockSpec((HALO_BF16, width), lambda b, c: (jnp.maximum((b * nc + c) * hb - 1, 0), col_block))

    xcol, bcol, ccol = 1, 2 * d_ssd // bc_w, 2 * d_ssd // bc_w + 1
    return pl.pallas_call(
        functools.partial(_ssd_main_kernel, heads_per_group=hpg),
        out_shape=jax.ShapeDtypeStruct((t, d_ssd), BF16),
        grid=(bsz, nc),
        in_specs=[
            cur(d_ssd, xcol), halo(d_ssd, xcol), cur(bc_w, bcol), halo(bc_w, bcol), cur(bc_w, ccol), halo(bc_w, ccol),
            pl.BlockSpec((SSD_CONV, conv_dim), lambda b, c: (0, 0)),
            pl.BlockSpec((1, conv_dim), lambda b, c: (0, 0)),
            pl.BlockSpec((SSD_CHUNK, d_ssd), rowmap),
            pl.BlockSpec((SSD_CHUNK, V7X_LANES), rowmap),
            pl.BlockSpec((SSD_CHUNK, V7X_LANES), rowmap),
            pl.BlockSpec((SSD_CHUNK, V7X_LANES), rowmap),
            pl.BlockSpec((1, d_ssd), lambda b, c: (0, 0)),
            pl.BlockSpec((1, d_ssd), lambda b, c: (0, 0)),
        ],
        out_specs=pl.BlockSpec((SSD_CHUNK, d_ssd), rowmap),
        scratch_shapes=[pltpu.VMEM((SSD_GROUPS, SSD_STATE, hpg * SSD_HEAD_DIM), F32),
                        pltpu.VMEM((SSD_CHUNK + HALO_BF16, d_ssd), F32),
                        pltpu.VMEM((SSD_CHUNK + HALO_BF16, bc_w), F32),
                        pltpu.VMEM((SSD_CHUNK + HALO_BF16, bc_w), F32)],
        compiler_params=_cparams("parallel", "arbitrary"),
        name="ssd_main",
    )(zx, zx, zx, zx, zx, zx, conv_w, conv_b.reshape(1, conv_dim), zx, dt, acs2, rowt, dskip_e,
      norm_w.reshape(1, d_ssd))


CONF_HALO = 32


SUBLANES = 8


def _conf_kernel(a_ref, g_ref, ah_ref, gh_ref, w_ref, b_ref, lw_ref, lb_ref, o_ref, buf_ref, sh_ref, *, ts, rc, lt):
    i = pl.program_id(1)
    d = a_ref.shape[1]
    ah = ah_ref[...].astype(F32)
    gh = gh_ref[...].astype(F32)
    buf_ref[0:CONF_HALO, :] = jnp.where(i > 0, ah * jax.nn.sigmoid(gh), 0.0)
    a = a_ref[...].astype(F32)
    g = g_ref[...].astype(F32)
    buf_ref[CONF_HALO:CONF_HALO + ts, :] = a * jax.nn.sigmoid(g)
    span = ts + CONF_HALO - SUBLANES
    for j in range(1, SUBLANES):
        sh_ref[j - 1, 0:span, :] = buf_ref[j:j + span, :]
    base = CONF_HALO - (CONF_KERNEL - 1)
    outs = []
    for r in range(ts // rc):
        cols = []
        for c in range(d // lt):
            ls = slice(c * lt, (c + 1) * lt)
            acc = jnp.broadcast_to(b_ref[:, ls], (rc, lt))
            for k in range(CONF_KERNEL):
                j = (base + k) % SUBLANES
                off = r * rc + (base + k) - j
                src = buf_ref[off:off + rc, ls] if j == 0 else sh_ref[j - 1, off:off + rc, ls]
                acc = acc + w_ref[k:k + 1, ls] * src
            cols.append(acc)
        outs.append(jnp.concatenate(cols, axis=1))
    u = jnp.concatenate(outs, axis=0)
    mu = jnp.mean(u, axis=-1, keepdims=True)
    uc = u - mu
    var = jnp.mean(uc * uc, axis=-1, keepdims=True)
    y = uc * lax.rsqrt(var + EPS) * lw_ref[...] + lb_ref[...]
    o_ref[...] = (y * jax.nn.sigmoid(y)).astype(o_ref.dtype)


def conf_module(conf, dw_w, dw_b, ln_w, ln_b, *, bsz, seq, ts=256, rc=32, lt=512):
    t = conf.shape[0]
    d = dw_w.shape[1]
    nsb = seq // ts
    hb = ts // CONF_HALO
    cur = lambda col: pl.BlockSpec((ts, d), lambda b, i: (b * nsb + i, col))
    halo = lambda col: pl.BlockSpec((CONF_HALO, d), lambda b, i: (jnp.maximum((b * nsb + i) * hb - 1, 0), col))
    vec = pl.BlockSpec((1, d), lambda b, i: (0, 0))
    return pl.pallas_call(
        functools.partial(_conf_kernel, ts=ts, rc=rc, lt=lt),
        out_shape=jax.ShapeDtypeStruct((t, d), BF16),
        grid=(bsz, nsb),
        in_specs=[cur(0), cur(1), halo(0), halo(1), pl.BlockSpec((CONF_KERNEL, d), lambda b, i: (0, 0)),
                  vec, vec, vec],
        out_specs=pl.BlockSpec((ts, d), lambda b, i: (b * nsb + i, 0)),
        scratch_shapes=[pltpu.VMEM((ts + CONF_HALO, d), F32), pltpu.VMEM((SUBLANES - 1, ts + CONF_HALO, d), F32)],
        compiler_params=_cparams("parallel", "parallel"),
        name="conf_module",
    )(conf, conf, conf, conf, dw_w, dw_b.reshape(1, d), ln_w.reshape(1, d), ln_b.reshape(1, d))


FFN_SUB = 128


FFN_NCHUNK = 512


def pack_bf16_pair(a, b):
    ua = lax.bitcast_convert_type(a.astype(BF16).astype(F32), jnp.uint32)
    ub = lax.bitcast_convert_type(b.astype(BF16).astype(F32), jnp.uint32)
    return (ua >> 16) | (ub & jnp.uint32(0xFFFF0000))


def unpack_bf16_pair(u):
    a = lax.bitcast_convert_type(u << 16, F32)
    b = lax.bitcast_convert_type(u & jnp.uint32(0xFFFF0000), F32)
    return a, b


def _ffn_kernel(ge_ref, gb_ref, gr_ref, x_ref, w1_ref, w3_ref, w2_ref, o_ref, acc_ref, *xb_scr, nf, gsz, packed):
    g = pl.program_id(0)
    f = pl.program_id(1)
    rows = gr_ref[g]
    valid = rows > 0
    d = acc_ref.shape[1]

    @pl.when(valid & (f == 0))
    def _():
        acc_ref[...] = jnp.zeros_like(acc_ref)
        if packed:
            a, b = unpack_bf16_pair(x_ref[...])
            xb_scr[0][:, 0:d // 2] = a.astype(BF16)
            xb_scr[0][:, d // 2:d] = b.astype(BF16)

    xsrc = xb_scr[0] if packed else x_ref

    def slab(r0, nrows, w1, w3, w2):
        xs = xsrc[pl.ds(r0, nrows), :]
        h1 = jnp.dot(xs, w1, preferred_element_type=F32)
        h3 = jnp.dot(xs, w3, preferred_element_type=F32)
        hh = (h1 * jax.nn.sigmoid(h1) * h3).astype(BF16)
        for c in range(d // FFN_NCHUNK):
            cs = slice(c * FFN_NCHUNK, (c + 1) * FFN_NCHUNK)
            acc_ref[pl.ds(r0, nrows), cs] += jnp.dot(hh, w2[:, cs], preferred_element_type=F32)

    nslab = (rows + FFN_SUB - 1) // FFN_SUB
    for ns in (range(1, gsz // FFN_SUB + 1) if packed else (gsz // FFN_SUB,)):
        @pl.when(nslab == ns)
        def _(ns=ns):
            slab(0, ns * FFN_SUB, w1_ref[...].astype(BF16), w3_ref[...].astype(BF16), w2_ref[...].astype(BF16))

    @pl.when(jnp.logical_not(valid) & (f == 0))
    def _():
        o_ref[...] = jnp.zeros_like(o_ref)

    @pl.when(valid & (f == nf - 1))
    def _():
        if packed:
            o_ref[...] = pack_bf16_pair(acc_ref[:, 0:d // 2], acc_ref[:, d // 2:d])
        else:
            o_ref[...] = acc_ref[...].astype(o_ref.dtype)


def ffn_groups(xs, w1, w3, w2, grp_expert, grp_block, grp_rows, *, gsz=1024, tf=256, packed=False, out_buffers=2):
    e, d, ff = w1.shape
    r, dx = xs.shape
    nf = ff // tf
    ng = grp_expert.shape[0]

    def fmap(f, g, gr):
        return jnp.where(gr[g] > 0, f, nf - 1)

    scratch = [pltpu.VMEM((gsz, d), F32)]
    if packed:
        scratch.append(pltpu.VMEM((gsz, d), BF16))
    grid_spec = pltpu.PrefetchScalarGridSpec(
        num_scalar_prefetch=3,
        grid=(ng, nf),
        in_specs=[
            pl.BlockSpec((gsz, dx), lambda g, f, ge, gb, gr: (gb[g], 0)),
            pl.BlockSpec((None, d, tf), lambda g, f, ge, gb, gr: (ge[g], 0, fmap(f, g, gr))),
            pl.BlockSpec((None, d, tf), lambda g, f, ge, gb, gr: (ge[g], 0, fmap(f, g, gr))),
            pl.BlockSpec((None, tf, d), lambda g, f, ge, gb, gr: (ge[g], fmap(f, g, gr), 0)),
        ],
        out_specs=pl.BlockSpec((gsz, dx), lambda g, f, ge, gb, gr: (gb[g], 0),
                               **({} if out_buffers == 2 else dict(pipeline_mode=pl.Buffered(out_buffers)))),
        scratch_shapes=scratch,
    )
    return pl.pallas_call(
        functools.partial(_ffn_kernel, nf=nf, gsz=gsz, packed=packed),
        out_shape=jax.ShapeDtypeStruct((r, dx), xs.dtype),
        grid_spec=grid_spec,
        compiler_params=_cparams("arbitrary", "arbitrary"),
        name="ffn_groups",
    )(grp_expert, grp_block, grp_rows, xs, w1, w3, w2)


def _row_copy(src_ref, src_row, dst_ref, dst_row, sem):
    return pltpu.make_async_copy(src_ref.at[pl.ds(src_row, 1), :], dst_ref.at[pl.ds(dst_row, 1), :], sem)


DMA_UNROLL = 8


def _dispatch_kernel(pos0_ref, pos1_ref, gr_ref, h_ref, xs_ref, zbuf, sem, zsem, *, gsz, ng):
    tm = h_ref.shape[0]
    step = pl.program_id(0)
    base = step * tm

    @pl.when(step == 0)
    def _():
        zbuf[...] = jnp.zeros_like(zbuf)
        for start in (True, False):
            for g in range(ng):
                rows = gr_ref[g]
                for c in range(gsz // FFN_SUB):
                    @pl.when(rows < (c + 1) * FFN_SUB)
                    def _(g=g, c=c, start=start):
                        cp = pltpu.make_async_copy(zbuf, xs_ref.at[pl.ds(g * gsz + c * FFN_SUB, FFN_SUB), :], zsem)
                        if start:
                            cp.start()
                        else:
                            cp.wait()

    def copies(i):
        return [_row_copy(h_ref, i, xs_ref, p[base + i], sem) for p in (pos0_ref, pos1_ref)]

    def issue(i, c):
        for cp in copies(i):
            cp.start()
        return c

    lax.fori_loop(0, tm, issue, 0, unroll=DMA_UNROLL)

    def drain(i, c):
        for cp in copies(i):
            cp.wait()
        return c

    lax.fori_loop(0, tm, drain, 0, unroll=DMA_UNROLL)


def moe_dispatch(hp, pos0, pos1, grp_rows, gsz, tm=512):
    t, w = hp.shape
    ng = grp_rows.shape[0]
    grid_spec = pltpu.PrefetchScalarGridSpec(
        num_scalar_prefetch=3,
        grid=(t // tm,),
        in_specs=[pl.BlockSpec((tm, w), lambda i, p0, p1, gr: (i, 0))],
        out_specs=pl.BlockSpec(memory_space=pl.ANY),
        scratch_shapes=[pltpu.VMEM((FFN_SUB, w), hp.dtype), pltpu.SemaphoreType.DMA, pltpu.SemaphoreType.DMA],
    )
    return pl.pallas_call(
        functools.partial(_dispatch_kernel, gsz=gsz, ng=ng),
        out_shape=jax.ShapeDtypeStruct((ng * gsz, w), hp.dtype),
        grid_spec=grid_spec,
        compiler_params=_cparams("arbitrary"),
        name="moe_dispatch",
    )(pos0, pos1, grp_rows, hp)


def _t5_bucket_np(dist):
    max_exact = REL_BUCKETS // 2
    d_f = np.maximum(dist, 1).astype(np.float32)
    large = max_exact + (np.log(d_f / np.float32(max_exact)) / np.float32(math.log(REL_MAX_DIST / max_exact))
                         * np.float32(REL_BUCKETS - max_exact)).astype(np.int32)
    large = np.minimum(large, REL_BUCKETS - 1)
    return np.where(dist < max_exact, dist, large)


RES = 4
TQ = ATTN_BLOCK
PIECE = TQ // RES


def _bucket_index_tables():
    assert DILATED_PATTERNS == ((128, 1), (512, 4), (2048, 16)) and RES == 4 and TQ == 128

    def fin(steps, dil, n_back=TQ):
        ok = (steps >= 0) & (steps <= n_back)
        return np.where(ok, _t5_bucket_np(np.clip(steps, 0, n_back) * dil), -1).astype(np.int32)

    i = np.arange(TQ)[:, None]
    t4 = fin(i + TQ - np.arange(2 * TQ)[None, :], 4)
    t16 = fin(i - np.arange(TQ)[None, :], 16)
    rq, mq = i // PIECE, i % PIECE
    jk = np.arange(2 * TQ)[None, :]
    rk, mk = jk // (2 * PIECE), jk % (2 * PIECE)
    t1 = fin(RES * (PIECE + mq - mk) + (rq - rk), 1)
    t1first = fin(RES * (mq - mk) + (rq - rk), 1)
    return t4, t1, t1first, t16


def _bias_tab_kernel(rel_ref, *refs):
    n = len(refs) // 2
    h = pl.program_id(0)
    for idx_ref, o_ref in zip(refs[:n], refs[n:]):
        idx = idx_ref[...]
        tab = jnp.full(idx.shape, NEG_INF, F32)
        for b in range(REL_BUCKETS):
            tab = jnp.where(idx == b, rel_ref[b, h] * LOG2E, tab)
        o_ref[0] = tab


def attn_bias_tables(rel_bias):
    idx_tabs = [jnp.asarray(t) for t in _bucket_index_tables()]
    nh = rel_bias.shape[1]
    return pl.pallas_call(
        _bias_tab_kernel,
        out_shape=tuple(jax.ShapeDtypeStruct((nh,) + t.shape, F32) for t in idx_tabs),
        grid=(nh,),
        in_specs=[pl.BlockSpec(memory_space=pltpu.SMEM)] + [pl.BlockSpec(t.shape, lambda h: (0, 0)) for t in idx_tabs],
        out_specs=tuple(pl.BlockSpec((1,) + t.shape, lambda h: (h, 0, 0)) for t in idx_tabs),
        compiler_params=_cparams("parallel"),
        name="attn_bias_tables",
    )(rel_bias.astype(F32), *idx_tabs)


def _attn_kernel(q_ref, k_ref, v_ref, t4_ref, t1_ref, t1f_ref, t16_ref, o_ref, acc, m_s, l_s, qs, ks, vs, od, md, ld,
                 kb, *, scale):
    nres, lsub, hd = q_ref.shape
    nt = (((1,), (1,)), ((), ()))

    def tile(q, k, v, bias):
        s = lax.dot_general(q, k, nt, preferred_element_type=F32) + bias
        m = jnp.max(s, axis=-1, keepdims=True)
        p = jnp.exp2(s - m)
        l = jnp.sum(p, axis=-1, keepdims=True)
        o = jnp.dot(p.astype(BF16), v, preferred_element_type=F32)
        return o, jnp.broadcast_to(m, o.shape), jnp.broadcast_to(l, o.shape)

    def merge(r, rows, o, m, l):
        m0 = m_s[r, rows, :]
        mn = jnp.maximum(m0, m)
        a = jnp.exp2(m0 - mn)
        b = jnp.exp2(m - mn)
        acc[r, rows, :] = acc[r, rows, :] * a + o * b
        l_s[r, rows, :] = l_s[r, rows, :] * a + l * b
        m_s[r, rows, :] = mn

    nsub = lsub // TQ
    for r in range(nres):
        qs[r] = q_ref[r].astype(F32)
        kf = k_ref[r].astype(F32) * (scale * LOG2E)
        ks[r] = kf
        kb[r] = kf.astype(BF16)
        vs[r] = v_ref[r].astype(F32)
    masked = jnp.full((TQ, TQ), NEG_INF, F32)
    bias16 = (jnp.concatenate([t16_ref[0], masked], axis=1), jnp.concatenate([masked, t16_ref[0]], axis=1))
    for r in range(nres):
        for pair in range(nsub // 2):
            sub = [pl.ds(2 * pair + e, TQ, stride=nsub) for e in range(2)]
            kk = jnp.concatenate([ks[r, s, :] for s in sub], axis=0).astype(BF16)
            vv = jnp.concatenate([vs[r, s, :] for s in sub], axis=0).astype(BF16)
            for e in range(2):
                o, m, l = tile(qs[r, sub[e], :].astype(BF16), kk, vv, bias16[e])
                od[r * nsub + 2 * pair + e] = o
                md[r * nsub + 2 * pair + e] = m
                ld[r * nsub + 2 * pair + e] = l

    for r in range(nres):
        for i in range(lsub // TQ):
            rows = pl.ds(i * TQ, TQ)
            if i == 0:
                krows = pl.ds(0, 2 * TQ)
                bias = jnp.concatenate([t4_ref[0, :, TQ:2 * TQ], masked], axis=1)
            else:
                krows = pl.ds((i - 1) * TQ, 2 * TQ)
                bias = t4_ref[0]
            o, m, l = tile(q_ref[r, rows, :], kb[r, krows, :], v_ref[r, krows, :], bias)
            acc[r, rows, :] = o
            m_s[r, rows, :] = m
            l_s[r, rows, :] = l

    for j in range(lsub // PIECE):
        q = jnp.concatenate([q_ref[r, pl.ds(j * PIECE, PIECE), :] for r in range(nres)], axis=0)
        if j == 0:
            krows, bias = pl.ds(0, 2 * PIECE), t1f_ref[0]
        else:
            krows, bias = pl.ds((j - 1) * PIECE, 2 * PIECE), t1_ref[0]
        k = jnp.concatenate([kb[r, krows, :] for r in range(nres)], axis=0)
        v = jnp.concatenate([v_ref[r, krows, :] for r in range(nres)], axis=0)
        o, m, l = tile(q, k, v, bias)
        for r in range(nres):
            part = slice(r * PIECE, (r + 1) * PIECE)
            merge(r, pl.ds(j * PIECE, PIECE), o[part], m[part], l[part])

    for r in range(nres):
        for c in range(nsub):
            merge(r, pl.ds(c, TQ, stride=nsub), od[r * nsub + c], md[r * nsub + c], ld[r * nsub + c])

    for r in range(nres):
        o_ref[r] = (acc[r] / l_s[r]).astype(o_ref.dtype)


def _attn_heads_kernel(q_ref, k_ref, v_ref, t4_ref, t1_ref, t1f_ref, t16_ref, o_ref, *scratch, scale, hd):
    for hh in range(q_ref.shape[2] // hd):
        lanes = pl.ds(hh * hd, hd)
        one = pl.ds(hh, 1)
        _attn_kernel(q_ref.at[:, :, lanes], k_ref.at[:, :, lanes], v_ref.at[:, :, lanes],
                     t4_ref.at[one], t1_ref.at[one], t1f_ref.at[one], t16_ref.at[one],
                     o_ref.at[:, :, lanes], *scratch, scale=scale)


ATTN_HEADS_PER_STEP = 2


def attention(qkv, tabs, *, bsz, seq, hd):
    dm = ATTN_HEADS * hd
    lsub = seq // RES
    hg = ATTN_HEADS_PER_STEP
    nhg = ATTN_HEADS // hg
    assert hd == V7X_LANES and lsub // TQ == 16 // RES and seq // 16 == TQ and ATTN_HEADS % hg == 0
    view = qkv.reshape(bsz * RES, lsub, 3 * dm)
    blk = lambda which: pl.BlockSpec((RES, lsub, hg * hd), lambda b, h: (b, 0, which * nhg + h))
    tab = lambda t: pl.BlockSpec((hg,) + t.shape[1:], lambda b, h: (h, 0, 0))
    st = pltpu.VMEM((RES, lsub, hd), F32)
    sub_f = pltpu.VMEM((RES * lsub // TQ, TQ, hd), F32)
    o = pl.pallas_call(
        functools.partial(_attn_heads_kernel, scale=hd ** -0.5, hd=hd),
        out_shape=jax.ShapeDtypeStruct((bsz * RES, lsub, dm), BF16),
        grid=(bsz, nhg),
        in_specs=[blk(0), blk(1), blk(2)] + [tab(t) for t in tabs],
        out_specs=pl.BlockSpec((RES, lsub, hg * hd), lambda b, h: (b, 0, h)),
        scratch_shapes=[st, st, st, st, st, st, sub_f, sub_f, sub_f, pltpu.VMEM((RES, lsub, hd), BF16)],
        compiler_params=_cparams("parallel", "arbitrary"),
        name="attention",
    )(view, view, view, *tabs)
    return o.reshape(bsz * seq, dm)


def _split_bf16(x):
    hi = x.astype(BF16)
    lo = (x - hi.astype(F32)).astype(BF16)
    return hi, lo


def _moe_route_kernel(x_ref, g_ref, wr_ref, h_ref, meta_ref, metat_ref, cnt_ref, carry_ref):
    step = pl.program_id(0)
    tm = x_ref.shape[0]

    @pl.when(step == 0)
    def _():
        carry_ref[...] = jnp.zeros_like(carry_ref)

    h = _rms(x_ref[...], g_ref[...])
    h_hi, h_lo = _split_bf16(h)
    dh = h.shape[1] // 2
    h_ref[...] = pack_bf16_pair(h[:, 0:dh], h[:, dh:2 * dh])
    w_hi, w_lo = _split_bf16(wr_ref[...])
    both = jnp.dot(h_hi, jnp.concatenate([w_hi, w_lo], axis=1), preferred_element_type=F32)
    logits = both[:, 0:V7X_LANES] + jnp.dot(h_lo, w_hi, preferred_element_type=F32) + both[:, V7X_LANES:2 * V7X_LANES]
    lane = lax.broadcasted_iota(jnp.int32, (tm, V7X_LANES), 1).astype(F32)
    lg = jnp.where(lane < N_EXPERTS, logits, NEG_INF)
    m1 = jnp.max(lg, axis=-1, keepdims=True)
    i1 = jnp.min(jnp.where(lg == m1, lane, float(V7X_LANES)), axis=-1, keepdims=True)
    lg2 = jnp.where(lane == i1, NEG_INF, lg)
    m2 = jnp.max(lg2, axis=-1, keepdims=True)
    i2 = jnp.min(jnp.where(lg2 == m2, lane, float(V7X_LANES)), axis=-1, keepdims=True)
    e2 = jnp.exp(m2 - m1)
    g1 = 1.0 / (1.0 + e2)
    g2 = e2 / (1.0 + e2)
    oh = ((lane == i1) | (lane == i2)).astype(BF16)
    r_i = lax.broadcasted_iota(jnp.int32, (tm, tm), 0)
    c_i = lax.broadcasted_iota(jnp.int32, (tm, tm), 1)
    tri = (c_i < r_i).astype(BF16)
    rank = jnp.dot(tri, oh, preferred_element_type=F32) + carry_ref[0:1, :]
    r1 = jnp.sum(jnp.where(lane == i1, rank, 0.0), axis=-1, keepdims=True)
    r2 = jnp.sum(jnp.where(lane == i2, rank, 0.0), axis=-1, keepdims=True)
    carry_ref[0:1, :] = carry_ref[0:1, :] + jnp.sum(oh.astype(F32), axis=0, keepdims=True)
    meta = jnp.where(lane == 0, i1.astype(F32), 0.0)
    meta = jnp.where(lane == 1, i2.astype(F32), meta)
    meta = jnp.where(lane == 2, g1, meta)
    meta = jnp.where(lane == 3, g2, meta)
    meta = jnp.where(lane == 4, r1, meta)
    meta = jnp.where(lane == 5, r2, meta)
    meta_ref[...] = meta
    for c in range(tm // V7X_LANES):
        blk = meta[c * V7X_LANES:(c + 1) * V7X_LANES, :].T
        metat_ref[:, c * V7X_LANES:(c + 1) * V7X_LANES] = blk[0:SUBLANES, :]
    cnt_ref[...] = jnp.broadcast_to(carry_ref[0:1, :], cnt_ref.shape)


def moe_route(x, g, w_router_p, tm=256):
    t, d = x.shape
    return pl.pallas_call(
        _moe_route_kernel,
        out_shape=(jax.ShapeDtypeStruct((t, d // 2), jnp.uint32), jax.ShapeDtypeStruct((t, V7X_LANES), F32),
                   jax.ShapeDtypeStruct((SUBLANES, t), F32), jax.ShapeDtypeStruct((8, V7X_LANES), F32)),
        grid=(t // tm,),
        in_specs=[pl.BlockSpec((tm, d), lambda i: (i, 0)), pl.BlockSpec((1, d), lambda i: (0, 0)),
                  pl.BlockSpec((d, V7X_LANES), lambda i: (0, 0))],
        out_specs=(pl.BlockSpec((tm, d // 2), lambda i: (i, 0)), pl.BlockSpec((tm, V7X_LANES), lambda i: (i, 0)),
                   pl.BlockSpec((SUBLANES, tm), lambda i: (0, i)), pl.BlockSpec((8, V7X_LANES), lambda i: (0, 0))),
        scratch_shapes=[pltpu.VMEM((8, V7X_LANES), F32)],
        compiler_params=_cparams("arbitrary"),
        name="moe_route",
    )(x, g.reshape(1, d), w_router_p)


def _moe_final_kernel(pos0_ref, pos1_ref, x_ref, meta_ref, g_ref, ys_ref, o_ref, ybuf, o_scr, sems, *, lsub,
                      nsteps):
    nres, tp, d = x_ref.shape
    dh = d // 2
    nj = lsub // tp
    b = pl.program_id(0)
    j = pl.program_id(1)
    step = b * nj + j
    slot = step % 2

    def copies(bb, jj, sl, r, mm):
        tok = (bb * nres + r) * lsub + jj * tp + mm
        return [_row_copy(ys_ref, p[tok], ybuf.at[sl, k], r * tp + mm, sems.at[sl])
                for k, p in enumerate((pos0_ref, pos1_ref))]

    def for_all_rows(bb, jj, sl, start):
        for r in range(nres):
            def body(mm, c, r=r):
                for cp in copies(bb, jj, sl, r, mm):
                    if start:
                        cp.start()
                    else:
                        cp.wait()
                return c

            lax.fori_loop(0, tp, body, 0, unroll=DMA_UNROLL)

    @pl.when(step == 0)
    def _():
        for_all_rows(b, j, slot, True)

    @pl.when(step + 1 < nsteps)
    def _():
        nxt = j + 1 == nj
        for_all_rows(jnp.where(nxt, b + 1, b), jnp.where(nxt, 0, j + 1), 1 - slot, True)

    for_all_rows(b, j, slot, False)
    for r in range(nres):
        rows = slice(r * tp, (r + 1) * tp)
        meta = meta_ref[r]
        g1 = meta[:, 2:3]
        g2 = meta[:, 3:4]
        a0, b0 = unpack_bf16_pair(ybuf[slot, 0, rows, :])
        a1, b1 = unpack_bf16_pair(ybuf[slot, 1, rows, :])
        x = x_ref[r]
        xa = x[:, 0:dh] + a0 * g1 + a1 * g2
        xb = x[:, dh:d] + b0 * g1 + b1 * g2
        ms = (jnp.sum(xa * xa, axis=-1, keepdims=True) + jnp.sum(xb * xb, axis=-1, keepdims=True)) * (1.0 / d)
        inv = lax.rsqrt(ms + EPS)
        ya = xa * inv * g_ref[:, 0:dh]
        yb = xb * inv * g_ref[:, dh:d]
        out_rows = pl.ds(r, tp, stride=nres)
        for s in range(dh // V7X_LANES):
            ls = slice(s * V7X_LANES, (s + 1) * V7X_LANES)
            o_scr[s, out_rows, :] = ya[:, ls]
            o_scr[dh // V7X_LANES + s, out_rows, :] = yb[:, ls]
    for s in range(d // V7X_LANES):
        o_ref[:, s * V7X_LANES:(s + 1) * V7X_LANES] = o_scr[s]


def moe_final(x, ys, pos0, pos1, meta, g, *, bsz, seq, tp=128):
    t, d = x.shape
    lsub = seq // RES
    tm = RES * tp
    xv = x.reshape(bsz * RES, lsub, d)
    mv = meta.reshape(bsz * RES, lsub, V7X_LANES)
    grid_spec = pltpu.PrefetchScalarGridSpec(
        num_scalar_prefetch=2,
        grid=(bsz, lsub // tp),
        in_specs=[pl.BlockSpec((RES, tp, d), lambda b, j, p0, p1: (b, j, 0)),
                  pl.BlockSpec((RES, tp, V7X_LANES), lambda b, j, p0, p1: (b, j, 0)),
                  pl.BlockSpec((1, d), lambda b, j, p0, p1: (0, 0)), pl.BlockSpec(memory_space=pl.ANY)],
        out_specs=pl.BlockSpec((tm, d), lambda b, j, p0, p1: (b * (lsub // tp) + j, 0)),
        scratch_shapes=[pltpu.VMEM((2, TOP_K, tm, d // 2), jnp.uint32), pltpu.VMEM((d // V7X_LANES, tm, V7X_LANES), F32),
                        pltpu.SemaphoreType.DMA((2,))],
    )
    return pl.pallas_call(
        functools.partial(_moe_final_kernel, lsub=lsub, nsteps=bsz * (lsub // tp)),
        out_shape=jax.ShapeDtypeStruct((t, d), F32),
        grid_spec=grid_spec,
        compiler_params=_cparams("arbitrary", "arbitrary"),
        name="moe_final",
    )(pos0, pos1, xv, mv, g.reshape(1, d), ys)


PROJ_TM = 2048


def _pad_lanes(v, fill=0.0):
    v = v.reshape(1, -1).astype(F32)
    return jnp.pad(v, ((0, 0), (0, V7X_LANES - v.shape[1])), constant_values=fill)


def even_layer(x, h, p, *, bsz, seq):
    t, d = x.shape
    d_ssd = d
    d_conf = d
    heads = d_ssd // SSD_HEAD_DIM
    conv_dim = d_ssd + 2 * SSD_GROUPS * SSD_STATE
    i1 = d_ssd + conv_dim
    i2 = i1 + heads
    w_in_t = jnp.swapaxes(p["w_in"], 0, 1)
    zx = matmul_nt(h, w_in_t, row_start=0, n_cols=i1, tn=1024, tm=PROJ_TM, name="in_proj_zx")
    dt_raw = matmul_nt(h, w_in_t, row_start=i1, n_cols=V7X_LANES, tn=V7X_LANES, out_dtype=F32, name="in_proj_dt")
    conf = matmul_nt(h, w_in_t, row_start=i2, n_cols=2 * d_conf, tn=1024, tm=PROJ_TM, name="in_proj_conf")
    u = conf_module(conf, p["conf_dw_w"], p["conf_dw_b"], p["conf_ln_w"], p["conf_ln_b"], bsz=bsz, seq=seq)
    dt, acs2, rowt = ssd_prep(dt_raw, _pad_lanes(p["dt_bias"]), _pad_lanes(p["a_log"]), bsz=bsz, seq=seq)
    dskip_e = jnp.repeat(p["d_skip"].astype(F32), SSD_HEAD_DIM).reshape(1, d_ssd)
    y_ssd = ssd_main(zx, p["conv_w"], p["conv_b"], dt, acs2, rowt, dskip_e, p["ssd_norm_w"],
                     bsz=bsz, seq=seq, d_ssd=d_ssd)
    return matmul([y_ssd, u], p["w_out"], n_cols=d, tn=256, tm=PROJ_TM, res=x, out_dtype=F32, name="even_out_proj")


def dense_ffn(h, w1, w3, w2, gsz=1024):
    t = h.shape[0]
    ng = t // gsz
    ge = jnp.zeros((ng,), jnp.int32)
    gb = jnp.arange(ng, dtype=jnp.int32)
    gr = jnp.full((ng,), gsz, jnp.int32)
    return ffn_groups(h, w1[None], w3[None], w2[None], ge, gb, gr, gsz=gsz, tf=512, out_buffers=1)


def odd_layer_attn(x, h, p, *, bsz, seq):
    t, d = x.shape
    hd = d // ATTN_HEADS
    qkv = matmul([h], p["w_qkv"], n_cols=3 * d, tn=1024, tm=PROJ_TM, name="qkv_proj")
    tabs = attn_bias_tables(p["rel_bias"])
    o = attention(qkv, tabs, bsz=bsz, seq=seq, hd=hd)
    return matmul([o], p["w_attn_out"], n_cols=d, tn=512, tm=2048, res=x, out_dtype=F32, name="attn_out_proj")


def moe_layer(x, norm_g, w_router, w1, w3, w2, final_g, *, bsz, seq, gsz=1280):
    t, d = x.shape
    ne = w1.shape[0]
    w_router_p = jnp.pad(w_router.astype(F32), ((0, 0), (0, V7X_LANES - ne)))
    h, meta, meta_t, cnt = moe_route(x, norm_g, w_router_p)
    counts = cnt[0, :ne].astype(jnp.int32)
    ngrp_e = (counts + gsz - 1) // gsz
    grp_end = jnp.cumsum(ngrp_e)
    grp_start = grp_end - ngrp_e
    ng = (t * TOP_K) // gsz + ne
    j = jnp.arange(ng, dtype=jnp.int32)
    total = grp_end[-1]
    last = jnp.maximum(total - 1, 0)
    jj = jnp.minimum(j, last)
    ge = jnp.searchsorted(grp_end, jj, side="right").astype(jnp.int32)
    ge = jnp.minimum(ge, ne - 1)
    rows = jnp.clip(counts[ge] - (jj - grp_start[ge]) * gsz, 0, gsz)
    gr = jnp.where(j < total, rows, 0).astype(jnp.int32)
    gb = j

    def dest_row(e, rank):
        off = jnp.zeros_like(rank)
        for k in range(ne):
            off = jnp.where(e == k, grp_start[k] * gsz, off)
        return off + rank

    pos0 = dest_row(meta_t[0].astype(jnp.int32), meta_t[4].astype(jnp.int32))
    pos1 = dest_row(meta_t[1].astype(jnp.int32), meta_t[5].astype(jnp.int32))
    xs = moe_dispatch(h, pos0, pos1, gr, gsz)
    ys = ffn_groups(xs, w1, w3, w2, ge, gb, gr, gsz=gsz, packed=True)
    return moe_final(x, ys, pos0, pos1, meta, final_g, bsz=bsz, seq=seq)


def kernel(x, norm_mix, norm_ffn, norm_final, even_w_in, ssd_conv_w, ssd_conv_b, ssd_dt_bias, ssd_a_log, ssd_d, ssd_norm_w, conf_dw_w, conf_dw_b, conf_ln_w, conf_ln_b, even_w_out, ffn_w1, ffn_w3, ffn_w2, attn_w_qkv, attn_w_out, rel_bias, moe_router, moe_w1, moe_w3, moe_w2):
    bsz, seq, d = x.shape
    assert norm_mix.shape[0] == 2, "two-layer trunk: one even (SSD+Conformer/FFN) and one odd (attention/MoE) layer"
    xf = x.reshape(bsz * seq, d)
    h = rmsnorm(xf, norm_mix[0])
    p_even = dict(w_in=even_w_in[0], conv_w=ssd_conv_w[0], conv_b=ssd_conv_b[0], dt_bias=ssd_dt_bias[0],
                  a_log=ssd_a_log[0], d_skip=ssd_d[0], ssd_norm_w=ssd_norm_w[0], conf_dw_w=conf_dw_w[0],
                  conf_dw_b=conf_dw_b[0], conf_ln_w=conf_ln_w[0], conf_ln_b=conf_ln_b[0], w_out=even_w_out[0])
    x1 = even_layer(xf, h, p_even, bsz=bsz, seq=seq)
    h1 = rmsnorm(x1, norm_ffn[0])
    y_ffn = dense_ffn(h1, ffn_w1[0], ffn_w3[0], ffn_w2[0])
    x2, h2 = add_norm_to_residue_major(x1, y_ffn, norm_mix[1], bsz=bsz, seq=seq, nres=RES)
    p_odd = dict(w_qkv=attn_w_qkv[0], w_attn_out=attn_w_out[0], rel_bias=rel_bias)
    x3 = odd_layer_attn(x2, h2, p_odd, bsz=bsz, seq=seq)
    out = moe_layer(x3, norm_ffn[1], moe_router[0], moe_w1[0], moe_w3[0], moe_w2[0], norm_final, bsz=bsz, seq=seq)
    return out.reshape(bsz, seq, d)
```
